```python
import math
import jax, jax.numpy as jnp
from jax import lax
import numpy as np


D_MODEL = 1024
BATCH = 8
SEQ = 2048
DEPTH = 4
DEC_BATCH = 128
DEC_SEQ = 1
PAST_LEN = 16384
PAGE_SIZE = 128

N_MIXERS = 4
EPS = 1e-6
GN_EPS = 1e-5
N_A = len(range(0, DEPTH, N_MIXERS))
N_B = len(range(1, DEPTH, N_MIXERS))
N_C = len(range(2, DEPTH, N_MIXERS))
N_D = len(range(3, DEPTH, N_MIXERS))
POOL_WINDOWS = (2, 4, 8, 16)
N_POOL = len(POOL_WINDOWS)
POOL_GROUP = D_MODEL // N_POOL
POOL_BUF = max(POOL_WINDOWS) - 1
CHUNK = 128
GM_INNER = D_MODEL
GM_GROUPS = 8
GM_GDIM = GM_INNER // GM_GROUPS
RET_HEADS = 8
RET_DK = D_MODEL // RET_HEADS
RET_DV = 2 * D_MODEL // RET_HEADS
RET_QK = RET_HEADS * RET_DK
RET_VW = RET_HEADS * RET_DV
RET_CHUNK = 128
ROPE_BASE = 10000.0
LRU_WIDTH = D_MODEL
LRU_HEADS = 8
LRU_BLOCK = LRU_WIDTH // LRU_HEADS
CONV_W = 4
LRU_C = 8.0
D_FF = 4 * D_MODEL

kernel_name = 'hybrid_pool_gmlp_retention_rglru_step'


def rmsnorm(x, g):
    xf = x.astype(jnp.float32)
    y = xf * lax.rsqrt(jnp.mean(xf * xf, axis=-1, keepdims=True) + EPS)
    return (y * g.astype(jnp.float32)).astype(x.dtype)


def pool_mixer(x, buf, pos0, g, w_grp, scale):
    B, L, D = x.shape
    h = rmsnorm(x, g)
    ext = jnp.concatenate([buf.astype(x.dtype), h], axis=1)
    extf = ext.astype(jnp.float32)
    cs = jnp.pad(jnp.cumsum(extf, axis=1), ((0, 0), (1, 0), (0, 0)))
    pos = pos0 + jnp.arange(L)
    means = []
    for gi, w in enumerate(POOL_WINDOWS):
        c0, c1 = gi * POOL_GROUP, (gi + 1) * POOL_GROUP
        hi = cs[:, POOL_BUF + 1:POOL_BUF + 1 + L, c0:c1]
        lo = cs[:, POOL_BUF + 1 - w:POOL_BUF + 1 - w + L, c0:c1]
        cnt = jnp.minimum(pos + 1, w).astype(jnp.float32)
        means.append((hi - lo) / cnt[None, :, None])
    d = (jnp.concatenate(means, axis=-1) - extf[:, POOL_BUF:]).reshape(B, L, N_POOL, POOL_GROUP)
    y = jnp.einsum('blgc,gce->blge', d, w_grp.astype(jnp.float32)).reshape(B, L, D)
    y = y * scale.astype(jnp.float32)
    return y.astype(x.dtype), ext[:, -POOL_BUF:]


def gmlp_mixer(x, g, w_in, b_in, ln_g, ln_b, w_s, b_s, w_out):
    B, L, _ = x.shape
    h = rmsnorm(x, g)
    z = jax.nn.gelu(h @ w_in + b_in)
    u, v = z[..., :GM_INNER], z[..., GM_INNER:]
    vf = v.astype(jnp.float32)
    mu = jnp.mean(vf, axis=-1, keepdims=True)
    var = jnp.mean(jnp.square(vf - mu), axis=-1, keepdims=True)
    vn = ((vf - mu) * lax.rsqrt(var + EPS) * ln_g.astype(jnp.float32) + ln_b.astype(jnp.float32)).astype(x.dtype)
    cl = min(L, CHUNK)
    Lp = -(-L // cl) * cl
    vp = jnp.pad(vn, ((0, 0), (0, Lp - L), (0, 0))).reshape(B, Lp // cl, cl, GM_GROUPS, GM_GDIM)
    ws = jnp.where(jnp.tril(jnp.ones((cl, cl), dtype=bool))[None], w_s[:, :cl, :cl], 0.0)
    mixed = jnp.einsum('gts,bnsgc->bntgc', ws, vp) + b_s[:, :cl].T[None, None, :, :, None]
    mixed = mixed.reshape(B, Lp, GM_INNER)[:, :L]
    y = (u * mixed) @ w_out
    return y.astype(x.dtype), vn


def rope(t, pos):
    half = t.shape[-1] // 2
    freqs = jnp.exp(-math.log(ROPE_BASE) * jnp.arange(half, dtype=jnp.float32) / half)
    ang = pos.astype(jnp.float32)[:, None] * freqs[None]
    cos = jnp.cos(ang)[None, :, None, :]
    sin = jnp.sin(ang)[None, :, None, :]
    t1, t2 = t[..., :half], t[..., half:]
    return jnp.concatenate([t1 * cos - t2 * sin, t1 * sin + t2 * cos], axis=-1)


def retention_mixer(x, s0, pos0, g, w_in, gn_g, gn_b, w_out):
    B, L, _ = x.shape
    h = rmsnorm(x, g)
    p = (h @ w_in).astype(jnp.float32)
    q = p[..., :RET_QK].reshape(B, L, RET_HEADS, RET_DK)
    k = p[..., RET_QK:2 * RET_QK].reshape(B, L, RET_HEADS, RET_DK)
    v = p[..., 2 * RET_QK:2 * RET_QK + RET_VW].reshape(B, L, RET_HEADS, RET_DV)
    gate = p[..., 2 * RET_QK + RET_VW:]
    pos = pos0 + jnp.arange(L)
    q = rope(q, pos)
    k = rope(k, pos) * (RET_DK ** -0.5)
    C = math.gcd(L, RET_CHUNK)
    N = L // C
    log_gamma = jnp.log1p(-jnp.exp2(-5.0 - jnp.arange(RET_HEADS, dtype=jnp.float32)))
    idx = jnp.arange(C, dtype=jnp.float32)
    diff = idx[:, None] - idx[None, :]
    dmask = jnp.where(diff[None] >= 0, jnp.exp(log_gamma[:, None, None] * jnp.maximum(diff, 0.0)[None]), 0.0)
    q_dec = jnp.exp(log_gamma[:, None] * (idx + 1.0))[..., None]
    k_dec = jnp.exp(log_gamma[:, None] * (C - 1.0 - idx))[..., None]
    chunk_dec = jnp.exp(log_gamma * C)[:, None, None]

    def to_chunks(t):
        return t.reshape(B, N, C, RET_HEADS, t.shape[-1]).transpose(1, 0, 3, 2, 4)

    def step(S, inp):
        qi, ki, vi = inp
        scores = jnp.einsum('bhtd,bhsd->bhts', qi, ki) * dmask
        o = jnp.einsum('bhts,bhsv->bhtv', scores, vi) + jnp.einsum('bhtd,bhdv->bhtv', qi * q_dec, S)
        S = S * chunk_dec + jnp.einsum('bhsd,bhsv->bhdv', ki * k_dec, vi)
        return S, o

    S, o = lax.scan(step, s0.astype(jnp.float32), (to_chunks(q), to_chunks(k), to_chunks(v)))
    o = o.transpose(1, 0, 3, 2, 4).reshape(B, L, RET_HEADS, RET_DV)
    mu = jnp.mean(o, axis=-1, keepdims=True)
    var = jnp.mean(jnp.square(o - mu), axis=-1, keepdims=True)
    on = ((o - mu) * lax.rsqrt(var + GN_EPS)).reshape(B, L, RET_VW)
    on = on * gn_g.astype(jnp.float32) + gn_b.astype(jnp.float32)
    y = (jax.nn.silu(gate) * on) @ w_out.astype(jnp.float32)
    return y.astype(x.dtype), S.astype(s0.dtype)


def lru_combine(c1, c2):
    a1, b1 = c1
    a2, b2 = c2
    return (a1 * a2, a2 * b1 + b2)


def rglru_mixer(x, conv_buf, h0, pos0, g, w_in, conv_w, conv_b, w_a, b_a, w_x, b_x, lam, w_out):
    B, L, _ = x.shape
    h = rmsnorm(x, g)
    z = h @ w_in
    gate = jax.nn.gelu(z[..., :LRU_WIDTH].astype(jnp.float32))
    xb = z[..., LRU_WIDTH:]
    ext = jnp.concatenate([conv_buf.astype(x.dtype), xb], axis=1)
    extf = ext.astype(jnp.float32)
    cw = conv_w.astype(jnp.float32)
    xc = conv_b.astype(jnp.float32) + sum(extf[:, j:j + L] * cw[j] for j in range(CONV_W))
    xh = xc.reshape(B, L, LRU_HEADS, LRU_BLOCK)
    r = jax.nn.sigmoid(jnp.einsum('blhi,hij->blhj', xh, w_a.astype(jnp.float32)).reshape(B, L, LRU_WIDTH) + b_a)
    i = jax.nn.sigmoid(jnp.einsum('blhi,hij->blhj', xh, w_x.astype(jnp.float32)).reshape(B, L, LRU_WIDTH) + b_x)
    log_a = -LRU_C * r * jax.nn.softplus(-lam.astype(jnp.float32))
    a = jnp.exp(log_a)
    mult = jnp.sqrt(-jnp.expm1(2.0 * log_a))
    reset = (pos0 + jnp.arange(L) == 0)[None, :, None]
    mult = jnp.where(reset, 1.0, mult)
    bvec = mult * (i * xc)
    bvec = bvec.at[:, 0].add(a[:, 0] * h0.astype(jnp.float32))
    hs = lax.associative_scan(lru_combine, (a, bvec), axis=1)[1]
    y = (hs * gate) @ w_out.astype(jnp.float32)
    return y.astype(x.dtype), ext[:, -(CONV_W - 1):], hs[:, -1].astype(h0.dtype)


def channel_mlp(x, g, w1, w2):
    h = rmsnorm(x, g)
    return (jnp.square(jax.nn.relu(h @ w1)) @ w2).astype(x.dtype)


def run_trunk(x, pos0, pool_bufs, ret_states, conv_bufs, lru_states, weights):
    (pool_norm, pool_w, pool_scale,
     gm_norm, gm_w_in, gm_b_in, gm_ln_g, gm_ln_b, gm_w_s, gm_b_s, gm_w_out,
     ret_norm, ret_w_in, ret_gn_g, ret_gn_b, ret_w_out,
     lru_norm, lru_w_in, lru_conv_w, lru_conv_b, lru_w_a, lru_b_a, lru_w_x, lru_b_x, lru_lam, lru_w_out,
     mlp_norm, mlp_w1, mlp_w2, final_norm) = weights
    new_pool, new_v, new_ret, new_conv, new_lru = [], [], [], [], []
    for li in range(DEPTH):
        m, j = li % N_MIXERS, li // N_MIXERS
        if m == 0:
            y, nb = pool_mixer(x, pool_bufs[j], pos0, pool_norm[j], pool_w[j], pool_scale[j])
            new_pool.append(nb)
        elif m == 1:
            y, nv = gmlp_mixer(x, gm_norm[j], gm_w_in[j], gm_b_in[j], gm_ln_g[j], gm_ln_b[j],
                               gm_w_s[j], gm_b_s[j], gm_w_out[j])
            new_v.append(nv)
        elif m == 2:
            y, ns = retention_mixer(x, ret_states[j], pos0, ret_norm[j], ret_w_in[j],
                                    ret_gn_g[j], ret_gn_b[j], ret_w_out[j])
            new_ret.append(ns)
        else:
            y, nc, nh = rglru_mixer(x, conv_bufs[j], lru_states[j], pos0, lru_norm[j], lru_w_in[j],
                                    lru_conv_w[j], lru_conv_b[j], lru_w_a[j], lru_b_a[j],
                                    lru_w_x[j], lru_b_x[j], lru_lam[j], lru_w_out[j])
            new_conv.append(nc)
            new_lru.append(nh)
        x = x + y
        x = x + channel_mlp(x, mlp_norm[li], mlp_w1[li], mlp_w2[li])
    out = rmsnorm(x, final_norm)
    return (out, jnp.stack(new_pool), jnp.stack(new_v), jnp.stack(new_ret),
            jnp.stack(new_conv), jnp.stack(new_lru))


def setup_inputs(seed: int = 0) -> dict:
    key = jax.random.key(seed)
    ks = iter(jax.random.split(key, 48))
    f32 = jnp.float32

    def nrm(shape, scale):
        return jax.random.normal(next(ks), shape, f32) * scale

    def gain(shape):
        return 1.0 + nrm(shape, 0.02)

    a0 = jax.random.uniform(next(ks), (N_D, LRU_WIDTH), f32, 0.9, 0.999)
    return {
        'x_prompt': nrm((BATCH, SEQ, D_MODEL), 1.0),
        'x_sample': nrm((DEC_BATCH, DEC_SEQ, D_MODEL), 1.0),
        'state_pool': nrm((N_A, DEC_BATCH, POOL_BUF, D_MODEL), 1.0),
        'state_ret': nrm((N_C, DEC_BATCH, RET_HEADS, RET_DK, RET_DV), 0.5),
        'state_conv': nrm((N_D, DEC_BATCH, CONV_W - 1, LRU_WIDTH), 1.0),
        'state_lru': nrm((N_D, DEC_BATCH, LRU_WIDTH), 0.5),
        'pool_norm': gain((N_A, D_MODEL)),
        'pool_w': nrm((N_A, N_POOL, POOL_GROUP, POOL_GROUP), POOL_GROUP ** -0.5),
        'pool_scale': gain((N_A, D_MODEL)),
        'gm_norm': gain((N_B, D_MODEL)),
        'gm_w_in': nrm((N_B, D_MODEL, 2 * GM_INNER), D_MODEL ** -0.5),
        'gm_b_in': nrm((N_B, 2 * GM_INNER), 0.02),
        'gm_ln_g': gain((N_B, GM_INNER)),
        'gm_ln_b': nrm((N_B, GM_INNER), 0.02),
        'gm_w_s': nrm((N_B, GM_GROUPS, CHUNK, CHUNK), CHUNK ** -0.5),
        'gm_b_s': gain((N_B, GM_GROUPS, CHUNK)),
        'gm_w_out': nrm((N_B, GM_INNER, D_MODEL), GM_INNER ** -0.5),
        'ret_norm': gain((N_C, D_MODEL)),
        'ret_w_in': nrm((N_C, D_MODEL, 2 * RET_QK + 2 * RET_VW), D_MODEL ** -0.5),
        'ret_gn_g': gain((N_C, RET_VW)),
        'ret_gn_b': nrm((N_C, RET_VW), 0.02),
        'ret_w_out': nrm((N_C, RET_VW, D_MODEL), RET_VW ** -0.5),
        'lru_norm': gain((N_D, D_MODEL)),
        'lru_w_in': nrm((N_D, D_MODEL, 2 * LRU_WIDTH), D_MODEL ** -0.5),
        'lru_conv_w': nrm((N_D, CONV_W, LRU_WIDTH), CONV_W ** -0.5),
        'lru_conv_b': nrm((N_D, LRU_WIDTH), 0.02),
        'lru_w_a': nrm((N_D, LRU_HEADS, LRU_BLOCK, LRU_BLOCK), LRU_BLOCK ** -0.5),
        'lru_b_a': nrm((N_D, LRU_WIDTH), 0.02),
        'lru_w_x': nrm((N_D, LRU_HEADS, LRU_BLOCK, LRU_BLOCK), LRU_BLOCK ** -0.5),
        'lru_b_x': nrm((N_D, LRU_WIDTH), 0.02),
        'lru_lam': jnp.log(a0) - jnp.log1p(-a0),
        'lru_w_out': nrm((N_D, LRU_WIDTH, D_MODEL), LRU_WIDTH ** -0.5),
        'mlp_norm': gain((DEPTH, D_MODEL)),
        'mlp_w1': nrm((DEPTH, D_MODEL, D_FF), D_MODEL ** -0.5),
        'mlp_w2': nrm((DEPTH, D_FF, D_MODEL), D_FF ** -0.5),
        'final_norm': gain((D_MODEL,)),
    }


def reference(x_prompt, x_sample, state_pool, state_ret, state_conv, state_lru,
              pool_norm, pool_w, pool_scale,
              gm_norm, gm_w_in, gm_b_in, gm_ln_g, gm_ln_b, gm_w_s, gm_b_s, gm_w_out,
              ret_norm, ret_w_in, ret_gn_g, ret_gn_b, ret_w_out,
              lru_norm, lru_w_in, lru_conv_w, lru_conv_b, lru_w_a, lru_b_a, lru_w_x, lru_b_x, lru_lam, lru_w_out,
              mlp_norm, mlp_w1, mlp_w2, final_norm):
    weights = (pool_norm, pool_w, pool_scale,
               gm_norm, gm_w_in, gm_b_in, gm_ln_g, gm_ln_b, gm_w_s, gm_b_s, gm_w_out,
               ret_norm, ret_w_in, ret_gn_g, ret_gn_b, ret_w_out,
               lru_norm, lru_w_in, lru_conv_w, lru_conv_b, lru_w_a, lru_b_a, lru_w_x, lru_b_x, lru_lam, lru_w_out,
               mlp_norm, mlp_w1, mlp_w2, final_norm)
    dt = x_prompt.dtype
    p_pool0 = jnp.zeros((N_A, BATCH, POOL_BUF, D_MODEL), dt)
    p_ret0 = jnp.zeros((N_C, BATCH, RET_HEADS, RET_DK, RET_DV), state_ret.dtype)
    p_conv0 = jnp.zeros((N_D, BATCH, CONV_W - 1, LRU_WIDTH), dt)
    p_lru0 = jnp.zeros((N_D, BATCH, LRU_WIDTH), state_lru.dtype)
    y_prompt, pool_p, _, ret_p, conv_p, lru_p = run_trunk(
        x_prompt, 0, p_pool0, p_ret0, p_conv0, p_lru0, weights)
    y_sample, pool_s, v_s, ret_s, conv_s, lru_s = run_trunk(
        x_sample, PAST_LEN, state_pool, state_ret, state_conv, state_lru, weights)
    return (y_prompt, y_sample, pool_p, pool_s, v_s, ret_p, ret_s, conv_p, conv_s, lru_p, lru_s)
```

```python
import functools
import math

import jax
import jax.numpy as jnp
import numpy as np
from jax import lax
from jax.experimental import pallas as pl
from jax.experimental.pallas import tpu as pltpu

F32 = jnp.float32
BF16 = jnp.bfloat16

D = 1024
EPS = 1e-6
GN_EPS = 1e-5
PAST_LEN = 16384
POOL_WINDOWS = (2, 4, 8, 16)
POOL_GROUP = D // len(POOL_WINDOWS)
POOL_BUF = max(POOL_WINDOWS) - 1
CHUNK = 128
GM_GROUPS = 8
HEADS = 8
DK = D // HEADS
DV = 2 * D // HEADS
QK = HEADS * DK
VW = HEADS * DV
ROPE_BASE = 10000.0
CONV_W = 4
LRU_C = 8.0
D_FF = 4 * D

MIB = 1024 * 1024
SUBLANES = 8
LANES = 128


def _params(vmem_mib, n_grid):
    return pltpu.CompilerParams(
        dimension_semantics=("arbitrary",) * n_grid,
        vmem_limit_bytes=vmem_mib * MIB,
    )


def _const(shape):
    zeros = (0,) * len(shape)
    return pl.BlockSpec(shape, lambda *_: zeros, pipeline_mode=pl.Buffered(1))


def _rms(x, g):
    ms = jnp.mean(x * x, axis=-1, keepdims=True)
    return x * lax.rsqrt(ms + EPS) * g


def _dot(a, b):
    return jnp.dot(a, b, preferred_element_type=F32)


def _softplus(x):
    return jnp.maximum(x, 0.0) + jnp.log1p(jnp.exp(-jnp.abs(x)))


MLP_FC = 512


def _mlp_body(x_ref, g_ref, w1_ref, w2_ref, fg_ref, o_ref, *, final):
    x = x_ref[...]
    h = _rms(x, g_ref[...]).astype(BF16)
    acc = x
    for c in range(D_FF // MLP_FC):
        lo, hi = c * MLP_FC, (c + 1) * MLP_FC
        a = _dot(h, w1_ref[:, lo:hi])
        a = jnp.square(jnp.maximum(a, 0.0)).astype(BF16)
        acc = acc + _dot(a, w2_ref[lo:hi, :])
    if final:
        acc = _rms(acc, fg_ref[...])
    o_ref[...] = acc


def _mlp(x2d, g, w1b, w2b, fg, *, final, tm):
    n = x2d.shape[0]
    row = pl.BlockSpec((tm, D), lambda i: (i, 0))
    return pl.pallas_call(
        functools.partial(_mlp_body, final=final),
        grid=(n // tm,),
        in_specs=[row, _const((1, D)), _const((D, D_FF)), _const((D_FF, D)), _const((1, D))],
        out_specs=row,
        out_shape=jax.ShapeDtypeStruct((n, D), F32),
        compiler_params=_params(48, 1),
        name="mlp",
    )(x2d, g.reshape(1, D), w1b, w2b, fg.reshape(1, D))


POOL_HALO = 16


def _pool_prompt_body(x_ref, g_ref, w_ref, sc_ref, o_ref, buf_ref, ext_ref, *, T):
    t = pl.program_id(1)

    @pl.when(t == 0)
    def _():
        ext_ref[0:POOL_HALO, :] = jnp.zeros((POOL_HALO, D), F32)

    x = x_ref[...]
    h = _rms(x, g_ref[...])
    ext_ref[POOL_HALO:POOL_HALO + T, :] = h
    pos = t * T + lax.broadcasted_iota(jnp.int32, (T, 1), 0)
    ys = []
    for gi, w in enumerate(POOL_WINDOWS):
        c0, c1 = gi * POOL_GROUP, (gi + 1) * POOL_GROUP
        hg = h[:, c0:c1]
        s = hg
        for k in range(1, w):
            s = s + ext_ref[POOL_HALO - k:POOL_HALO - k + T, c0:c1]
        cnt = jnp.minimum(pos + 1, w).astype(F32)
        d = s / cnt - hg
        ys.append(_dot(d.astype(BF16), w_ref[gi]))
    y = jnp.concatenate(ys, axis=-1) * sc_ref[...]
    o_ref[...] = x + y
    tail = ext_ref[T:T + POOL_HALO, :]
    ext_ref[0:POOL_HALO, :] = tail
    buf_ref[...] = tail[POOL_HALO - POOL_BUF:, :]


def _pool_prompt(x, g, wb, sc, *, T=512):
    B, L, _ = x.shape
    blk = pl.BlockSpec((None, T, D), lambda b, t: (b, t, 0))
    return pl.pallas_call(
        functools.partial(_pool_prompt_body, T=T),
        grid=(B, L // T),
        in_specs=[blk, _const((1, D)), _const((len(POOL_WINDOWS), POOL_GROUP, POOL_GROUP)), _const((1, D))],
        out_specs=[blk, pl.BlockSpec((None, POOL_BUF, D), lambda b, t: (b, 0, 0))],
        out_shape=[jax.ShapeDtypeStruct((B, L, D), F32), jax.ShapeDtypeStruct((B, POOL_BUF, D), F32)],
        scratch_shapes=[pltpu.VMEM((T + POOL_HALO, D), F32)],
        compiler_params=_params(32, 2),
        name="pool_prompt",
    )(x, g.reshape(1, D), wb, sc.reshape(1, D))


def _pool_sample_body(x_ref, buf_ref, g_ref, w_ref, sc_ref, o_ref, nbuf_ref, *, pos0):
    x = x_ref[...]
    h = _rms(x, g_ref[...])
    ys = []
    for gi, w in enumerate(POOL_WINDOWS):
        c0, c1 = gi * POOL_GROUP, (gi + 1) * POOL_GROUP
        hg = h[:, c0:c1]
        s = hg
        for k in range(1, w):
            r = (POOL_BUF - k) * D
            s = s + buf_ref[:, r + c0:r + c1]
        cnt = float(min(pos0 + 1, w))
        d = s / cnt - hg
        ys.append(_dot(d.astype(BF16), w_ref[gi]))
    y = jnp.concatenate(ys, axis=-1) * sc_ref[...]
    o_ref[...] = x + y
    nbuf_ref[:, 0:(POOL_BUF - 1) * D] = buf_ref[:, D:POOL_BUF * D]
    nbuf_ref[:, (POOL_BUF - 1) * D:] = h


def _pool_sample(x2d, buf2d, g, wb, sc, *, pos0, bt=32):
    n = x2d.shape[0]
    row = pl.BlockSpec((bt, D), lambda i: (i, 0))
    brow = pl.BlockSpec((bt, POOL_BUF * D), lambda i: (i, 0))
    return pl.pallas_call(
        functools.partial(_pool_sample_body, pos0=pos0),
        grid=(n // bt,),
        in_specs=[row, brow, _const((1, D)), _const((len(POOL_WINDOWS), POOL_GROUP, POOL_GROUP)), _const((1, D))],
        out_specs=[row, brow],
        out_shape=[jax.ShapeDtypeStruct((n, D), F32), jax.ShapeDtypeStruct((n, POOL_BUF * D), F32)],
        compiler_params=_params(32, 1),
        name="pool_sample",
    )(x2d, buf2d, g.reshape(1, D), wb, sc.reshape(1, D))


def _gmlp_front(x, g_ref, win_ref, bin_ref, lng_ref, lnb_ref):
    h = _rms(x, g_ref[...]).astype(BF16)
    z = jax.nn.gelu(_dot(h, win_ref[...]) + bin_ref[...])
    u, v = z[:, :D], z[:, D:]
    mu = jnp.mean(v, axis=-1, keepdims=True)
    vc = v - mu
    var = jnp.mean(jnp.square(vc), axis=-1, keepdims=True)
    vn = vc * lax.rsqrt(var + EPS) * lng_ref[...] + lnb_ref[...]
    return u, vn


def _gmlp_prompt_body(x_ref, g_ref, win_ref, bin_ref, lng_ref, lnb_ref, ws_ref, bs_ref, wout_ref,
                      o_ref, *, T):
    x = x_ref[...]
    u, vn = _gmlp_front(x, g_ref, win_ref, bin_ref, lng_ref, lnb_ref)
    vnb = vn.astype(BF16)
    ti = lax.broadcasted_iota(jnp.int32, (CHUNK, CHUNK), 0)
    si = lax.broadcasted_iota(jnp.int32, (CHUNK, CHUNK), 1)
    causal = ti >= si
    wsm = [jnp.where(causal, ws_ref[gi], 0.0).astype(BF16) for gi in range(GM_GROUPS)]
    rows = []
    for n in range(T // CHUNK):
        r0, r1 = n * CHUNK, (n + 1) * CHUNK
        cols = [_dot(wsm[gi], vnb[r0:r1, gi * CHUNK:(gi + 1) * CHUNK]) for gi in range(GM_GROUPS)]
        rows.append(jnp.concatenate(cols, axis=-1) + bs_ref[...])
    mixed = jnp.concatenate(rows, axis=0)
    y = _dot((u * mixed).astype(BF16), wout_ref[...])
    o_ref[...] = x + y


def _gmlp_prompt(x, g, winb, b_in, ln_g, ln_b, w_s, bs_full, woutb, *, T=512):
    B, L, _ = x.shape
    blk = pl.BlockSpec((None, T, D), lambda b, t: (b, t, 0))
    return pl.pallas_call(
        functools.partial(_gmlp_prompt_body, T=T),
        grid=(B, L // T),
        in_specs=[blk, _const((1, D)), _const((D, 2 * D)), _const((1, 2 * D)), _const((1, D)), _const((1, D)),
                  _const((GM_GROUPS, CHUNK, CHUNK)), _const((CHUNK, D)), _const((D, D))],
        out_specs=blk,
        out_shape=jax.ShapeDtypeStruct((B, L, D), F32),
        compiler_params=_params(48, 2),
        name="gmlp_prompt",
    )(x, g.reshape(1, D), winb, b_in.reshape(1, 2 * D), ln_g.reshape(1, D), ln_b.reshape(1, D),
      w_s, bs_full, woutb)


def _gmlp_sample_body(x_ref, g_ref, win_ref, bin_ref, lng_ref, lnb_ref, sw_ref, sb_ref, wout_ref,
                      o_ref, vn_ref):
    x = x_ref[...]
    u, vn = _gmlp_front(x, g_ref, win_ref, bin_ref, lng_ref, lnb_ref)
    mixed = vn * sw_ref[...] + sb_ref[...]
    y = _dot((u * mixed).astype(BF16), wout_ref[...])
    o_ref[...] = x + y
    vn_ref[...] = vn


def _gmlp_sample(x2d, g, winb, b_in, ln_g, ln_b, sw, sb, woutb):
    n = x2d.shape[0]
    full = _const((n, D))
    return pl.pallas_call(
        _gmlp_sample_body,
        grid=(1,),
        in_specs=[full, _const((1, D)), _const((D, 2 * D)), _const((1, 2 * D)), _const((1, D)), _const((1, D)),
                  _const((1, D)), _const((1, D)), _const((D, D))],
        out_specs=[pl.BlockSpec((n, D), lambda i: (0, 0)), pl.BlockSpec((n, D), lambda i: (0, 0))],
        out_shape=[jax.ShapeDtypeStruct((n, D), F32), jax.ShapeDtypeStruct((n, D), F32)],
        compiler_params=_params(32, 1),
        name="gmlp_sample",
    )(x2d, g.reshape(1, D), winb, b_in.reshape(1, 2 * D), ln_g.reshape(1, D), ln_b.reshape(1, D),
      sw, sb, woutb)


def _ret_constants(C):
    log_gamma = np.log1p(-np.exp2(-5.0 - np.arange(HEADS, dtype=np.float64)))
    idx = np.arange(C, dtype=np.float64)
    diff = idx[:, None] - idx[None, :]
    dmask = np.where(diff[None] >= 0, np.exp(log_gamma[:, None, None] * np.maximum(diff, 0.0)[None]), 0.0)
    q_dec = np.exp(log_gamma[:, None] * (idx + 1.0))
    k_dec = np.exp(log_gamma[:, None] * (C - 1.0 - idx))
    chunk_dec = np.exp(log_gamma * C)
    return dmask.astype(np.float32), q_dec.astype(np.float32), k_dec.astype(np.float32), chunk_dec


def _rope_tables(pos):
    half = DK // 2
    freqs = np.exp(-math.log(ROPE_BASE) * np.arange(half, dtype=np.float64) / half)
    ang = np.asarray(pos, dtype=np.float64)[:, None] * freqs[None]
    cos = np.concatenate([np.cos(ang), np.cos(ang)], axis=-1)
    sin = np.concatenate([-np.sin(ang), np.sin(ang)], axis=-1)
    return cos, sin


def _rope(t, cos, sin):
    return t * cos + pltpu.roll(t, DK // 2, axis=1) * sin


def _group_norm_gate(o, gate, gng, gnb):
    mu = jnp.mean(o, axis=-1, keepdims=True)
    oc = o - mu
    var = jnp.mean(jnp.square(oc), axis=-1, keepdims=True)
    on = oc * lax.rsqrt(var + GN_EPS) * gng + gnb
    return jax.nn.silu(gate) * on


def _ret_prompt_body(x_ref, g_ref, win_ref, cq_ref, sq_ref, ck_ref, sk_ref, dm_ref, qd_ref, kd_ref,
                     gng_ref, gnb_ref, wout_ref, o_ref, s_out_ref, p_scr, gated_scr, s_scr,
                     *, T, chunk_dec):
    t = pl.program_id(1)

    @pl.when(t == 0)
    def _():
        s_scr[...] = jnp.zeros_like(s_scr)

    x = x_ref[...]
    h = _rms(x, g_ref[...]).astype(BF16)
    p_scr[...] = _dot(h, win_ref[...])

    def chunk(n, carry):
        r0 = pl.multiple_of(n * CHUNK, CHUNK)
        rows = pl.ds(r0, CHUNK)
        cq, sq = cq_ref[rows, :], sq_ref[rows, :]
        ck, sk = ck_ref[rows, :], sk_ref[rows, :]
        for hd in range(HEADS):
            q = _rope(p_scr[rows, hd * DK:(hd + 1) * DK], cq, sq)
            k = _rope(p_scr[rows, QK + hd * DK:QK + (hd + 1) * DK], ck, sk)
            v = p_scr[rows, 2 * QK + hd * DV:2 * QK + (hd + 1) * DV]
            gate = p_scr[rows, 2 * QK + VW + hd * DV:2 * QK + VW + (hd + 1) * DV]
            vb = v.astype(BF16)
            s_old = s_scr[hd]
            scores = lax.dot_general(q.astype(BF16), k.astype(BF16), (((1,), (1,)), ((), ())),
                                     preferred_element_type=F32) * dm_ref[hd]
            o = _dot(scores.astype(BF16), vb) + _dot((q * qd_ref[hd]).astype(BF16), s_old.astype(BF16))
            kt = (k * kd_ref[hd]).T.astype(BF16)
            s_scr[hd] = s_old * chunk_dec[hd] + _dot(kt, vb)
            gated = _group_norm_gate(o, gate, gng_ref[:, hd * DV:(hd + 1) * DV],
                                     gnb_ref[:, hd * DV:(hd + 1) * DV])
            gated_scr[rows, hd * DV:(hd + 1) * DV] = gated.astype(BF16)
        return carry

    lax.fori_loop(0, T // CHUNK, chunk, 0)
    y = _dot(gated_scr[...], wout_ref[...])
    o_ref[...] = x + y
    s_out_ref[...] = s_scr[...]


def _ret_prompt(x, g, winb, gn_g, gn_b, woutb, *, T=512):
    B, L, _ = x.shape
    dmask, q_dec, k_dec, chunk_dec = _ret_constants(CHUNK)
    cos, sin = _rope_tables(np.arange(L))
    scale = DK ** -0.5
    cq, sq = jnp.asarray(cos, F32), jnp.asarray(sin, F32)
    ck, sk = jnp.asarray(cos * scale, F32), jnp.asarray(sin * scale, F32)
    qd = jnp.asarray(np.broadcast_to(q_dec[:, :, None], (HEADS, CHUNK, DK)))
    kd = jnp.asarray(np.broadcast_to(k_dec[:, :, None], (HEADS, CHUNK, DK)))
    blk = pl.BlockSpec((None, T, D), lambda b, t: (b, t, 0))
    tab = pl.BlockSpec((T, DK), lambda b, t: (t, 0))
    hcc = _const((HEADS, CHUNK, CHUNK))
    return pl.pallas_call(
        functools.partial(_ret_prompt_body, T=T, chunk_dec=tuple(float(c) for c in chunk_dec)),
        grid=(B, L // T),
        in_specs=[blk, _const((1, D)), _const((D, 2 * QK + 2 * VW)), tab, tab, tab, tab,
                  hcc, hcc, hcc, _const((1, VW)), _const((1, VW)), _const((VW, D))],
        out_specs=[blk, pl.BlockSpec((None, HEADS, DK, DV), lambda b, t: (b, 0, 0, 0))],
        out_shape=[jax.ShapeDtypeStruct((B, L, D), F32), jax.ShapeDtypeStruct((B, HEADS, DK, DV), F32)],
        scratch_shapes=[pltpu.VMEM((T, 2 * QK + 2 * VW), F32), pltpu.VMEM((T, VW), BF16),
                        pltpu.VMEM((HEADS, DK, DV), F32)],
        compiler_params=_params(56, 2),
        name="ret_prompt",
    )(x, g.reshape(1, D), winb, cq, sq, ck, sk, jnp.asarray(dmask), qd, kd,
      gn_g.reshape(1, VW), gn_b.reshape(1, VW), woutb)


def _ret_proj_sample_body(x_ref, g_ref, win_ref, cq_ref, sq_ref, ck_ref, sk_ref,
                          q_ref, k_ref, v_ref, gate_ref):
    h = _rms(x_ref[...], g_ref[...]).astype(BF16)
    p = _dot(h, win_ref[...])
    for hd in range(HEADS):
        q_ref[:, hd * DK:(hd + 1) * DK] = _rope(p[:, hd * DK:(hd + 1) * DK], cq_ref[...], sq_ref[...])
        k_ref[:, hd * DK:(hd + 1) * DK] = _rope(p[:, QK + hd * DK:QK + (hd + 1) * DK], ck_ref[...], sk_ref[...])
    v_ref[...] = p[:, 2 * QK:2 * QK + VW]
    gate_ref[...] = p[:, 2 * QK + VW:]


def _ret_proj_sample(x2d, g, winb, *, pos0):
    n = x2d.shape[0]
    cos, sin = _rope_tables(np.array([pos0]))
    scale = DK ** -0.5
    tabs = [jnp.asarray(a, F32) for a in (cos, sin, cos * scale, sin * scale)]
    out = lambda w: pl.BlockSpec((n, w), lambda i: (0, 0))
    return pl.pallas_call(
        _ret_proj_sample_body,
        grid=(1,),
        in_specs=[_const((n, D)), _const((1, D)), _const((D, 2 * QK + 2 * VW))] + [_const((1, DK))] * 4,
        out_specs=[out(QK), out(QK), out(VW), out(VW)],
        out_shape=[jax.ShapeDtypeStruct((n, w), F32) for w in (QK, QK, VW, VW)],
        compiler_params=_params(40, 1),
        name="ret_proj_sample",
    )(x2d, g.reshape(1, D), winb, *tabs)


def _ret_state_sample_body(q_ref, k_ref, v_ref, s_ref, o_ref, s_out_ref, *, bt, gamma):
    pad = jnp.zeros((DK - bt, DK), F32)
    for hd in range(HEADS):
        qh = q_ref[:, hd * DK:(hd + 1) * DK]
        kh = k_ref[:, hd * DK:(hd + 1) * DK]
        qt = jnp.concatenate([qh, pad], axis=0).T
        kt = jnp.concatenate([kh, pad], axis=0).T
        qk = jnp.sum(qh * kh, axis=-1, keepdims=True)
        for b in range(bt):
            v = v_ref[b:b + 1, hd * DV:(hd + 1) * DV]
            s_old = s_ref[b, hd]
            qs = jnp.sum(s_old * (qt[:, b:b + 1] * gamma[hd]), axis=0, keepdims=True)
            o_ref[b:b + 1, hd * DV:(hd + 1) * DV] = qk[b:b + 1, :] * v + qs
            s_out_ref[b, hd] = s_old * gamma[hd] + kt[:, b:b + 1] * v


def _ret_state_sample(q, k, v, s, *, bt=8):
    n = q.shape[0]
    _, _, _, chunk_dec = _ret_constants(1)
    row = lambda w: pl.BlockSpec((bt, w), lambda i: (i, 0))
    sblk = pl.BlockSpec((bt, HEADS, DK, DV), lambda i: (i, 0, 0, 0))
    return pl.pallas_call(
        functools.partial(_ret_state_sample_body, bt=bt, gamma=tuple(float(c) for c in chunk_dec)),
        grid=(n // bt,),
        in_specs=[row(QK), row(QK), row(VW), sblk],
        out_specs=[row(VW), sblk],
        out_shape=[jax.ShapeDtypeStruct((n, VW), F32), jax.ShapeDtypeStruct(s.shape, F32)],
        compiler_params=_params(48, 1),
        name="ret_state_sample",
    )(q, k, v, s)


def _ret_out_sample_body(x_ref, o_ref_in, gate_ref, gng_ref, gnb_ref, wout_ref, o_ref):
    parts = []
    for hd in range(HEADS):
        cols = slice(hd * DV, (hd + 1) * DV)
        parts.append(_group_norm_gate(o_ref_in[:, cols], gate_ref[:, cols], gng_ref[:, cols], gnb_ref[:, cols]))
    gated = jnp.concatenate(parts, axis=-1).astype(BF16)
    o_ref[...] = x_ref[...] + _dot(gated, wout_ref[...])


def _ret_out_sample(x2d, o, gate, gn_g, gn_b, woutb):
    n = x2d.shape[0]
    return pl.pallas_call(
        _ret_out_sample_body,
        grid=(1,),
        in_specs=[_const((n, D)), _const((n, VW)), _const((n, VW)), _const((1, VW)), _const((1, VW)),
                  _const((VW, D))],
        out_specs=pl.BlockSpec((n, D), lambda i: (0, 0)),
        out_shape=jax.ShapeDtypeStruct((n, D), F32),
        compiler_params=_params(32, 1),
        name="ret_out_sample",
    )(x2d, o, gate, gn_g.reshape(1, VW), gn_b.reshape(1, VW), woutb)


LRU_HALO = 8


def _lru_gates(xc, wax_ref, ba_ref, bx_ref, lam_ref):
    xcb = xc.astype(BF16)
    rs, is_ = [], []
    for hd in range(HEADS):
        ri = _dot(xcb[:, hd * DK:(hd + 1) * DK], wax_ref[hd])
        rs.append(ri[:, :DK])
        is_.append(ri[:, DK:])
    r = jax.nn.sigmoid(jnp.concatenate(rs, axis=-1) + ba_ref[...])
    i = jax.nn.sigmoid(jnp.concatenate(is_, axis=-1) + bx_ref[...])
    log_a = -LRU_C * r * _softplus(-lam_ref[...])
    a = jnp.exp(log_a)
    mult = jnp.sqrt(-jnp.tanh(log_a) * (a * a + 1.0))
    return a, mult, i


def _lru_prompt_body(x_ref, g_ref, win_ref, cw_ref, cb_ref, wax_ref, ba_ref, bx_ref, lam_ref, wout_ref,
                     o_ref, conv_ref, hlast_ref, cext, a_scr, b_scr, hs_scr, h_scr, *, NB, T):
    t = pl.program_id(0)
    N = NB * T

    @pl.when(t == 0)
    def _():
        cext[:, 0:LRU_HALO, :] = jnp.zeros((NB, LRU_HALO, D), F32)
        h_scr[...] = jnp.zeros_like(h_scr)

    x = x_ref[...].reshape(N, D)
    h = _rms(x, g_ref[...]).astype(BF16)
    z = _dot(h, win_ref[...])
    gate = jax.nn.gelu(z[:, :D])
    cext[:, LRU_HALO:LRU_HALO + T, :] = z[:, D:].reshape(NB, T, D)
    cw = cw_ref[...]
    acc = cext[:, LRU_HALO - 3:LRU_HALO - 3 + T, :] * cw[0:1, :]
    for j in range(1, CONV_W):
        acc = acc + cext[:, LRU_HALO - 3 + j:LRU_HALO - 3 + j + T, :] * cw[j:j + 1, :]
    xc3 = cb_ref[...] + acc
    tail = cext[:, T + LRU_HALO - 3:T + LRU_HALO, :]
    cext[:, LRU_HALO - 3:LRU_HALO, :] = tail
    conv_ref[...] = tail

    xc = xc3.reshape(N, D)
    a, mult, i = _lru_gates(xc, wax_ref, ba_ref, bx_ref, lam_ref)
    pos = t * T + lax.broadcasted_iota(jnp.int32, (NB, T, 1), 1)
    mult3 = jnp.where(pos == 0, 1.0, mult.reshape(NB, T, D))
    bvec = (mult3 * (i * xc).reshape(NB, T, D)).reshape(N, D)
    for j in range(D // LANES):
        a_scr[j] = a[:, j * LANES:(j + 1) * LANES]
        b_scr[j] = bvec[:, j * LANES:(j + 1) * LANES]

    def step(s, hprev):
        rows = pl.ds(s, NB, stride=T)
        hnew = []
        for j in range(D // LANES):
            hj = a_scr[j, rows, :] * hprev[j] + b_scr[j, rows, :]
            hs_scr[j, rows, :] = hj
            hnew.append(hj)
        return tuple(hnew)

    h0 = tuple(h_scr[:, j * LANES:(j + 1) * LANES] for j in range(D // LANES))
    hfin = jnp.concatenate(lax.fori_loop(0, T, step, h0, unroll=8), axis=-1)
    h_scr[...] = hfin
    hlast_ref[...] = hfin
    hs = jnp.concatenate([hs_scr[j] for j in range(D // LANES)], axis=-1)
    y = _dot((hs * gate).astype(BF16), wout_ref[...])
    o_ref[...] = (x + y).reshape(NB, T, D)


def _lru_prompt(x, g, winb, conv_w, conv_b, waxb, b_a, b_x, lam, woutb, *, T=64):
    B, L, _ = x.shape
    blk = pl.BlockSpec((B, T, D), lambda t: (0, t, 0))
    vec = _const((1, D))
    return pl.pallas_call(
        functools.partial(_lru_prompt_body, NB=B, T=T),
        grid=(L // T,),
        in_specs=[blk, vec, _const((D, 2 * D)), _const((CONV_W, D)), vec, _const((HEADS, DK, 2 * DK)),
                  vec, vec, vec, _const((D, D))],
        out_specs=[blk, pl.BlockSpec((B, CONV_W - 1, D), lambda t: (0, 0, 0)),
                   pl.BlockSpec((B, D), lambda t: (0, 0))],
        out_shape=[jax.ShapeDtypeStruct((B, L, D), F32), jax.ShapeDtypeStruct((B, CONV_W - 1, D), F32),
                   jax.ShapeDtypeStruct((B, D), F32)],
        scratch_shapes=[pltpu.VMEM((B, T + LRU_HALO, D), F32)]
        + [pltpu.VMEM((D // LANES, B * T, LANES), F32)] * 3 + [pltpu.VMEM((B, D), F32)],
        compiler_params=_params(48, 1),
        name="lru_prompt",
    )(x, g.reshape(1, D), winb, conv_w, conv_b.reshape(1, D), waxb, b_a.reshape(1, D), b_x.reshape(1, D),
      lam.reshape(1, D), woutb)


def _lru_sample_body(x_ref, cbuf_ref, h0_ref, g_ref, win_ref, cw_ref, cb_ref, wax_ref, ba_ref, bx_ref,
                     lam_ref, wout_ref, o_ref, nconv_ref, hnew_ref, *, pos0):
    x = x_ref[...]
    h = _rms(x, g_ref[...]).astype(BF16)
    z = _dot(h, win_ref[...])
    gate = jax.nn.gelu(z[:, :D])
    xb = z[:, D:]
    cw = cw_ref[...]
    acc = cbuf_ref[:, 0:D] * cw[0:1, :]
    for j in range(1, CONV_W - 1):
        acc = acc + cbuf_ref[:, j * D:(j + 1) * D] * cw[j:j + 1, :]
    acc = acc + xb * cw[CONV_W - 1:CONV_W, :]
    xc = cb_ref[...] + acc
    a, mult, i = _lru_gates(xc, wax_ref, ba_ref, bx_ref, lam_ref)
    if pos0 == 0:
        mult = jnp.ones_like(mult)
    hnew = a * h0_ref[...] + mult * (i * xc)
    hnew_ref[...] = hnew
    nconv_ref[:, 0:(CONV_W - 2) * D] = cbuf_ref[:, D:(CONV_W - 1) * D]
    nconv_ref[:, (CONV_W - 2) * D:] = xb
    o_ref[...] = x + _dot((hnew * gate).astype(BF16), wout_ref[...])


def _lru_sample(x2d, cbuf2d, h0, g, winb, conv_w, conv_b, waxb, b_a, b_x, lam, woutb, *, pos0):
    n = x2d.shape[0]
    vec = _const((1, D))
    cw = (CONV_W - 1) * D
    out = lambda w: pl.BlockSpec((n, w), lambda i: (0, 0))
    return pl.pallas_call(
        functools.partial(_lru_sample_body, pos0=pos0),
        grid=(1,),
        in_specs=[_const((n, D)), _const((n, cw)), _const((n, D)), vec, _const((D, 2 * D)), _const((CONV_W, D)),
                  vec, _const((HEADS, DK, 2 * DK)), vec, vec, vec, _const((D, D))],
        out_specs=[out(D), out(cw), out(D)],
        out_shape=[jax.ShapeDtypeStruct((n, D), F32), jax.ShapeDtypeStruct((n, cw), F32),
                   jax.ShapeDtypeStruct((n, D), F32)],
        compiler_params=_params(32, 1),
        name="lru_sample",
    )(x2d, cbuf2d, h0, g.reshape(1, D), winb, conv_w, conv_b.reshape(1, D), waxb, b_a.reshape(1, D),
      b_x.reshape(1, D), lam.reshape(1, D), woutb)


def kernel(x_prompt, x_sample, state_pool, state_ret, state_conv, state_lru, pool_norm, pool_w, pool_scale, gm_norm, gm_w_in, gm_b_in, gm_ln_g, gm_ln_b, gm_w_s, gm_b_s, gm_w_out, ret_norm, ret_w_in, ret_gn_g, ret_gn_b, ret_w_out, lru_norm, lru_w_in, lru_conv_w, lru_conv_b, lru_w_a, lru_b_a, lru_w_x, lru_b_x, lru_lam, lru_w_out, mlp_norm, mlp_w1, mlp_w2, final_norm):
    B, L, _ = x_prompt.shape
    NS = x_sample.shape[0]
    bf = lambda w: w.astype(BF16)
    w1b, w2b = bf(mlp_w1), bf(mlp_w2)

    def mlp_p(x, li, final=False):
        return _mlp(x.reshape(B * L, D), mlp_norm[li], w1b[li], w2b[li], final_norm, final=final,
                    tm=512).reshape(B, L, D)

    def mlp_s(x, li, final=False):
        return _mlp(x, mlp_norm[li], w1b[li], w2b[li], final_norm, final=final, tm=NS)

    pool_wb = bf(pool_w[0])
    xp, pool_p = _pool_prompt(x_prompt, pool_norm[0], pool_wb, pool_scale[0])
    xs, pool_s = _pool_sample(x_sample.reshape(NS, D), state_pool[0].reshape(NS, POOL_BUF * D),
                              pool_norm[0], pool_wb, pool_scale[0], pos0=PAST_LEN)
    xp, xs = mlp_p(xp, 0), mlp_s(xs, 0)

    gm_winb, gm_woutb = bf(gm_w_in[0]), bf(gm_w_out[0])
    bs_full = jnp.repeat(gm_b_s[0].T, CHUNK, axis=1)
    xp = _gmlp_prompt(xp, gm_norm[0], gm_winb, gm_b_in[0], gm_ln_g[0], gm_ln_b[0], gm_w_s[0], bs_full, gm_woutb)
    sw = jnp.repeat(gm_w_s[0][:, 0, 0], CHUNK).reshape(1, D)
    sb = jnp.repeat(gm_b_s[0][:, 0], CHUNK).reshape(1, D)
    xs, v_s = _gmlp_sample(xs, gm_norm[0], gm_winb, gm_b_in[0], gm_ln_g[0], gm_ln_b[0], sw, sb, gm_woutb)
    xp, xs = mlp_p(xp, 1), mlp_s(xs, 1)

    ret_winb, ret_woutb = bf(ret_w_in[0]), bf(ret_w_out[0])
    xp, ret_p = _ret_prompt(xp, ret_norm[0], ret_winb, ret_gn_g[0], ret_gn_b[0], ret_woutb)
    q, k, v, gate = _ret_proj_sample(xs, ret_norm[0], ret_winb, pos0=PAST_LEN)
    o, ret_s = _ret_state_sample(q, k, v, state_ret[0])
    xs = _ret_out_sample(xs, o, gate, ret_gn_g[0], ret_gn_b[0], ret_woutb)
    xp, xs = mlp_p(xp, 2), mlp_s(xs, 2)

    lru_winb, lru_woutb = bf(lru_w_in[0]), bf(lru_w_out[0])
    waxb = bf(jnp.concatenate([lru_w_a[0], lru_w_x[0]], axis=-1))
    xp, conv_p, lru_p = _lru_prompt(xp, lru_norm[0], lru_winb, lru_conv_w[0], lru_conv_b[0], waxb,
                                    lru_b_a[0], lru_b_x[0], lru_lam[0], lru_woutb)
    xs, conv_s, lru_s = _lru_sample(xs, state_conv[0].reshape(NS, (CONV_W - 1) * D), state_lru[0],
                                    lru_norm[0], lru_winb, lru_conv_w[0], lru_conv_b[0], waxb,
                                    lru_b_a[0], lru_b_x[0], lru_lam[0], lru_woutb, pos0=PAST_LEN)
    yp, ys = mlp_p(xp, 3, final=True), mlp_s(xs, 3, final=True)

    return (yp, ys.reshape(NS, 1, D),
            pool_p[None], pool_s.reshape(1, NS, POOL_BUF, D),
            v_s.reshape(1, NS, 1, D),
            ret_p[None], ret_s[None],
            conv_p[None], conv_s.reshape(1, NS, CONV_W - 1, D),
            lru_p[None], lru_s[None])
```

```python
import functools
import math

import jax
import jax.numpy as jnp
import numpy as np
from jax import lax
from jax.experimental import pallas as pl
from jax.experimental.pallas import tpu as pltpu

F32 = jnp.float32
BF16 = jnp.bfloat16

D = 1024
EPS = 1e-6
GN_EPS = 1e-5
PAST_LEN = 16384
POOL_WINDOWS = (2, 4, 8, 16)
POOL_GROUP = D // len(POOL_WINDOWS)
POOL_BUF = max(POOL_WINDOWS) - 1
CHUNK = 128
GM_GROUPS = 8
HEADS = 8
DK = D // HEADS
DV = 2 * D // HEADS
QK = HEADS * DK
VW = HEADS * DV
ROPE_BASE = 10000.0
CONV_W = 4
LRU_C = 8.0
D_FF = 4 * D

MIB = 1024 * 1024
SUBLANES = 8
LANES = 128


def _params(vmem_mib, n_grid):
    return pltpu.CompilerParams(
        dimension_semantics=("arbitrary",) * n_grid,
        vmem_limit_bytes=vmem_mib * MIB,
    )


def _const(shape):
    zeros = (0,) * len(shape)
    return pl.BlockSpec(shape, lambda *_: zeros, pipeline_mode=pl.Buffered(1))


def _rms(x, g):
    ms = jnp.mean(x * x, axis=-1, keepdims=True)
    return x * lax.rsqrt(ms + EPS) * g


def _dot(a, b):
    return jnp.dot(a, b, preferred_element_type=F32)


def _softplus(x):
    return jnp.maximum(x, 0.0) + jnp.log1p(jnp.exp(-jnp.abs(x)))


MLP_FC = 512


def _mlp_body(x_ref, g_ref, w1_ref, w2_ref, fg_ref, o_ref, *, final):
    x = x_ref[...]
    h = _rms(x, g_ref[...]).astype(BF16)
    acc = x
    for c in range(D_FF // MLP_FC):
        lo, hi = c * MLP_FC, (c + 1) * MLP_FC
        a = _dot(h, w1_ref[:, lo:hi])
        a = jnp.square(jnp.maximum(a, 0.0)).astype(BF16)
        acc = acc + _dot(a, w2_ref[lo:hi, :])
    if final:
        acc = _rms(acc, fg_ref[...])
    o_ref[...] = acc


def _mlp(x2d, g, w1b, w2b, fg, *, final, tm):
    n = x2d.shape[0]
    row = pl.BlockSpec((tm, D), lambda i: (i, 0))
    return pl.pallas_call(
        functools.partial(_mlp_body, final=final),
        grid=(n // tm,),
        in_specs=[row, _const((1, D)), _const((D, D_FF)), _const((D_FF, D)), _const((1, D))],
        out_specs=row,
        out_shape=jax.ShapeDtypeStruct((n, D), F32),
        compiler_params=_params(48, 1),
        name="mlp",
    )(x2d, g.reshape(1, D), w1b, w2b, fg.reshape(1, D))


POOL_HALO = 16


def _pool_prompt_body(x_ref, g_ref, w_ref, sc_ref, o_ref, buf_ref, ext_ref, *, T):
    t = pl.program_id(1)

    @pl.when(t == 0)
    def _():
        ext_ref[0:POOL_HALO, :] = jnp.zeros((POOL_HALO, D), F32)

    x = x_ref[...]
    h = _rms(x, g_ref[...])
    ext_ref[POOL_HALO:POOL_HALO + T, :] = h
    pos = t * T + lax.broadcasted_iota(jnp.int32, (T, 1), 0)
    ys = []
    for gi, w in enumerate(POOL_WINDOWS):
        c0, c1 = gi * POOL_GROUP, (gi + 1) * POOL_GROUP
        hg = h[:, c0:c1]
        s = hg
        for k in range(1, w):
            s = s + ext_ref[POOL_HALO - k:POOL_HALO - k + T, c0:c1]
        cnt = jnp.minimum(pos + 1, w).astype(F32)
        d = s / cnt - hg
        ys.append(_dot(d.astype(BF16), w_ref[gi]))
    y = jnp.concatenate(ys, axis=-1) * sc_ref[...]
    o_ref[...] = x + y
    tail = ext_ref[T:T + POOL_HALO, :]
    ext_ref[0:POOL_HALO, :] = tail
    buf_ref[...] = tail[POOL_HALO - POOL_BUF:, :]


def _pool_prompt(x, g, wb, sc, *, T=512):
    B, L, _ = x.shape
    blk = pl.BlockSpec((None, T, D), lambda b, t: (b, t, 0))
    return pl.pallas_call(
        functools.partial(_pool_prompt_body, T=T),
        grid=(B, L // T),
        in_specs=[blk, _const((1, D)), _const((len(POOL_WINDOWS), POOL_GROUP, POOL_GROUP)), _const((1, D))],
        out_specs=[blk, pl.BlockSpec((None, POOL_BUF, D), lambda b, t: (b, 0, 0))],
        out_shape=[jax.ShapeDtypeStruct((B, L, D), F32), jax.ShapeDtypeStruct((B, POOL_BUF, D), F32)],
        scratch_shapes=[pltpu.VMEM((T + POOL_HALO, D), F32)],
        compiler_params=_params(32, 2),
        name="pool_prompt",
    )(x, g.reshape(1, D), wb, sc.reshape(1, D))


def _pool_sample_body(x_ref, buf_ref, g_ref, w_ref, sc_ref, o_ref, nbuf_ref, *, pos0):
    x = x_ref[...]
    h = _rms(x, g_ref[...])
    ys = []
    for gi, w in enumerate(POOL_WINDOWS):
        c0, c1 = gi * POOL_GROUP, (gi + 1) * POOL_GROUP
        hg = h[:, c0:c1]
        s = hg
        for k in range(1, w):
            r = (POOL_BUF - k) * D
            s = s + buf_ref[:, r + c0:r + c1]
        cnt = float(min(pos0 + 1, w))
        d = s / cnt - hg
        ys.append(_dot(d.astype(BF16), w_ref[gi]))
    y = jnp.concatenate(ys, axis=-1) * sc_ref[...]
    o_ref[...] = x + y
    nbuf_ref[:, 0:(POOL_BUF - 1) * D] = buf_ref[:, D:POOL_BUF * D]
    nbuf_ref[:, (POOL_BUF - 1) * D:] = h


def _pool_sample(x2d, buf2d, g, wb, sc, *, pos0, bt=32):
    n = x2d.shape[0]
    row = pl.BlockSpec((bt, D), lambda i: (i, 0))
    brow = pl.BlockSpec((bt, POOL_BUF * D), lambda i: (i, 0))
    return pl.pallas_call(
        functools.partial(_pool_sample_body, pos0=pos0),
        grid=(n // bt,),
        in_specs=[row, brow, _const((1, D)), _const((len(POOL_WINDOWS), POOL_GROUP, POOL_GROUP)), _const((1, D))],
        out_specs=[row, brow],
        out_shape=[jax.ShapeDtypeStruct((n, D), F32), jax.ShapeDtypeStruct((n, POOL_BUF * D), F32)],
        compiler_params=_params(32, 1),
        name="pool_sample",
    )(x2d, buf2d, g.reshape(1, D), wb, sc.reshape(1, D))


def _gmlp_front(x, g_ref, win_ref, bin_ref, lng_ref, lnb_ref):
    h = _rms(x, g_ref[...]).astype(BF16)
    z = jax.nn.gelu(_dot(h, win_ref[...]) + bin_ref[...])
    u, v = z[:, :D], z[:, D:]
    mu = jnp.mean(v, axis=-1, keepdims=True)
    vc = v - mu
    var = jnp.mean(jnp.square(vc), axis=-1, keepdims=True)
    vn = vc * lax.rsqrt(var + EPS) * lng_ref[...] + lnb_ref[...]
    return u, vn


def _gmlp_prompt_body(x_ref, g_ref, win_ref, bin_ref, lng_ref, lnb_ref, ws_ref, bs_ref, wout_ref,
                      o_ref, *, T):
    x = x_ref[...]
    u, vn = _gmlp_front(x, g_ref, win_ref, bin_ref, lng_ref, lnb_ref)
    vnb = vn.astype(BF16)
    ti = lax.broadcasted_iota(jnp.int32, (CHUNK, CHUNK), 0)
    si = lax.broadcasted_iota(jnp.int32, (CHUNK, CHUNK), 1)
    causal = ti >= si
    wsm = [jnp.where(causal, ws_ref[gi], 0.0).astype(BF16) for gi in range(GM_GROUPS)]
    rows = []
    for n in range(T // CHUNK):
        r0, r1 = n * CHUNK, (n + 1) * CHUNK
        cols = [_dot(wsm[gi], vnb[r0:r1, gi * CHUNK:(gi + 1) * CHUNK]) for gi in range(GM_GROUPS)]
        rows.append(jnp.concatenate(cols, axis=-1) + bs_ref[...])
    mixed = jnp.concatenate(rows, axis=0)
    y = _dot((u * mixed).astype(BF16), wout_ref[...])
    o_ref[...] = x + y


def _gmlp_prompt(x, g, winb, b_in, ln_g, ln_b, w_s, bs_full, woutb, *, T=512):
    B, L, _ = x.shape
    blk = pl.BlockSpec((None, T, D), lambda b, t: (b, t, 0))
    return pl.pallas_call(
        functools.partial(_gmlp_prompt_body, T=T),
        grid=(B, L // T),
        in_specs=[blk, _const((1, D)), _const((D, 2 * D)), _const((1, 2 * D)), _const((1, D)), _const((1, D)),
                  _const((GM_GROUPS, CHUNK, CHUNK)), _const((CHUNK, D)), _const((D, D))],
        out_specs=blk,
        out_shape=jax.ShapeDtypeStruct((B, L, D), F32),
        compiler_params=_params(48, 2),
        name="gmlp_prompt",
    )(x, g.reshape(1, D), winb, b_in.reshape(1, 2 * D), ln_g.reshape(1, D), ln_b.reshape(1, D),
      w_s, bs_full, woutb)


def _gmlp_sample_body(x_ref, g_ref, win_ref, bin_ref, lng_ref, lnb_ref, sw_ref, sb_ref, wout_ref,
                      o_ref, vn_ref):
    x = x_ref[...]
    u, vn = _gmlp_front(x, g_ref, win_ref, bin_ref, lng_ref, lnb_ref)
    mixed = vn * sw_ref[...] + sb_ref[...]
    y = _dot((u * mixed).astype(BF16), wout_ref[...])
    o_ref[...] = x + y
    vn_ref[...] = vn


def _gmlp_sample(x2d, g, winb, b_in, ln_g, ln_b, sw, sb, woutb):
    n = x2d.shape[0]
    full = _const((n, D))
    return pl.pallas_call(
        _gmlp_sample_body,
        grid=(1,),
        in_specs=[full, _const((1, D)), _const((D, 2 * D)), _const((1, 2 * D)), _const((1, D)), _const((1, D)),
                  _const((1, D)), _const((1, D)), _const((D, D))],
        out_specs=[pl.BlockSpec((n, D), lambda i: (0, 0)), pl.BlockSpec((n, D), lambda i: (0, 0))],
        out_shape=[jax.ShapeDtypeStruct((n, D), F32), jax.ShapeDtypeStruct((n, D), F32)],
        compiler_params=_params(32, 1),
        name="gmlp_sample",
    )(x2d, g.reshape(1, D), winb, b_in.reshape(1, 2 * D), ln_g.reshape(1, D), ln_b.reshape(1, D),
      sw, sb, woutb)


def _ret_constants(C):
    log_gamma = np.log1p(-np.exp2(-5.0 - np.arange(HEADS, dtype=np.float64)))
    idx = np.arange(C, dtype=np.float64)
    diff = idx[:, None] - idx[None, :]
    dmask = np.where(diff[None] >= 0, np.exp(log_gamma[:, None, None] * np.maximum(diff, 0.0)[None]), 0.0)
    q_dec = np.exp(log_gamma[:, None] * (idx + 1.0))
    k_dec = np.exp(log_gamma[:, None] * (C - 1.0 - idx))
    chunk_dec = np.exp(log_gamma * C)
    return dmask.astype(np.float32), q_dec.astype(np.float32), k_dec.astype(np.float32), chunk_dec


def _rope_tables(pos):
    half = DK // 2
    freqs = np.exp(-math.log(ROPE_BASE) * np.arange(half, dtype=np.float64) / half)
    ang = np.asarray(pos, dtype=np.float64)[:, None] * freqs[None]
    cos = np.concatenate([np.cos(ang), np.cos(ang)], axis=-1)
    sin = np.concatenate([-np.sin(ang), np.sin(ang)], axis=-1)
    return cos, sin


def _rope(t, cos, sin):
    return t * cos + pltpu.roll(t, DK // 2, axis=1) * sin


def _group_norm_gate(o, gate, gng, gnb):
    mu = jnp.mean(o, axis=-1, keepdims=True)
    oc = o - mu
    var = jnp.mean(jnp.square(oc), axis=-1, keepdims=True)
    on = oc * lax.rsqrt(var + GN_EPS) * gng + gnb
    return jax.nn.silu(gate) * on


def _ret_prompt_body(x_ref, g_ref, win_ref, cq_ref, sq_ref, ck_ref, sk_ref, dm_ref, qd_ref, kd_ref,
                     gng_ref, gnb_ref, wout_ref, o_ref, s_out_ref, gated_scr, s_scr,
                     *, T, chunk_dec):
    t = pl.program_id(1)

    @pl.when(t == 0)
    def _():
        s_scr[...] = jnp.zeros_like(s_scr)

    x = x_ref[...]
    h = _rms(x, g_ref[...]).astype(BF16)
    cq, sq, ck, sk = cq_ref[...], sq_ref[...], ck_ref[...], sk_ref[...]
    for pair in range(HEADS // 2):
        q2 = _dot(h, win_ref[:, 2 * DK * pair:2 * DK * (pair + 1)])
        k2 = _dot(h, win_ref[:, QK + 2 * DK * pair:QK + 2 * DK * (pair + 1)])
        v2 = _dot(h, win_ref[:, 2 * QK + 2 * DV * pair:2 * QK + 2 * DV * (pair + 1)])
        g2 = _dot(h, win_ref[:, 2 * QK + VW + 2 * DV * pair:2 * QK + VW + 2 * DV * (pair + 1)])
        for sub in range(2):
            hd = 2 * pair + sub
            q = _rope(q2[:, sub * DK:(sub + 1) * DK], cq, sq)
            k = _rope(k2[:, sub * DK:(sub + 1) * DK], ck, sk)
            qb, kb = q.astype(BF16), k.astype(BF16)
            gng = gng_ref[:, hd * DV:(hd + 1) * DV]
            gnb = gnb_ref[:, hd * DV:(hd + 1) * DV]
            s = s_scr[hd]
            for c in range(T // CHUNK):
                rows = slice(c * CHUNK, (c + 1) * CHUNK)
                vb = v2[rows, sub * DV:(sub + 1) * DV].astype(BF16)
                scores = lax.dot_general(qb[rows], kb[rows], (((1,), (1,)), ((), ())),
                                         preferred_element_type=F32) * dm_ref[hd]
                o = _dot(scores.astype(BF16), vb) + _dot((q[rows] * qd_ref[hd]).astype(BF16), s.astype(BF16))
                kt = (k[rows] * kd_ref[hd]).T.astype(BF16)
                s = s * chunk_dec[hd] + _dot(kt, vb)
                gated = _group_norm_gate(o, g2[rows, sub * DV:(sub + 1) * DV], gng, gnb)
                gated_scr[rows, hd * DV:(hd + 1) * DV] = gated.astype(BF16)
            s_scr[hd] = s
    y = _dot(gated_scr[...], wout_ref[...])
    o_ref[...] = x + y
    s_out_ref[...] = s_scr[...]


def _ret_prompt(x, g, winb, gn_g, gn_b, woutb, *, T=512):
    B, L, _ = x.shape
    dmask, q_dec, k_dec, chunk_dec = _ret_constants(CHUNK)
    cos, sin = _rope_tables(np.arange(L))
    scale = DK ** -0.5
    cq, sq = jnp.asarray(cos, F32), jnp.asarray(sin, F32)
    ck, sk = jnp.asarray(cos * scale, F32), jnp.asarray(sin * scale, F32)
    qd = jnp.asarray(np.broadcast_to(q_dec[:, :, None], (HEADS, CHUNK, DK)))
    kd = jnp.asarray(np.broadcast_to(k_dec[:, :, None], (HEADS, CHUNK, DK)))
    blk = pl.BlockSpec((None, T, D), lambda b, t: (b, t, 0))
    tab = pl.BlockSpec((T, DK), lambda b, t: (t, 0))
    hcc = _const((HEADS, CHUNK, CHUNK))
    return pl.pallas_call(
        functools.partial(_ret_prompt_body, T=T, chunk_dec=tuple(float(c) for c in chunk_dec)),
        grid=(B, L // T),
        in_specs=[blk, _const((1, D)), _const((D, 2 * QK + 2 * VW)), tab, tab, tab, tab,
                  hcc, hcc, hcc, _const((1, VW)), _const((1, VW)), _const((VW, D))],
        out_specs=[blk, pl.BlockSpec((None, HEADS, DK, DV), lambda b, t: (b, 0, 0, 0))],
        out_shape=[jax.ShapeDtypeStruct((B, L, D), F32), jax.ShapeDtypeStruct((B, HEADS, DK, DV), F32)],
        scratch_shapes=[pltpu.VMEM((T, VW), BF16), pltpu.VMEM((HEADS, DK, DV), F32)],
        compiler_params=_params(56, 2),
        name="ret_prompt",
    )(x, g.reshape(1, D), winb, cq, sq, ck, sk, jnp.asarray(dmask), qd, kd,
      gn_g.reshape(1, VW), gn_b.reshape(1, VW), woutb)


def _ret_proj_sample_body(x_ref, g_ref, win_ref, cq_ref, sq_ref, ck_ref, sk_ref,
                          q_ref, k_ref, v_ref, gate_ref):
    h = _rms(x_ref[...], g_ref[...]).astype(BF16)
    p = _dot(h, win_ref[...])
    for hd in range(HEADS):
        q_ref[:, hd * DK:(hd + 1) * DK] = _rope(p[:, hd * DK:(hd + 1) * DK], cq_ref[...], sq_ref[...])
        k_ref[:, hd * DK:(hd + 1) * DK] = _rope(p[:, QK + hd * DK:QK + (hd + 1) * DK], ck_ref[...], sk_ref[...])
    v_ref[...] = p[:, 2 * QK:2 * QK + VW]
    gate_ref[...] = p[:, 2 * QK + VW:]


def _ret_proj_sample(x2d, g, winb, *, pos0):
    n = x2d.shape[0]
    cos, sin = _rope_tables(np.array([pos0]))
    scale = DK ** -0.5
    tabs = [jnp.asarray(a, F32) for a in (cos, sin, cos * scale, sin * scale)]
    out = lambda w: pl.BlockSpec((n, w), lambda i: (0, 0))
    return pl.pallas_call(
        _ret_proj_sample_body,
        grid=(1,),
        in_specs=[_const((n, D)), _const((1, D)), _const((D, 2 * QK + 2 * VW))] + [_const((1, DK))] * 4,
        out_specs=[out(QK), out(QK), out(VW), out(VW)],
        out_shape=[jax.ShapeDtypeStruct((n, w), F32) for w in (QK, QK, VW, VW)],
        compiler_params=_params(40, 1),
        name="ret_proj_sample",
    )(x2d, g.reshape(1, D), winb, *tabs)


def _ret_state_sample_body(q_ref, k_ref, v_ref, s_ref, o_ref, s_out_ref, *, bt, gamma):
    pad = jnp.zeros((DK - bt, DK), F32)
    for hd in range(HEADS):
        qh = q_ref[:, hd * DK:(hd + 1) * DK]
        kh = k_ref[:, hd * DK:(hd + 1) * DK]
        qt = jnp.concatenate([qh, pad], axis=0).T
        kt = jnp.concatenate([kh, pad], axis=0).T
        qk = jnp.sum(qh * kh, axis=-1, keepdims=True)
        for b in range(bt):
            v = v_ref[b:b + 1, hd * DV:(hd + 1) * DV]
            s_old = s_ref[b, hd]
            qs = jnp.sum(s_old * (qt[:, b:b + 1] * gamma[hd]), axis=0, keepdims=True)
            o_ref[b:b + 1, hd * DV:(hd + 1) * DV] = qk[b:b + 1, :] * v + qs
            s_out_ref[b, hd] = s_old * gamma[hd] + kt[:, b:b + 1] * v


def _ret_state_sample(q, k, v, s, *, bt=8):
    n = q.shape[0]
    _, _, _, chunk_dec = _ret_constants(1)
    row = lambda w: pl.BlockSpec((bt, w), lambda i: (i, 0))
    sblk = pl.BlockSpec((bt, HEADS, DK, DV), lambda i: (i, 0, 0, 0))
    return pl.pallas_call(
        functools.partial(_ret_state_sample_body, bt=bt, gamma=tuple(float(c) for c in chunk_dec)),
        grid=(n // bt,),
        in_specs=[row(QK), row(QK), row(VW), sblk],
        out_specs=[row(VW), sblk],
        out_shape=[jax.ShapeDtypeStruct((n, VW), F32), jax.ShapeDtypeStruct(s.shape, F32)],
        compiler_params=_params(48, 1),
        name="ret_state_sample",
    )(q, k, v, s)


def _ret_out_sample_body(x_ref, o_ref_in, gate_ref, gng_ref, gnb_ref, wout_ref, o_ref):
    parts = []
    for hd in range(HEADS):
        cols = slice(hd * DV, (hd + 1) * DV)
        parts.append(_group_norm_gate(o_ref_in[:, cols], gate_ref[:, cols], gng_ref[:, cols], gnb_ref[:, cols]))
    gated = jnp.concatenate(parts, axis=-1).astype(BF16)
    o_ref[...] = x_ref[...] + _dot(gated, wout_ref[...])


def _ret_out_sample(x2d, o, gate, gn_g, gn_b, woutb):
    n = x2d.shape[0]
    return pl.pallas_call(
        _ret_out_sample_body,
        grid=(1,),
        in_specs=[_const((n, D)), _const((n, VW)), _const((n, VW)), _const((1, VW)), _const((1, VW)),
                  _const((VW, D))],
        out_specs=pl.BlockSpec((n, D), lambda i: (0, 0)),
        out_shape=jax.ShapeDtypeStruct((n, D), F32),
        compiler_params=_params(32, 1),
        name="ret_out_sample",
    )(x2d, o, gate, gn_g.reshape(1, VW), gn_b.reshape(1, VW), woutb)


LRU_HALO = 8


def _lru_gates(xc, wax_ref, ba_ref, bx_ref, lam_ref):
    xcb = xc.astype(BF16)
    rs, is_ = [], []
    for hd in range(HEADS):
        ri = _dot(xcb[:, hd * DK:(hd + 1) * DK], wax_ref[hd])
        rs.append(ri[:, :DK])
        is_.append(ri[:, DK:])
    r = jax.nn.sigmoid(jnp.concatenate(rs, axis=-1) + ba_ref[...])
    i = jax.nn.sigmoid(jnp.concatenate(is_, axis=-1) + bx_ref[...])
    log_a = -LRU_C * r * _softplus(-lam_ref[...])
    a = jnp.exp(log_a)
    mult = jnp.sqrt(-jnp.tanh(log_a) * (a * a + 1.0))
    return a, mult, i


def _lru_prompt_body(x_ref, g_ref, win_ref, cw_ref, cb_ref, wax_ref, ba_ref, bx_ref, lam_ref, wout_ref,
                     o_ref, conv_ref, hlast_ref, xb_scr, hs_scr, carry_scr, h_scr, *, NB, T):
    t = pl.program_id(0)
    N = NB * T
    PS = T + 1
    nblk = D // LANES
    last = pl.num_programs(0) - 1

    @pl.when(t == 0)
    def _():
        carry_scr[...] = jnp.zeros_like(carry_scr)
        h_scr[...] = jnp.zeros_like(h_scr)

    x = x_ref[...].reshape(N, D)
    h = _rms(x, g_ref[...]).astype(BF16)
    z = _dot(h, win_ref[...])
    gate = jax.nn.gelu(z[:, :D])
    for j in range(nblk):
        for b in range(NB):
            xb_scr[j, b * PS:b * PS + T, :] = z[b * T:(b + 1) * T, D + j * LANES:D + (j + 1) * LANES]
    ext = jnp.concatenate(
        [carry_scr[...]]
        + [jnp.concatenate([xb_scr[j, pl.ds(s, NB, stride=PS), :] for j in range(nblk)], axis=-1)
           for s in range(T)], axis=0)
    halo = (CONV_W - 1) * NB
    carry_scr[...] = ext[N:N + halo, :]

    @pl.when(t == last)
    def _():
        for j in range(CONV_W - 1):
            conv_ref[:, j, :] = ext[N + j * NB:N + (j + 1) * NB, :]

    cw = cw_ref[...]
    acc = ext[0:N, :] * cw[0:1, :]
    for j in range(1, CONV_W):
        acc = acc + ext[j * NB:j * NB + N, :] * cw[j:j + 1, :]
    xc = cb_ref[...] + acc

    a, mult, i = _lru_gates(xc, wax_ref, ba_ref, bx_ref, lam_ref)
    row = lax.broadcasted_iota(jnp.int32, (N, 1), 0)
    pos = t * T + lax.shift_right_logical(row, NB.bit_length() - 1)
    bvec = jnp.where(pos == 0, 1.0, mult) * (i * xc)
    hcur = h_scr[...]
    for s in range(T):
        hcur = a[s * NB:(s + 1) * NB, :] * hcur + bvec[s * NB:(s + 1) * NB, :]
        for j in range(nblk):
            hs_scr[j, pl.ds(s, NB, stride=PS), :] = hcur[:, j * LANES:(j + 1) * LANES]
    h_scr[...] = hcur
    hlast_ref[...] = hcur
    hs = jnp.concatenate(
        [jnp.concatenate([hs_scr[j, b * PS:b * PS + T, :] for b in range(NB)], axis=0) for j in range(nblk)],
        axis=-1)
    y = _dot((hs * gate).astype(BF16), wout_ref[...])
    o_ref[...] = (x + y).reshape(NB, T, D)


def _lru_prompt(x, g, winb, conv_w, conv_b, waxb, b_a, b_x, lam, woutb, *, T=64):
    B, L, _ = x.shape
    blk = pl.BlockSpec((B, T, D), lambda t: (0, t, 0))
    vec = _const((1, D))
    return pl.pallas_call(
        functools.partial(_lru_prompt_body, NB=B, T=T),
        grid=(L // T,),
        in_specs=[blk, vec, _const((D, 2 * D)), _const((CONV_W, D)), vec, _const((HEADS, DK, 2 * DK)),
                  vec, vec, vec, _const((D, D))],
        out_specs=[blk, pl.BlockSpec((B, CONV_W - 1, D), lambda t: (0, 0, 0)),
                   pl.BlockSpec((B, D), lambda t: (0, 0))],
        out_shape=[jax.ShapeDtypeStruct((B, L, D), F32), jax.ShapeDtypeStruct((B, CONV_W - 1, D), F32),
                   jax.ShapeDtypeStruct((B, D), F32)],
        scratch_shapes=[pltpu.VMEM((D // LANES, B * (T + 1), LANES), F32)] * 2
        + [pltpu.VMEM(((CONV_W - 1) * B, D), F32), pltpu.VMEM((B, D), F32)],
        compiler_params=_params(48, 1),
        name="lru_prompt",
    )(x, g.reshape(1, D), winb, conv_w, conv_b.reshape(1, D), waxb, b_a.reshape(1, D), b_x.reshape(1, D),
      lam.reshape(1, D), woutb)


def _lru_sample_body(x_ref, cbuf_ref, h0_ref, g_ref, win_ref, cw_ref, cb_ref, wax_ref, ba_ref, bx_ref,
                     lam_ref, wout_ref, o_ref, nconv_ref, hnew_ref, *, pos0):
    x = x_ref[...]
    h = _rms(x, g_ref[...]).astype(BF16)
    z = _dot(h, win_ref[...])
    gate = jax.nn.gelu(z[:, :D])
    xb = z[:, D:]
    cw = cw_ref[...]
    acc = cbuf_ref[:, 0:D] * cw[0:1, :]
    for j in range(1, CONV_W - 1):
        acc = acc + cbuf_ref[:, j * D:(j + 1) * D] * cw[j:j + 1, :]
    acc = acc + xb * cw[CONV_W - 1:CONV_W, :]
    xc = cb_ref[...] + acc
    a, mult, i = _lru_gates(xc, wax_ref, ba_ref, bx_ref, lam_ref)
    if pos0 == 0:
        mult = jnp.ones_like(mult)
    hnew = a * h0_ref[...] + mult * (i * xc)
    hnew_ref[...] = hnew
    nconv_ref[:, 0:(CONV_W - 2) * D] = cbuf_ref[:, D:(CONV_W - 1) * D]
    nconv_ref[:, (CONV_W - 2) * D:] = xb
    o_ref[...] = x + _dot((hnew * gate).astype(BF16), wout_ref[...])


def _lru_sample(x2d, cbuf2d, h0, g, winb, conv_w, conv_b, waxb, b_a, b_x, lam, woutb, *, pos0):
    n = x2d.shape[0]
    vec = _const((1, D))
    cw = (CONV_W - 1) * D
    out = lambda w: pl.BlockSpec((n, w), lambda i: (0, 0))
    return pl.pallas_call(
        functools.partial(_lru_sample_body, pos0=pos0),
        grid=(1,),
        in_specs=[_const((n, D)), _const((n, cw)), _const((n, D)), vec, _const((D, 2 * D)), _const((CONV_W, D)),
                  vec, _const((HEADS, DK, 2 * DK)), vec, vec, vec, _const((D, D))],
        out_specs=[out(D), out(cw), out(D)],
        out_shape=[jax.ShapeDtypeStruct((n, D), F32), jax.ShapeDtypeStruct((n, cw), F32),
                   jax.ShapeDtypeStruct((n, D), F32)],
        compiler_params=_params(32, 1),
        name="lru_sample",
    )(x2d, cbuf2d, h0, g.reshape(1, D), winb, conv_w, conv_b.reshape(1, D), waxb, b_a.reshape(1, D),
      b_x.reshape(1, D), lam.reshape(1, D), woutb)


def kernel(x_prompt, x_sample, state_pool, state_ret, state_conv, state_lru, pool_norm, pool_w, pool_scale, gm_norm, gm_w_in, gm_b_in, gm_ln_g, gm_ln_b, gm_w_s, gm_b_s, gm_w_out, ret_norm, ret_w_in, ret_gn_g, ret_gn_b, ret_w_out, lru_norm, lru_w_in, lru_conv_w, lru_conv_b, lru_w_a, lru_b_a, lru_w_x, lru_b_x, lru_lam, lru_w_out, mlp_norm, mlp_w1, mlp_w2, final_norm):
    B, L, _ = x_prompt.shape
    NS = x_sample.shape[0]
    bf = lambda w: w.astype(BF16)
    w1b = [bf(mlp_w1[li]) for li in range(mlp_w1.shape[0])]
    w2b = [bf(mlp_w2[li]) for li in range(mlp_w2.shape[0])]

    def mlp_p(x, li, final=False):
        return _mlp(x.reshape(B * L, D), mlp_norm[li], w1b[li], w2b[li], final_norm, final=final,
                    tm=512).reshape(B, L, D)

    def mlp_s(x, li, final=False):
        return _mlp(x, mlp_norm[li], w1b[li], w2b[li], final_norm, final=final, tm=NS)

    pool_wb = bf(pool_w[0])
    xp, pool_p = _pool_prompt(x_prompt, pool_norm[0], pool_wb, pool_scale[0])
    xs, pool_s = _pool_sample(x_sample.reshape(NS, D), state_pool[0].reshape(NS, POOL_BUF * D),
                              pool_norm[0], pool_wb, pool_scale[0], pos0=PAST_LEN)
    xp, xs = mlp_p(xp, 0), mlp_s(xs, 0)

    gm_winb, gm_woutb = bf(gm_w_in[0]), bf(gm_w_out[0])
    bs_full = jnp.repeat(gm_b_s[0].T, CHUNK, axis=1)
    xp = _gmlp_prompt(xp, gm_norm[0], gm_winb, gm_b_in[0], gm_ln_g[0], gm_ln_b[0], gm_w_s[0], bs_full, gm_woutb)
    sw = jnp.repeat(gm_w_s[0][:, 0, 0], CHUNK).reshape(1, D)
    sb = jnp.repeat(gm_b_s[0][:, 0], CHUNK).reshape(1, D)
    xs, v_s = _gmlp_sample(xs, gm_norm[0], gm_winb, gm_b_in[0], gm_ln_g[0], gm_ln_b[0], sw, sb, gm_woutb)
    xp, xs = mlp_p(xp, 1), mlp_s(xs, 1)

    ret_winb, ret_woutb = bf(ret_w_in[0]), bf(ret_w_out[0])
    xp, ret_p = _ret_prompt(xp, ret_norm[0], ret_winb, ret_gn_g[0], ret_gn_b[0], ret_woutb)
    q, k, v, gate = _ret_proj_sample(xs, ret_norm[0], ret_winb, pos0=PAST_LEN)
    o, ret_s = _ret_state_sample(q, k, v, state_ret[0])
    xs = _ret_out_sample(xs, o, gate, ret_gn_g[0], ret_gn_b[0], ret_woutb)
    xp, xs = mlp_p(xp, 2), mlp_s(xs, 2)

    lru_winb, lru_woutb = bf(lru_w_in[0]), bf(lru_w_out[0])
    waxb = bf(jnp.concatenate([lru_w_a[0], lru_w_x[0]], axis=-1))
    xp, conv_p, lru_p = _lru_prompt(xp, lru_norm[0], lru_winb, lru_conv_w[0], lru_conv_b[0], waxb,
                                    lru_b_a[0], lru_b_x[0], lru_lam[0], lru_woutb)
    xs, conv_s, lru_s = _lru_sample(xs, state_conv[0].reshape(NS, (CONV_W - 1) * D), state_lru[0],
                                    lru_norm[0], lru_winb, lru_conv_w[0], lru_conv_b[0], waxb,
                                    lru_b_a[0], lru_b_x[0], lru_lam[0], lru_woutb, pos0=PAST_LEN)
    yp, ys = mlp_p(xp, 3, final=True), mlp_s(xs, 3, final=True)

    return (yp, ys.reshape(NS, 1, D),
            pool_p[None], pool_s.reshape(1, NS, POOL_BUF, D),
            v_s.reshape(1, NS, 1, D),
            ret_p[None], ret_s[None],
            conv_p[None], conv_s.reshape(1, NS, CONV_W - 1, D),
            lru_p[None], lru_s[None])
```

```python
import functools
import math

import jax
import jax.numpy as jnp
import numpy as np
from jax import lax
from jax.experimental import pallas as pl
from jax.experimental.pallas import tpu as pltpu

F32 = jnp.float32
BF16 = jnp.bfloat16

D = 1024
EPS = 1e-6
GN_EPS = 1e-5
PAST_LEN = 16384
POOL_WINDOWS = (2, 4, 8, 16)
POOL_GROUP = D // len(POOL_WINDOWS)
POOL_BUF = max(POOL_WINDOWS) - 1
CHUNK = 128
GM_GROUPS = 8
HEADS = 8
DK = D // HEADS
DV = 2 * D // HEADS
QK = HEADS * DK
VW = HEADS * DV
ROPE_BASE = 10000.0
CONV_W = 4
LRU_C = 8.0
D_FF = 4 * D

MIB = 1024 * 1024
SUBLANES = 8
LANES = 128


def _params(vmem_mib, n_grid):
    return pltpu.CompilerParams(
        dimension_semantics=("arbitrary",) * n_grid,
        vmem_limit_bytes=vmem_mib * MIB,
    )


def _const(shape):
    zeros = (0,) * len(shape)
    return pl.BlockSpec(shape, lambda *_: zeros, pipeline_mode=pl.Buffered(1))


def _rms(x, g):
    ms = jnp.mean(x * x, axis=-1, keepdims=True)
    return x * lax.rsqrt(ms + EPS) * g


def _dot(a, b):
    return jnp.dot(a, b, preferred_element_type=F32)


def _softplus(x):
    return jnp.maximum(x, 0.0) + jnp.log1p(jnp.exp(-jnp.abs(x)))


MLP_FC = 512
STAGE_BYTES = 2 * MIB


def _load_cast(src, dst, stg, sem):
    ch = stg.shape[1]
    n = src.shape[0] // ch

    def copy(c, slot):
        return pltpu.make_async_copy(src.at[pl.ds(c * ch, ch), :], stg.at[slot], sem.at[slot])

    copy(0, 0).start()
    for c in range(n):
        slot = c % 2
        if c + 1 < n:
            copy(c + 1, 1 - slot).start()
        copy(c, slot).wait()
        dst[c * ch:(c + 1) * ch, :] = stg[slot].astype(BF16)


def _mlp_body(xp_ref, xs_ref, g_ref, fg_ref, w1_hbm, w2_hbm, op_ref, os_ref,
              w1b, w2b, stg1, stg2, sem1, sem2, *, li, final, n_p):
    i = pl.program_id(0)

    @pl.when(i == 0)
    def _():
        _load_cast(w1_hbm.at[li], w1b, stg1, sem1)
        _load_cast(w2_hbm.at[li], w2b, stg2, sem2)

    def mlp(x):
        h = _rms(x, g_ref[...]).astype(BF16)
        acc = x
        for c in range(D_FF // MLP_FC):
            lo, hi = c * MLP_FC, (c + 1) * MLP_FC
            a = _dot(h, w1b[:, lo:hi])
            a = jnp.square(jnp.maximum(a, 0.0)).astype(BF16)
            acc = acc + _dot(a, w2b[lo:hi, :])
        if final:
            acc = _rms(acc, fg_ref[...])
        return acc

    @pl.when(i < n_p)
    def _():
        op_ref[...] = mlp(xp_ref[...])

    @pl.when(i == n_p)
    def _():
        os_ref[...] = mlp(xs_ref[...])


def _mlp(xp2d, xs2d, g, w1, w2, fg, *, li, final, tm=512):
    n, ns = xp2d.shape[0], xs2d.shape[0]
    n_p = n // tm
    row = pl.BlockSpec((tm, D), lambda i: (jnp.minimum(i, n_p - 1), 0))
    srow = pl.BlockSpec((ns, D), lambda i: (0, 0))
    hbm = pl.BlockSpec(memory_space=pl.ANY)
    ch1 = STAGE_BYTES // (4 * D_FF)
    ch2 = STAGE_BYTES // (4 * D)
    return pl.pallas_call(
        functools.partial(_mlp_body, li=li, final=final, n_p=n_p),
        grid=(n_p + 1,),
        in_specs=[row, _const((ns, D)), _const((1, D)), _const((1, D)), hbm, hbm],
        out_specs=[row, srow],
        out_shape=[jax.ShapeDtypeStruct((n, D), F32), jax.ShapeDtypeStruct((ns, D), F32)],
        scratch_shapes=[pltpu.VMEM((D, D_FF), BF16), pltpu.VMEM((D_FF, D), BF16),
                        pltpu.VMEM((2, ch1, D_FF), F32), pltpu.VMEM((2, ch2, D), F32),
                        pltpu.SemaphoreType.DMA((2,)), pltpu.SemaphoreType.DMA((2,))],
        compiler_params=_params(48, 1),
        name="mlp",
    )(xp2d, xs2d, g.reshape(1, D), fg.reshape(1, D), w1, w2)


POOL_HALO = 16


def _pool_prompt_body(x_ref, g_ref, w_ref, sc_ref, o_ref, buf_ref, ext_ref, *, T):
    t = pl.program_id(1)

    @pl.when(t == 0)
    def _():
        ext_ref[0:POOL_HALO, :] = jnp.zeros((POOL_HALO, D), F32)

    x = x_ref[...]
    h = _rms(x, g_ref[...])
    ext_ref[POOL_HALO:POOL_HALO + T, :] = h
    pos = t * T + lax.broadcasted_iota(jnp.int32, (T, 1), 0)
    ys = []
    for gi, w in enumerate(POOL_WINDOWS):
        c0, c1 = gi * POOL_GROUP, (gi + 1) * POOL_GROUP
        hg = h[:, c0:c1]
        s = hg
        for k in range(1, w):
            s = s + ext_ref[POOL_HALO - k:POOL_HALO - k + T, c0:c1]
        cnt = jnp.minimum(pos + 1, w).astype(F32)
        d = s / cnt - hg
        ys.append(_dot(d.astype(BF16), w_ref[gi]))
    y = jnp.concatenate(ys, axis=-1) * sc_ref[...]
    o_ref[...] = x + y
    tail = ext_ref[T:T + POOL_HALO, :]
    ext_ref[0:POOL_HALO, :] = tail
    buf_ref[...] = tail[POOL_HALO - POOL_BUF:, :]


def _pool_prompt(x, g, wb, sc, *, T=512):
    B, L, _ = x.shape
    blk = pl.BlockSpec((None, T, D), lambda b, t: (b, t, 0))
    return pl.pallas_call(
        functools.partial(_pool_prompt_body, T=T),
        grid=(B, L // T),
        in_specs=[blk, _const((1, D)), _const((len(POOL_WINDOWS), POOL_GROUP, POOL_GROUP)), _const((1, D))],
        out_specs=[blk, pl.BlockSpec((None, POOL_BUF, D), lambda b, t: (b, 0, 0))],
        out_shape=[jax.ShapeDtypeStruct((B, L, D), F32), jax.ShapeDtypeStruct((B, POOL_BUF, D), F32)],
        scratch_shapes=[pltpu.VMEM((T + POOL_HALO, D), F32)],
        compiler_params=_params(32, 2),
        name="pool_prompt",
    )(x, g.reshape(1, D), wb, sc.reshape(1, D))


def _pool_sample_body(x_ref, buf_ref, g_ref, w_ref, sc_ref, o_ref, nbuf_ref, *, pos0):
    x = x_ref[...]
    h = _rms(x, g_ref[...])
    ys = []
    for gi, w in enumerate(POOL_WINDOWS):
        c0, c1 = gi * POOL_GROUP, (gi + 1) * POOL_GROUP
        hg = h[:, c0:c1]
        s = hg
        for k in range(1, w):
            r = (POOL_BUF - k) * D
            s = s + buf_ref[:, r + c0:r + c1]
        cnt = float(min(pos0 + 1, w))
        d = s / cnt - hg
        ys.append(_dot(d.astype(BF16), w_ref[gi]))
    y = jnp.concatenate(ys, axis=-1) * sc_ref[...]
    o_ref[...] = x + y
    nbuf_ref[:, 0:(POOL_BUF - 1) * D] = buf_ref[:, D:POOL_BUF * D]
    nbuf_ref[:, (POOL_BUF - 1) * D:] = h


def _pool_sample(x2d, buf2d, g, wb, sc, *, pos0, bt=32):
    n = x2d.shape[0]
    row = pl.BlockSpec((bt, D), lambda i: (i, 0))
    brow = pl.BlockSpec((bt, POOL_BUF * D), lambda i: (i, 0))
    return pl.pallas_call(
        functools.partial(_pool_sample_body, pos0=pos0),
        grid=(n // bt,),
        in_specs=[row, brow, _const((1, D)), _const((len(POOL_WINDOWS), POOL_GROUP, POOL_GROUP)), _const((1, D))],
        out_specs=[row, brow],
        out_shape=[jax.ShapeDtypeStruct((n, D), F32), jax.ShapeDtypeStruct((n, POOL_BUF * D), F32)],
        compiler_params=_params(32, 1),
        name="pool_sample",
    )(x2d, buf2d, g.reshape(1, D), wb, sc.reshape(1, D))


def _gmlp_front(x, g_ref, win_ref, bin_ref, lng_ref, lnb_ref):
    h = _rms(x, g_ref[...]).astype(BF16)
    z = jax.nn.gelu(_dot(h, win_ref[...]) + bin_ref[...])
    u, v = z[:, :D], z[:, D:]
    mu = jnp.mean(v, axis=-1, keepdims=True)
    vc = v - mu
    var = jnp.mean(jnp.square(vc), axis=-1, keepdims=True)
    vn = vc * lax.rsqrt(var + EPS) * lng_ref[...] + lnb_ref[...]
    return u, vn


def _gmlp_prompt_body(x_ref, g_ref, win_ref, bin_ref, lng_ref, lnb_ref, ws_ref, bs_ref, wout_ref,
                      o_ref, *, T):
    ti = lax.broadcasted_iota(jnp.int32, (CHUNK, CHUNK), 0)
    si = lax.broadcasted_iota(jnp.int32, (CHUNK, CHUNK), 1)
    causal = ti >= si
    wsm = [jnp.where(causal, ws_ref[gi], 0.0).astype(BF16) for gi in range(GM_GROUPS)]
    x = x_ref[...]
    u, vn = _gmlp_front(x, g_ref, win_ref, bin_ref, lng_ref, lnb_ref)
    vnb = vn.astype(BF16)
    rows = []
    for n in range(T // CHUNK):
        r0, r1 = n * CHUNK, (n + 1) * CHUNK
        cols = [_dot(wsm[gi], vnb[r0:r1, gi * CHUNK:(gi + 1) * CHUNK]) for gi in range(GM_GROUPS)]
        rows.append(jnp.concatenate(cols, axis=-1) + bs_ref[...])
    mixed = jnp.concatenate(rows, axis=0)
    y = _dot((u * mixed).astype(BF16), wout_ref[...])
    o_ref[...] = x + y


def _gmlp_prompt(x, g, winb, b_in, ln_g, ln_b, w_s, bs_full, woutb, *, T=512):
    B, L, _ = x.shape
    blk = pl.BlockSpec((None, T, D), lambda b, t: (b, t, 0))
    return pl.pallas_call(
        functools.partial(_gmlp_prompt_body, T=T),
        grid=(B, L // T),
        in_specs=[blk, _const((1, D)), _const((D, 2 * D)), _const((1, 2 * D)), _const((1, D)), _const((1, D)),
                  _const((GM_GROUPS, CHUNK, CHUNK)), _const((CHUNK, D)), _const((D, D))],
        out_specs=blk,
        out_shape=jax.ShapeDtypeStruct((B, L, D), F32),
        compiler_params=_params(48, 2),
        name="gmlp_prompt",
    )(x, g.reshape(1, D), winb, b_in.reshape(1, 2 * D), ln_g.reshape(1, D), ln_b.reshape(1, D),
      w_s, bs_full, woutb)


def _gmlp_sample_body(x_ref, g_ref, win_ref, bin_ref, lng_ref, lnb_ref, sw_ref, sb_ref, wout_ref,
                      o_ref, vn_ref):
    x = x_ref[...]
    u, vn = _gmlp_front(x, g_ref, win_ref, bin_ref, lng_ref, lnb_ref)
    mixed = vn * sw_ref[...] + sb_ref[...]
    y = _dot((u * mixed).astype(BF16), wout_ref[...])
    o_ref[...] = x + y
    vn_ref[...] = vn


def _gmlp_sample(x2d, g, winb, b_in, ln_g, ln_b, sw, sb, woutb):
    n = x2d.shape[0]
    full = _const((n, D))
    return pl.pallas_call(
        _gmlp_sample_body,
        grid=(1,),
        in_specs=[full, _const((1, D)), _const((D, 2 * D)), _const((1, 2 * D)), _const((1, D)), _const((1, D)),
                  _const((1, D)), _const((1, D)), _const((D, D))],
        out_specs=[pl.BlockSpec((n, D), lambda i: (0, 0)), pl.BlockSpec((n, D), lambda i: (0, 0))],
        out_shape=[jax.ShapeDtypeStruct((n, D), F32), jax.ShapeDtypeStruct((n, D), F32)],
        compiler_params=_params(32, 1),
        name="gmlp_sample",
    )(x2d, g.reshape(1, D), winb, b_in.reshape(1, 2 * D), ln_g.reshape(1, D), ln_b.reshape(1, D),
      sw, sb, woutb)


def _ret_constants(C):
    log_gamma = np.log1p(-np.exp2(-5.0 - np.arange(HEADS, dtype=np.float64)))
    idx = np.arange(C, dtype=np.float64)
    diff = idx[:, None] - idx[None, :]
    dmask = np.where(diff[None] >= 0, np.exp(log_gamma[:, None, None] * np.maximum(diff, 0.0)[None]), 0.0)
    q_dec = np.exp(log_gamma[:, None] * (idx + 1.0))
    k_dec = np.exp(log_gamma[:, None] * (C - 1.0 - idx))
    chunk_dec = np.exp(log_gamma * C)
    return dmask.astype(np.float32), q_dec.astype(np.float32), k_dec.astype(np.float32), chunk_dec


def _rope_tables(pos):
    half = DK // 2
    freqs = np.exp(-math.log(ROPE_BASE) * np.arange(half, dtype=np.float64) / half)
    ang = np.asarray(pos, dtype=np.float64)[:, None] * freqs[None]
    cos = np.concatenate([np.cos(ang), np.cos(ang)], axis=-1)
    sin = np.concatenate([-np.sin(ang), np.sin(ang)], axis=-1)
    return cos, sin


def _rope(t, cos, sin):
    return t * cos + pltpu.roll(t, DK // 2, axis=1) * sin


def _group_norm_gate(o, gate, gng, gnb):
    mu = jnp.mean(o, axis=-1, keepdims=True)
    oc = o - mu
    var = jnp.mean(jnp.square(oc), axis=-1, keepdims=True)
    on = oc * lax.rsqrt(var + GN_EPS) * gng + gnb
    return jax.nn.silu(gate) * on


def _ret_prompt_body(x_ref, g_ref, win_ref, cq_ref, sq_ref, ck_ref, sk_ref, dm_ref, qd_ref, kd_ref,
                     gng_ref, gnb_ref, wout_ref, o_ref, s_out_ref, gated_scr, s_scr,
                     *, T, chunk_dec):
    t = pl.program_id(1)

    @pl.when(t == 0)
    def _():
        s_scr[...] = jnp.zeros_like(s_scr)

    x = x_ref[...]
    h = _rms(x, g_ref[...]).astype(BF16)
    cq, sq, ck, sk = cq_ref[...], sq_ref[...], ck_ref[...], sk_ref[...]
    for pair in range(HEADS // 2):
        q2 = _dot(h, win_ref[:, 2 * DK * pair:2 * DK * (pair + 1)])
        k2 = _dot(h, win_ref[:, QK + 2 * DK * pair:QK + 2 * DK * (pair + 1)])
        v2 = _dot(h, win_ref[:, 2 * QK + 2 * DV * pair:2 * QK + 2 * DV * (pair + 1)])
        g2 = _dot(h, win_ref[:, 2 * QK + VW + 2 * DV * pair:2 * QK + VW + 2 * DV * (pair + 1)])
        for sub in range(2):
            hd = 2 * pair + sub
            q = _rope(q2[:, sub * DK:(sub + 1) * DK], cq, sq)
            k = _rope(k2[:, sub * DK:(sub + 1) * DK], ck, sk)
            qb, kb = q.astype(BF16), k.astype(BF16)
            gng = gng_ref[:, hd * DV:(hd + 1) * DV]
            gnb = gnb_ref[:, hd * DV:(hd + 1) * DV]
            s = s_scr[hd]
            for c in range(T // CHUNK):
                rows = slice(c * CHUNK, (c + 1) * CHUNK)
                vb = v2[rows, sub * DV:(sub + 1) * DV].astype(BF16)
                scores = lax.dot_general(qb[rows], kb[rows], (((1,), (1,)), ((), ())),
                                         preferred_element_type=F32) * dm_ref[hd]
                o = _dot(scores.astype(BF16), vb) + _dot((q[rows] * qd_ref[hd]).astype(BF16), s.astype(BF16))
                kt = (k[rows] * kd_ref[hd]).T.astype(BF16)
                s = s * chunk_dec[hd] + _dot(kt, vb)
                gated = _group_norm_gate(o, g2[rows, sub * DV:(sub + 1) * DV], gng, gnb)
                gated_scr[rows, hd * DV:(hd + 1) * DV] = gated.astype(BF16)
            s_scr[hd] = s
    y = _dot(gated_scr[...], wout_ref[...])
    o_ref[...] = x + y
    s_out_ref[...] = s_scr[...]


def _ret_prompt(x, g, winb, gn_g, gn_b, woutb, *, T=512):
    B, L, _ = x.shape
    dmask, q_dec, k_dec, chunk_dec = _ret_constants(CHUNK)
    cos, sin = _rope_tables(np.arange(L))
    scale = DK ** -0.5
    cq, sq = jnp.asarray(cos, F32), jnp.asarray(sin, F32)
    ck, sk = jnp.asarray(cos * scale, F32), jnp.asarray(sin * scale, F32)
    qd = jnp.asarray(np.broadcast_to(q_dec[:, :, None], (HEADS, CHUNK, DK)))
    kd = jnp.asarray(np.broadcast_to(k_dec[:, :, None], (HEADS, CHUNK, DK)))
    blk = pl.BlockSpec((None, T, D), lambda b, t: (b, t, 0))
    tab = pl.BlockSpec((T, DK), lambda b, t: (t, 0))
    hcc = _const((HEADS, CHUNK, CHUNK))
    return pl.pallas_call(
        functools.partial(_ret_prompt_body, T=T, chunk_dec=tuple(float(c) for c in chunk_dec)),
        grid=(B, L // T),
        in_specs=[blk, _const((1, D)), _const((D, 2 * QK + 2 * VW)), tab, tab, tab, tab,
                  hcc, hcc, hcc, _const((1, VW)), _const((1, VW)), _const((VW, D))],
        out_specs=[blk, pl.BlockSpec((None, HEADS, DK, DV), lambda b, t: (b, 0, 0, 0))],
        out_shape=[jax.ShapeDtypeStruct((B, L, D), F32), jax.ShapeDtypeStruct((B, HEADS, DK, DV), F32)],
        scratch_shapes=[pltpu.VMEM((T, VW), BF16), pltpu.VMEM((HEADS, DK, DV), F32)],
        compiler_params=_params(56, 2),
        name="ret_prompt",
    )(x, g.reshape(1, D), winb, cq, sq, ck, sk, jnp.asarray(dmask), qd, kd,
      gn_g.reshape(1, VW), gn_b.reshape(1, VW), woutb)


def _ret_proj_sample_body(x_ref, g_ref, win_ref, cq_ref, sq_ref, ck_ref, sk_ref,
                          q_ref, k_ref, v_ref, gate_ref):
    h = _rms(x_ref[...], g_ref[...]).astype(BF16)
    p = _dot(h, win_ref[...])
    for hd in range(HEADS):
        q_ref[:, hd * DK:(hd + 1) * DK] = _rope(p[:, hd * DK:(hd + 1) * DK], cq_ref[...], sq_ref[...])
        k_ref[:, hd * DK:(hd + 1) * DK] = _rope(p[:, QK + hd * DK:QK + (hd + 1) * DK], ck_ref[...], sk_ref[...])
    v_ref[...] = p[:, 2 * QK:2 * QK + VW]
    gate_ref[...] = p[:, 2 * QK + VW:]


def _ret_proj_sample(x2d, g, winb, *, pos0):
    n = x2d.shape[0]
    cos, sin = _rope_tables(np.array([pos0]))
    scale = DK ** -0.5
    tabs = [jnp.asarray(a, F32) for a in (cos, sin, cos * scale, sin * scale)]
    out = lambda w: pl.BlockSpec((n, w), lambda i: (0, 0))
    return pl.pallas_call(
        _ret_proj_sample_body,
        grid=(1,),
        in_specs=[_const((n, D)), _const((1, D)), _const((D, 2 * QK + 2 * VW))] + [_const((1, DK))] * 4,
        out_specs=[out(QK), out(QK), out(VW), out(VW)],
        out_shape=[jax.ShapeDtypeStruct((n, w), F32) for w in (QK, QK, VW, VW)],
        compiler_params=_params(40, 1),
        name="ret_proj_sample",
    )(x2d, g.reshape(1, D), winb, *tabs)


def _ret_state_sample_body(q_ref, k_ref, v_ref, s_ref, o_ref, s_out_ref, *, bt, gamma):
    pad = jnp.zeros((DK - bt, DK), F32)
    for hd in range(HEADS):
        qh = q_ref[:, hd * DK:(hd + 1) * DK]
        kh = k_ref[:, hd * DK:(hd + 1) * DK]
        qt = jnp.concatenate([qh, pad], axis=0).T
        kt = jnp.concatenate([kh, pad], axis=0).T
        qk = jnp.sum(qh * kh, axis=-1, keepdims=True)
        for b in range(bt):
            v = v_ref[b:b + 1, hd * DV:(hd + 1) * DV]
            s_old = s_ref[b, hd]
            qs = jnp.sum(s_old * (qt[:, b:b + 1] * gamma[hd]), axis=0, keepdims=True)
            o_ref[b:b + 1, hd * DV:(hd + 1) * DV] = qk[b:b + 1, :] * v + qs
            s_out_ref[b, hd] = s_old * gamma[hd] + kt[:, b:b + 1] * v


def _ret_state_sample(q, k, v, s, *, bt=8):
    n = q.shape[0]
    _, _, _, chunk_dec = _ret_constants(1)
    row = lambda w: pl.BlockSpec((bt, w), lambda i: (i, 0))
    sblk = pl.BlockSpec((bt, HEADS, DK, DV), lambda i: (i, 0, 0, 0))
    return pl.pallas_call(
        functools.partial(_ret_state_sample_body, bt=bt, gamma=tuple(float(c) for c in chunk_dec)),
        grid=(n // bt,),
        in_specs=[row(QK), row(QK), row(VW), sblk],
        out_specs=[row(VW), sblk],
        out_shape=[jax.ShapeDtypeStruct((n, VW), F32), jax.ShapeDtypeStruct(s.shape, F32)],
        compiler_params=_params(48, 1),
        name="ret_state_sample",
    )(q, k, v, s)


def _ret_out_sample_body(x_ref, o_ref_in, gate_ref, gng_ref, gnb_ref, wout_ref, o_ref):
    parts = []
    for hd in range(HEADS):
        cols = slice(hd * DV, (hd + 1) * DV)
        parts.append(_group_norm_gate(o_ref_in[:, cols], gate_ref[:, cols], gng_ref[:, cols], gnb_ref[:, cols]))
    gated = jnp.concatenate(parts, axis=-1).astype(BF16)
    o_ref[...] = x_ref[...] + _dot(gated, wout_ref[...])


def _ret_out_sample(x2d, o, gate, gn_g, gn_b, woutb):
    n = x2d.shape[0]
    return pl.pallas_call(
        _ret_out_sample_body,
        grid=(1,),
        in_specs=[_const((n, D)), _const((n, VW)), _const((n, VW)), _const((1, VW)), _const((1, VW)),
                  _const((VW, D))],
        out_specs=pl.BlockSpec((n, D), lambda i: (0, 0)),
        out_shape=jax.ShapeDtypeStruct((n, D), F32),
        compiler_params=_params(32, 1),
        name="ret_out_sample",
    )(x2d, o, gate, gn_g.reshape(1, VW), gn_b.reshape(1, VW), woutb)


LRU_HALO = 8


def _lru_gates(xc, wax_ref, ba, bx, lam, heads):
    xcb = xc.astype(BF16)
    rs, is_ = [], []
    for n, hd in enumerate(heads):
        ri = _dot(xcb[:, n * DK:(n + 1) * DK], wax_ref[hd])
        rs.append(ri[:, :DK])
        is_.append(ri[:, DK:])
    r = jax.nn.sigmoid(jnp.concatenate(rs, axis=-1) + ba)
    i = jax.nn.sigmoid(jnp.concatenate(is_, axis=-1) + bx)
    log_a = -LRU_C * r * _softplus(-lam)
    a = jnp.exp(log_a)
    mult = jnp.sqrt(-jnp.tanh(log_a) * (a * a + 1.0))
    return a, mult, i


LRU_GROUP_HEADS = 2


def _lru_prompt_body(x_ref, g_ref, win_ref, cw_ref, cb_ref, wax_ref, ba_ref, bx_ref, lam_ref, wout_ref,
                     o_ref, conv_ref, hlast_ref, xb_scr, hs_scr, carry_scr, h_scr, *, NB, T):
    t = pl.program_id(0)
    N = NB * T
    PS = T + 1
    halo = (CONV_W - 1) * NB
    gw = LRU_GROUP_HEADS * DK
    gblk = gw // LANES

    @pl.when(t == 0)
    def _():
        carry_scr[...] = jnp.zeros_like(carry_scr)
        h_scr[...] = jnp.zeros_like(h_scr)

    x = x_ref[...].reshape(N, D)
    h = _rms(x, g_ref[...]).astype(BF16)
    row = lax.broadcasted_iota(jnp.int32, (N, 1), 0)
    pos = t * T + lax.shift_right_logical(row, NB.bit_length() - 1)
    out = x
    def project(grp):
        c0, c1 = grp * gw, (grp + 1) * gw
        return _dot(h, win_ref[:, c0:c1]), _dot(h, win_ref[:, D + c0:D + c1])

    ngrp = D // gw
    nxt = project(0)
    for grp in range(ngrp):
        c0, c1 = grp * gw, (grp + 1) * gw
        blks = range(grp * gblk, (grp + 1) * gblk)
        zg, xb = nxt
        if grp + 1 < ngrp:
            nxt = project(grp + 1)
        gate = jax.nn.gelu(zg)
        for n, j in enumerate(blks):
            for b in range(NB):
                xb_scr[j, b * PS:b * PS + T, :] = xb[b * T:(b + 1) * T, n * LANES:(n + 1) * LANES]
        ext = jnp.concatenate(
            [carry_scr[:, c0:c1]]
            + [jnp.concatenate([xb_scr[j, pl.ds(s, NB, stride=PS), :] for j in blks], axis=-1)
               for s in range(T)], axis=0)
        carry_scr[:, c0:c1] = ext[N:N + halo, :]
        acc = ext[0:N, :] * cw_ref[0:1, c0:c1]
        for j in range(1, CONV_W):
            acc = acc + ext[j * NB:j * NB + N, :] * cw_ref[j:j + 1, c0:c1]
        xc = cb_ref[:, c0:c1] + acc
        heads = range(grp * LRU_GROUP_HEADS, (grp + 1) * LRU_GROUP_HEADS)
        a, mult, i = _lru_gates(xc, wax_ref, ba_ref[:, c0:c1], bx_ref[:, c0:c1], lam_ref[:, c0:c1], heads)
        bvec = jnp.where(pos == 0, 1.0, mult) * (i * xc)
        hcur = h_scr[:, c0:c1]
        for s in range(T):
            hcur = a[s * NB:(s + 1) * NB, :] * hcur + bvec[s * NB:(s + 1) * NB, :]
            for n, j in enumerate(blks):
                hs_scr[j, pl.ds(s, NB, stride=PS), :] = hcur[:, n * LANES:(n + 1) * LANES]
        h_scr[:, c0:c1] = hcur
        hs = jnp.concatenate(
            [jnp.concatenate([hs_scr[j, b * PS:b * PS + T, :] for b in range(NB)], axis=0) for j in blks],
            axis=-1)
        out = out + _dot((hs * gate).astype(BF16), wout_ref[c0:c1, :])
    o_ref[...] = out.reshape(NB, T, D)
    hlast_ref[...] = h_scr[...]

    @pl.when(t == pl.num_programs(0) - 1)
    def _():
        for j in range(CONV_W - 1):
            conv_ref[:, j, :] = carry_scr[j * NB:(j + 1) * NB, :]


def _lru_prompt(x, g, winb, conv_w, conv_b, waxb, b_a, b_x, lam, woutb, *, T=64):
    B, L, _ = x.shape
    blk = pl.BlockSpec((B, T, D), lambda t: (0, t, 0))
    vec = _const((1, D))
    return pl.pallas_call(
        functools.partial(_lru_prompt_body, NB=B, T=T),
        grid=(L // T,),
        in_specs=[blk, vec, _const((D, 2 * D)), _const((CONV_W, D)), vec, _const((HEADS, DK, 2 * DK)),
                  vec, vec, vec, _const((D, D))],
        out_specs=[blk, pl.BlockSpec((B, CONV_W - 1, D), lambda t: (0, 0, 0)),
                   pl.BlockSpec((B, D), lambda t: (0, 0))],
        out_shape=[jax.ShapeDtypeStruct((B, L, D), F32), jax.ShapeDtypeStruct((B, CONV_W - 1, D), F32),
                   jax.ShapeDtypeStruct((B, D), F32)],
        scratch_shapes=[pltpu.VMEM((D // LANES, B * (T + 1), LANES), F32)] * 2
        + [pltpu.VMEM(((CONV_W - 1) * B, D), F32), pltpu.VMEM((B, D), F32)],
        compiler_params=_params(48, 1),
        name="lru_prompt",
    )(x, g.reshape(1, D), winb, conv_w, conv_b.reshape(1, D), waxb, b_a.reshape(1, D), b_x.reshape(1, D),
      lam.reshape(1, D), woutb)


def _lru_sample_body(x_ref, cbuf_ref, h0_ref, g_ref, win_ref, cw_ref, cb_ref, wax_ref, ba_ref, bx_ref,
                     lam_ref, wout_ref, o_ref, nconv_ref, hnew_ref, *, pos0):
    x = x_ref[...]
    h = _rms(x, g_ref[...]).astype(BF16)
    z = _dot(h, win_ref[...])
    gate = jax.nn.gelu(z[:, :D])
    xb = z[:, D:]
    cw = cw_ref[...]
    acc = cbuf_ref[:, 0:D] * cw[0:1, :]
    for j in range(1, CONV_W - 1):
        acc = acc + cbuf_ref[:, j * D:(j + 1) * D] * cw[j:j + 1, :]
    acc = acc + xb * cw[CONV_W - 1:CONV_W, :]
    xc = cb_ref[...] + acc
    a, mult, i = _lru_gates(xc, wax_ref, ba_ref[...], bx_ref[...], lam_ref[...], range(HEADS))
    if pos0 == 0:
        mult = jnp.ones_like(mult)
    hnew = a * h0_ref[...] + mult * (i * xc)
    hnew_ref[...] = hnew
    nconv_ref[:, 0:(CONV_W - 2) * D] = cbuf_ref[:, D:(CONV_W - 1) * D]
    nconv_ref[:, (CONV_W - 2) * D:] = xb
    o_ref[...] = x + _dot((hnew * gate).astype(BF16), wout_ref[...])


def _lru_sample(x2d, cbuf2d, h0, g, winb, conv_w, conv_b, waxb, b_a, b_x, lam, woutb, *, pos0):
    n = x2d.shape[0]
    vec = _const((1, D))
    cw = (CONV_W - 1) * D
    out = lambda w: pl.BlockSpec((n, w), lambda i: (0, 0))
    return pl.pallas_call(
        functools.partial(_lru_sample_body, pos0=pos0),
        grid=(1,),
        in_specs=[_const((n, D)), _const((n, cw)), _const((n, D)), vec, _const((D, 2 * D)), _const((CONV_W, D)),
                  vec, _const((HEADS, DK, 2 * DK)), vec, vec, vec, _const((D, D))],
        out_specs=[out(D), out(cw), out(D)],
        out_shape=[jax.ShapeDtypeStruct((n, D), F32), jax.ShapeDtypeStruct((n, cw), F32),
                   jax.ShapeDtypeStruct((n, D), F32)],
        compiler_params=_params(32, 1),
        name="lru_sample",
    )(x2d, cbuf2d, h0, g.reshape(1, D), winb, conv_w, conv_b.reshape(1, D), waxb, b_a.reshape(1, D),
      b_x.reshape(1, D), lam.reshape(1, D), woutb)


def kernel(x_prompt, x_sample, state_pool, state_ret, state_conv, state_lru, pool_norm, pool_w, pool_scale, gm_norm, gm_w_in, gm_b_in, gm_ln_g, gm_ln_b, gm_w_s, gm_b_s, gm_w_out, ret_norm, ret_w_in, ret_gn_g, ret_gn_b, ret_w_out, lru_norm, lru_w_in, lru_conv_w, lru_conv_b, lru_w_a, lru_b_a, lru_w_x, lru_b_x, lru_lam, lru_w_out, mlp_norm, mlp_w1, mlp_w2, final_norm):
    B, L, _ = x_prompt.shape
    NS = x_sample.shape[0]
    bf = lambda w: w.astype(BF16)

    def mlp(xp, xs, li, final=False):
        yp, ys = _mlp(xp.reshape(B * L, D), xs, mlp_norm[li], mlp_w1, mlp_w2, final_norm, li=li, final=final)
        return yp.reshape(B, L, D), ys

    pool_wb = bf(pool_w[0])
    xp, pool_p = _pool_prompt(x_prompt, pool_norm[0], pool_wb, pool_scale[0])
    xs, pool_s = _pool_sample(x_sample.reshape(NS, D), state_pool[0].reshape(NS, POOL_BUF * D),
                              pool_norm[0], pool_wb, pool_scale[0], pos0=PAST_LEN)
    xp, xs = mlp(xp, xs, 0)

    gm_winb, gm_woutb = bf(gm_w_in[0]), bf(gm_w_out[0])
    bs_full = jnp.repeat(gm_b_s[0].T, CHUNK, axis=1)
    xp = _gmlp_prompt(xp, gm_norm[0], gm_winb, gm_b_in[0], gm_ln_g[0], gm_ln_b[0], gm_w_s[0], bs_full, gm_woutb)
    sw = jnp.repeat(gm_w_s[0][:, 0, 0], CHUNK).reshape(1, D)
    sb = jnp.repeat(gm_b_s[0][:, 0], CHUNK).reshape(1, D)
    xs, v_s = _gmlp_sample(xs, gm_norm[0], gm_winb, gm_b_in[0], gm_ln_g[0], gm_ln_b[0], sw, sb, gm_woutb)
    xp, xs = mlp(xp, xs, 1)

    ret_winb, ret_woutb = bf(ret_w_in[0]), bf(ret_w_out[0])
    xp, ret_p = _ret_prompt(xp, ret_norm[0], ret_winb, ret_gn_g[0], ret_gn_b[0], ret_woutb)
    q, k, v, gate = _ret_proj_sample(xs, ret_norm[0], ret_winb, pos0=PAST_LEN)
    o, ret_s = _ret_state_sample(q, k, v, state_ret[0])
    xs = _ret_out_sample(xs, o, gate, ret_gn_g[0], ret_gn_b[0], ret_woutb)
    xp, xs = mlp(xp, xs, 2)

    lru_winb, lru_woutb = bf(lru_w_in[0]), bf(lru_w_out[0])
    waxb = bf(jnp.concatenate([lru_w_a[0], lru_w_x[0]], axis=-1))
    xp, conv_p, lru_p = _lru_prompt(xp, lru_norm[0], lru_winb, lru_conv_w[0], lru_conv_b[0], waxb,
                                    lru_b_a[0], lru_b_x[0], lru_lam[0], lru_woutb)
    xs, conv_s, lru_s = _lru_sample(xs, state_conv[0].reshape(NS, (CONV_W - 1) * D), state_lru[0],
                                    lru_norm[0], lru_winb, lru_conv_w[0], lru_conv_b[0], waxb,
                                    lru_b_a[0], lru_b_x[0], lru_lam[0], lru_woutb, pos0=PAST_LEN)
    yp, ys = mlp(xp, xs, 3, final=True)

    return (yp, ys.reshape(NS, 1, D),
            pool_p[None], pool_s.reshape(1, NS, POOL_BUF, D),
            v_s.reshape(1, NS, 1, D),
            ret_p[None], ret_s[None],
            conv_p[None], conv_s.reshape(1, NS, CONV_W - 1, D),
            lru_p[None], lru_s[None])
```

```python
import functools
import math

import jax
import jax.numpy as jnp
import numpy as np
from jax import lax
from jax.experimental import pallas as pl
from jax.experimental.pallas import tpu as pltpu

F32 = jnp.float32
BF16 = jnp.bfloat16

D = 1024
EPS = 1e-6
GN_EPS = 1e-5
PAST_LEN = 16384
POOL_WINDOWS = (2, 4, 8, 16)
POOL_GROUP = D // len(POOL_WINDOWS)
POOL_BUF = max(POOL_WINDOWS) - 1
CHUNK = 128
GM_GROUPS = 8
HEADS = 8
DK = D // HEADS
DV = 2 * D // HEADS
QK = HEADS * DK
VW = HEADS * DV
ROPE_BASE = 10000.0
CONV_W = 4
LRU_C = 8.0
D_FF = 4 * D

MIB = 1024 * 1024
SUBLANES = 8
LANES = 128


def _params(vmem_mib, n_grid):
    return pltpu.CompilerParams(
        dimension_semantics=("arbitrary",) * n_grid,
        vmem_limit_bytes=vmem_mib * MIB,
    )


def _const(shape):
    zeros = (0,) * len(shape)
    return pl.BlockSpec(shape, lambda *_: zeros, pipeline_mode=pl.Buffered(1))


def _rms(x, g):
    ms = jnp.mean(x * x, axis=-1, keepdims=True)
    return x * lax.rsqrt(ms + EPS) * g


def _dot(a, b):
    return jnp.dot(a, b, preferred_element_type=F32)


def _interleave(*queues):
    pos = [0] * len(queues)
    while any(p < len(q) for p, q in zip(pos, queues)):
        _, i = min((pos[i] / len(q), i) for i, q in enumerate(queues) if pos[i] < len(q))
        queues[i][pos[i]]()
        pos[i] += 1


def _softplus(x):
    return jnp.maximum(x, 0.0) + jnp.log1p(jnp.exp(-jnp.abs(x)))


MLP_FC = 512
STAGE_BYTES = 2 * MIB


def _load_cast(src, dst, stg, sem):
    ch = stg.shape[1]
    n = src.shape[0] // ch

    def copy(c, slot):
        return pltpu.make_async_copy(src.at[pl.ds(c * ch, ch), :], stg.at[slot], sem.at[slot])

    copy(0, 0).start()
    for c in range(n):
        slot = c % 2
        if c + 1 < n:
            copy(c + 1, 1 - slot).start()
        copy(c, slot).wait()
        dst[c * ch:(c + 1) * ch, :] = stg[slot].astype(BF16)


def _mlp_thunks(read_x, write_out, g_ref, fg_ref, w1b, w2b, final):
    st = {}

    def head():
        x = read_x()
        st["h"] = _rms(x, g_ref[...]).astype(BF16)
        st["acc"] = x

    def up(c):
        def run():
            a = _dot(st["h"], w1b[:, c * MLP_FC:(c + 1) * MLP_FC])
            st["a"] = jnp.square(jnp.maximum(a, 0.0)).astype(BF16)
        return run

    def down(c):
        def run():
            st["acc"] = st["acc"] + _dot(st["a"], w2b[c * MLP_FC:(c + 1) * MLP_FC, :])
        return run

    def tail():
        acc = st["acc"]
        write_out(_rms(acc, fg_ref[...]) if final else acc)

    return [head] + [f(c) for c in range(D_FF // MLP_FC) for f in (up, down)] + [tail]


def _run(thunks):
    for thunk in thunks:
        thunk()


def _store(ref):
    def write(v):
        ref[...] = v
    return write


def _mlp_body(xp_ref, xs_ref, g_ref, fg_ref, w1_hbm, w2_hbm, op_ref, os_ref,
              w1b, w2b, stg1, stg2, sem1, sem2, *, li, final, n_p):
    i = pl.program_id(0)

    @pl.when(i == 0)
    def _():
        _load_cast(w1_hbm.at[li], w1b, stg1, sem1)
        _load_cast(w2_hbm.at[li], w2b, stg2, sem2)

    @pl.when(i < n_p)
    def _():
        _run(_mlp_thunks(lambda: xp_ref[...], _store(op_ref), g_ref, fg_ref, w1b, w2b, final))

    @pl.when(i == n_p)
    def _():
        _run(_mlp_thunks(lambda: xs_ref[...], _store(os_ref), g_ref, fg_ref, w1b, w2b, final))


def _mlp(xp2d, xs2d, g, w1, w2, fg, *, li, final, tm=512):
    n, ns = xp2d.shape[0], xs2d.shape[0]
    n_p = n // tm
    row = pl.BlockSpec((tm, D), lambda i: (jnp.minimum(i, n_p - 1), 0))
    srow = pl.BlockSpec((ns, D), lambda i: (0, 0))
    hbm = pl.BlockSpec(memory_space=pl.ANY)
    ch1 = STAGE_BYTES // (4 * D_FF)
    ch2 = STAGE_BYTES // (4 * D)
    return pl.pallas_call(
        functools.partial(_mlp_body, li=li, final=final, n_p=n_p),
        grid=(n_p + 1,),
        in_specs=[row, _const((ns, D)), _const((1, D)), _const((1, D)), hbm, hbm],
        out_specs=[row, srow],
        out_shape=[jax.ShapeDtypeStruct((n, D), F32), jax.ShapeDtypeStruct((ns, D), F32)],
        scratch_shapes=[pltpu.VMEM((D, D_FF), BF16), pltpu.VMEM((D_FF, D), BF16),
                        pltpu.VMEM((2, ch1, D_FF), F32), pltpu.VMEM((2, ch2, D), F32),
                        pltpu.SemaphoreType.DMA((2,)), pltpu.SemaphoreType.DMA((2,))],
        compiler_params=_params(48, 1),
        name="mlp",
    )(xp2d, xs2d, g.reshape(1, D), fg.reshape(1, D), w1, w2)


POOL_HALO = 16


def _pool_thunks(x_ref, g_ref, w_ref, sc_ref, write_out, buf_ref, ext_ref, *, T, t):
    st = {}

    def head():
        ext_ref[0:POOL_HALO, :] = jnp.where(t == 0, 0.0, ext_ref[0:POOL_HALO, :])
        x = x_ref[...]
        h = _rms(x, g_ref[...])
        ext_ref[POOL_HALO:POOL_HALO + T, :] = h
        st["x"], st["h"] = x, h
        st["pos"] = t * T + lax.broadcasted_iota(jnp.int32, (T, 1), 0)
        st["y"] = [None] * len(POOL_WINDOWS)

    def group(gi, w):
        def run():
            c0, c1 = gi * POOL_GROUP, (gi + 1) * POOL_GROUP
            hg = st["h"][:, c0:c1]
            s = hg
            for k in range(1, w):
                s = s + ext_ref[POOL_HALO - k:POOL_HALO - k + T, c0:c1]
            cnt = jnp.minimum(st["pos"] + 1, w).astype(F32)
            d = s / cnt - hg
            st["y"][gi] = _dot(d.astype(BF16), w_ref[gi])
        return run

    def tail():
        y = jnp.concatenate(st["y"], axis=-1) * sc_ref[...]
        write_out(st["x"] + y)
        last = ext_ref[T:T + POOL_HALO, :]
        ext_ref[0:POOL_HALO, :] = last
        buf_ref[...] = last[POOL_HALO - POOL_BUF:, :]

    return [head] + [group(gi, w) for gi, w in enumerate(POOL_WINDOWS)] + [tail]


def _pool_mlp_body(x_ref, xs_ref, pg_ref, pw_ref, psc_ref, g_ref, fg_ref, w1_hbm, w2_hbm,
                   o_ref, os_ref, buf_ref, ext_ref, x1_scr, w1b, w2b, stg1, stg2, sem1, sem2,
                   *, T, tiles_per_seq, n, li):
    s = pl.program_id(0)
    t = lax.rem(s, tiles_per_seq)
    slot = lax.rem(s, 2)

    @pl.when(s == 0)
    def _():
        ext_ref[0:POOL_HALO, :] = jnp.zeros((POOL_HALO, D), F32)
        _load_cast(w1_hbm.at[li], w1b, stg1, sem1)
        _load_cast(w2_hbm.at[li], w2b, stg2, sem2)

    def write_x1(v):
        x1_scr[slot] = v

    def mixer():
        return _pool_thunks(x_ref, pg_ref, pw_ref, psc_ref, write_x1, buf_ref, ext_ref, T=T, t=t)

    def mlp():
        return _mlp_thunks(lambda: x1_scr[1 - slot], _store(o_ref), g_ref, fg_ref, w1b, w2b, False)

    @pl.when(s == 0)
    def _():
        _run(mixer())

    @pl.when((s > 0) & (s < n))
    def _():
        _interleave(mlp(), mixer())

    @pl.when(s == n)
    def _():
        _run(mlp())
        _run(_mlp_thunks(lambda: xs_ref[...], _store(os_ref), g_ref, fg_ref, w1b, w2b, False))


def _pool_mlp(x, xs2d, pg, pwb, psc, g, w1, w2, fg, *, li, T=512):
    B, L, _ = x.shape
    ns = xs2d.shape[0]
    tps = L // T
    n = B * tps
    tile = lambda s: (jnp.minimum(s, n - 1) // tps, lax.rem(jnp.minimum(s, n - 1), tps), 0)
    prev = lambda s: (jnp.maximum(s - 1, 0) // tps, lax.rem(jnp.maximum(s - 1, 0), tps), 0)
    hbm = pl.BlockSpec(memory_space=pl.ANY)
    ch1 = STAGE_BYTES // (4 * D_FF)
    ch2 = STAGE_BYTES // (4 * D)
    return pl.pallas_call(
        functools.partial(_pool_mlp_body, T=T, tiles_per_seq=tps, n=n, li=li),
        grid=(n + 1,),
        in_specs=[pl.BlockSpec((None, T, D), tile), _const((ns, D)), _const((1, D)),
                  _const((len(POOL_WINDOWS), POOL_GROUP, POOL_GROUP)), _const((1, D)),
                  _const((1, D)), _const((1, D)), hbm, hbm],
        out_specs=[pl.BlockSpec((None, T, D), prev), pl.BlockSpec((ns, D), lambda s: (0, 0)),
                   pl.BlockSpec((None, POOL_BUF, D), lambda s: (jnp.minimum(s, n - 1) // tps, 0, 0))],
        out_shape=[jax.ShapeDtypeStruct((B, L, D), F32), jax.ShapeDtypeStruct((ns, D), F32),
                   jax.ShapeDtypeStruct((B, POOL_BUF, D), F32)],
        scratch_shapes=[pltpu.VMEM((T + POOL_HALO, D), F32), pltpu.VMEM((2, T, D), F32),
                        pltpu.VMEM((D, D_FF), BF16), pltpu.VMEM((D_FF, D), BF16),
                        pltpu.VMEM((2, ch1, D_FF), F32), pltpu.VMEM((2, ch2, D), F32),
                        pltpu.SemaphoreType.DMA((2,)), pltpu.SemaphoreType.DMA((2,))],
        compiler_params=_params(56, 1),
        name="pool_mlp",
    )(x, xs2d, pg.reshape(1, D), pwb, psc.reshape(1, D), g.reshape(1, D), fg.reshape(1, D), w1, w2)


def _pool_sample_body(x_ref, buf_ref, g_ref, w_ref, sc_ref, o_ref, nbuf_ref, *, pos0):
    x = x_ref[...]
    h = _rms(x, g_ref[...])
    ys = []
    for gi, w in enumerate(POOL_WINDOWS):
        c0, c1 = gi * POOL_GROUP, (gi + 1) * POOL_GROUP
        hg = h[:, c0:c1]
        s = hg
        for k in range(1, w):
            s = s + buf_ref[POOL_BUF - k, :, c0:c1]
        cnt = float(min(pos0 + 1, w))
        d = s / cnt - hg
        ys.append(_dot(d.astype(BF16), w_ref[gi]))
    y = jnp.concatenate(ys, axis=-1) * sc_ref[...]
    o_ref[...] = x + y
    nbuf_ref[0:POOL_BUF - 1] = buf_ref[1:POOL_BUF]
    nbuf_ref[POOL_BUF - 1] = h


def _pool_sample(x2d, buf, g, wb, sc, *, pos0, bt=32):
    n = x2d.shape[0]
    row = pl.BlockSpec((bt, D), lambda i: (i, 0))
    brow = pl.BlockSpec((POOL_BUF, bt, D), lambda i: (0, i, 0))
    return pl.pallas_call(
        functools.partial(_pool_sample_body, pos0=pos0),
        grid=(n // bt,),
        in_specs=[row, brow, _const((1, D)), _const((len(POOL_WINDOWS), POOL_GROUP, POOL_GROUP)), _const((1, D))],
        out_specs=[row, brow],
        out_shape=[jax.ShapeDtypeStruct((n, D), F32), jax.ShapeDtypeStruct((POOL_BUF, n, D), F32)],
        compiler_params=_params(32, 1),
        name="pool_sample",
    )(x2d, buf, g.reshape(1, D), wb, sc.reshape(1, D))


def _gmlp_front(x, g_ref, win_ref, bin_ref, lng_ref, lnb_ref):
    h = _rms(x, g_ref[...]).astype(BF16)
    z = jax.nn.gelu(_dot(h, win_ref[...]) + bin_ref[...])
    u, v = z[:, :D], z[:, D:]
    mu = jnp.mean(v, axis=-1, keepdims=True)
    vc = v - mu
    var = jnp.mean(jnp.square(vc), axis=-1, keepdims=True)
    vn = vc * lax.rsqrt(var + EPS) * lng_ref[...] + lnb_ref[...]
    return u, vn


def _gmlp_prompt_body(x_ref, g_ref, win_ref, bin_ref, lng_ref, lnb_ref, ws_ref, bs_ref, wout_ref,
                      o_ref, *, T):
    ti = lax.broadcasted_iota(jnp.int32, (CHUNK, CHUNK), 0)
    si = lax.broadcasted_iota(jnp.int32, (CHUNK, CHUNK), 1)
    causal = ti >= si
    wsm = [jnp.where(causal, ws_ref[gi], 0.0).astype(BF16) for gi in range(GM_GROUPS)]
    x = x_ref[...]
    u, vn = _gmlp_front(x, g_ref, win_ref, bin_ref, lng_ref, lnb_ref)
    vnb = vn.astype(BF16)
    rows = []
    for n in range(T // CHUNK):
        r0, r1 = n * CHUNK, (n + 1) * CHUNK
        cols = [_dot(wsm[gi], vnb[r0:r1, gi * CHUNK:(gi + 1) * CHUNK]) for gi in range(GM_GROUPS)]
        rows.append(jnp.concatenate(cols, axis=-1) + bs_ref[...])
    mixed = jnp.concatenate(rows, axis=0)
    y = _dot((u * mixed).astype(BF16), wout_ref[...])
    o_ref[...] = x + y


def _gmlp_prompt(x, g, winb, b_in, ln_g, ln_b, w_s, bs_full, woutb, *, T=512):
    B, L, _ = x.shape
    blk = pl.BlockSpec((None, T, D), lambda b, t: (b, t, 0))
    return pl.pallas_call(
        functools.partial(_gmlp_prompt_body, T=T),
        grid=(B, L // T),
        in_specs=[blk, _const((1, D)), _const((D, 2 * D)), _const((1, 2 * D)), _const((1, D)), _const((1, D)),
                  _const((GM_GROUPS, CHUNK, CHUNK)), _const((CHUNK, D)), _const((D, D))],
        out_specs=blk,
        out_shape=jax.ShapeDtypeStruct((B, L, D), F32),
        compiler_params=_params(48, 2),
        name="gmlp_prompt",
    )(x, g.reshape(1, D), winb, b_in.reshape(1, 2 * D), ln_g.reshape(1, D), ln_b.reshape(1, D),
      w_s, bs_full, woutb)


def _gmlp_sample_body(x_ref, g_ref, win_ref, bin_ref, lng_ref, lnb_ref, sw_ref, sb_ref, wout_ref,
                      o_ref, vn_ref):
    x = x_ref[...]
    u, vn = _gmlp_front(x, g_ref, win_ref, bin_ref, lng_ref, lnb_ref)
    mixed = vn * sw_ref[...] + sb_ref[...]
    y = _dot((u * mixed).astype(BF16), wout_ref[...])
    o_ref[...] = x + y
    vn_ref[...] = vn


def _gmlp_sample(x2d, g, winb, b_in, ln_g, ln_b, sw, sb, woutb):
    n = x2d.shape[0]
    full = _const((n, D))
    return pl.pallas_call(
        _gmlp_sample_body,
        grid=(1,),
        in_specs=[full, _const((1, D)), _const((D, 2 * D)), _const((1, 2 * D)), _const((1, D)), _const((1, D)),
                  _const((1, D)), _const((1, D)), _const((D, D))],
        out_specs=[pl.BlockSpec((n, D), lambda i: (0, 0)), pl.BlockSpec((n, D), lambda i: (0, 0))],
        out_shape=[jax.ShapeDtypeStruct((n, D), F32), jax.ShapeDtypeStruct((n, D), F32)],
        compiler_params=_params(32, 1),
        name="gmlp_sample",
    )(x2d, g.reshape(1, D), winb, b_in.reshape(1, 2 * D), ln_g.reshape(1, D), ln_b.reshape(1, D),
      sw, sb, woutb)


def _ret_constants(C):
    log_gamma = np.log1p(-np.exp2(-5.0 - np.arange(HEADS, dtype=np.float64)))
    idx = np.arange(C, dtype=np.float64)
    diff = idx[:, None] - idx[None, :]
    dmask = np.where(diff[None] >= 0, np.exp(log_gamma[:, None, None] * np.maximum(diff, 0.0)[None]), 0.0)
    q_dec = np.exp(log_gamma[:, None] * (idx + 1.0))
    k_dec = np.exp(log_gamma[:, None] * (C - 1.0 - idx))
    chunk_dec = np.exp(log_gamma * C)
    return dmask.astype(np.float32), q_dec.astype(np.float32), k_dec.astype(np.float32), chunk_dec


def _rope_tables(pos):
    half = DK // 2
    freqs = np.exp(-math.log(ROPE_BASE) * np.arange(half, dtype=np.float64) / half)
    ang = np.asarray(pos, dtype=np.float64)[:, None] * freqs[None]
    cos = np.concatenate([np.cos(ang), np.cos(ang)], axis=-1)
    sin = np.concatenate([-np.sin(ang), np.sin(ang)], axis=-1)
    return cos, sin


def _rope(t, cos, sin):
    return t * cos + pltpu.roll(t, DK // 2, axis=1) * sin


def _group_norm_gate(o, gate, gng, gnb):
    mu = jnp.mean(o, axis=-1, keepdims=True)
    oc = o - mu
    var = jnp.mean(jnp.square(oc), axis=-1, keepdims=True)
    on = oc * lax.rsqrt(var + GN_EPS) * gng + gnb
    return jax.nn.silu(gate) * on


def _ret_prompt_body(x_ref, g_ref, win_ref, cq_ref, sq_ref, ck_ref, sk_ref, dm_ref, qd_ref, kd_ref,
                     gng_ref, gnb_ref, wout_ref, o_ref, s_out_ref, gated_scr, s_scr,
                     *, T, chunk_dec):
    t = pl.program_id(1)

    @pl.when(t == 0)
    def _():
        s_scr[...] = jnp.zeros_like(s_scr)

    x = x_ref[...]
    h = _rms(x, g_ref[...]).astype(BF16)
    cq, sq, ck, sk = cq_ref[...], sq_ref[...], ck_ref[...], sk_ref[...]
    n_pairs = HEADS // 2

    def projection(pair, dst):
        def piece(name, lo, width):
            def run():
                dst[name] = _dot(h, win_ref[:, lo:lo + width])
            return run
        v0, g0 = 2 * QK + 2 * DV * pair, 2 * QK + VW + 2 * DV * pair
        return [piece("q", 2 * DK * pair, 2 * DK), piece("k", QK + 2 * DK * pair, 2 * DK),
                piece("v0", v0, DV), piece("v1", v0 + DV, DV),
                piece("g0", g0, DV), piece("g1", g0 + DV, DV)]

    def chunk_work(pair, src):
        state = {}

        def prep(sub):
            def run():
                q = _rope(src["q"][:, sub * DK:(sub + 1) * DK], cq, sq)
                k = _rope(src["k"][:, sub * DK:(sub + 1) * DK], ck, sk)
                state[sub] = dict(q=q, k=k, qb=q.astype(BF16), kb=k.astype(BF16), s=s_scr[2 * pair + sub])
            return run

        def chunk(sub, c):
            def run():
                hd = 2 * pair + sub
                st = state[sub]
                rows = slice(c * CHUNK, (c + 1) * CHUNK)
                vb = src["v%d" % sub][rows, :].astype(BF16)
                scores = lax.dot_general(st["qb"][rows], st["kb"][rows], (((1,), (1,)), ((), ())),
                                         preferred_element_type=F32) * dm_ref[hd]
                o = (_dot(scores.astype(BF16), vb)
                     + _dot((st["q"][rows] * qd_ref[hd]).astype(BF16), st["s"].astype(BF16)))
                kt = (st["k"][rows] * kd_ref[hd]).T.astype(BF16)
                st["s"] = st["s"] * chunk_dec[hd] + _dot(kt, vb)
                gated = _group_norm_gate(o, src["g%d" % sub][rows, :], gng_ref[:, hd * DV:(hd + 1) * DV],
                                         gnb_ref[:, hd * DV:(hd + 1) * DV])
                gated_scr[rows, hd * DV:(hd + 1) * DV] = gated.astype(BF16)
            return run

        def finish(sub):
            def run():
                s_scr[2 * pair + sub] = state[sub]["s"]
            return run

        return ([prep(0), prep(1)] + [chunk(sub, c) for c in range(T // CHUNK) for sub in range(2)]
                + [finish(0), finish(1)])

    acc = [x]

    def out_projection(pair):
        def run():
            cols = slice(pair * 2 * DV, (pair + 1) * 2 * DV)
            acc[0] = acc[0] + _dot(gated_scr[:, cols], wout_ref[cols, :])
        return run

    cur = {}
    for run in projection(0, cur):
        run()
    for pair in range(n_pairs):
        nxt = {}
        if pair + 1 < n_pairs:
            matmul_queue = projection(pair + 1, nxt)
        else:
            matmul_queue = [out_projection(p) for p in range(n_pairs - 1)]
        _interleave(matmul_queue, chunk_work(pair, cur))
        cur = nxt
    out_projection(n_pairs - 1)()
    o_ref[...] = acc[0]
    s_out_ref[...] = s_scr[...]


def _ret_prompt(x, g, winb, gn_g, gn_b, woutb, *, T=512):
    B, L, _ = x.shape
    dmask, q_dec, k_dec, chunk_dec = _ret_constants(CHUNK)
    cos, sin = _rope_tables(np.arange(L))
    scale = DK ** -0.5
    cq, sq = jnp.asarray(cos, F32), jnp.asarray(sin, F32)
    ck, sk = jnp.asarray(cos * scale, F32), jnp.asarray(sin * scale, F32)
    qd = jnp.asarray(np.broadcast_to(q_dec[:, :, None], (HEADS, CHUNK, DK)))
    kd = jnp.asarray(np.broadcast_to(k_dec[:, :, None], (HEADS, CHUNK, DK)))
    blk = pl.BlockSpec((None, T, D), lambda b, t: (b, t, 0))
    tab = pl.BlockSpec((T, DK), lambda b, t: (t, 0))
    hcc = _const((HEADS, CHUNK, CHUNK))
    return pl.pallas_call(
        functools.partial(_ret_prompt_body, T=T, chunk_dec=tuple(float(c) for c in chunk_dec)),
        grid=(B, L // T),
        in_specs=[blk, _const((1, D)), _const((D, 2 * QK + 2 * VW)), tab, tab, tab, tab,
                  hcc, hcc, hcc, _const((1, VW)), _const((1, VW)), _const((VW, D))],
        out_specs=[blk, pl.BlockSpec((None, HEADS, DK, DV), lambda b, t: (b, 0, 0, 0))],
        out_shape=[jax.ShapeDtypeStruct((B, L, D), F32), jax.ShapeDtypeStruct((B, HEADS, DK, DV), F32)],
        scratch_shapes=[pltpu.VMEM((T, VW), BF16), pltpu.VMEM((HEADS, DK, DV), F32)],
        compiler_params=_params(56, 2),
        name="ret_prompt",
    )(x, g.reshape(1, D), winb, cq, sq, ck, sk, jnp.asarray(dmask), qd, kd,
      gn_g.reshape(1, VW), gn_b.reshape(1, VW), woutb)


def _ret_proj_sample_body(x_ref, g_ref, win_ref, cq_ref, sq_ref, ck_ref, sk_ref,
                          q_ref, k_ref, v_ref, gate_ref):
    h = _rms(x_ref[...], g_ref[...]).astype(BF16)
    p = _dot(h, win_ref[...])
    for hd in range(HEADS):
        q_ref[:, hd * DK:(hd + 1) * DK] = _rope(p[:, hd * DK:(hd + 1) * DK], cq_ref[...], sq_ref[...])
        k_ref[:, hd * DK:(hd + 1) * DK] = _rope(p[:, QK + hd * DK:QK + (hd + 1) * DK], ck_ref[...], sk_ref[...])
    v_ref[...] = p[:, 2 * QK:2 * QK + VW]
    gate_ref[...] = p[:, 2 * QK + VW:]


def _ret_proj_sample(x2d, g, winb, *, pos0):
    n = x2d.shape[0]
    cos, sin = _rope_tables(np.array([pos0]))
    scale = DK ** -0.5
    tabs = [jnp.asarray(a, F32) for a in (cos, sin, cos * scale, sin * scale)]
    out = lambda w: pl.BlockSpec((n, w), lambda i: (0, 0))
    return pl.pallas_call(
        _ret_proj_sample_body,
        grid=(1,),
        in_specs=[_const((n, D)), _const((1, D)), _const((D, 2 * QK + 2 * VW))] + [_const((1, DK))] * 4,
        out_specs=[out(QK), out(QK), out(VW), out(VW)],
        out_shape=[jax.ShapeDtypeStruct((n, w), F32) for w in (QK, QK, VW, VW)],
        compiler_params=_params(40, 1),
        name="ret_proj_sample",
    )(x2d, g.reshape(1, D), winb, *tabs)


def _ret_state_sample_body(q_ref, k_ref, v_ref, s_ref, o_ref, s_out_ref, *, bt, gamma):
    pad = jnp.zeros((DK - bt, DK), F32)
    for hd in range(HEADS):
        qh = q_ref[:, hd * DK:(hd + 1) * DK]
        kh = k_ref[:, hd * DK:(hd + 1) * DK]
        qt = jnp.concatenate([qh, pad], axis=0).T
        kt = jnp.concatenate([kh, pad], axis=0).T
        qk = jnp.sum(qh * kh, axis=-1, keepdims=True)
        for b in range(bt):
            v = v_ref[b:b + 1, hd * DV:(hd + 1) * DV]
            s_old = s_ref[b, hd]
            qs = jnp.sum(s_old * (qt[:, b:b + 1] * gamma[hd]), axis=0, keepdims=True)
            o_ref[b:b + 1, hd * DV:(hd + 1) * DV] = qk[b:b + 1, :] * v + qs
            s_out_ref[b, hd] = s_old * gamma[hd] + kt[:, b:b + 1] * v


def _ret_state_sample(q, k, v, s, *, bt=8):
    n = q.shape[0]
    _, _, _, chunk_dec = _ret_constants(1)
    row = lambda w: pl.BlockSpec((bt, w), lambda i: (i, 0))
    sblk = pl.BlockSpec((bt, HEADS, DK, DV), lambda i: (i, 0, 0, 0))
    return pl.pallas_call(
        functools.partial(_ret_state_sample_body, bt=bt, gamma=tuple(float(c) for c in chunk_dec)),
        grid=(n // bt,),
        in_specs=[row(QK), row(QK), row(VW), sblk],
        out_specs=[row(VW), sblk],
        out_shape=[jax.ShapeDtypeStruct((n, VW), F32), jax.ShapeDtypeStruct(s.shape, F32)],
        compiler_params=_params(48, 1),
        name="ret_state_sample",
    )(q, k, v, s)


def _ret_out_sample_body(x_ref, o_ref_in, gate_ref, gng_ref, gnb_ref, wout_ref, o_ref):
    parts = []
    for hd in range(HEADS):
        cols = slice(hd * DV, (hd + 1) * DV)
        parts.append(_group_norm_gate(o_ref_in[:, cols], gate_ref[:, cols], gng_ref[:, cols], gnb_ref[:, cols]))
    gated = jnp.concatenate(parts, axis=-1).astype(BF16)
    o_ref[...] = x_ref[...] + _dot(gated, wout_ref[...])


def _ret_out_sample(x2d, o, gate, gn_g, gn_b, woutb):
    n = x2d.shape[0]
    return pl.pallas_call(
        _ret_out_sample_body,
        grid=(1,),
        in_specs=[_const((n, D)), _const((n, VW)), _const((n, VW)), _const((1, VW)), _const((1, VW)),
                  _const((VW, D))],
        out_specs=pl.BlockSpec((n, D), lambda i: (0, 0)),
        out_shape=jax.ShapeDtypeStruct((n, D), F32),
        compiler_params=_params(32, 1),
        name="ret_out_sample",
    )(x2d, o, gate, gn_g.reshape(1, VW), gn_b.reshape(1, VW), woutb)


LRU_HALO = 8


def _lru_gates(xc, wax_ref, ba, bx, lam, heads):
    xcb = xc.astype(BF16)
    rs, is_ = [], []
    for n, hd in enumerate(heads):
        ri = _dot(xcb[:, n * DK:(n + 1) * DK], wax_ref[hd])
        rs.append(ri[:, :DK])
        is_.append(ri[:, DK:])
    r = jax.nn.sigmoid(jnp.concatenate(rs, axis=-1) + ba)
    i = jax.nn.sigmoid(jnp.concatenate(is_, axis=-1) + bx)
    log_a = -LRU_C * r * _softplus(-lam)
    a = jnp.exp(log_a)
    mult = jnp.sqrt(-jnp.tanh(log_a) * (a * a + 1.0))
    return a, mult, i


LRU_GROUP_HEADS = 2


def _lru_prompt_body(x_ref, g_ref, win_ref, cw_ref, cb_ref, wax_ref, ba_ref, bx_ref, lam_ref, wout_ref,
                     o_ref, conv_ref, hlast_ref, xb_scr, hs_scr, carry_scr, h_scr, *, NB, T):
    t = pl.program_id(0)
    N = NB * T
    PS = T + 1
    halo = (CONV_W - 1) * NB
    gw = LRU_GROUP_HEADS * DK
    gblk = gw // LANES

    @pl.when(t == 0)
    def _():
        carry_scr[...] = jnp.zeros_like(carry_scr)
        h_scr[...] = jnp.zeros_like(h_scr)

    x = x_ref[...].reshape(N, D)
    h = _rms(x, g_ref[...]).astype(BF16)
    row = lax.broadcasted_iota(jnp.int32, (N, 1), 0)
    pos = t * T + lax.shift_right_logical(row, NB.bit_length() - 1)
    out = x
    def project(grp):
        c0, c1 = grp * gw, (grp + 1) * gw
        return _dot(h, win_ref[:, c0:c1]), _dot(h, win_ref[:, D + c0:D + c1])

    ngrp = D // gw
    nxt = project(0)
    for grp in range(ngrp):
        c0, c1 = grp * gw, (grp + 1) * gw
        blks = range(grp * gblk, (grp + 1) * gblk)
        zg, xb = nxt
        if grp + 1 < ngrp:
            nxt = project(grp + 1)
        gate = jax.nn.gelu(zg)
        for n, j in enumerate(blks):
            for b in range(NB):
                xb_scr[j, b * PS:b * PS + T, :] = xb[b * T:(b + 1) * T, n * LANES:(n + 1) * LANES]
        ext = jnp.concatenate(
            [carry_scr[:, c0:c1]]
            + [jnp.concatenate([xb_scr[j, pl.ds(s, NB, stride=PS), :] for j in blks], axis=-1)
               for s in range(T)], axis=0)
        carry_scr[:, c0:c1] = ext[N:N + halo, :]
        acc = ext[0:N, :] * cw_ref[0:1, c0:c1]
        for j in range(1, CONV_W):
            acc = acc + ext[j * NB:j * NB + N, :] * cw_ref[j:j + 1, c0:c1]
        xc = cb_ref[:, c0:c1] + acc
        heads = range(grp * LRU_GROUP_HEADS, (grp + 1) * LRU_GROUP_HEADS)
        a, mult, i = _lru_gates(xc, wax_ref, ba_ref[:, c0:c1], bx_ref[:, c0:c1], lam_ref[:, c0:c1], heads)
        bvec = jnp.where(pos == 0, 1.0, mult) * (i * xc)
        hcur = h_scr[:, c0:c1]
        for s in range(T):
            hcur = a[s * NB:(s + 1) * NB, :] * hcur + bvec[s * NB:(s + 1) * NB, :]
            for n, j in enumerate(blks):
                hs_scr[j, pl.ds(s, NB, stride=PS), :] = hcur[:, n * LANES:(n + 1) * LANES]
        h_scr[:, c0:c1] = hcur
        hs = jnp.concatenate(
            [jnp.concatenate([hs_scr[j, b * PS:b * PS + T, :] for b in range(NB)], axis=0) for j in blks],
            axis=-1)
        out = out + _dot((hs * gate).astype(BF16), wout_ref[c0:c1, :])
    o_ref[...] = out.reshape(NB, T, D)
    hlast_ref[...] = h_scr[...]

    @pl.when(t == pl.num_programs(0) - 1)
    def _():
        for j in range(CONV_W - 1):
            conv_ref[:, j, :] = carry_scr[j * NB:(j + 1) * NB, :]


def _lru_prompt(x, g, winb, conv_w, conv_b, waxb, b_a, b_x, lam, woutb, *, T=64):
    B, L, _ = x.shape
    blk = pl.BlockSpec((B, T, D), lambda t: (0, t, 0))
    vec = _const((1, D))
    return pl.pallas_call(
        functools.partial(_lru_prompt_body, NB=B, T=T),
        grid=(L // T,),
        in_specs=[blk, vec, _const((D, 2 * D)), _const((CONV_W, D)), vec, _const((HEADS, DK, 2 * DK)),
                  vec, vec, vec, _const((D, D))],
        out_specs=[blk, pl.BlockSpec((B, CONV_W - 1, D), lambda t: (0, 0, 0)),
                   pl.BlockSpec((B, D), lambda t: (0, 0))],
        out_shape=[jax.ShapeDtypeStruct((B, L, D), F32), jax.ShapeDtypeStruct((B, CONV_W - 1, D), F32),
                   jax.ShapeDtypeStruct((B, D), F32)],
        scratch_shapes=[pltpu.VMEM((D // LANES, B * (T + 1), LANES), F32)] * 2
        + [pltpu.VMEM(((CONV_W - 1) * B, D), F32), pltpu.VMEM((B, D), F32)],
        compiler_params=_params(48, 1),
        name="lru_prompt",
    )(x, g.reshape(1, D), winb, conv_w, conv_b.reshape(1, D), waxb, b_a.reshape(1, D), b_x.reshape(1, D),
      lam.reshape(1, D), woutb)


def _lru_sample_body(x_ref, cbuf_ref, h0_ref, g_ref, win_ref, cw_ref, cb_ref, wax_ref, ba_ref, bx_ref,
                     lam_ref, wout_ref, o_ref, nconv_ref, hnew_ref, *, pos0):
    x = x_ref[...]
    h = _rms(x, g_ref[...]).astype(BF16)
    z = _dot(h, win_ref[...])
    gate = jax.nn.gelu(z[:, :D])
    xb = z[:, D:]
    cw = cw_ref[...]
    acc = cbuf_ref[0] * cw[0:1, :]
    for j in range(1, CONV_W - 1):
        acc = acc + cbuf_ref[j] * cw[j:j + 1, :]
    acc = acc + xb * cw[CONV_W - 1:CONV_W, :]
    xc = cb_ref[...] + acc
    a, mult, i = _lru_gates(xc, wax_ref, ba_ref[...], bx_ref[...], lam_ref[...], range(HEADS))
    if pos0 == 0:
        mult = jnp.ones_like(mult)
    hnew = a * h0_ref[...] + mult * (i * xc)
    hnew_ref[...] = hnew
    nconv_ref[0:CONV_W - 2] = cbuf_ref[1:CONV_W - 1]
    nconv_ref[CONV_W - 2] = xb
    o_ref[...] = x + _dot((hnew * gate).astype(BF16), wout_ref[...])


def _lru_sample(x2d, cbuf, h0, g, winb, conv_w, conv_b, waxb, b_a, b_x, lam, woutb, *, pos0):
    n = x2d.shape[0]
    vec = _const((1, D))
    cshape = (CONV_W - 1, n, D)
    out = pl.BlockSpec((n, D), lambda i: (0, 0))
    return pl.pallas_call(
        functools.partial(_lru_sample_body, pos0=pos0),
        grid=(1,),
        in_specs=[_const((n, D)), _const(cshape), _const((n, D)), vec, _const((D, 2 * D)), _const((CONV_W, D)),
                  vec, _const((HEADS, DK, 2 * DK)), vec, vec, vec, _const((D, D))],
        out_specs=[out, pl.BlockSpec(cshape, lambda i: (0, 0, 0)), out],
        out_shape=[jax.ShapeDtypeStruct((n, D), F32), jax.ShapeDtypeStruct(cshape, F32),
                   jax.ShapeDtypeStruct((n, D), F32)],
        compiler_params=_params(32, 1),
        name="lru_sample",
    )(x2d, cbuf, h0, g.reshape(1, D), winb, conv_w, conv_b.reshape(1, D), waxb, b_a.reshape(1, D),
      b_x.reshape(1, D), lam.reshape(1, D), woutb)


def kernel(x_prompt, x_sample, state_pool, state_ret, state_conv, state_lru, pool_norm, pool_w, pool_scale, gm_norm, gm_w_in, gm_b_in, gm_ln_g, gm_ln_b, gm_w_s, gm_b_s, gm_w_out, ret_norm, ret_w_in, ret_gn_g, ret_gn_b, ret_w_out, lru_norm, lru_w_in, lru_conv_w, lru_conv_b, lru_w_a, lru_b_a, lru_w_x, lru_b_x, lru_lam, lru_w_out, mlp_norm, mlp_w1, mlp_w2, final_norm):
    B, L, _ = x_prompt.shape
    NS = x_sample.shape[0]
    bf = lambda w: w.astype(BF16)

    def mlp(xp, xs, li, final=False):
        yp, ys = _mlp(xp.reshape(B * L, D), xs, mlp_norm[li], mlp_w1, mlp_w2, final_norm, li=li, final=final)
        return yp.reshape(B, L, D), ys

    pool_wb = bf(pool_w[0])
    xs, pool_s = _pool_sample(x_sample.reshape(NS, D), jnp.swapaxes(state_pool[0], 0, 1),
                              pool_norm[0], pool_wb, pool_scale[0], pos0=PAST_LEN)
    xp, xs, pool_p = _pool_mlp(x_prompt, xs, pool_norm[0], pool_wb, pool_scale[0], mlp_norm[0],
                               mlp_w1, mlp_w2, final_norm, li=0)

    gm_winb, gm_woutb = bf(gm_w_in[0]), bf(gm_w_out[0])
    bs_full = jnp.repeat(gm_b_s[0].T, CHUNK, axis=1)
    xp = _gmlp_prompt(xp, gm_norm[0], gm_winb, gm_b_in[0], gm_ln_g[0], gm_ln_b[0], gm_w_s[0], bs_full, gm_woutb)
    sw = jnp.repeat(gm_w_s[0][:, 0, 0], CHUNK).reshape(1, D)
    sb = jnp.repeat(gm_b_s[0][:, 0], CHUNK).reshape(1, D)
    xs, v_s = _gmlp_sample(xs, gm_norm[0], gm_winb, gm_b_in[0], gm_ln_g[0], gm_ln_b[0], sw, sb, gm_woutb)
    xp, xs = mlp(xp, xs, 1)

    ret_winb, ret_woutb = bf(ret_w_in[0]), bf(ret_w_out[0])
    xp, ret_p = _ret_prompt(xp, ret_norm[0], ret_winb, ret_gn_g[0], ret_gn_b[0], ret_woutb)
    q, k, v, gate = _ret_proj_sample(xs, ret_norm[0], ret_winb, pos0=PAST_LEN)
    o, ret_s = _ret_state_sample(q, k, v, state_ret[0])
    xs = _ret_out_sample(xs, o, gate, ret_gn_g[0], ret_gn_b[0], ret_woutb)
    xp, xs = mlp(xp, xs, 2)

    lru_winb, lru_woutb = bf(lru_w_in[0]), bf(lru_w_out[0])
    waxb = bf(jnp.concatenate([lru_w_a[0], lru_w_x[0]], axis=-1))
    xp, conv_p, lru_p = _lru_prompt(xp, lru_norm[0], lru_winb, lru_conv_w[0], lru_conv_b[0], waxb,
                                    lru_b_a[0], lru_b_x[0], lru_lam[0], lru_woutb)
    xs, conv_s, lru_s = _lru_sample(xs, jnp.swapaxes(state_conv[0], 0, 1), state_lru[0],
                                    lru_norm[0], lru_winb, lru_conv_w[0], lru_conv_b[0], waxb,
                                    lru_b_a[0], lru_b_x[0], lru_lam[0], lru_woutb, pos0=PAST_LEN)
    yp, ys = mlp(xp, xs, 3, final=True)

    return (yp, ys.reshape(NS, 1, D),
            pool_p[None], jnp.swapaxes(pool_s, 0, 1)[None],
            v_s.reshape(1, NS, 1, D),
            ret_p[None], ret_s[None],
            conv_p[None], jnp.swapaxes(conv_s, 0, 1)[None],
            lru_p[None], lru_s[None])
```

```python
import functools
import math

import jax
import jax.numpy as jnp
import numpy as np
from jax import lax
from jax.experimental import pallas as pl
from jax.experimental.pallas import tpu as pltpu

F32 = jnp.float32
BF16 = jnp.bfloat16

D = 1024
EPS = 1e-6
GN_EPS = 1e-5
PAST_LEN = 16384
POOL_WINDOWS = (2, 4, 8, 16)
POOL_GROUP = D // len(POOL_WINDOWS)
POOL_BUF = max(POOL_WINDOWS) - 1
CHUNK = 128
GM_GROUPS = 8
HEADS = 8
DK = D // HEADS
DV = 2 * D // HEADS
QK = HEADS * DK
VW = HEADS * DV
ROPE_BASE = 10000.0
CONV_W = 4
LRU_C = 8.0
D_FF = 4 * D

MIB = 1024 * 1024
SUBLANES = 8
LANES = 128


def _params(vmem_mib, n_grid):
    return pltpu.CompilerParams(
        dimension_semantics=("arbitrary",) * n_grid,
        vmem_limit_bytes=vmem_mib * MIB,
    )


def _const(shape):
    zeros = (0,) * len(shape)
    return pl.BlockSpec(shape, lambda *_: zeros, pipeline_mode=pl.Buffered(1))


def _rms(x, g):
    ms = jnp.mean(x * x, axis=-1, keepdims=True)
    return x * lax.rsqrt(ms + EPS) * g


def _dot(a, b):
    return jnp.dot(a, b, preferred_element_type=F32)


def _cost(thunk, cost):
    thunk.cost = cost
    return thunk


def _interleave(*queues):
    cost = lambda thunk: getattr(thunk, "cost", 1)
    total = [sum(cost(th) for th in q) for q in queues]
    done = [0] * len(queues)
    pos = [0] * len(queues)
    while any(p < len(q) for p, q in zip(pos, queues)):
        _, i = min((done[i] / total[i], i) for i, q in enumerate(queues) if pos[i] < len(q))
        thunk = queues[i][pos[i]]
        thunk()
        done[i] += cost(thunk)
        pos[i] += 1


def _softplus(x):
    return jnp.maximum(x, 0.0) + jnp.log1p(jnp.exp(-jnp.abs(x)))


MLP_FC = 512
STAGE_BYTES = MIB


def _load_cast(src, dst, stg, sem):
    ch = stg.shape[1]
    n = src.shape[0] // ch

    def copy(c, slot):
        return pltpu.make_async_copy(src.at[pl.ds(c * ch, ch), :], stg.at[slot], sem.at[slot])

    copy(0, 0).start()
    for c in range(n):
        slot = c % 2
        if c + 1 < n:
            copy(c + 1, 1 - slot).start()
        copy(c, slot).wait()
        dst[c * ch:(c + 1) * ch, :] = stg[slot].astype(BF16)


def _mlp_thunks(read_x, write_out, g_ref, fg_ref, w1b, w2b, final):
    st = {}

    def head():
        x = read_x()
        st["h"] = _rms(x, g_ref[...]).astype(BF16)
        st["acc"] = x

    def up(c):
        def run():
            a = _dot(st["h"], w1b[:, c * MLP_FC:(c + 1) * MLP_FC])
            st["a"] = jnp.square(jnp.maximum(a, 0.0)).astype(BF16)
        return run

    def down(c):
        def run():
            st["acc"] = st["acc"] + _dot(st["a"], w2b[c * MLP_FC:(c + 1) * MLP_FC, :])
        return run

    def tail():
        acc = st["acc"]
        write_out(_rms(acc, fg_ref[...]) if final else acc)

    return ([_cost(head, 2)] + [_cost(f(c), 4) for c in range(D_FF // MLP_FC) for f in (up, down)]
            + [_cost(tail, 1)])


def _run(thunks):
    for thunk in thunks:
        thunk()


def _store(ref):
    def write(v):
        ref[...] = v
    return write


def _mlp_body(xp_ref, xs_ref, g_ref, fg_ref, w1_hbm, w2_hbm, op_ref, os_ref,
              w1b, w2b, stg1, stg2, sem1, sem2, *, li, final, n_p):
    i = pl.program_id(0)

    @pl.when(i == 0)
    def _():
        _load_cast(w1_hbm.at[li], w1b, stg1, sem1)
        _load_cast(w2_hbm.at[li], w2b, stg2, sem2)

    @pl.when(i < n_p)
    def _():
        _run(_mlp_thunks(lambda: xp_ref[...], _store(op_ref), g_ref, fg_ref, w1b, w2b, final))

    @pl.when(i == n_p)
    def _():
        _run(_mlp_thunks(lambda: xs_ref[...], _store(os_ref), g_ref, fg_ref, w1b, w2b, final))


def _mlp(xp2d, xs2d, g, w1, w2, fg, *, li, final, tm=512):
    n, ns = xp2d.shape[0], xs2d.shape[0]
    n_p = n // tm
    row = pl.BlockSpec((tm, D), lambda i: (jnp.minimum(i, n_p - 1), 0))
    srow = pl.BlockSpec((ns, D), lambda i: (0, 0))
    hbm = pl.BlockSpec(memory_space=pl.ANY)
    ch1 = STAGE_BYTES // (4 * D_FF)
    ch2 = STAGE_BYTES // (4 * D)
    return pl.pallas_call(
        functools.partial(_mlp_body, li=li, final=final, n_p=n_p),
        grid=(n_p + 1,),
        in_specs=[row, _const((ns, D)), _const((1, D)), _const((1, D)), hbm, hbm],
        out_specs=[row, srow],
        out_shape=[jax.ShapeDtypeStruct((n, D), F32), jax.ShapeDtypeStruct((ns, D), F32)],
        scratch_shapes=[pltpu.VMEM((D, D_FF), BF16), pltpu.VMEM((D_FF, D), BF16),
                        pltpu.VMEM((2, ch1, D_FF), F32), pltpu.VMEM((2, ch2, D), F32),
                        pltpu.SemaphoreType.DMA((2,)), pltpu.SemaphoreType.DMA((2,))],
        compiler_params=_params(48, 1),
        name="mlp",
    )(xp2d, xs2d, g.reshape(1, D), fg.reshape(1, D), w1, w2)


POOL_HALO = 16


def _pool_thunks(x_ref, g_ref, w_ref, sc_ref, write_out, buf_ref, ext_ref, *, T, t):
    st = {}

    def head():
        ext_ref[0:POOL_HALO, :] = jnp.where(t == 0, 0.0, ext_ref[0:POOL_HALO, :])
        x = x_ref[...]
        h = _rms(x, g_ref[...])
        ext_ref[POOL_HALO:POOL_HALO + T, :] = h
        st["x"], st["h"] = x, h
        st["pos"] = t * T + lax.broadcasted_iota(jnp.int32, (T, 1), 0)
        st["y"] = [None] * len(POOL_WINDOWS)

    def group(gi, w):
        def run():
            c0, c1 = gi * POOL_GROUP, (gi + 1) * POOL_GROUP
            hg = st["h"][:, c0:c1]
            s = hg
            for k in range(1, w):
                s = s + ext_ref[POOL_HALO - k:POOL_HALO - k + T, c0:c1]
            cnt = jnp.minimum(st["pos"] + 1, w).astype(F32)
            d = s / cnt - hg
            st["y"][gi] = _dot(d.astype(BF16), w_ref[gi])
        return run

    def tail():
        y = jnp.concatenate(st["y"], axis=-1) * sc_ref[...]
        write_out(st["x"] + y)
        last = ext_ref[T:T + POOL_HALO, :]
        ext_ref[0:POOL_HALO, :] = last
        buf_ref[...] = last[POOL_HALO - POOL_BUF:, :]

    return [_cost(th, 3) for th in [head] + [group(gi, w) for gi, w in enumerate(POOL_WINDOWS)] + [tail]]


def _pool_mlp_body(x_ref, xs_ref, pg_ref, pw_ref, psc_ref, g_ref, fg_ref, w1_hbm, w2_hbm,
                   o_ref, os_ref, buf_ref, ext_ref, x1_scr, w1b, w2b, stg1, stg2, sem1, sem2,
                   *, T, tiles_per_seq, n, li):
    s = pl.program_id(0)
    t = lax.rem(s, tiles_per_seq)
    slot = lax.rem(s, 2)

    @pl.when(s == 0)
    def _():
        ext_ref[0:POOL_HALO, :] = jnp.zeros((POOL_HALO, D), F32)
        _load_cast(w1_hbm.at[li], w1b, stg1, sem1)
        _load_cast(w2_hbm.at[li], w2b, stg2, sem2)

    def write_x1(v):
        x1_scr[slot] = v

    def mixer():
        return _pool_thunks(x_ref, pg_ref, pw_ref, psc_ref, write_x1, buf_ref, ext_ref, T=T, t=t)

    def mlp():
        return _mlp_thunks(lambda: x1_scr[1 - slot], _store(o_ref), g_ref, fg_ref, w1b, w2b, False)

    @pl.when(s == 0)
    def _():
        _run(mixer())

    @pl.when((s > 0) & (s < n))
    def _():
        _interleave(mlp(), mixer())

    @pl.when(s == n)
    def _():
        _run(mlp())
        _run(_mlp_thunks(lambda: xs_ref[...], _store(os_ref), g_ref, fg_ref, w1b, w2b, False))


def _pool_mlp(x, xs2d, pg, pwb, psc, g, w1, w2, fg, *, li, T=512):
    B, L, _ = x.shape
    ns = xs2d.shape[0]
    tps = L // T
    n = B * tps
    tile = lambda s: (jnp.minimum(s, n - 1) // tps, lax.rem(jnp.minimum(s, n - 1), tps), 0)
    prev = lambda s: (jnp.maximum(s - 1, 0) // tps, lax.rem(jnp.maximum(s - 1, 0), tps), 0)
    hbm = pl.BlockSpec(memory_space=pl.ANY)
    ch1 = STAGE_BYTES // (4 * D_FF)
    ch2 = STAGE_BYTES // (4 * D)
    return pl.pallas_call(
        functools.partial(_pool_mlp_body, T=T, tiles_per_seq=tps, n=n, li=li),
        grid=(n + 1,),
        in_specs=[pl.BlockSpec((None, T, D), tile), _const((ns, D)), _const((1, D)),
                  _const((len(POOL_WINDOWS), POOL_GROUP, POOL_GROUP)), _const((1, D)),
                  _const((1, D)), _const((1, D)), hbm, hbm],
        out_specs=[pl.BlockSpec((None, T, D), prev), pl.BlockSpec((ns, D), lambda s: (0, 0)),
                   pl.BlockSpec((None, POOL_BUF, D), lambda s: (jnp.minimum(s, n - 1) // tps, 0, 0))],
        out_shape=[jax.ShapeDtypeStruct((B, L, D), F32), jax.ShapeDtypeStruct((ns, D), F32),
                   jax.ShapeDtypeStruct((B, POOL_BUF, D), F32)],
        scratch_shapes=[pltpu.VMEM((T + POOL_HALO, D), F32), pltpu.VMEM((2, T, D), F32),
                        pltpu.VMEM((D, D_FF), BF16), pltpu.VMEM((D_FF, D), BF16),
                        pltpu.VMEM((2, ch1, D_FF), F32), pltpu.VMEM((2, ch2, D), F32),
                        pltpu.SemaphoreType.DMA((2,)), pltpu.SemaphoreType.DMA((2,))],
        compiler_params=_params(56, 1),
        name="pool_mlp",
    )(x, xs2d, pg.reshape(1, D), pwb, psc.reshape(1, D), g.reshape(1, D), fg.reshape(1, D), w1, w2)


def _pool_sample_body(x_ref, buf_ref, g_ref, w_ref, sc_ref, o_ref, nbuf_ref, *, pos0):
    x = x_ref[...]
    h = _rms(x, g_ref[...])
    ys = []
    for gi, w in enumerate(POOL_WINDOWS):
        c0, c1 = gi * POOL_GROUP, (gi + 1) * POOL_GROUP
        hg = h[:, c0:c1]
        s = hg
        for k in range(1, w):
            s = s + buf_ref[POOL_BUF - k, :, c0:c1]
        cnt = float(min(pos0 + 1, w))
        d = s / cnt - hg
        ys.append(_dot(d.astype(BF16), w_ref[gi]))
    y = jnp.concatenate(ys, axis=-1) * sc_ref[...]
    o_ref[...] = x + y
    nbuf_ref[0:POOL_BUF - 1] = buf_ref[1:POOL_BUF]
    nbuf_ref[POOL_BUF - 1] = h


def _pool_sample(x2d, buf, g, wb, sc, *, pos0, bt=32):
    n = x2d.shape[0]
    row = pl.BlockSpec((bt, D), lambda i: (i, 0))
    brow = pl.BlockSpec((POOL_BUF, bt, D), lambda i: (0, i, 0))
    return pl.pallas_call(
        functools.partial(_pool_sample_body, pos0=pos0),
        grid=(n // bt,),
        in_specs=[row, brow, _const((1, D)), _const((len(POOL_WINDOWS), POOL_GROUP, POOL_GROUP)), _const((1, D))],
        out_specs=[row, brow],
        out_shape=[jax.ShapeDtypeStruct((n, D), F32), jax.ShapeDtypeStruct((POOL_BUF, n, D), F32)],
        compiler_params=_params(32, 1),
        name="pool_sample",
    )(x2d, buf, g.reshape(1, D), wb, sc.reshape(1, D))


def _gmlp_front(x, g_ref, win_ref, bin_ref, lng_ref, lnb_ref):
    h = _rms(x, g_ref[...]).astype(BF16)
    z = jax.nn.gelu(_dot(h, win_ref[...]) + bin_ref[...])
    u, v = z[:, :D], z[:, D:]
    mu = jnp.mean(v, axis=-1, keepdims=True)
    vc = v - mu
    var = jnp.mean(jnp.square(vc), axis=-1, keepdims=True)
    vn = vc * lax.rsqrt(var + EPS) * lng_ref[...] + lnb_ref[...]
    return u, vn


def _gmlp_prompt_body(x_ref, g_ref, win_ref, bin_ref, lng_ref, lnb_ref, ws_ref, bs_ref, wout_ref,
                      o_ref, *, T):
    ti = lax.broadcasted_iota(jnp.int32, (CHUNK, CHUNK), 0)
    si = lax.broadcasted_iota(jnp.int32, (CHUNK, CHUNK), 1)
    causal = ti >= si
    wsm = [jnp.where(causal, ws_ref[gi], 0.0).astype(BF16) for gi in range(GM_GROUPS)]
    x = x_ref[...]
    u, vn = _gmlp_front(x, g_ref, win_ref, bin_ref, lng_ref, lnb_ref)
    vnb = vn.astype(BF16)
    rows = []
    for n in range(T // CHUNK):
        r0, r1 = n * CHUNK, (n + 1) * CHUNK
        cols = [_dot(wsm[gi], vnb[r0:r1, gi * CHUNK:(gi + 1) * CHUNK]) for gi in range(GM_GROUPS)]
        rows.append(jnp.concatenate(cols, axis=-1) + bs_ref[...])
    mixed = jnp.concatenate(rows, axis=0)
    y = _dot((u * mixed).astype(BF16), wout_ref[...])
    o_ref[...] = x + y


def _gmlp_prompt(x, g, winb, b_in, ln_g, ln_b, w_s, bs_full, woutb, *, T=512):
    B, L, _ = x.shape
    blk = pl.BlockSpec((None, T, D), lambda b, t: (b, t, 0))
    return pl.pallas_call(
        functools.partial(_gmlp_prompt_body, T=T),
        grid=(B, L // T),
        in_specs=[blk, _const((1, D)), _const((D, 2 * D)), _const((1, 2 * D)), _const((1, D)), _const((1, D)),
                  _const((GM_GROUPS, CHUNK, CHUNK)), _const((CHUNK, D)), _const((D, D))],
        out_specs=blk,
        out_shape=jax.ShapeDtypeStruct((B, L, D), F32),
        compiler_params=_params(48, 2),
        name="gmlp_prompt",
    )(x, g.reshape(1, D), winb, b_in.reshape(1, 2 * D), ln_g.reshape(1, D), ln_b.reshape(1, D),
      w_s, bs_full, woutb)


def _gmlp_sample_body(x_ref, g_ref, win_ref, bin_ref, lng_ref, lnb_ref, sw_ref, sb_ref, wout_ref,
                      o_ref, vn_ref):
    x = x_ref[...]
    u, vn = _gmlp_front(x, g_ref, win_ref, bin_ref, lng_ref, lnb_ref)
    mixed = vn * sw_ref[...] + sb_ref[...]
    y = _dot((u * mixed).astype(BF16), wout_ref[...])
    o_ref[...] = x + y
    vn_ref[...] = vn


def _gmlp_sample(x2d, g, winb, b_in, ln_g, ln_b, sw, sb, woutb):
    n = x2d.shape[0]
    full = _const((n, D))
    return pl.pallas_call(
        _gmlp_sample_body,
        grid=(1,),
        in_specs=[full, _const((1, D)), _const((D, 2 * D)), _const((1, 2 * D)), _const((1, D)), _const((1, D)),
                  _const((1, D)), _const((1, D)), _const((D, D))],
        out_specs=[pl.BlockSpec((n, D), lambda i: (0, 0)), pl.BlockSpec((n, D), lambda i: (0, 0))],
        out_shape=[jax.ShapeDtypeStruct((n, D), F32), jax.ShapeDtypeStruct((n, D), F32)],
        compiler_params=_params(32, 1),
        name="gmlp_sample",
    )(x2d, g.reshape(1, D), winb, b_in.reshape(1, 2 * D), ln_g.reshape(1, D), ln_b.reshape(1, D),
      sw, sb, woutb)


def _ret_constants(C):
    log_gamma = np.log1p(-np.exp2(-5.0 - np.arange(HEADS, dtype=np.float64)))
    idx = np.arange(C, dtype=np.float64)
    diff = idx[:, None] - idx[None, :]
    dmask = np.where(diff[None] >= 0, np.exp(log_gamma[:, None, None] * np.maximum(diff, 0.0)[None]), 0.0)
    q_dec = np.exp(log_gamma[:, None] * (idx + 1.0))
    k_dec = np.exp(log_gamma[:, None] * (C - 1.0 - idx))
    chunk_dec = np.exp(log_gamma * C)
    return dmask.astype(np.float32), q_dec.astype(np.float32), k_dec.astype(np.float32), chunk_dec


def _rope_tables(pos):
    half = DK // 2
    freqs = np.exp(-math.log(ROPE_BASE) * np.arange(half, dtype=np.float64) / half)
    ang = np.asarray(pos, dtype=np.float64)[:, None] * freqs[None]
    cos = np.concatenate([np.cos(ang), np.cos(ang)], axis=-1)
    sin = np.concatenate([-np.sin(ang), np.sin(ang)], axis=-1)
    return cos, sin


def _rope(t, cos, sin):
    return t * cos + pltpu.roll(t, DK // 2, axis=1) * sin


def _group_norm_gate(o, gate, gng, gnb):
    mu = jnp.mean(o, axis=-1, keepdims=True)
    oc = o - mu
    var = jnp.mean(jnp.square(oc), axis=-1, keepdims=True)
    on = oc * lax.rsqrt(var + GN_EPS) * gng + gnb
    return jax.nn.silu(gate) * on


def _ret_prompt_body(x_ref, g_ref, win_ref, cq_ref, sq_ref, ck_ref, sk_ref, dm_ref, qd_ref, kd_ref,
                     gng_ref, gnb_ref, wout_ref, o_ref, s_out_ref, gated_scr, s_scr,
                     *, T, chunk_dec):
    t = pl.program_id(1)

    @pl.when(t == 0)
    def _():
        s_scr[...] = jnp.zeros_like(s_scr)

    x = x_ref[...]
    h = _rms(x, g_ref[...]).astype(BF16)
    cq, sq, ck, sk = cq_ref[...], sq_ref[...], ck_ref[...], sk_ref[...]
    n_pairs = HEADS // 2

    def projection(pair, dst):
        def piece(name, lo, width):
            def run():
                dst[name] = _dot(h, win_ref[:, lo:lo + width])
            return run
        v0, g0 = 2 * QK + 2 * DV * pair, 2 * QK + VW + 2 * DV * pair
        return [piece("q", 2 * DK * pair, 2 * DK), piece("k", QK + 2 * DK * pair, 2 * DK),
                piece("v0", v0, DV), piece("v1", v0 + DV, DV),
                piece("g0", g0, DV), piece("g1", g0 + DV, DV)]

    def chunk_work(pair, src):
        state = {}

        def prep(sub):
            def run():
                q = _rope(src["q"][:, sub * DK:(sub + 1) * DK], cq, sq)
                k = _rope(src["k"][:, sub * DK:(sub + 1) * DK], ck, sk)
                state[sub] = dict(q=q, k=k, qb=q.astype(BF16), kb=k.astype(BF16), s=s_scr[2 * pair + sub])
            return run

        def chunk(sub, c):
            def run():
                hd = 2 * pair + sub
                st = state[sub]
                rows = slice(c * CHUNK, (c + 1) * CHUNK)
                vb = src["v%d" % sub][rows, :].astype(BF16)
                scores = lax.dot_general(st["qb"][rows], st["kb"][rows], (((1,), (1,)), ((), ())),
                                         preferred_element_type=F32) * dm_ref[hd]
                o = (_dot(scores.astype(BF16), vb)
                     + _dot((st["q"][rows] * qd_ref[hd]).astype(BF16), st["s"].astype(BF16)))
                kt = (st["k"][rows] * kd_ref[hd]).T.astype(BF16)
                st["s"] = st["s"] * chunk_dec[hd] + _dot(kt, vb)
                gated = _group_norm_gate(o, src["g%d" % sub][rows, :], gng_ref[:, hd * DV:(hd + 1) * DV],
                                         gnb_ref[:, hd * DV:(hd + 1) * DV])
                gated_scr[rows, hd * DV:(hd + 1) * DV] = gated.astype(BF16)
            return run

        def finish(sub):
            def run():
                s_scr[2 * pair + sub] = state[sub]["s"]
            return run

        return ([prep(0), prep(1)] + [chunk(sub, c) for c in range(T // CHUNK) for sub in range(2)]
                + [finish(0), finish(1)])

    acc = [x]

    def out_projection(pair):
        def run():
            cols = slice(pair * 2 * DV, (pair + 1) * 2 * DV)
            acc[0] = acc[0] + _dot(gated_scr[:, cols], wout_ref[cols, :])
        return run

    cur = {}
    for run in projection(0, cur):
        run()
    for pair in range(n_pairs):
        nxt = {}
        if pair + 1 < n_pairs:
            matmul_queue = projection(pair + 1, nxt)
        else:
            matmul_queue = [out_projection(p) for p in range(n_pairs - 1)]
        _interleave(matmul_queue, chunk_work(pair, cur))
        cur = nxt
    out_projection(n_pairs - 1)()
    o_ref[...] = acc[0]
    s_out_ref[...] = s_scr[...]


def _ret_prompt(x, g, winb, gn_g, gn_b, woutb, *, T=512):
    B, L, _ = x.shape
    dmask, q_dec, k_dec, chunk_dec = _ret_constants(CHUNK)
    cos, sin = _rope_tables(np.arange(L))
    scale = DK ** -0.5
    cq, sq = jnp.asarray(cos, F32), jnp.asarray(sin, F32)
    ck, sk = jnp.asarray(cos * scale, F32), jnp.asarray(sin * scale, F32)
    qd = jnp.asarray(np.broadcast_to(q_dec[:, :, None], (HEADS, CHUNK, DK)))
    kd = jnp.asarray(np.broadcast_to(k_dec[:, :, None], (HEADS, CHUNK, DK)))
    blk = pl.BlockSpec((None, T, D), lambda b, t: (b, t, 0))
    tab = pl.BlockSpec((T, DK), lambda b, t: (t, 0))
    hcc = _const((HEADS, CHUNK, CHUNK))
    return pl.pallas_call(
        functools.partial(_ret_prompt_body, T=T, chunk_dec=tuple(float(c) for c in chunk_dec)),
        grid=(B, L // T),
        in_specs=[blk, _const((1, D)), _const((D, 2 * QK + 2 * VW)), tab, tab, tab, tab,
                  hcc, hcc, hcc, _const((1, VW)), _const((1, VW)), _const((VW, D))],
        out_specs=[blk, pl.BlockSpec((None, HEADS, DK, DV), lambda b, t: (b, 0, 0, 0))],
        out_shape=[jax.ShapeDtypeStruct((B, L, D), F32), jax.ShapeDtypeStruct((B, HEADS, DK, DV), F32)],
        scratch_shapes=[pltpu.VMEM((T, VW), BF16), pltpu.VMEM((HEADS, DK, DV), F32)],
        compiler_params=_params(56, 2),
        name="ret_prompt",
    )(x, g.reshape(1, D), winb, cq, sq, ck, sk, jnp.asarray(dmask), qd, kd,
      gn_g.reshape(1, VW), gn_b.reshape(1, VW), woutb)


def _ret_proj_sample_body(x_ref, g_ref, win_ref, cq_ref, sq_ref, ck_ref, sk_ref,
                          q_ref, k_ref, v_ref, gate_ref):
    h = _rms(x_ref[...], g_ref[...]).astype(BF16)
    p = _dot(h, win_ref[...])
    for hd in range(HEADS):
        q_ref[:, hd * DK:(hd + 1) * DK] = _rope(p[:, hd * DK:(hd + 1) * DK], cq_ref[...], sq_ref[...])
        k_ref[:, hd * DK:(hd + 1) * DK] = _rope(p[:, QK + hd * DK:QK + (hd + 1) * DK], ck_ref[...], sk_ref[...])
    v_ref[...] = p[:, 2 * QK:2 * QK + VW]
    gate_ref[...] = p[:, 2 * QK + VW:]


def _ret_proj_sample(x2d, g, winb, *, pos0):
    n = x2d.shape[0]
    cos, sin = _rope_tables(np.array([pos0]))
    scale = DK ** -0.5
    tabs = [jnp.asarray(a, F32) for a in (cos, sin, cos * scale, sin * scale)]
    out = lambda w: pl.BlockSpec((n, w), lambda i: (0, 0))
    return pl.pallas_call(
        _ret_proj_sample_body,
        grid=(1,),
        in_specs=[_const((n, D)), _const((1, D)), _const((D, 2 * QK + 2 * VW))] + [_const((1, DK))] * 4,
        out_specs=[out(QK), out(QK), out(VW), out(VW)],
        out_shape=[jax.ShapeDtypeStruct((n, w), F32) for w in (QK, QK, VW, VW)],
        compiler_params=_params(40, 1),
        name="ret_proj_sample",
    )(x2d, g.reshape(1, D), winb, *tabs)


def _ret_state_sample_body(q_ref, k_ref, v_ref, s_ref, o_ref, s_out_ref, *, bt, gamma):
    pad = jnp.zeros((DK - bt, DK), F32)
    for hd in range(HEADS):
        qh = q_ref[:, hd * DK:(hd + 1) * DK]
        kh = k_ref[:, hd * DK:(hd + 1) * DK]
        qt = jnp.concatenate([qh, pad], axis=0).T
        kt = jnp.concatenate([kh, pad], axis=0).T
        qk = jnp.sum(qh * kh, axis=-1, keepdims=True)
        for b in range(bt):
            v = v_ref[b:b + 1, hd * DV:(hd + 1) * DV]
            s_old = s_ref[b, hd]
            qs = jnp.sum(s_old * (qt[:, b:b + 1] * gamma[hd]), axis=0, keepdims=True)
            o_ref[b:b + 1, hd * DV:(hd + 1) * DV] = qk[b:b + 1, :] * v + qs
            s_out_ref[b, hd] = s_old * gamma[hd] + kt[:, b:b + 1] * v


def _ret_state_sample(q, k, v, s, *, bt=8):
    n = q.shape[0]
    _, _, _, chunk_dec = _ret_constants(1)
    row = lambda w: pl.BlockSpec((bt, w), lambda i: (i, 0))
    sblk = pl.BlockSpec((bt, HEADS, DK, DV), lambda i: (i, 0, 0, 0))
    return pl.pallas_call(
        functools.partial(_ret_state_sample_body, bt=bt, gamma=tuple(float(c) for c in chunk_dec)),
        grid=(n // bt,),
        in_specs=[row(QK), row(QK), row(VW), sblk],
        out_specs=[row(VW), sblk],
        out_shape=[jax.ShapeDtypeStruct((n, VW), F32), jax.ShapeDtypeStruct(s.shape, F32)],
        compiler_params=_params(48, 1),
        name="ret_state_sample",
    )(q, k, v, s)


def _ret_out_sample_body(x_ref, o_ref_in, gate_ref, gng_ref, gnb_ref, wout_ref, o_ref):
    parts = []
    for hd in range(HEADS):
        cols = slice(hd * DV, (hd + 1) * DV)
        parts.append(_group_norm_gate(o_ref_in[:, cols], gate_ref[:, cols], gng_ref[:, cols], gnb_ref[:, cols]))
    gated = jnp.concatenate(parts, axis=-1).astype(BF16)
    o_ref[...] = x_ref[...] + _dot(gated, wout_ref[...])


def _ret_out_sample(x2d, o, gate, gn_g, gn_b, woutb):
    n = x2d.shape[0]
    return pl.pallas_call(
        _ret_out_sample_body,
        grid=(1,),
        in_specs=[_const((n, D)), _const((n, VW)), _const((n, VW)), _const((1, VW)), _const((1, VW)),
                  _const((VW, D))],
        out_specs=pl.BlockSpec((n, D), lambda i: (0, 0)),
        out_shape=jax.ShapeDtypeStruct((n, D), F32),
        compiler_params=_params(32, 1),
        name="ret_out_sample",
    )(x2d, o, gate, gn_g.reshape(1, VW), gn_b.reshape(1, VW), woutb)


LRU_HALO = 8


def _lru_gates(xc, wax_ref, ba, bx, lam, heads):
    xcb = xc.astype(BF16)
    rs, is_ = [], []
    for n, hd in enumerate(heads):
        ri = _dot(xcb[:, n * DK:(n + 1) * DK], wax_ref[hd])
        rs.append(ri[:, :DK])
        is_.append(ri[:, DK:])
    r = jax.nn.sigmoid(jnp.concatenate(rs, axis=-1) + ba)
    i = jax.nn.sigmoid(jnp.concatenate(is_, axis=-1) + bx)
    log_a = -LRU_C * r * _softplus(-lam)
    a = jnp.exp(log_a)
    mult = jnp.sqrt(-jnp.tanh(log_a) * (a * a + 1.0))
    return a, mult, i


LRU_GROUP_HEADS = 2


def _lru_thunks(x_ref, g_ref, win_ref, cw_ref, cb_ref, wax_ref, ba_ref, bx_ref, lam_ref, wout_ref,
                write_out, xb_scr, hs_scr, carry_scr, h_scr, *, NB, T, t):
    N = NB * T
    PS = T + 1
    halo = (CONV_W - 1) * NB
    gw = LRU_GROUP_HEADS * DK
    gblk = gw // LANES
    ngrp = D // gw
    st = {}

    def head():
        x = x_ref[...].reshape(N, D)
        st["h"] = _rms(x, g_ref[...]).astype(BF16)
        row = lax.broadcasted_iota(jnp.int32, (N, 1), 0)
        st["pos"] = t * T + lax.shift_right_logical(row, NB.bit_length() - 1)
        st["out"] = x

    def project(grp, branch):
        def run():
            lo = branch * D + grp * gw
            st["z", branch, grp] = _dot(st["h"], win_ref[:, lo:lo + gw])
        return run

    def gate_and_stage(grp, n):
        def run():
            j = grp * gblk + n
            lanes = slice(n * LANES, (n + 1) * LANES)
            st["gate", j] = jax.nn.gelu(st["z", 0, grp][:, lanes])
            xb = st["z", 1, grp][:, lanes]
            for b in range(NB):
                xb_scr[j, b * PS:b * PS + T, :] = xb[b * T:(b + 1) * T, :]
        return run

    def conv(j):
        def run():
            cols = slice(j * LANES, (j + 1) * LANES)
            ext = jnp.concatenate(
                [carry_scr[:, cols]] + [xb_scr[j, pl.ds(s, NB, stride=PS), :] for s in range(T)], axis=0)
            carry_scr[:, cols] = ext[N:N + halo, :]
            acc = ext[0:N, :] * cw_ref[0:1, cols]
            for k in range(1, CONV_W):
                acc = acc + ext[k * NB:k * NB + N, :] * cw_ref[k:k + 1, cols]
            st["xc", j] = cb_ref[:, cols] + acc
        return run

    def gates(j):
        def run():
            cols = slice(j * LANES, (j + 1) * LANES)
            xc = st.pop(("xc", j))
            a, mult, i = _lru_gates(xc, wax_ref, ba_ref[:, cols], bx_ref[:, cols], lam_ref[:, cols], [j])
            st["ab", j] = (a, jnp.where(st["pos"] == 0, 1.0, mult) * (i * xc))
        return run

    def scan(j):
        def run():
            cols = slice(j * LANES, (j + 1) * LANES)
            a, bvec = st.pop(("ab", j))
            hcur = h_scr[:, cols]
            for s in range(T):
                hcur = a[s * NB:(s + 1) * NB, :] * hcur + bvec[s * NB:(s + 1) * NB, :]
                hs_scr[j, pl.ds(s, NB, stride=PS), :] = hcur
            h_scr[:, cols] = hcur
        return run

    def gated(j):
        def run():
            hs = jnp.concatenate([hs_scr[j, b * PS:b * PS + T, :] for b in range(NB)], axis=0)
            st["hg", j] = (hs * st.pop(("gate", j))).astype(BF16)
        return run

    def project_out(grp):
        def run():
            c0, c1 = grp * gw, (grp + 1) * gw
            hg = jnp.concatenate([st.pop(("hg", j)) for j in range(grp * gblk, (grp + 1) * gblk)], axis=1)
            st["out"] = st["out"] + _dot(hg, wout_ref[c0:c1, :])
        return run

    def tail():
        write_out(st["out"])

    queue = [_cost(head, 2), _cost(project(0, 0), 2), _cost(project(0, 1), 2)]
    for grp in range(ngrp):
        if grp + 1 < ngrp:
            queue += [_cost(project(grp + 1, 0), 2), _cost(project(grp + 1, 1), 2)]
        blks = range(grp * gblk, (grp + 1) * gblk)
        queue += [gate_and_stage(grp, n) for n in range(gblk)]
        for stage, cost in ((conv, 1), (gates, 3), (scan, 1.5), (gated, 0.5)):
            queue += [_cost(stage(j), cost) for j in blks]
        queue.append(_cost(project_out(grp), 2))
    return queue + [_cost(tail, 0.5)]


def _lru_mlp_body(x_ref, xs_ref, lg_ref, win_ref, cw_ref, cb_ref, wax_ref, ba_ref, bx_ref, lam_ref, wout_ref,
                  g_ref, fg_ref, w1_hbm, w2_hbm, o_ref, os_ref, conv_ref, hlast_ref,
                  xb_scr, hs_scr, carry_scr, h_scr, x1_scr, w1b, w2b, stg1, stg2, sem1, sem2,
                  *, NB, T, n, li, final):
    s = pl.program_id(0)
    slot = lax.rem(s, 2)

    @pl.when(s == 0)
    def _():
        carry_scr[...] = jnp.zeros_like(carry_scr)
        h_scr[...] = jnp.zeros_like(h_scr)
        _load_cast(w1_hbm.at[li], w1b, stg1, sem1)
        _load_cast(w2_hbm.at[li], w2b, stg2, sem2)

    def write_x1(v):
        x1_scr[slot] = v

    def write_o(v):
        o_ref[...] = v.reshape(NB, T, D)

    def mixer():
        return _lru_thunks(x_ref, lg_ref, win_ref, cw_ref, cb_ref, wax_ref, ba_ref, bx_ref, lam_ref, wout_ref,
                           write_x1, xb_scr, hs_scr, carry_scr, h_scr, NB=NB, T=T, t=s)

    def mlp():
        return _mlp_thunks(lambda: x1_scr[1 - slot], write_o, g_ref, fg_ref, w1b, w2b, final)

    @pl.when(s == 0)
    def _():
        _run(mixer())

    @pl.when((s > 0) & (s < n))
    def _():
        _interleave(mlp(), mixer())

    @pl.when(s == n)
    def _():
        _run(mlp())
        _run(_mlp_thunks(lambda: xs_ref[...], _store(os_ref), g_ref, fg_ref, w1b, w2b, final))
        hlast_ref[...] = h_scr[...]
        for j in range(CONV_W - 1):
            conv_ref[:, j, :] = carry_scr[j * NB:(j + 1) * NB, :]


def _lru_mlp(x, xs2d, lg, winb, conv_w, conv_b, waxb, b_a, b_x, lam, woutb, g, w1, w2, fg, *, li, final, T=64):
    B, L, _ = x.shape
    ns = xs2d.shape[0]
    n = L // T
    vec = _const((1, D))
    hbm = pl.BlockSpec(memory_space=pl.ANY)
    ch1 = STAGE_BYTES // (4 * D_FF)
    ch2 = STAGE_BYTES // (4 * D)
    return pl.pallas_call(
        functools.partial(_lru_mlp_body, NB=B, T=T, n=n, li=li, final=final),
        grid=(n + 1,),
        in_specs=[pl.BlockSpec((B, T, D), lambda s: (0, jnp.minimum(s, n - 1), 0)), _const((ns, D)),
                  vec, _const((D, 2 * D)), _const((CONV_W, D)), vec, _const((HEADS, DK, 2 * DK)),
                  vec, vec, vec, _const((D, D)), vec, vec, hbm, hbm],
        out_specs=[pl.BlockSpec((B, T, D), lambda s: (0, jnp.maximum(s - 1, 0), 0)),
                   pl.BlockSpec((ns, D), lambda s: (0, 0)),
                   pl.BlockSpec((B, CONV_W - 1, D), lambda s: (0, 0, 0)),
                   pl.BlockSpec((B, D), lambda s: (0, 0))],
        out_shape=[jax.ShapeDtypeStruct((B, L, D), F32), jax.ShapeDtypeStruct((ns, D), F32),
                   jax.ShapeDtypeStruct((B, CONV_W - 1, D), F32), jax.ShapeDtypeStruct((B, D), F32)],
        scratch_shapes=[pltpu.VMEM((D // LANES, B * (T + 1), LANES), F32)] * 2
        + [pltpu.VMEM(((CONV_W - 1) * B, D), F32), pltpu.VMEM((B, D), F32),
           pltpu.VMEM((2, B * T, D), F32), pltpu.VMEM((D, D_FF), BF16), pltpu.VMEM((D_FF, D), BF16),
           pltpu.VMEM((2, ch1, D_FF), F32), pltpu.VMEM((2, ch2, D), F32),
           pltpu.SemaphoreType.DMA((2,)), pltpu.SemaphoreType.DMA((2,))],
        compiler_params=_params(56, 1),
        name="lru_mlp",
    )(x, xs2d, lg.reshape(1, D), winb, conv_w, conv_b.reshape(1, D), waxb, b_a.reshape(1, D),
      b_x.reshape(1, D), lam.reshape(1, D), woutb, g.reshape(1, D), fg.reshape(1, D), w1, w2)


def _lru_sample_body(x_ref, cbuf_ref, h0_ref, g_ref, win_ref, cw_ref, cb_ref, wax_ref, ba_ref, bx_ref,
                     lam_ref, wout_ref, o_ref, nconv_ref, hnew_ref, *, pos0):
    x = x_ref[...]
    h = _rms(x, g_ref[...]).astype(BF16)
    z = _dot(h, win_ref[...])
    gate = jax.nn.gelu(z[:, :D])
    xb = z[:, D:]
    cw = cw_ref[...]
    acc = cbuf_ref[0] * cw[0:1, :]
    for j in range(1, CONV_W - 1):
        acc = acc + cbuf_ref[j] * cw[j:j + 1, :]
    acc = acc + xb * cw[CONV_W - 1:CONV_W, :]
    xc = cb_ref[...] + acc
    a, mult, i = _lru_gates(xc, wax_ref, ba_ref[...], bx_ref[...], lam_ref[...], range(HEADS))
    if pos0 == 0:
        mult = jnp.ones_like(mult)
    hnew = a * h0_ref[...] + mult * (i * xc)
    hnew_ref[...] = hnew
    nconv_ref[0:CONV_W - 2] = cbuf_ref[1:CONV_W - 1]
    nconv_ref[CONV_W - 2] = xb
    o_ref[...] = x + _dot((hnew * gate).astype(BF16), wout_ref[...])


def _lru_sample(x2d, cbuf, h0, g, winb, conv_w, conv_b, waxb, b_a, b_x, lam, woutb, *, pos0):
    n = x2d.shape[0]
    vec = _const((1, D))
    cshape = (CONV_W - 1, n, D)
    out = pl.BlockSpec((n, D), lambda i: (0, 0))
    return pl.pallas_call(
        functools.partial(_lru_sample_body, pos0=pos0),
        grid=(1,),
        in_specs=[_const((n, D)), _const(cshape), _const((n, D)), vec, _const((D, 2 * D)), _const((CONV_W, D)),
                  vec, _const((HEADS, DK, 2 * DK)), vec, vec, vec, _const((D, D))],
        out_specs=[out, pl.BlockSpec(cshape, lambda i: (0, 0, 0)), out],
        out_shape=[jax.ShapeDtypeStruct((n, D), F32), jax.ShapeDtypeStruct(cshape, F32),
                   jax.ShapeDtypeStruct((n, D), F32)],
        compiler_params=_params(32, 1),
        name="lru_sample",
    )(x2d, cbuf, h0, g.reshape(1, D), winb, conv_w, conv_b.reshape(1, D), waxb, b_a.reshape(1, D),
      b_x.reshape(1, D), lam.reshape(1, D), woutb)


def kernel(x_prompt, x_sample, state_pool, state_ret, state_conv, state_lru, pool_norm, pool_w, pool_scale, gm_norm, gm_w_in, gm_b_in, gm_ln_g, gm_ln_b, gm_w_s, gm_b_s, gm_w_out, ret_norm, ret_w_in, ret_gn_g, ret_gn_b, ret_w_out, lru_norm, lru_w_in, lru_conv_w, lru_conv_b, lru_w_a, lru_b_a, lru_w_x, lru_b_x, lru_lam, lru_w_out, mlp_norm, mlp_w1, mlp_w2, final_norm):
    B, L, _ = x_prompt.shape
    NS = x_sample.shape[0]
    bf = lambda w: w.astype(BF16)

    def mlp(xp, xs, li, final=False):
        yp, ys = _mlp(xp.reshape(B * L, D), xs, mlp_norm[li], mlp_w1, mlp_w2, final_norm, li=li, final=final)
        return yp.reshape(B, L, D), ys

    pool_wb = bf(pool_w[0])
    xs, pool_s = _pool_sample(x_sample.reshape(NS, D), jnp.swapaxes(state_pool[0], 0, 1),
                              pool_norm[0], pool_wb, pool_scale[0], pos0=PAST_LEN)
    xp, xs, pool_p = _pool_mlp(x_prompt, xs, pool_norm[0], pool_wb, pool_scale[0], mlp_norm[0],
                               mlp_w1, mlp_w2, final_norm, li=0)

    gm_winb, gm_woutb = bf(gm_w_in[0]), bf(gm_w_out[0])
    bs_full = jnp.repeat(gm_b_s[0].T, CHUNK, axis=1)
    xp = _gmlp_prompt(xp, gm_norm[0], gm_winb, gm_b_in[0], gm_ln_g[0], gm_ln_b[0], gm_w_s[0], bs_full, gm_woutb)
    sw = jnp.repeat(gm_w_s[0][:, 0, 0], CHUNK).reshape(1, D)
    sb = jnp.repeat(gm_b_s[0][:, 0], CHUNK).reshape(1, D)
    xs, v_s = _gmlp_sample(xs, gm_norm[0], gm_winb, gm_b_in[0], gm_ln_g[0], gm_ln_b[0], sw, sb, gm_woutb)
    xp, xs = mlp(xp, xs, 1)

    ret_winb, ret_woutb = bf(ret_w_in[0]), bf(ret_w_out[0])
    xp, ret_p = _ret_prompt(xp, ret_norm[0], ret_winb, ret_gn_g[0], ret_gn_b[0], ret_woutb)
    q, k, v, gate = _ret_proj_sample(xs, ret_norm[0], ret_winb, pos0=PAST_LEN)
    o, ret_s = _ret_state_sample(q, k, v, state_ret[0])
    xs = _ret_out_sample(xs, o, gate, ret_gn_g[0], ret_gn_b[0], ret_woutb)
    xp, xs = mlp(xp, xs, 2)

    lru_winb, lru_woutb = bf(lru_w_in[0]), bf(lru_w_out[0])
    waxb = bf(jnp.concatenate([lru_w_a[0], lru_w_x[0]], axis=-1))
    xs, conv_s, lru_s = _lru_sample(xs, jnp.swapaxes(state_conv[0], 0, 1), state_lru[0],
                                    lru_norm[0], lru_winb, lru_conv_w[0], lru_conv_b[0], waxb,
                                    lru_b_a[0], lru_b_x[0], lru_lam[0], lru_woutb, pos0=PAST_LEN)
    yp, ys, conv_p, lru_p = _lru_mlp(xp, xs, lru_norm[0], lru_winb, lru_conv_w[0], lru_conv_b[0], waxb,
                                     lru_b_a[0], lru_b_x[0], lru_lam[0], lru_woutb, mlp_norm[3],
                                     mlp_w1, mlp_w2, final_norm, li=3, final=True)

    return (yp, ys.reshape(NS, 1, D),
            pool_p[None], jnp.swapaxes(pool_s, 0, 1)[None],
            v_s.reshape(1, NS, 1, D),
            ret_p[None], ret_s[None],
            conv_p[None], jnp.swapaxes(conv_s, 0, 1)[None],
            lru_p[None], lru_s[None])
```

```python
import functools
import math

import jax
import jax.numpy as jnp
import numpy as np
from jax import lax
from jax.experimental import pallas as pl
from jax.experimental.pallas import tpu as pltpu

F32 = jnp.float32
BF16 = jnp.bfloat16

D = 1024
EPS = 1e-6
GN_EPS = 1e-5
PAST_LEN = 16384
POOL_WINDOWS = (2, 4, 8, 16)
POOL_GROUP = D // len(POOL_WINDOWS)
POOL_BUF = max(POOL_WINDOWS) - 1
CHUNK = 128
GM_GROUPS = 8
HEADS = 8
DK = D // HEADS
DV = 2 * D // HEADS
QK = HEADS * DK
VW = HEADS * DV
ROPE_BASE = 10000.0
CONV_W = 4
LRU_C = 8.0
D_FF = 4 * D

MIB = 1024 * 1024
SUBLANES = 8
LANES = 128


def _params(vmem_mib, n_grid):
    return pltpu.CompilerParams(
        dimension_semantics=("arbitrary",) * n_grid,
        vmem_limit_bytes=vmem_mib * MIB,
    )


def _const(shape):
    zeros = (0,) * len(shape)
    return pl.BlockSpec(shape, lambda *_: zeros, pipeline_mode=pl.Buffered(1))


def _rms(x, g):
    ms = jnp.mean(x * x, axis=-1, keepdims=True)
    return x * lax.rsqrt(ms + EPS) * g


def _dot(a, b):
    return jnp.dot(a, b, preferred_element_type=F32)


def _interleave(*queues):
    pos = [0] * len(queues)
    while any(p < len(q) for p, q in zip(pos, queues)):
        _, i = min((pos[i] / len(q), i) for i, q in enumerate(queues) if pos[i] < len(q))
        queues[i][pos[i]]()
        pos[i] += 1


def _softplus(x):
    return jnp.maximum(x, 0.0) + jnp.log1p(jnp.exp(-jnp.abs(x)))


GELU_K = math.sqrt(2.0 / math.pi)


def _gelu(x):
    half = 0.5 * x
    return half + half * jnp.tanh(x * (GELU_K + (GELU_K * 0.044715) * (x * x)))


MLP_FC = 512
STAGE_BYTES = 2 * MIB


def _load_cast(src, dst, stg, sem):
    ch = stg.shape[1]
    n = src.shape[0] // ch

    def copy(c, slot):
        return pltpu.make_async_copy(src.at[pl.ds(c * ch, ch), :], stg.at[slot], sem.at[slot])

    copy(0, 0).start()
    for c in range(n):
        slot = c % 2
        if c + 1 < n:
            copy(c + 1, 1 - slot).start()
        copy(c, slot).wait()
        dst[c * ch:(c + 1) * ch, :] = stg[slot].astype(BF16)


def _mlp_thunks(read_x, write_out, g_ref, fg_ref, w1b, w2b, final, fetch_w1=None, fetch_w2=None):
    st = {}

    def head():
        x = read_x()
        st["h"] = _rms(x, g_ref[...]).astype(BF16)
        st["acc"] = x

    def up(c):
        def run():
            if fetch_w1 is not None:
                fetch_w1(c)
            a = _dot(st["h"], w1b[:, c * MLP_FC:(c + 1) * MLP_FC])
            st["a"] = jnp.square(jnp.maximum(a, 0.0)).astype(BF16)
        return run

    def down(c):
        def run():
            if fetch_w2 is not None:
                fetch_w2(c)
            st["acc"] = st["acc"] + _dot(st["a"], w2b[c * MLP_FC:(c + 1) * MLP_FC, :])
        return run

    def tail():
        acc = st["acc"]
        write_out(_rms(acc, fg_ref[...]) if final else acc)

    return [head] + [f(c) for c in range(D_FF // MLP_FC) for f in (up, down)] + [tail]


def _run(thunks):
    for thunk in thunks:
        thunk()


def _store(ref):
    def write(v):
        ref[...] = v
    return write


def _mlp_body(xp_ref, xs_ref, g_ref, fg_ref, w1_hbm, w2_hbm, op_ref, os_ref,
              w1b, w2b, stg1, stg2, sem1, sem2, *, li, final, n_p):
    i = pl.program_id(0)
    nch = D_FF // MLP_FC

    def w1_copy(c):
        return pltpu.make_async_copy(w1_hbm.at[li, :, pl.ds(c * MLP_FC, MLP_FC)], stg1.at[c % 2], sem1.at[c % 2])

    def w2_copy(c):
        return pltpu.make_async_copy(w2_hbm.at[li, pl.ds(c * MLP_FC, MLP_FC), :], stg2.at[c % 2], sem2.at[c % 2])

    def fetch_w1(c):
        w1_copy(c).wait()
        w1b[:, c * MLP_FC:(c + 1) * MLP_FC] = stg1[c % 2].astype(BF16)
        if c + 2 < nch:
            w1_copy(c + 2).start()

    def fetch_w2(c):
        w2_copy(c).wait()
        w2b[c * MLP_FC:(c + 1) * MLP_FC, :] = stg2[c % 2].astype(BF16)
        if c + 2 < nch:
            w2_copy(c + 2).start()

    @pl.when(i == 0)
    def _():
        for c in range(min(2, nch)):
            w1_copy(c).start()
            w2_copy(c).start()
        _run(_mlp_thunks(lambda: xp_ref[...], _store(op_ref), g_ref, fg_ref, w1b, w2b, final,
                         fetch_w1, fetch_w2))

    @pl.when((i > 0) & (i < n_p))
    def _():
        _run(_mlp_thunks(lambda: xp_ref[...], _store(op_ref), g_ref, fg_ref, w1b, w2b, final))

    @pl.when(i == n_p)
    def _():
        _run(_mlp_thunks(lambda: xs_ref[...], _store(os_ref), g_ref, fg_ref, w1b, w2b, final))


def _mlp(xp2d, xs2d, g, w1, w2, fg, *, li, final, tm=512):
    n, ns = xp2d.shape[0], xs2d.shape[0]
    n_p = n // tm
    row = pl.BlockSpec((tm, D), lambda i: (jnp.minimum(i, n_p - 1), 0))
    srow = pl.BlockSpec((ns, D), lambda i: (0, 0))
    hbm = pl.BlockSpec(memory_space=pl.ANY)
    return pl.pallas_call(
        functools.partial(_mlp_body, li=li, final=final, n_p=n_p),
        grid=(n_p + 1,),
        in_specs=[row, _const((ns, D)), _const((1, D)), _const((1, D)), hbm, hbm],
        out_specs=[row, srow],
        out_shape=[jax.ShapeDtypeStruct((n, D), F32), jax.ShapeDtypeStruct((ns, D), F32)],
        scratch_shapes=[pltpu.VMEM((D, D_FF), BF16), pltpu.VMEM((D_FF, D), BF16),
                        pltpu.VMEM((2, D, MLP_FC), F32), pltpu.VMEM((2, MLP_FC, D), F32),
                        pltpu.SemaphoreType.DMA((2,)), pltpu.SemaphoreType.DMA((2,))],
        compiler_params=_params(48, 1),
        name="mlp",
    )(xp2d, xs2d, g.reshape(1, D), fg.reshape(1, D), w1, w2)


POOL_HALO = 16


def _pool_thunks(x_ref, g_ref, w_ref, sc_ref, write_out, buf_ref, ext_ref, *, T, t):
    st = {}

    def head():
        ext_ref[0:POOL_HALO, :] = jnp.where(t == 0, 0.0, ext_ref[0:POOL_HALO, :])
        x = x_ref[...]
        h = _rms(x, g_ref[...])
        ext_ref[POOL_HALO:POOL_HALO + T, :] = h
        st["x"], st["h"] = x, h
        st["pos"] = t * T + lax.broadcasted_iota(jnp.int32, (T, 1), 0)
        st["y"] = [None] * len(POOL_WINDOWS)

    def group(gi, w):
        def run():
            c0, c1 = gi * POOL_GROUP, (gi + 1) * POOL_GROUP
            hg = st["h"][:, c0:c1]
            s = hg
            for k in range(1, w):
                s = s + ext_ref[POOL_HALO - k:POOL_HALO - k + T, c0:c1]
            cnt = jnp.minimum(st["pos"] + 1, w).astype(F32)
            d = s / cnt - hg
            st["y"][gi] = _dot(d.astype(BF16), w_ref[gi])
        return run

    def tail():
        y = jnp.concatenate(st["y"], axis=-1) * sc_ref[...]
        write_out(st["x"] + y)
        last = ext_ref[T:T + POOL_HALO, :]
        ext_ref[0:POOL_HALO, :] = last
        buf_ref[...] = last[POOL_HALO - POOL_BUF:, :]

    return [head] + [group(gi, w) for gi, w in enumerate(POOL_WINDOWS)] + [tail]


def _pool_mlp_body(x_ref, xs_ref, pg_ref, pw_ref, psc_ref, g_ref, fg_ref, w1_hbm, w2_hbm,
                   o_ref, os_ref, buf_ref, ext_ref, x1_scr, w1b, w2b, stg1, stg2, sem1, sem2,
                   *, T, tiles_per_seq, n, li):
    s = pl.program_id(0)
    t = lax.rem(s, tiles_per_seq)
    slot = lax.rem(s, 2)

    @pl.when(s == 0)
    def _():
        ext_ref[0:POOL_HALO, :] = jnp.zeros((POOL_HALO, D), F32)
        _load_cast(w1_hbm.at[li], w1b, stg1, sem1)
        _load_cast(w2_hbm.at[li], w2b, stg2, sem2)

    def write_x1(v):
        x1_scr[slot] = v

    def mixer():
        return _pool_thunks(x_ref, pg_ref, pw_ref, psc_ref, write_x1, buf_ref, ext_ref, T=T, t=t)

    def mlp():
        return _mlp_thunks(lambda: x1_scr[1 - slot], _store(o_ref), g_ref, fg_ref, w1b, w2b, False)

    @pl.when(s == 0)
    def _():
        _run(mixer())

    @pl.when((s > 0) & (s < n))
    def _():
        _interleave(mlp(), mixer())

    @pl.when(s == n)
    def _():
        _run(mlp())
        _run(_mlp_thunks(lambda: xs_ref[...], _store(os_ref), g_ref, fg_ref, w1b, w2b, False))


def _pool_mlp(x, xs2d, pg, pwb, psc, g, w1, w2, fg, *, li, T=512):
    B, L, _ = x.shape
    ns = xs2d.shape[0]
    tps = L // T
    n = B * tps
    tile = lambda s: (jnp.minimum(s, n - 1) // tps, lax.rem(jnp.minimum(s, n - 1), tps), 0)
    prev = lambda s: (jnp.maximum(s - 1, 0) // tps, lax.rem(jnp.maximum(s - 1, 0), tps), 0)
    hbm = pl.BlockSpec(memory_space=pl.ANY)
    ch1 = STAGE_BYTES // (4 * D_FF)
    ch2 = STAGE_BYTES // (4 * D)
    return pl.pallas_call(
        functools.partial(_pool_mlp_body, T=T, tiles_per_seq=tps, n=n, li=li),
        grid=(n + 1,),
        in_specs=[pl.BlockSpec((None, T, D), tile), _const((ns, D)), _const((1, D)),
                  _const((len(POOL_WINDOWS), POOL_GROUP, POOL_GROUP)), _const((1, D)),
                  _const((1, D)), _const((1, D)), hbm, hbm],
        out_specs=[pl.BlockSpec((None, T, D), prev), pl.BlockSpec((ns, D), lambda s: (0, 0)),
                   pl.BlockSpec((None, POOL_BUF, D), lambda s: (jnp.minimum(s, n - 1) // tps, 0, 0))],
        out_shape=[jax.ShapeDtypeStruct((B, L, D), F32), jax.ShapeDtypeStruct((ns, D), F32),
                   jax.ShapeDtypeStruct((B, POOL_BUF, D), F32)],
        scratch_shapes=[pltpu.VMEM((T + POOL_HALO, D), F32), pltpu.VMEM((2, T, D), F32),
                        pltpu.VMEM((D, D_FF), BF16), pltpu.VMEM((D_FF, D), BF16),
                        pltpu.VMEM((2, ch1, D_FF), F32), pltpu.VMEM((2, ch2, D), F32),
                        pltpu.SemaphoreType.DMA((2,)), pltpu.SemaphoreType.DMA((2,))],
        compiler_params=_params(56, 1),
        name="pool_mlp",
    )(x, xs2d, pg.reshape(1, D), pwb, psc.reshape(1, D), g.reshape(1, D), fg.reshape(1, D), w1, w2)


def _pool_sample_body(x_ref, buf_ref, g_ref, w_ref, sc_ref, o_ref, nbuf_ref, *, pos0):
    x = x_ref[...]
    h = _rms(x, g_ref[...])
    ys = []
    for gi, w in enumerate(POOL_WINDOWS):
        c0, c1 = gi * POOL_GROUP, (gi + 1) * POOL_GROUP
        hg = h[:, c0:c1]
        s = hg
        for k in range(1, w):
            s = s + buf_ref[POOL_BUF - k, :, c0:c1]
        cnt = float(min(pos0 + 1, w))
        d = s / cnt - hg
        ys.append(_dot(d.astype(BF16), w_ref[gi]))
    y = jnp.concatenate(ys, axis=-1) * sc_ref[...]
    o_ref[...] = x + y
    nbuf_ref[0:POOL_BUF - 1] = buf_ref[1:POOL_BUF]
    nbuf_ref[POOL_BUF - 1] = h


def _pool_sample(x2d, buf, g, wb, sc, *, pos0, bt=32):
    n = x2d.shape[0]
    row = pl.BlockSpec((bt, D), lambda i: (i, 0))
    brow = pl.BlockSpec((POOL_BUF, bt, D), lambda i: (0, i, 0))
    return pl.pallas_call(
        functools.partial(_pool_sample_body, pos0=pos0),
        grid=(n // bt,),
        in_specs=[row, brow, _const((1, D)), _const((len(POOL_WINDOWS), POOL_GROUP, POOL_GROUP)), _const((1, D))],
        out_specs=[row, brow],
        out_shape=[jax.ShapeDtypeStruct((n, D), F32), jax.ShapeDtypeStruct((POOL_BUF, n, D), F32)],
        compiler_params=_params(32, 1),
        name="pool_sample",
    )(x2d, buf, g.reshape(1, D), wb, sc.reshape(1, D))


def _gmlp_front(x, g_ref, win_ref, bin_ref, lng_ref, lnb_ref):
    h = _rms(x, g_ref[...]).astype(BF16)
    z = _gelu(_dot(h, win_ref[...]) + bin_ref[...])
    u, v = z[:, :D], z[:, D:]
    mu = jnp.mean(v, axis=-1, keepdims=True)
    vc = v - mu
    var = jnp.mean(jnp.square(vc), axis=-1, keepdims=True)
    vn = vc * lax.rsqrt(var + EPS) * lng_ref[...] + lnb_ref[...]
    return u, vn


def _gmlp_prompt_body(x_ref, g_ref, win_ref, bin_ref, lng_ref, lnb_ref, ws_ref, bs_ref, wout_ref,
                      o_ref, *, T):
    ti = lax.broadcasted_iota(jnp.int32, (CHUNK, CHUNK), 0)
    si = lax.broadcasted_iota(jnp.int32, (CHUNK, CHUNK), 1)
    causal = ti >= si
    wsm = [jnp.where(causal, ws_ref[gi], 0.0).astype(BF16) for gi in range(GM_GROUPS)]
    x = x_ref[...]
    u, vn = _gmlp_front(x, g_ref, win_ref, bin_ref, lng_ref, lnb_ref)
    vnb = vn.astype(BF16)
    rows = []
    for n in range(T // CHUNK):
        r0, r1 = n * CHUNK, (n + 1) * CHUNK
        cols = [_dot(wsm[gi], vnb[r0:r1, gi * CHUNK:(gi + 1) * CHUNK]) for gi in range(GM_GROUPS)]
        rows.append(jnp.concatenate(cols, axis=-1) + bs_ref[...])
    mixed = jnp.concatenate(rows, axis=0)
    y = _dot((u * mixed).astype(BF16), wout_ref[...])
    o_ref[...] = x + y


def _gmlp_prompt(x, g, winb, b_in, ln_g, ln_b, w_s, bs_full, woutb, *, T=1024):
    B, L, _ = x.shape
    blk = pl.BlockSpec((None, T, D), lambda b, t: (b, t, 0))
    return pl.pallas_call(
        functools.partial(_gmlp_prompt_body, T=T),
        grid=(B, L // T),
        in_specs=[blk, _const((1, D)), _const((D, 2 * D)), _const((1, 2 * D)), _const((1, D)), _const((1, D)),
                  _const((GM_GROUPS, CHUNK, CHUNK)), _const((CHUNK, D)), _const((D, D))],
        out_specs=blk,
        out_shape=jax.ShapeDtypeStruct((B, L, D), F32),
        compiler_params=_params(48, 2),
        name="gmlp_prompt",
    )(x, g.reshape(1, D), winb, b_in.reshape(1, 2 * D), ln_g.reshape(1, D), ln_b.reshape(1, D),
      w_s, bs_full, woutb)


def _gmlp_sample_body(x_ref, g_ref, win_ref, bin_ref, lng_ref, lnb_ref, sw_ref, sb_ref, wout_ref,
                      o_ref, vn_ref):
    x = x_ref[...]
    u, vn = _gmlp_front(x, g_ref, win_ref, bin_ref, lng_ref, lnb_ref)
    mixed = vn * sw_ref[...] + sb_ref[...]
    y = _dot((u * mixed).astype(BF16), wout_ref[...])
    o_ref[...] = x + y
    vn_ref[...] = vn


def _gmlp_sample(x2d, g, winb, b_in, ln_g, ln_b, sw, sb, woutb):
    n = x2d.shape[0]
    full = _const((n, D))
    return pl.pallas_call(
        _gmlp_sample_body,
        grid=(1,),
        in_specs=[full, _const((1, D)), _const((D, 2 * D)), _const((1, 2 * D)), _const((1, D)), _const((1, D)),
                  _const((1, D)), _const((1, D)), _const((D, D))],
        out_specs=[pl.BlockSpec((n, D), lambda i: (0, 0)), pl.BlockSpec((n, D), lambda i: (0, 0))],
        out_shape=[jax.ShapeDtypeStruct((n, D), F32), jax.ShapeDtypeStruct((n, D), F32)],
        compiler_params=_params(32, 1),
        name="gmlp_sample",
    )(x2d, g.reshape(1, D), winb, b_in.reshape(1, 2 * D), ln_g.reshape(1, D), ln_b.reshape(1, D),
      sw, sb, woutb)


def _ret_constants(C):
    log_gamma = np.log1p(-np.exp2(-5.0 - np.arange(HEADS, dtype=np.float64)))
    idx = np.arange(C, dtype=np.float64)
    diff = idx[:, None] - idx[None, :]
    dmask = np.where(diff[None] >= 0, np.exp(log_gamma[:, None, None] * np.maximum(diff, 0.0)[None]), 0.0)
    q_dec = np.exp(log_gamma[:, None] * (idx + 1.0))
    k_dec = np.exp(log_gamma[:, None] * (C - 1.0 - idx))
    chunk_dec = np.exp(log_gamma * C)
    return dmask.astype(np.float32), q_dec.astype(np.float32), k_dec.astype(np.float32), chunk_dec


def _rope_tables(pos):
    half = DK // 2
    freqs = np.exp(-math.log(ROPE_BASE) * np.arange(half, dtype=np.float64) / half)
    ang = np.asarray(pos, dtype=np.float64)[:, None] * freqs[None]
    cos = np.concatenate([np.cos(ang), np.cos(ang)], axis=-1)
    sin = np.concatenate([-np.sin(ang), np.sin(ang)], axis=-1)
    return cos, sin


def _rope(t, cos, sin):
    return t * cos + pltpu.roll(t, DK // 2, axis=1) * sin


def _group_norm_gate(o, gate, gng, gnb):
    mu = jnp.mean(o, axis=-1, keepdims=True)
    oc = o - mu
    var = jnp.mean(jnp.square(oc), axis=-1, keepdims=True)
    on = oc * lax.rsqrt(var + GN_EPS) * gng + gnb
    return jax.nn.silu(gate) * on


def _ret_prompt_body(x_ref, g_ref, win_ref, cq_ref, sq_ref, ck_ref, sk_ref, dm_ref, qd_ref, kd_ref,
                     gng_ref, gnb_ref, wout_ref, o_ref, s_out_ref, gated_scr, s_scr,
                     *, T, chunk_dec):
    t = pl.program_id(1)

    @pl.when(t == 0)
    def _():
        s_scr[...] = jnp.zeros_like(s_scr)

    x = x_ref[...]
    h = _rms(x, g_ref[...]).astype(BF16)
    cq, sq, ck, sk = cq_ref[...], sq_ref[...], ck_ref[...], sk_ref[...]
    n_pairs = HEADS // 2

    def projection(pair, dst):
        def piece(name, lo, width):
            def run():
                dst[name] = _dot(h, win_ref[:, lo:lo + width])
            return run
        v0, g0 = 2 * QK + 2 * DV * pair, 2 * QK + VW + 2 * DV * pair
        return [piece("q", 2 * DK * pair, 2 * DK), piece("k", QK + 2 * DK * pair, 2 * DK),
                piece("v0", v0, DV), piece("v1", v0 + DV, DV),
                piece("g0", g0, DV), piece("g1", g0 + DV, DV)]

    def chunk_work(pair, src):
        state = {}

        def prep(sub):
            def run():
                q = _rope(src["q"][:, sub * DK:(sub + 1) * DK], cq, sq)
                k = _rope(src["k"][:, sub * DK:(sub + 1) * DK], ck, sk)
                state[sub] = dict(q=q, k=k, qb=q.astype(BF16), kb=k.astype(BF16), s=s_scr[2 * pair + sub])
            return run

        def chunk(sub, c):
            def run():
                hd = 2 * pair + sub
                st = state[sub]
                rows = slice(c * CHUNK, (c + 1) * CHUNK)
                vb = src["v%d" % sub][rows, :].astype(BF16)
                scores = lax.dot_general(st["qb"][rows], st["kb"][rows], (((1,), (1,)), ((), ())),
                                         preferred_element_type=F32) * dm_ref[hd]
                o = (_dot(scores.astype(BF16), vb)
                     + _dot((st["q"][rows] * qd_ref[hd]).astype(BF16), st["s"].astype(BF16)))
                kt = (st["k"][rows] * kd_ref[hd]).T.astype(BF16)
                st["s"] = st["s"] * chunk_dec[hd] + _dot(kt, vb)
                gated = _group_norm_gate(o, src["g%d" % sub][rows, :], gng_ref[:, hd * DV:(hd + 1) * DV],
                                         gnb_ref[:, hd * DV:(hd + 1) * DV])
                gated_scr[rows, hd * DV:(hd + 1) * DV] = gated.astype(BF16)
            return run

        def finish(sub):
            def run():
                s_scr[2 * pair + sub] = state[sub]["s"]
            return run

        return ([prep(0), prep(1)] + [chunk(sub, c) for c in range(T // CHUNK) for sub in range(2)]
                + [finish(0), finish(1)])

    acc = [x]

    def out_projection(pair):
        def run():
            cols = slice(pair * 2 * DV, (pair + 1) * 2 * DV)
            acc[0] = acc[0] + _dot(gated_scr[:, cols], wout_ref[cols, :])
        return run

    cur = {}
    for run in projection(0, cur):
        run()
    for pair in range(n_pairs):
        nxt = {}
        if pair + 1 < n_pairs:
            matmul_queue = projection(pair + 1, nxt)
        else:
            matmul_queue = [out_projection(p) for p in range(n_pairs - 1)]
        _interleave(matmul_queue, chunk_work(pair, cur))
        cur = nxt
    out_projection(n_pairs - 1)()
    o_ref[...] = acc[0]
    s_out_ref[...] = s_scr[...]


def _ret_prompt(x, g, winb, gn_g, gn_b, woutb, *, T=512):
    B, L, _ = x.shape
    dmask, q_dec, k_dec, chunk_dec = _ret_constants(CHUNK)
    cos, sin = _rope_tables(np.arange(L))
    scale = DK ** -0.5
    cq, sq = jnp.asarray(cos, F32), jnp.asarray(sin, F32)
    ck, sk = jnp.asarray(cos * scale, F32), jnp.asarray(sin * scale, F32)
    qd = jnp.asarray(np.broadcast_to(q_dec[:, :, None], (HEADS, CHUNK, DK)))
    kd = jnp.asarray(np.broadcast_to(k_dec[:, :, None], (HEADS, CHUNK, DK)))
    blk = pl.BlockSpec((None, T, D), lambda b, t: (b, t, 0))
    tab = pl.BlockSpec((T, DK), lambda b, t: (t, 0))
    hcc = _const((HEADS, CHUNK, CHUNK))
    return pl.pallas_call(
        functools.partial(_ret_prompt_body, T=T, chunk_dec=tuple(float(c) for c in chunk_dec)),
        grid=(B, L // T),
        in_specs=[blk, _const((1, D)), _const((D, 2 * QK + 2 * VW)), tab, tab, tab, tab,
                  hcc, hcc, hcc, _const((1, VW)), _const((1, VW)), _const((VW, D))],
        out_specs=[blk, pl.BlockSpec((None, HEADS, DK, DV), lambda b, t: (b, 0, 0, 0))],
        out_shape=[jax.ShapeDtypeStruct((B, L, D), F32), jax.ShapeDtypeStruct((B, HEADS, DK, DV), F32)],
        scratch_shapes=[pltpu.VMEM((T, VW), BF16), pltpu.VMEM((HEADS, DK, DV), F32)],
        compiler_params=_params(56, 2),
        name="ret_prompt",
    )(x, g.reshape(1, D), winb, cq, sq, ck, sk, jnp.asarray(dmask), qd, kd,
      gn_g.reshape(1, VW), gn_b.reshape(1, VW), woutb)


def _ret_proj_sample_body(x_ref, g_ref, win_ref, cq_ref, sq_ref, ck_ref, sk_ref,
                          q_ref, k_ref, v_ref, gate_ref):
    h = _rms(x_ref[...], g_ref[...]).astype(BF16)
    p = _dot(h, win_ref[...])
    for hd in range(HEADS):
        q_ref[:, hd * DK:(hd + 1) * DK] = _rope(p[:, hd * DK:(hd + 1) * DK], cq_ref[...], sq_ref[...])
        k_ref[:, hd * DK:(hd + 1) * DK] = _rope(p[:, QK + hd * DK:QK + (hd + 1) * DK], ck_ref[...], sk_ref[...])
    v_ref[...] = p[:, 2 * QK:2 * QK + VW]
    gate_ref[...] = p[:, 2 * QK + VW:]


def _ret_proj_sample(x2d, g, winb, *, pos0):
    n = x2d.shape[0]
    cos, sin = _rope_tables(np.array([pos0]))
    scale = DK ** -0.5
    tabs = [jnp.asarray(a, F32) for a in (cos, sin, cos * scale, sin * scale)]
    out = lambda w: pl.BlockSpec((n, w), lambda i: (0, 0))
    return pl.pallas_call(
        _ret_proj_sample_body,
        grid=(1,),
        in_specs=[_const((n, D)), _const((1, D)), _const((D, 2 * QK + 2 * VW))] + [_const((1, DK))] * 4,
        out_specs=[out(QK), out(QK), out(VW), out(VW)],
        out_shape=[jax.ShapeDtypeStruct((n, w), F32) for w in (QK, QK, VW, VW)],
        compiler_params=_params(40, 1),
        name="ret_proj_sample",
    )(x2d, g.reshape(1, D), winb, *tabs)


def _ret_state_sample_body(q_ref, k_ref, v_ref, s_ref, o_ref, s_out_ref, *, bt, gamma):
    pad = jnp.zeros((DK - bt, DK), F32)
    for hd in range(HEADS):
        qh = q_ref[:, hd * DK:(hd + 1) * DK]
        kh = k_ref[:, hd * DK:(hd + 1) * DK]
        qt = jnp.concatenate([qh, pad], axis=0).T
        kt = jnp.concatenate([kh, pad], axis=0).T
        qk = jnp.sum(qh * kh, axis=-1, keepdims=True)
        for b in range(bt):
            v = v_ref[b:b + 1, hd * DV:(hd + 1) * DV]
            s_old = s_ref[b, hd]
            qs = jnp.sum(s_old * (qt[:, b:b + 1] * gamma[hd]), axis=0, keepdims=True)
            o_ref[b:b + 1, hd * DV:(hd + 1) * DV] = qk[b:b + 1, :] * v + qs
            s_out_ref[b, hd] = s_old * gamma[hd] + kt[:, b:b + 1] * v


def _ret_state_sample(q, k, v, s, *, bt=8):
    n = q.shape[0]
    _, _, _, chunk_dec = _ret_constants(1)
    row = lambda w: pl.BlockSpec((bt, w), lambda i: (i, 0))
    sblk = pl.BlockSpec((bt, HEADS, DK, DV), lambda i: (i, 0, 0, 0))
    return pl.pallas_call(
        functools.partial(_ret_state_sample_body, bt=bt, gamma=tuple(float(c) for c in chunk_dec)),
        grid=(n // bt,),
        in_specs=[row(QK), row(QK), row(VW), sblk],
        out_specs=[row(VW), sblk],
        out_shape=[jax.ShapeDtypeStruct((n, VW), F32), jax.ShapeDtypeStruct(s.shape, F32)],
        compiler_params=_params(48, 1),
        name="ret_state_sample",
    )(q, k, v, s)


def _ret_out_sample_body(x_ref, o_ref_in, gate_ref, gng_ref, gnb_ref, wout_ref, o_ref):
    parts = []
    for hd in range(HEADS):
        cols = slice(hd * DV, (hd + 1) * DV)
        parts.append(_group_norm_gate(o_ref_in[:, cols], gate_ref[:, cols], gng_ref[:, cols], gnb_ref[:, cols]))
    gated = jnp.concatenate(parts, axis=-1).astype(BF16)
    o_ref[...] = x_ref[...] + _dot(gated, wout_ref[...])


def _ret_out_sample(x2d, o, gate, gn_g, gn_b, woutb):
    n = x2d.shape[0]
    return pl.pallas_call(
        _ret_out_sample_body,
        grid=(1,),
        in_specs=[_const((n, D)), _const((n, VW)), _const((n, VW)), _const((1, VW)), _const((1, VW)),
                  _const((VW, D))],
        out_specs=pl.BlockSpec((n, D), lambda i: (0, 0)),
        out_shape=jax.ShapeDtypeStruct((n, D), F32),
        compiler_params=_params(32, 1),
        name="ret_out_sample",
    )(x2d, o, gate, gn_g.reshape(1, VW), gn_b.reshape(1, VW), woutb)


def _lru_gates(xc, wax_ref, ba, bx, lam, heads):
    xcb = xc.astype(BF16)
    rs, is_ = [], []
    for n, hd in enumerate(heads):
        ri = _dot(xcb[:, n * DK:(n + 1) * DK], wax_ref[hd])
        rs.append(ri[:, :DK])
        is_.append(ri[:, DK:])
    r = jax.nn.sigmoid(jnp.concatenate(rs, axis=-1) + ba)
    i = jax.nn.sigmoid(jnp.concatenate(is_, axis=-1) + bx)
    log_a = r * (-LRU_C * _softplus(-lam))
    a = jnp.exp(log_a)
    mult = jnp.sqrt(-jnp.tanh(log_a) * (a * a + 1.0))
    return a, mult, i


LRU_GROUP_HEADS = 2


def _lru_prompt_body(x_ref, g_ref, win_ref, cw_ref, cb_ref, wax_ref, ba_ref, bx_ref, lam_ref, wout_ref,
                     o_ref, conv_ref, hlast_ref, xb_scr, hs_scr, carry_scr, h_scr, *, NB, T):
    t = pl.program_id(0)
    N = NB * T
    PS = T + 1
    halo = (CONV_W - 1) * NB
    gw = LRU_GROUP_HEADS * DK
    gblk = gw // LANES

    @pl.when(t == 0)
    def _():
        carry_scr[...] = jnp.zeros_like(carry_scr)
        h_scr[...] = jnp.zeros_like(h_scr)

    x = x_ref[...].reshape(N, D)
    h = _rms(x, g_ref[...]).astype(BF16)
    row = lax.broadcasted_iota(jnp.int32, (N, 1), 0)
    pos = t * T + lax.shift_right_logical(row, NB.bit_length() - 1)
    out = x
    def project(grp):
        c0, c1 = grp * gw, (grp + 1) * gw
        return _dot(h, win_ref[:, c0:c1]), _dot(h, win_ref[:, D + c0:D + c1])

    ngrp = D // gw
    nxt = project(0)
    for grp in range(ngrp):
        c0, c1 = grp * gw, (grp + 1) * gw
        blks = range(grp * gblk, (grp + 1) * gblk)
        zg, xb = nxt
        if grp + 1 < ngrp:
            nxt = project(grp + 1)
        gate = _gelu(zg)
        for n, j in enumerate(blks):
            for b in range(NB):
                xb_scr[j, b * PS:b * PS + T, :] = xb[b * T:(b + 1) * T, n * LANES:(n + 1) * LANES]
        ext = jnp.concatenate(
            [carry_scr[:, c0:c1]]
            + [jnp.concatenate([xb_scr[j, pl.ds(s, NB, stride=PS), :] for j in blks], axis=-1)
               for s in range(T)], axis=0)
        carry_scr[:, c0:c1] = ext[N:N + halo, :]
        acc = ext[0:N, :] * cw_ref[0:1, c0:c1]
        for j in range(1, CONV_W):
            acc = acc + ext[j * NB:j * NB + N, :] * cw_ref[j:j + 1, c0:c1]
        xc = cb_ref[:, c0:c1] + acc
        heads = range(grp * LRU_GROUP_HEADS, (grp + 1) * LRU_GROUP_HEADS)
        a, mult, i = _lru_gates(xc, wax_ref, ba_ref[:, c0:c1], bx_ref[:, c0:c1], lam_ref[:, c0:c1], heads)
        bvec = jnp.where(pos == 0, 1.0, mult) * (i * xc)
        hcur = h_scr[:, c0:c1]
        for s in range(T):
            hcur = a[s * NB:(s + 1) * NB, :] * hcur + bvec[s * NB:(s + 1) * NB, :]
            for n, j in enumerate(blks):
                hs_scr[j, pl.ds(s, NB, stride=PS), :] = hcur[:, n * LANES:(n + 1) * LANES]
        h_scr[:, c0:c1] = hcur
        hs = jnp.concatenate(
            [jnp.concatenate([hs_scr[j, b * PS:b * PS + T, :] for b in range(NB)], axis=0) for j in blks],
            axis=-1)
        out = out + _dot((hs * gate).astype(BF16), wout_ref[c0:c1, :])
    o_ref[...] = out.reshape(NB, T, D)
    hlast_ref[...] = h_scr[...]

    @pl.when(t == pl.num_programs(0) - 1)
    def _():
        for j in range(CONV_W - 1):
            conv_ref[:, j, :] = carry_scr[j * NB:(j + 1) * NB, :]


def _lru_prompt(x, g, winb, conv_w, conv_b, waxb, b_a, b_x, lam, woutb, *, T=128):
    B, L, _ = x.shape
    blk = pl.BlockSpec((B, T, D), lambda t: (0, t, 0))
    vec = _const((1, D))
    return pl.pallas_call(
        functools.partial(_lru_prompt_body, NB=B, T=T),
        grid=(L // T,),
        in_specs=[blk, vec, _const((D, 2 * D)), _const((CONV_W, D)), vec, _const((HEADS, DK, 2 * DK)),
                  vec, vec, vec, _const((D, D))],
        out_specs=[blk, pl.BlockSpec((B, CONV_W - 1, D), lambda t: (0, 0, 0)),
                   pl.BlockSpec((B, D), lambda t: (0, 0))],
        out_shape=[jax.ShapeDtypeStruct((B, L, D), F32), jax.ShapeDtypeStruct((B, CONV_W - 1, D), F32),
                   jax.ShapeDtypeStruct((B, D), F32)],
        scratch_shapes=[pltpu.VMEM((D // LANES, B * (T + 1), LANES), F32)] * 2
        + [pltpu.VMEM(((CONV_W - 1) * B, D), F32), pltpu.VMEM((B, D), F32)],
        compiler_params=_params(48, 1),
        name="lru_prompt",
    )(x, g.reshape(1, D), winb, conv_w, conv_b.reshape(1, D), waxb, b_a.reshape(1, D), b_x.reshape(1, D),
      lam.reshape(1, D), woutb)


def _lru_sample_body(x_ref, cbuf_ref, h0_ref, g_ref, win_ref, cw_ref, cb_ref, wax_ref, ba_ref, bx_ref,
                     lam_ref, wout_ref, o_ref, nconv_ref, hnew_ref, *, pos0):
    x = x_ref[...]
    h = _rms(x, g_ref[...]).astype(BF16)
    z = _dot(h, win_ref[...])
    gate = _gelu(z[:, :D])
    xb = z[:, D:]
    cw = cw_ref[...]
    acc = cbuf_ref[0] * cw[0:1, :]
    for j in range(1, CONV_W - 1):
        acc = acc + cbuf_ref[j] * cw[j:j + 1, :]
    acc = acc + xb * cw[CONV_W - 1:CONV_W, :]
    xc = cb_ref[...] + acc
    a, mult, i = _lru_gates(xc, wax_ref, ba_ref[...], bx_ref[...], lam_ref[...], range(HEADS))
    if pos0 == 0:
        mult = jnp.ones_like(mult)
    hnew = a * h0_ref[...] + mult * (i * xc)
    hnew_ref[...] = hnew
    nconv_ref[0:CONV_W - 2] = cbuf_ref[1:CONV_W - 1]
    nconv_ref[CONV_W - 2] = xb
    o_ref[...] = x + _dot((hnew * gate).astype(BF16), wout_ref[...])


def _lru_sample(x2d, cbuf, h0, g, winb, conv_w, conv_b, waxb, b_a, b_x, lam, woutb, *, pos0):
    n = x2d.shape[0]
    vec = _const((1, D))
    cshape = (CONV_W - 1, n, D)
    out = pl.BlockSpec((n, D), lambda i: (0, 0))
    return pl.pallas_call(
        functools.partial(_lru_sample_body, pos0=pos0),
        grid=(1,),
        in_specs=[_const((n, D)), _const(cshape), _const((n, D)), vec, _const((D, 2 * D)), _const((CONV_W, D)),
                  vec, _const((HEADS, DK, 2 * DK)), vec, vec, vec, _const((D, D))],
        out_specs=[out, pl.BlockSpec(cshape, lambda i: (0, 0, 0)), out],
        out_shape=[jax.ShapeDtypeStruct((n, D), F32), jax.ShapeDtypeStruct(cshape, F32),
                   jax.ShapeDtypeStruct((n, D), F32)],
        compiler_params=_params(32, 1),
        name="lru_sample",
    )(x2d, cbuf, h0, g.reshape(1, D), winb, conv_w, conv_b.reshape(1, D), waxb, b_a.reshape(1, D),
      b_x.reshape(1, D), lam.reshape(1, D), woutb)


def kernel(x_prompt, x_sample, state_pool, state_ret, state_conv, state_lru, pool_norm, pool_w, pool_scale, gm_norm, gm_w_in, gm_b_in, gm_ln_g, gm_ln_b, gm_w_s, gm_b_s, gm_w_out, ret_norm, ret_w_in, ret_gn_g, ret_gn_b, ret_w_out, lru_norm, lru_w_in, lru_conv_w, lru_conv_b, lru_w_a, lru_b_a, lru_w_x, lru_b_x, lru_lam, lru_w_out, mlp_norm, mlp_w1, mlp_w2, final_norm):
    B, L, _ = x_prompt.shape
    NS = x_sample.shape[0]
    bf = lambda w: w.astype(BF16)

    def mlp(xp, xs, li, final=False):
        yp, ys = _mlp(xp.reshape(B * L, D), xs, mlp_norm[li], mlp_w1, mlp_w2, final_norm, li=li, final=final)
        return yp.reshape(B, L, D), ys

    pool_wb = bf(pool_w[0])
    xs, pool_s = _pool_sample(x_sample.reshape(NS, D), jnp.swapaxes(state_pool[0], 0, 1),
                              pool_norm[0], pool_wb, pool_scale[0], pos0=PAST_LEN)
    xp, xs, pool_p = _pool_mlp(x_prompt, xs, pool_norm[0], pool_wb, pool_scale[0], mlp_norm[0],
                               mlp_w1, mlp_w2, final_norm, li=0)

    gm_winb, gm_woutb = bf(gm_w_in[0]), bf(gm_w_out[0])
    bs_full = jnp.repeat(gm_b_s[0].T, CHUNK, axis=1)
    xp = _gmlp_prompt(xp, gm_norm[0], gm_winb, gm_b_in[0], gm_ln_g[0], gm_ln_b[0], gm_w_s[0], bs_full, gm_woutb)
    sw = jnp.repeat(gm_w_s[0][:, 0, 0], CHUNK).reshape(1, D)
    sb = jnp.repeat(gm_b_s[0][:, 0], CHUNK).reshape(1, D)
    xs, v_s = _gmlp_sample(xs, gm_norm[0], gm_winb, gm_b_in[0], gm_ln_g[0], gm_ln_b[0], sw, sb, gm_woutb)
    xp, xs = mlp(xp, xs, 1)

    ret_winb, ret_woutb = bf(ret_w_in[0]), bf(ret_w_out[0])
    xp, ret_p = _ret_prompt(xp, ret_norm[0], ret_winb, ret_gn_g[0], ret_gn_b[0], ret_woutb)
    q, k, v, gate = _ret_proj_sample(xs, ret_norm[0], ret_winb, pos0=PAST_LEN)
    o, ret_s = _ret_state_sample(q, k, v, state_ret[0])
    xs = _ret_out_sample(xs, o, gate, ret_gn_g[0], ret_gn_b[0], ret_woutb)
    xp, xs = mlp(xp, xs, 2)

    lru_winb, lru_woutb = bf(lru_w_in[0]), bf(lru_w_out[0])
    waxb = bf(jnp.concatenate([lru_w_a[0], lru_w_x[0]], axis=-1))
    xp, conv_p, lru_p = _lru_prompt(xp, lru_norm[0], lru_winb, lru_conv_w[0], lru_conv_b[0], waxb,
                                    lru_b_a[0], lru_b_x[0], lru_lam[0], lru_woutb)
    xs, conv_s, lru_s = _lru_sample(xs, jnp.swapaxes(state_conv[0], 0, 1), state_lru[0],
                                    lru_norm[0], lru_winb, lru_conv_w[0], lru_conv_b[0], waxb,
                                    lru_b_a[0], lru_b_x[0], lru_lam[0], lru_woutb, pos0=PAST_LEN)
    yp, ys = mlp(xp, xs, 3, final=True)

    return (yp, ys.reshape(NS, 1, D),
            pool_p[None], jnp.swapaxes(pool_s, 0, 1)[None],
            v_s.reshape(1, NS, 1, D),
            ret_p[None], ret_s[None],
            conv_p[None], jnp.swapaxes(conv_s, 0, 1)[None],
            lru_p[None], lru_s[None])
```

```python
import functools
import math

import jax
import jax.numpy as jnp
import numpy as np
from jax import lax
from jax.experimental import pallas as pl
from jax.experimental.pallas import tpu as pltpu

F32 = jnp.float32
BF16 = jnp.bfloat16

D = 1024
EPS = 1e-6
GN_EPS = 1e-5
PAST_LEN = 16384
POOL_WINDOWS = (2, 4, 8, 16)
POOL_GROUP = D // len(POOL_WINDOWS)
POOL_BUF = max(POOL_WINDOWS) - 1
CHUNK = 128
GM_GROUPS = 8
HEADS = 8
DK = D // HEADS
DV = 2 * D // HEADS
QK = HEADS * DK
VW = HEADS * DV
ROPE_BASE = 10000.0
CONV_W = 4
LRU_C = 8.0
D_FF = 4 * D

MIB = 1024 * 1024
SUBLANES = 8
LANES = 128


def _params(vmem_mib, n_grid):
    return pltpu.CompilerParams(
        dimension_semantics=("arbitrary",) * n_grid,
        vmem_limit_bytes=vmem_mib * MIB,
    )


def _const(shape):
    zeros = (0,) * len(shape)
    return pl.BlockSpec(shape, lambda *_: zeros, pipeline_mode=pl.Buffered(1))


def _rms(x, g):
    ms = jnp.mean(x * x, axis=-1, keepdims=True)
    return x * lax.rsqrt(ms + EPS) * g


def _dot(a, b):
    return jnp.dot(a, b, preferred_element_type=F32)


def _interleave(*queues):
    pos = [0] * len(queues)
    while any(p < len(q) for p, q in zip(pos, queues)):
        _, i = min((pos[i] / len(q), i) for i, q in enumerate(queues) if pos[i] < len(q))
        queues[i][pos[i]]()
        pos[i] += 1


def _softplus(x):
    return jnp.maximum(x, 0.0) + jnp.log1p(jnp.exp(-jnp.abs(x)))


GELU_K = math.sqrt(2.0 / math.pi)


def _gelu(x):
    half = 0.5 * x
    return half + half * jnp.tanh(x * (GELU_K + (GELU_K * 0.044715) * (x * x)))


MLP_FC = 512
STAGE_BYTES = 2 * MIB


def _load_cast(src, dst, stg, sem):
    ch = stg.shape[1]
    n = src.shape[0] // ch

    def copy(c, slot):
        return pltpu.make_async_copy(src.at[pl.ds(c * ch, ch), :], stg.at[slot], sem.at[slot])

    copy(0, 0).start()
    for c in range(n):
        slot = c % 2
        if c + 1 < n:
            copy(c + 1, 1 - slot).start()
        copy(c, slot).wait()
        dst[c * ch:(c + 1) * ch, :] = stg[slot].astype(BF16)


def _mlp_thunks(read_x, write_out, g_ref, fg_ref, w1b, w2b, final, fetch_w1=None, fetch_w2=None):
    st = {}

    def head():
        x = read_x()
        st["h"] = _rms(x, g_ref[...]).astype(BF16)
        st["acc"] = x

    def up(c):
        def run():
            if fetch_w1 is not None:
                fetch_w1(c)
            a = _dot(st["h"], w1b[:, c * MLP_FC:(c + 1) * MLP_FC])
            st["a"] = jnp.square(jnp.maximum(a, 0.0)).astype(BF16)
        return run

    def down(c):
        def run():
            if fetch_w2 is not None:
                fetch_w2(c)
            st["acc"] = st["acc"] + _dot(st["a"], w2b[c * MLP_FC:(c + 1) * MLP_FC, :])
        return run

    def tail():
        acc = st["acc"]
        write_out(_rms(acc, fg_ref[...]) if final else acc)

    return [head] + [f(c) for c in range(D_FF // MLP_FC) for f in (up, down)] + [tail]


def _run(thunks):
    for thunk in thunks:
        thunk()


def _store(ref):
    def write(v):
        ref[...] = v
    return write


def _weight_stream(w1_hbm, w2_hbm, li, w1b, w2b, stg1, stg2, sem1, sem2):
    nch = D_FF // MLP_FC

    def w1_copy(c):
        return pltpu.make_async_copy(w1_hbm.at[li, :, pl.ds(c * MLP_FC, MLP_FC)], stg1.at[c % 2], sem1.at[c % 2])

    def w2_copy(c):
        return pltpu.make_async_copy(w2_hbm.at[li, pl.ds(c * MLP_FC, MLP_FC), :], stg2.at[c % 2], sem2.at[c % 2])

    def prime():
        for c in range(min(2, nch)):
            w1_copy(c).start()
            w2_copy(c).start()

    def fetch_w1(c):
        w1_copy(c).wait()
        w1b[:, c * MLP_FC:(c + 1) * MLP_FC] = stg1[c % 2].astype(BF16)
        if c + 2 < nch:
            w1_copy(c + 2).start()

    def fetch_w2(c):
        w2_copy(c).wait()
        w2b[c * MLP_FC:(c + 1) * MLP_FC, :] = stg2[c % 2].astype(BF16)
        if c + 2 < nch:
            w2_copy(c + 2).start()

    return prime, fetch_w1, fetch_w2


def _mlp_body(xp_ref, xs_ref, g_ref, fg_ref, w1_hbm, w2_hbm, op_ref, os_ref,
              w1b, w2b, stg1, stg2, sem1, sem2, *, li, final, n_p):
    i = pl.program_id(0)
    prime, fetch_w1, fetch_w2 = _weight_stream(w1_hbm, w2_hbm, li, w1b, w2b, stg1, stg2, sem1, sem2)

    @pl.when(i == 0)
    def _():
        prime()
        _run(_mlp_thunks(lambda: xp_ref[...], _store(op_ref), g_ref, fg_ref, w1b, w2b, final,
                         fetch_w1, fetch_w2))

    @pl.when((i > 0) & (i < n_p))
    def _():
        _run(_mlp_thunks(lambda: xp_ref[...], _store(op_ref), g_ref, fg_ref, w1b, w2b, final))

    @pl.when(i == n_p)
    def _():
        _run(_mlp_thunks(lambda: xs_ref[...], _store(os_ref), g_ref, fg_ref, w1b, w2b, final))


def _mlp(xp2d, xs2d, g, w1, w2, fg, *, li, final, tm=512):
    n, ns = xp2d.shape[0], xs2d.shape[0]
    n_p = n // tm
    row = pl.BlockSpec((tm, D), lambda i: (jnp.minimum(i, n_p - 1), 0))
    srow = pl.BlockSpec((ns, D), lambda i: (0, 0))
    hbm = pl.BlockSpec(memory_space=pl.ANY)
    return pl.pallas_call(
        functools.partial(_mlp_body, li=li, final=final, n_p=n_p),
        grid=(n_p + 1,),
        in_specs=[row, _const((ns, D)), _const((1, D)), _const((1, D)), hbm, hbm],
        out_specs=[row, srow],
        out_shape=[jax.ShapeDtypeStruct((n, D), F32), jax.ShapeDtypeStruct((ns, D), F32)],
        scratch_shapes=[pltpu.VMEM((D, D_FF), BF16), pltpu.VMEM((D_FF, D), BF16),
                        pltpu.VMEM((2, D, MLP_FC), F32), pltpu.VMEM((2, MLP_FC, D), F32),
                        pltpu.SemaphoreType.DMA((2,)), pltpu.SemaphoreType.DMA((2,))],
        compiler_params=_params(48, 1),
        name="mlp",
    )(xp2d, xs2d, g.reshape(1, D), fg.reshape(1, D), w1, w2)


POOL_HALO = 16


def _pool_thunks(x_ref, g_ref, w_ref, sc_ref, write_out, buf_ref, ext_ref, *, T, t):
    st = {}

    def head():
        ext_ref[0:POOL_HALO, :] = jnp.where(t == 0, 0.0, ext_ref[0:POOL_HALO, :])
        x = x_ref[...]
        h = _rms(x, g_ref[...])
        ext_ref[POOL_HALO:POOL_HALO + T, :] = h
        st["x"], st["h"] = x, h
        st["pos"] = t * T + lax.broadcasted_iota(jnp.int32, (T, 1), 0)
        st["y"] = [None] * len(POOL_WINDOWS)

    def group(gi, w):
        def run():
            c0, c1 = gi * POOL_GROUP, (gi + 1) * POOL_GROUP
            hg = st["h"][:, c0:c1]
            s = hg
            for k in range(1, w):
                s = s + ext_ref[POOL_HALO - k:POOL_HALO - k + T, c0:c1]
            cnt = jnp.minimum(st["pos"] + 1, w).astype(F32)
            d = s / cnt - hg
            st["y"][gi] = _dot(d.astype(BF16), w_ref[gi])
        return run

    def tail():
        y = jnp.concatenate(st["y"], axis=-1) * sc_ref[...]
        write_out(st["x"] + y)
        last = ext_ref[T:T + POOL_HALO, :]
        ext_ref[0:POOL_HALO, :] = last
        buf_ref[...] = last[POOL_HALO - POOL_BUF:, :]

    return [head] + [group(gi, w) for gi, w in enumerate(POOL_WINDOWS)] + [tail]


def _pool_mlp_body(x_ref, xs_ref, pg_ref, pw_ref, psc_ref, g_ref, fg_ref, w1_hbm, w2_hbm,
                   o_ref, os_ref, buf_ref, ext_ref, x1_scr, w1b, w2b, stg1, stg2, sem1, sem2,
                   *, T, tiles_per_seq, n, li):
    s = pl.program_id(0)
    t = lax.rem(s, tiles_per_seq)
    slot = lax.rem(s, 2)

    @pl.when(s == 0)
    def _():
        ext_ref[0:POOL_HALO, :] = jnp.zeros((POOL_HALO, D), F32)
        _load_cast(w1_hbm.at[li], w1b, stg1, sem1)
        _load_cast(w2_hbm.at[li], w2b, stg2, sem2)

    def write_x1(v):
        x1_scr[slot] = v

    def mixer():
        return _pool_thunks(x_ref, pg_ref, pw_ref, psc_ref, write_x1, buf_ref, ext_ref, T=T, t=t)

    def mlp():
        return _mlp_thunks(lambda: x1_scr[1 - slot], _store(o_ref), g_ref, fg_ref, w1b, w2b, False)

    @pl.when(s == 0)
    def _():
        _run(mixer())

    @pl.when((s > 0) & (s < n))
    def _():
        _interleave(mlp(), mixer())

    @pl.when(s == n)
    def _():
        _run(mlp())
        _run(_mlp_thunks(lambda: xs_ref[...], _store(os_ref), g_ref, fg_ref, w1b, w2b, False))


def _pool_mlp(x, xs2d, pg, pwb, psc, g, w1, w2, fg, *, li, T=512):
    B, L, _ = x.shape
    ns = xs2d.shape[0]
    tps = L // T
    n = B * tps
    tile = lambda s: (jnp.minimum(s, n - 1) // tps, lax.rem(jnp.minimum(s, n - 1), tps), 0)
    prev = lambda s: (jnp.maximum(s - 1, 0) // tps, lax.rem(jnp.maximum(s - 1, 0), tps), 0)
    hbm = pl.BlockSpec(memory_space=pl.ANY)
    ch1 = STAGE_BYTES // (4 * D_FF)
    ch2 = STAGE_BYTES // (4 * D)
    return pl.pallas_call(
        functools.partial(_pool_mlp_body, T=T, tiles_per_seq=tps, n=n, li=li),
        grid=(n + 1,),
        in_specs=[pl.BlockSpec((None, T, D), tile), _const((ns, D)), _const((1, D)),
                  _const((len(POOL_WINDOWS), POOL_GROUP, POOL_GROUP)), _const((1, D)),
                  _const((1, D)), _const((1, D)), hbm, hbm],
        out_specs=[pl.BlockSpec((None, T, D), prev), pl.BlockSpec((ns, D), lambda s: (0, 0)),
                   pl.BlockSpec((None, POOL_BUF, D), lambda s: (jnp.minimum(s, n - 1) // tps, 0, 0))],
        out_shape=[jax.ShapeDtypeStruct((B, L, D), F32), jax.ShapeDtypeStruct((ns, D), F32),
                   jax.ShapeDtypeStruct((B, POOL_BUF, D), F32)],
        scratch_shapes=[pltpu.VMEM((T + POOL_HALO, D), F32), pltpu.VMEM((2, T, D), F32),
                        pltpu.VMEM((D, D_FF), BF16), pltpu.VMEM((D_FF, D), BF16),
                        pltpu.VMEM((2, ch1, D_FF), F32), pltpu.VMEM((2, ch2, D), F32),
                        pltpu.SemaphoreType.DMA((2,)), pltpu.SemaphoreType.DMA((2,))],
        compiler_params=_params(56, 1),
        name="pool_mlp",
    )(x, xs2d, pg.reshape(1, D), pwb, psc.reshape(1, D), g.reshape(1, D), fg.reshape(1, D), w1, w2)


def _pool_sample_body(x_ref, buf_ref, g_ref, w_ref, sc_ref, o_ref, nbuf_ref, *, pos0):
    x = x_ref[...]
    h = _rms(x, g_ref[...])
    ys = []
    for gi, w in enumerate(POOL_WINDOWS):
        c0, c1 = gi * POOL_GROUP, (gi + 1) * POOL_GROUP
        hg = h[:, c0:c1]
        s = hg
        for k in range(1, w):
            s = s + buf_ref[POOL_BUF - k, :, c0:c1]
        cnt = float(min(pos0 + 1, w))
        d = s / cnt - hg
        ys.append(_dot(d.astype(BF16), w_ref[gi]))
    y = jnp.concatenate(ys, axis=-1) * sc_ref[...]
    o_ref[...] = x + y
    nbuf_ref[0:POOL_BUF - 1] = buf_ref[1:POOL_BUF]
    nbuf_ref[POOL_BUF - 1] = h


def _pool_sample(x2d, buf, g, wb, sc, *, pos0, bt=32):
    n = x2d.shape[0]
    row = pl.BlockSpec((bt, D), lambda i: (i, 0))
    brow = pl.BlockSpec((POOL_BUF, bt, D), lambda i: (0, i, 0))
    return pl.pallas_call(
        functools.partial(_pool_sample_body, pos0=pos0),
        grid=(n // bt,),
        in_specs=[row, brow, _const((1, D)), _const((len(POOL_WINDOWS), POOL_GROUP, POOL_GROUP)), _const((1, D))],
        out_specs=[row, brow],
        out_shape=[jax.ShapeDtypeStruct((n, D), F32), jax.ShapeDtypeStruct((POOL_BUF, n, D), F32)],
        compiler_params=_params(32, 1),
        name="pool_sample",
    )(x2d, buf, g.reshape(1, D), wb, sc.reshape(1, D))


def _gmlp_front(x, g_ref, win_ref, bin_ref, lng_ref, lnb_ref):
    h = _rms(x, g_ref[...]).astype(BF16)
    z = _gelu(_dot(h, win_ref[...]) + bin_ref[...])
    u, v = z[:, :D], z[:, D:]
    mu = jnp.mean(v, axis=-1, keepdims=True)
    vc = v - mu
    var = jnp.mean(jnp.square(vc), axis=-1, keepdims=True)
    vn = vc * lax.rsqrt(var + EPS) * lng_ref[...] + lnb_ref[...]
    return u, vn


def _gmlp_prompt_body(x_ref, g_ref, win_ref, bin_ref, lng_ref, lnb_ref, ws_ref, bs_ref, wout_ref,
                      o_ref, *, T):
    ti = lax.broadcasted_iota(jnp.int32, (CHUNK, CHUNK), 0)
    si = lax.broadcasted_iota(jnp.int32, (CHUNK, CHUNK), 1)
    causal = ti >= si
    wsm = [jnp.where(causal, ws_ref[gi], 0.0).astype(BF16) for gi in range(GM_GROUPS)]
    x = x_ref[...]
    u, vn = _gmlp_front(x, g_ref, win_ref, bin_ref, lng_ref, lnb_ref)
    vnb = vn.astype(BF16)
    rows = []
    for n in range(T // CHUNK):
        r0, r1 = n * CHUNK, (n + 1) * CHUNK
        cols = [_dot(wsm[gi], vnb[r0:r1, gi * CHUNK:(gi + 1) * CHUNK]) for gi in range(GM_GROUPS)]
        rows.append(jnp.concatenate(cols, axis=-1) + bs_ref[...])
    mixed = jnp.concatenate(rows, axis=0)
    y = _dot((u * mixed).astype(BF16), wout_ref[...])
    o_ref[...] = x + y


def _gmlp_prompt(x, g, winb, b_in, ln_g, ln_b, w_s, bs_full, woutb, *, T=1024):
    B, L, _ = x.shape
    blk = pl.BlockSpec((None, T, D), lambda b, t: (b, t, 0))
    return pl.pallas_call(
        functools.partial(_gmlp_prompt_body, T=T),
        grid=(B, L // T),
        in_specs=[blk, _const((1, D)), _const((D, 2 * D)), _const((1, 2 * D)), _const((1, D)), _const((1, D)),
                  _const((GM_GROUPS, CHUNK, CHUNK)), _const((CHUNK, D)), _const((D, D))],
        out_specs=blk,
        out_shape=jax.ShapeDtypeStruct((B, L, D), F32),
        compiler_params=_params(48, 2),
        name="gmlp_prompt",
    )(x, g.reshape(1, D), winb, b_in.reshape(1, 2 * D), ln_g.reshape(1, D), ln_b.reshape(1, D),
      w_s, bs_full, woutb)


def _gmlp_sample_body(x_ref, g_ref, win_ref, bin_ref, lng_ref, lnb_ref, sw_ref, sb_ref, wout_ref,
                      o_ref, vn_ref):
    x = x_ref[...]
    u, vn = _gmlp_front(x, g_ref, win_ref, bin_ref, lng_ref, lnb_ref)
    mixed = vn * sw_ref[...] + sb_ref[...]
    y = _dot((u * mixed).astype(BF16), wout_ref[...])
    o_ref[...] = x + y
    vn_ref[...] = vn


def _gmlp_sample(x2d, g, winb, b_in, ln_g, ln_b, sw, sb, woutb):
    n = x2d.shape[0]
    full = _const((n, D))
    return pl.pallas_call(
        _gmlp_sample_body,
        grid=(1,),
        in_specs=[full, _const((1, D)), _const((D, 2 * D)), _const((1, 2 * D)), _const((1, D)), _const((1, D)),
                  _const((1, D)), _const((1, D)), _const((D, D))],
        out_specs=[pl.BlockSpec((n, D), lambda i: (0, 0)), pl.BlockSpec((n, D), lambda i: (0, 0))],
        out_shape=[jax.ShapeDtypeStruct((n, D), F32), jax.ShapeDtypeStruct((n, D), F32)],
        compiler_params=_params(32, 1),
        name="gmlp_sample",
    )(x2d, g.reshape(1, D), winb, b_in.reshape(1, 2 * D), ln_g.reshape(1, D), ln_b.reshape(1, D),
      sw, sb, woutb)


def _ret_constants(C):
    log_gamma = np.log1p(-np.exp2(-5.0 - np.arange(HEADS, dtype=np.float64)))
    idx = np.arange(C, dtype=np.float64)
    diff = idx[:, None] - idx[None, :]
    dmask = np.where(diff[None] >= 0, np.exp(log_gamma[:, None, None] * np.maximum(diff, 0.0)[None]), 0.0)
    q_dec = np.exp(log_gamma[:, None] * (idx + 1.0))
    k_dec = np.exp(log_gamma[:, None] * (C - 1.0 - idx))
    chunk_dec = np.exp(log_gamma * C)
    return dmask.astype(np.float32), q_dec.astype(np.float32), k_dec.astype(np.float32), chunk_dec


def _rope_tables(pos):
    half = DK // 2
    freqs = np.exp(-math.log(ROPE_BASE) * np.arange(half, dtype=np.float64) / half)
    ang = np.asarray(pos, dtype=np.float64)[:, None] * freqs[None]
    cos = np.concatenate([np.cos(ang), np.cos(ang)], axis=-1)
    sin = np.concatenate([-np.sin(ang), np.sin(ang)], axis=-1)
    return cos, sin


def _rope(t, cos, sin):
    return t * cos + pltpu.roll(t, DK // 2, axis=1) * sin


def _group_norm_gate(o, gate, gng, gnb):
    mu = jnp.mean(o, axis=-1, keepdims=True)
    oc = o - mu
    var = jnp.mean(jnp.square(oc), axis=-1, keepdims=True)
    on = oc * lax.rsqrt(var + GN_EPS) * gng + gnb
    return jax.nn.silu(gate) * on


def _ret_prompt_body(x_ref, g_ref, win_ref, cq_ref, sq_ref, ck_ref, sk_ref, dm_ref, qd_ref, kd_ref,
                     gng_ref, gnb_ref, wout_ref, o_ref, s_out_ref, gated_scr, s_scr,
                     *, T, chunk_dec):
    t = pl.program_id(1)

    @pl.when(t == 0)
    def _():
        s_scr[...] = jnp.zeros_like(s_scr)

    x = x_ref[...]
    h = _rms(x, g_ref[...]).astype(BF16)
    cq, sq, ck, sk = cq_ref[...], sq_ref[...], ck_ref[...], sk_ref[...]
    n_pairs = HEADS // 2

    def projection(pair, dst):
        def piece(name, lo, width):
            def run():
                dst[name] = _dot(h, win_ref[:, lo:lo + width])
            return run
        v0, g0 = 2 * QK + 2 * DV * pair, 2 * QK + VW + 2 * DV * pair
        return [piece("q", 2 * DK * pair, 2 * DK), piece("k", QK + 2 * DK * pair, 2 * DK),
                piece("v0", v0, DV), piece("v1", v0 + DV, DV),
                piece("g0", g0, DV), piece("g1", g0 + DV, DV)]

    def chunk_work(pair, src):
        state = {}

        def prep(sub):
            def run():
                q = _rope(src["q"][:, sub * DK:(sub + 1) * DK], cq, sq)
                k = _rope(src["k"][:, sub * DK:(sub + 1) * DK], ck, sk)
                state[sub] = dict(q=q, k=k, qb=q.astype(BF16), kb=k.astype(BF16), s=s_scr[2 * pair + sub])
            return run

        def chunk(sub, c):
            def run():
                hd = 2 * pair + sub
                st = state[sub]
                rows = slice(c * CHUNK, (c + 1) * CHUNK)
                vb = src["v%d" % sub][rows, :].astype(BF16)
                scores = lax.dot_general(st["qb"][rows], st["kb"][rows], (((1,), (1,)), ((), ())),
                                         preferred_element_type=F32) * dm_ref[hd]
                o = (_dot(scores.astype(BF16), vb)
                     + _dot((st["q"][rows] * qd_ref[hd]).astype(BF16), st["s"].astype(BF16)))
                kt = (st["k"][rows] * kd_ref[hd]).T.astype(BF16)
                st["s"] = st["s"] * chunk_dec[hd] + _dot(kt, vb)
                gated = _group_norm_gate(o, src["g%d" % sub][rows, :], gng_ref[:, hd * DV:(hd + 1) * DV],
                                         gnb_ref[:, hd * DV:(hd + 1) * DV])
                gated_scr[rows, hd * DV:(hd + 1) * DV] = gated.astype(BF16)
            return run

        def finish(sub):
            def run():
                s_scr[2 * pair + sub] = state[sub]["s"]
            return run

        return ([prep(0), prep(1)] + [chunk(sub, c) for c in range(T // CHUNK) for sub in range(2)]
                + [finish(0), finish(1)])

    acc = [x]

    def out_projection(pair):
        def run():
            cols = slice(pair * 2 * DV, (pair + 1) * 2 * DV)
            acc[0] = acc[0] + _dot(gated_scr[:, cols], wout_ref[cols, :])
        return run

    cur = {}
    for run in projection(0, cur):
        run()
    for pair in range(n_pairs):
        nxt = {}
        if pair + 1 < n_pairs:
            matmul_queue = projection(pair + 1, nxt)
        else:
            matmul_queue = [out_projection(p) for p in range(n_pairs - 1)]
        _interleave(matmul_queue, chunk_work(pair, cur))
        cur = nxt
    out_projection(n_pairs - 1)()
    o_ref[...] = acc[0]
    s_out_ref[...] = s_scr[...]


def _ret_prompt(x, g, winb, gn_g, gn_b, woutb, *, T=512):
    B, L, _ = x.shape
    dmask, q_dec, k_dec, chunk_dec = _ret_constants(CHUNK)
    cos, sin = _rope_tables(np.arange(L))
    scale = DK ** -0.5
    cq, sq = jnp.asarray(cos, F32), jnp.asarray(sin, F32)
    ck, sk = jnp.asarray(cos * scale, F32), jnp.asarray(sin * scale, F32)
    qd = jnp.asarray(np.broadcast_to(q_dec[:, :, None], (HEADS, CHUNK, DK)))
    kd = jnp.asarray(np.broadcast_to(k_dec[:, :, None], (HEADS, CHUNK, DK)))
    blk = pl.BlockSpec((None, T, D), lambda b, t: (b, t, 0))
    tab = pl.BlockSpec((T, DK), lambda b, t: (t, 0))
    hcc = _const((HEADS, CHUNK, CHUNK))
    return pl.pallas_call(
        functools.partial(_ret_prompt_body, T=T, chunk_dec=tuple(float(c) for c in chunk_dec)),
        grid=(B, L // T),
        in_specs=[blk, _const((1, D)), _const((D, 2 * QK + 2 * VW)), tab, tab, tab, tab,
                  hcc, hcc, hcc, _const((1, VW)), _const((1, VW)), _const((VW, D))],
        out_specs=[blk, pl.BlockSpec((None, HEADS, DK, DV), lambda b, t: (b, 0, 0, 0))],
        out_shape=[jax.ShapeDtypeStruct((B, L, D), F32), jax.ShapeDtypeStruct((B, HEADS, DK, DV), F32)],
        scratch_shapes=[pltpu.VMEM((T, VW), BF16), pltpu.VMEM((HEADS, DK, DV), F32)],
        compiler_params=_params(56, 2),
        name="ret_prompt",
    )(x, g.reshape(1, D), winb, cq, sq, ck, sk, jnp.asarray(dmask), qd, kd,
      gn_g.reshape(1, VW), gn_b.reshape(1, VW), woutb)


def _ret_proj_sample_body(x_ref, g_ref, win_ref, cq_ref, sq_ref, ck_ref, sk_ref,
                          qt_ref, kt_ref, v_ref, gate_ref, ov_ref):
    h = _rms(x_ref[...], g_ref[...]).astype(BF16)
    p = _dot(h, win_ref[...])
    for hd in range(HEADS):
        q = _rope(p[:, hd * DK:(hd + 1) * DK], cq_ref[...], sq_ref[...])
        k = _rope(p[:, QK + hd * DK:QK + (hd + 1) * DK], ck_ref[...], sk_ref[...])
        v = p[:, 2 * QK + hd * DV:2 * QK + (hd + 1) * DV]
        qt_ref[hd] = q.T
        kt_ref[hd] = k.T
        ov_ref[:, hd * DV:(hd + 1) * DV] = jnp.sum(q * k, axis=-1, keepdims=True) * v
    v_ref[...] = p[:, 2 * QK:2 * QK + VW]
    gate_ref[...] = p[:, 2 * QK + VW:]


def _ret_proj_sample(x2d, g, winb, *, pos0):
    n = x2d.shape[0]
    cos, sin = _rope_tables(np.array([pos0]))
    scale = DK ** -0.5
    tabs = [jnp.asarray(a, F32) for a in (cos, sin, cos * scale, sin * scale)]
    row = pl.BlockSpec((n, VW), lambda i: (0, 0))
    tr = pl.BlockSpec((HEADS, DK, n), lambda i: (0, 0, 0))
    return pl.pallas_call(
        _ret_proj_sample_body,
        grid=(1,),
        in_specs=[_const((n, D)), _const((1, D)), _const((D, 2 * QK + 2 * VW))] + [_const((1, DK))] * 4,
        out_specs=[tr, tr, row, row, row],
        out_shape=[jax.ShapeDtypeStruct((HEADS, DK, n), F32)] * 2 + [jax.ShapeDtypeStruct((n, VW), F32)] * 3,
        compiler_params=_params(40, 1),
        name="ret_proj_sample",
    )(x2d, g.reshape(1, D), winb, *tabs)


def _mlp_ret_body(xp_ref, xs_ref, g_ref, fg_ref, w1_hbm, w2_hbm,
                  qt_ref, kt_ref, v_ref, ov_ref, gate_ref, gng_ref, gnb_ref, rwout_ref, s_hbm,
                  op_ref, os_ref, snew_hbm,
                  w1b, w2b, stg1, stg2, sem1, sem2, s_in, s_out, sem_in, sem_out, o_scr,
                  *, li, n_p, ns, gamma):
    i = pl.program_id(0)
    spb = ns // n_p
    prime, fetch_w1, fetch_w2 = _weight_stream(w1_hbm, w2_hbm, li, w1b, w2b, stg1, stg2, sem1, sem2)

    def load(c, slot):
        return pltpu.make_async_copy(s_hbm.at[c], s_in.at[slot], sem_in.at[slot])

    def store(c, slot):
        return pltpu.make_async_copy(s_out.at[slot], snew_hbm.at[c], sem_out.at[slot])

    def state_update(j):
        def run():
            c = i * spb + j
            slot = j % 2
            load(c, slot).wait()
            store(c, slot).wait()
            lane = lax.broadcasted_iota(jnp.int32, (DK, ns), 1) == c
            for hd in range(HEADS):
                cols = slice(hd * DV, (hd + 1) * DV)
                s_old = s_in[slot, hd]
                qcol = jnp.sum(jnp.where(lane, qt_ref[hd], 0.0), axis=1, keepdims=True)
                kcol = jnp.sum(jnp.where(lane, kt_ref[hd], 0.0), axis=1, keepdims=True)
                v = v_ref[pl.ds(c, 1), cols]
                o_scr[pl.ds(c, 1), cols] = jnp.sum(s_old * (qcol * gamma[hd]), axis=0, keepdims=True)
                s_out[slot, hd] = s_old * gamma[hd] + kcol * v
            store(c, slot).start()
            load(jnp.minimum(c + 2, ns - 1), slot).start()
        return run

    def states():
        return [state_update(j) for j in range(spb)]

    @pl.when(i == 0)
    def _():
        prime()
        s_out[...] = jnp.zeros_like(s_out)
        for slot in range(2):
            load(slot, slot).start()
            store(slot, slot).start()
        _interleave(_mlp_thunks(lambda: xp_ref[...], _store(op_ref), g_ref, fg_ref, w1b, w2b, False,
                                fetch_w1, fetch_w2), states())

    @pl.when((i > 0) & (i < n_p))
    def _():
        _interleave(_mlp_thunks(lambda: xp_ref[...], _store(op_ref), g_ref, fg_ref, w1b, w2b, False), states())

    @pl.when(i == n_p)
    def _():
        for slot in range(2):
            load(ns - 1, slot).wait()
            store(ns - 1, slot).wait()
        parts = []
        for hd in range(HEADS):
            cols = slice(hd * DV, (hd + 1) * DV)
            o = ov_ref[:, cols] + o_scr[:, cols]
            parts.append(_group_norm_gate(o, gate_ref[:, cols], gng_ref[:, cols], gnb_ref[:, cols]))
        gated = jnp.concatenate(parts, axis=-1).astype(BF16)
        xs1 = xs_ref[...] + _dot(gated, rwout_ref[...])
        _run(_mlp_thunks(lambda: xs1, _store(os_ref), g_ref, fg_ref, w1b, w2b, False))


def _mlp_ret(xp2d, xs2d, g, w1, w2, fg, qt, kt, v, ov, gate, gn_g, gn_b, rwoutb, s, *, li, tm=512):
    n, ns = xp2d.shape[0], xs2d.shape[0]
    n_p = n // tm
    _, _, _, chunk_dec = _ret_constants(1)
    row = pl.BlockSpec((tm, D), lambda i: (jnp.minimum(i, n_p - 1), 0))
    srow = pl.BlockSpec((ns, D), lambda i: (0, 0))
    hbm = pl.BlockSpec(memory_space=pl.ANY)
    wide = _const((ns, VW))
    return pl.pallas_call(
        functools.partial(_mlp_ret_body, li=li, n_p=n_p, ns=ns, gamma=tuple(float(c) for c in chunk_dec)),
        grid=(n_p + 1,),
        in_specs=[row, _const((ns, D)), _const((1, D)), _const((1, D)), hbm, hbm,
                  _const((HEADS, DK, ns)), _const((HEADS, DK, ns)), wide, wide, wide,
                  _const((1, VW)), _const((1, VW)), _const((VW, D)), hbm],
        out_specs=[row, srow, hbm],
        out_shape=[jax.ShapeDtypeStruct((n, D), F32), jax.ShapeDtypeStruct((ns, D), F32),
                   jax.ShapeDtypeStruct(s.shape, F32)],
        scratch_shapes=[pltpu.VMEM((D, D_FF), BF16), pltpu.VMEM((D_FF, D), BF16),
                        pltpu.VMEM((2, D, MLP_FC), F32), pltpu.VMEM((2, MLP_FC, D), F32),
                        pltpu.SemaphoreType.DMA((2,)), pltpu.SemaphoreType.DMA((2,)),
                        pltpu.VMEM((2, HEADS, DK, DV), F32), pltpu.VMEM((2, HEADS, DK, DV), F32),
                        pltpu.SemaphoreType.DMA((2,)), pltpu.SemaphoreType.DMA((2,)),
                        pltpu.VMEM((ns, VW), F32)],
        compiler_params=_params(58, 1),
        name="mlp_ret",
    )(xp2d, xs2d, g.reshape(1, D), fg.reshape(1, D), w1, w2, qt, kt, v, ov, gate,
      gn_g.reshape(1, VW), gn_b.reshape(1, VW), rwoutb, s)


def _lru_gates(xc, wax_ref, ba, bx, lam, heads):
    xcb = xc.astype(BF16)
    rs, is_ = [], []
    for n, hd in enumerate(heads):
        ri = _dot(xcb[:, n * DK:(n + 1) * DK], wax_ref[hd])
        rs.append(ri[:, :DK])
        is_.append(ri[:, DK:])
    r = jax.nn.sigmoid(jnp.concatenate(rs, axis=-1) + ba)
    i = jax.nn.sigmoid(jnp.concatenate(is_, axis=-1) + bx)
    log_a = r * (-LRU_C * _softplus(-lam))
    a = jnp.exp(log_a)
    mult = jnp.sqrt(-jnp.tanh(log_a) * (a * a + 1.0))
    return a, mult, i


LRU_GROUP_HEADS = 2


def _lru_prompt_body(x_ref, g_ref, win_ref, cw_ref, cb_ref, wax_ref, ba_ref, bx_ref, lam_ref, wout_ref,
                     o_ref, conv_ref, hlast_ref, xb_scr, hs_scr, carry_scr, h_scr, *, NB, T):
    t = pl.program_id(0)
    N = NB * T
    PS = T + 1
    halo = (CONV_W - 1) * NB
    gw = LRU_GROUP_HEADS * DK
    gblk = gw // LANES

    @pl.when(t == 0)
    def _():
        carry_scr[...] = jnp.zeros_like(carry_scr)
        h_scr[...] = jnp.zeros_like(h_scr)

    x = x_ref[...].reshape(N, D)
    h = _rms(x, g_ref[...]).astype(BF16)
    row = lax.broadcasted_iota(jnp.int32, (N, 1), 0)
    pos = t * T + lax.shift_right_logical(row, NB.bit_length() - 1)
    out = x
    def project(grp):
        c0, c1 = grp * gw, (grp + 1) * gw
        return _dot(h, win_ref[:, c0:c1]), _dot(h, win_ref[:, D + c0:D + c1])

    ngrp = D // gw
    nxt = project(0)
    for grp in range(ngrp):
        c0, c1 = grp * gw, (grp + 1) * gw
        blks = range(grp * gblk, (grp + 1) * gblk)
        zg, xb = nxt
        if grp + 1 < ngrp:
            nxt = project(grp + 1)
        gate = _gelu(zg)
        for n, j in enumerate(blks):
            for b in range(NB):
                xb_scr[j, b * PS:b * PS + T, :] = xb[b * T:(b + 1) * T, n * LANES:(n + 1) * LANES]
        ext = jnp.concatenate(
            [carry_scr[:, c0:c1]]
            + [jnp.concatenate([xb_scr[j, pl.ds(s, NB, stride=PS), :] for j in blks], axis=-1)
               for s in range(T)], axis=0)
        carry_scr[:, c0:c1] = ext[N:N + halo, :]
        acc = ext[0:N, :] * cw_ref[0:1, c0:c1]
        for j in range(1, CONV_W):
            acc = acc + ext[j * NB:j * NB + N, :] * cw_ref[j:j + 1, c0:c1]
        xc = cb_ref[:, c0:c1] + acc
        heads = range(grp * LRU_GROUP_HEADS, (grp + 1) * LRU_GROUP_HEADS)
        a, mult, i = _lru_gates(xc, wax_ref, ba_ref[:, c0:c1], bx_ref[:, c0:c1], lam_ref[:, c0:c1], heads)
        bvec = jnp.where(pos == 0, 1.0, mult) * (i * xc)
        hcur = h_scr[:, c0:c1]
        for s in range(T):
            hcur = a[s * NB:(s + 1) * NB, :] * hcur + bvec[s * NB:(s + 1) * NB, :]
            for n, j in enumerate(blks):
                hs_scr[j, pl.ds(s, NB, stride=PS), :] = hcur[:, n * LANES:(n + 1) * LANES]
        h_scr[:, c0:c1] = hcur
        hs = jnp.concatenate(
            [jnp.concatenate([hs_scr[j, b * PS:b * PS + T, :] for b in range(NB)], axis=0) for j in blks],
            axis=-1)
        out = out + _dot((hs * gate).astype(BF16), wout_ref[c0:c1, :])
    o_ref[...] = out.reshape(NB, T, D)
    hlast_ref[...] = h_scr[...]

    @pl.when(t == pl.num_programs(0) - 1)
    def _():
        for j in range(CONV_W - 1):
            conv_ref[:, j, :] = carry_scr[j * NB:(j + 1) * NB, :]


def _lru_prompt(x, g, winb, conv_w, conv_b, waxb, b_a, b_x, lam, woutb, *, T=128):
    B, L, _ = x.shape
    blk = pl.BlockSpec((B, T, D), lambda t: (0, t, 0))
    vec = _const((1, D))
    return pl.pallas_call(
        functools.partial(_lru_prompt_body, NB=B, T=T),
        grid=(L // T,),
        in_specs=[blk, vec, _const((D, 2 * D)), _const((CONV_W, D)), vec, _const((HEADS, DK, 2 * DK)),
                  vec, vec, vec, _const((D, D))],
        out_specs=[blk, pl.BlockSpec((B, CONV_W - 1, D), lambda t: (0, 0, 0)),
                   pl.BlockSpec((B, D), lambda t: (0, 0))],
        out_shape=[jax.ShapeDtypeStruct((B, L, D), F32), jax.ShapeDtypeStruct((B, CONV_W - 1, D), F32),
                   jax.ShapeDtypeStruct((B, D), F32)],
        scratch_shapes=[pltpu.VMEM((D // LANES, B * (T + 1), LANES), F32)] * 2
        + [pltpu.VMEM(((CONV_W - 1) * B, D), F32), pltpu.VMEM((B, D), F32)],
        compiler_params=_params(48, 1),
        name="lru_prompt",
    )(x, g.reshape(1, D), winb, conv_w, conv_b.reshape(1, D), waxb, b_a.reshape(1, D), b_x.reshape(1, D),
      lam.reshape(1, D), woutb)


def _lru_sample_body(x_ref, cbuf_ref, h0_ref, g_ref, win_ref, cw_ref, cb_ref, wax_ref, ba_ref, bx_ref,
                     lam_ref, wout_ref, o_ref, nconv_ref, hnew_ref, *, pos0):
    x = x_ref[...]
    h = _rms(x, g_ref[...]).astype(BF16)
    z = _dot(h, win_ref[...])
    gate = _gelu(z[:, :D])
    xb = z[:, D:]
    cw = cw_ref[...]
    acc = cbuf_ref[0] * cw[0:1, :]
    for j in range(1, CONV_W - 1):
        acc = acc + cbuf_ref[j] * cw[j:j + 1, :]
    acc = acc + xb * cw[CONV_W - 1:CONV_W, :]
    xc = cb_ref[...] + acc
    a, mult, i = _lru_gates(xc, wax_ref, ba_ref[...], bx_ref[...], lam_ref[...], range(HEADS))
    if pos0 == 0:
        mult = jnp.ones_like(mult)
    hnew = a * h0_ref[...] + mult * (i * xc)
    hnew_ref[...] = hnew
    nconv_ref[0:CONV_W - 2] = cbuf_ref[1:CONV_W - 1]
    nconv_ref[CONV_W - 2] = xb
    o_ref[...] = x + _dot((hnew * gate).astype(BF16), wout_ref[...])


def _lru_sample(x2d, cbuf, h0, g, winb, conv_w, conv_b, waxb, b_a, b_x, lam, woutb, *, pos0):
    n = x2d.shape[0]
    vec = _const((1, D))
    cshape = (CONV_W - 1, n, D)
    out = pl.BlockSpec((n, D), lambda i: (0, 0))
    return pl.pallas_call(
        functools.partial(_lru_sample_body, pos0=pos0),
        grid=(1,),
        in_specs=[_const((n, D)), _const(cshape), _const((n, D)), vec, _const((D, 2 * D)), _const((CONV_W, D)),
                  vec, _const((HEADS, DK, 2 * DK)), vec, vec, vec, _const((D, D))],
        out_specs=[out, pl.BlockSpec(cshape, lambda i: (0, 0, 0)), out],
        out_shape=[jax.ShapeDtypeStruct((n, D), F32), jax.ShapeDtypeStruct(cshape, F32),
                   jax.ShapeDtypeStruct((n, D), F32)],
        compiler_params=_params(32, 1),
        name="lru_sample",
    )(x2d, cbuf, h0, g.reshape(1, D), winb, conv_w, conv_b.reshape(1, D), waxb, b_a.reshape(1, D),
      b_x.reshape(1, D), lam.reshape(1, D), woutb)


def kernel(x_prompt, x_sample, state_pool, state_ret, state_conv, state_lru, pool_norm, pool_w, pool_scale, gm_norm, gm_w_in, gm_b_in, gm_ln_g, gm_ln_b, gm_w_s, gm_b_s, gm_w_out, ret_norm, ret_w_in, ret_gn_g, ret_gn_b, ret_w_out, lru_norm, lru_w_in, lru_conv_w, lru_conv_b, lru_w_a, lru_b_a, lru_w_x, lru_b_x, lru_lam, lru_w_out, mlp_norm, mlp_w1, mlp_w2, final_norm):
    B, L, _ = x_prompt.shape
    NS = x_sample.shape[0]
    bf = lambda w: w.astype(BF16)

    def mlp(xp, xs, li, final=False):
        yp, ys = _mlp(xp.reshape(B * L, D), xs, mlp_norm[li], mlp_w1, mlp_w2, final_norm, li=li, final=final)
        return yp.reshape(B, L, D), ys

    pool_wb = bf(pool_w[0])
    xs, pool_s = _pool_sample(x_sample.reshape(NS, D), jnp.swapaxes(state_pool[0], 0, 1),
                              pool_norm[0], pool_wb, pool_scale[0], pos0=PAST_LEN)
    xp, xs, pool_p = _pool_mlp(x_prompt, xs, pool_norm[0], pool_wb, pool_scale[0], mlp_norm[0],
                               mlp_w1, mlp_w2, final_norm, li=0)

    gm_winb, gm_woutb = bf(gm_w_in[0]), bf(gm_w_out[0])
    bs_full = jnp.repeat(gm_b_s[0].T, CHUNK, axis=1)
    xp = _gmlp_prompt(xp, gm_norm[0], gm_winb, gm_b_in[0], gm_ln_g[0], gm_ln_b[0], gm_w_s[0], bs_full, gm_woutb)
    sw = jnp.repeat(gm_w_s[0][:, 0, 0], CHUNK).reshape(1, D)
    sb = jnp.repeat(gm_b_s[0][:, 0], CHUNK).reshape(1, D)
    xs, v_s = _gmlp_sample(xs, gm_norm[0], gm_winb, gm_b_in[0], gm_ln_g[0], gm_ln_b[0], sw, sb, gm_woutb)
    xp, xs = mlp(xp, xs, 1)

    ret_winb, ret_woutb = bf(ret_w_in[0]), bf(ret_w_out[0])
    xp, ret_p = _ret_prompt(xp, ret_norm[0], ret_winb, ret_gn_g[0], ret_gn_b[0], ret_woutb)
    qt, kt, v, gate, ov = _ret_proj_sample(xs, ret_norm[0], ret_winb, pos0=PAST_LEN)
    xp, xs, ret_s = _mlp_ret(xp.reshape(B * L, D), xs, mlp_norm[2], mlp_w1, mlp_w2, final_norm,
                             qt, kt, v, ov, gate, ret_gn_g[0], ret_gn_b[0], ret_woutb, state_ret[0], li=2)
    xp = xp.reshape(B, L, D)

    lru_winb, lru_woutb = bf(lru_w_in[0]), bf(lru_w_out[0])
    waxb = bf(jnp.concatenate([lru_w_a[0], lru_w_x[0]], axis=-1))
    xp, conv_p, lru_p = _lru_prompt(xp, lru_norm[0], lru_winb, lru_conv_w[0], lru_conv_b[0], waxb,
                                    lru_b_a[0], lru_b_x[0], lru_lam[0], lru_woutb)
    xs, conv_s, lru_s = _lru_sample(xs, jnp.swapaxes(state_conv[0], 0, 1), state_lru[0],
                                    lru_norm[0], lru_winb, lru_conv_w[0], lru_conv_b[0], waxb,
                                    lru_b_a[0], lru_b_x[0], lru_lam[0], lru_woutb, pos0=PAST_LEN)
    yp, ys = mlp(xp, xs, 3, final=True)

    return (yp, ys.reshape(NS, 1, D),
            pool_p[None], jnp.swapaxes(pool_s, 0, 1)[None],
            v_s.reshape(1, NS, 1, D),
            ret_p[None], ret_s[None],
            conv_p[None], jnp.swapaxes(conv_s, 0, 1)[None],
            lru_p[None], lru_s[None])
```

```python
import functools
import math

import jax
import jax.numpy as jnp
import numpy as np
from jax import lax
from jax.experimental import pallas as pl
from jax.experimental.pallas import tpu as pltpu

F32 = jnp.float32
BF16 = jnp.bfloat16

D = 1024
EPS = 1e-6
GN_EPS = 1e-5
PAST_LEN = 16384
POOL_WINDOWS = (2, 4, 8, 16)
POOL_GROUP = D // len(POOL_WINDOWS)
POOL_BUF = max(POOL_WINDOWS) - 1
CHUNK = 128
GM_GROUPS = 8
HEADS = 8
DK = D // HEADS
DV = 2 * D // HEADS
QK = HEADS * DK
VW = HEADS * DV
ROPE_BASE = 10000.0
CONV_W = 4
LRU_C = 8.0
D_FF = 4 * D

MIB = 1024 * 1024
SUBLANES = 8
LANES = 128


def _params(vmem_mib, n_grid):
    return pltpu.CompilerParams(
        dimension_semantics=("arbitrary",) * n_grid,
        vmem_limit_bytes=vmem_mib * MIB,
    )


def _const(shape):
    zeros = (0,) * len(shape)
    return pl.BlockSpec(shape, lambda *_: zeros, pipeline_mode=pl.Buffered(1))


def _rms(x, g):
    ms = jnp.mean(x * x, axis=-1, keepdims=True)
    return x * lax.rsqrt(ms + EPS) * g


def _dot(a, b):
    return jnp.dot(a, b, preferred_element_type=F32)


def _interleave(*queues):
    pos = [0] * len(queues)
    while any(p < len(q) for p, q in zip(pos, queues)):
        _, i = min((pos[i] / len(q), i) for i, q in enumerate(queues) if pos[i] < len(q))
        queues[i][pos[i]]()
        pos[i] += 1


def _softplus(x):
    return jnp.maximum(x, 0.0) + jnp.log1p(jnp.exp(-jnp.abs(x)))


GELU_K = math.sqrt(2.0 / math.pi)


def _gelu(x):
    half = 0.5 * x
    return half + half * jnp.tanh(x * (GELU_K + (GELU_K * 0.044715) * (x * x)))


MLP_FC = 512
STAGE_BYTES = 2 * MIB


def _load_cast(src, dst, stg, sem):
    ch = stg.shape[1]
    n = src.shape[0] // ch

    def copy(c, slot):
        return pltpu.make_async_copy(src.at[pl.ds(c * ch, ch), :], stg.at[slot], sem.at[slot])

    copy(0, 0).start()
    for c in range(n):
        slot = c % 2
        if c + 1 < n:
            copy(c + 1, 1 - slot).start()
        copy(c, slot).wait()
        dst[c * ch:(c + 1) * ch, :] = stg[slot].astype(BF16)


def _mlp_thunks(read_x, write_out, g_ref, fg_ref, w1b, w2b, final, fetch_w1=None, fetch_w2=None):
    st = {}

    def head():
        x = read_x()
        st["h"] = _rms(x, g_ref[...]).astype(BF16)
        st["acc"] = x

    def up(c):
        def run():
            if fetch_w1 is not None:
                fetch_w1(c)
            a = _dot(st["h"], w1b[:, c * MLP_FC:(c + 1) * MLP_FC])
            st["a"] = jnp.square(jnp.maximum(a, 0.0)).astype(BF16)
        return run

    def down(c):
        def run():
            if fetch_w2 is not None:
                fetch_w2(c)
            st["acc"] = st["acc"] + _dot(st["a"], w2b[c * MLP_FC:(c + 1) * MLP_FC, :])
        return run

    def tail():
        acc = st["acc"]
        write_out(_rms(acc, fg_ref[...]) if final else acc)

    return [head] + [f(c) for c in range(D_FF // MLP_FC) for f in (up, down)] + [tail]


def _run(thunks):
    for thunk in thunks:
        thunk()


def _store(ref):
    def write(v):
        ref[...] = v
    return write


def _weight_stream(w1_hbm, w2_hbm, li, w1b, w2b, stg1, stg2, sem1, sem2):
    nch = D_FF // MLP_FC

    def w1_copy(c):
        return pltpu.make_async_copy(w1_hbm.at[li, :, pl.ds(c * MLP_FC, MLP_FC)], stg1.at[c % 2], sem1.at[c % 2])

    def w2_copy(c):
        return pltpu.make_async_copy(w2_hbm.at[li, pl.ds(c * MLP_FC, MLP_FC), :], stg2.at[c % 2], sem2.at[c % 2])

    def prime():
        for c in range(min(2, nch)):
            w1_copy(c).start()
            w2_copy(c).start()

    def fetch_w1(c):
        w1_copy(c).wait()
        w1b[:, c * MLP_FC:(c + 1) * MLP_FC] = stg1[c % 2].astype(BF16)
        if c + 2 < nch:
            w1_copy(c + 2).start()

    def fetch_w2(c):
        w2_copy(c).wait()
        w2b[c * MLP_FC:(c + 1) * MLP_FC, :] = stg2[c % 2].astype(BF16)
        if c + 2 < nch:
            w2_copy(c + 2).start()

    return prime, fetch_w1, fetch_w2


def _mlp_body(xp_ref, xs_ref, g_ref, fg_ref, w1_hbm, w2_hbm, op_ref, os_ref,
              w1b, w2b, stg1, stg2, sem1, sem2, *, li, final, n_p):
    i = pl.program_id(0)
    prime, fetch_w1, fetch_w2 = _weight_stream(w1_hbm, w2_hbm, li, w1b, w2b, stg1, stg2, sem1, sem2)

    @pl.when(i == 0)
    def _():
        prime()
        _run(_mlp_thunks(lambda: xp_ref[...], _store(op_ref), g_ref, fg_ref, w1b, w2b, final,
                         fetch_w1, fetch_w2))

    @pl.when((i > 0) & (i < n_p))
    def _():
        _run(_mlp_thunks(lambda: xp_ref[...], _store(op_ref), g_ref, fg_ref, w1b, w2b, final))

    @pl.when(i == n_p)
    def _():
        _run(_mlp_thunks(lambda: xs_ref[...], _store(os_ref), g_ref, fg_ref, w1b, w2b, final))


def _mlp(xp2d, xs2d, g, w1, w2, fg, *, li, final, tm=512):
    n, ns = xp2d.shape[0], xs2d.shape[0]
    n_p = n // tm
    row = pl.BlockSpec((tm, D), lambda i: (jnp.minimum(i, n_p - 1), 0))
    srow = pl.BlockSpec((ns, D), lambda i: (0, 0))
    hbm = pl.BlockSpec(memory_space=pl.ANY)
    return pl.pallas_call(
        functools.partial(_mlp_body, li=li, final=final, n_p=n_p),
        grid=(n_p + 1,),
        in_specs=[row, _const((ns, D)), _const((1, D)), _const((1, D)), hbm, hbm],
        out_specs=[row, srow],
        out_shape=[jax.ShapeDtypeStruct((n, D), F32), jax.ShapeDtypeStruct((ns, D), F32)],
        scratch_shapes=[pltpu.VMEM((D, D_FF), BF16), pltpu.VMEM((D_FF, D), BF16),
                        pltpu.VMEM((2, D, MLP_FC), F32), pltpu.VMEM((2, MLP_FC, D), F32),
                        pltpu.SemaphoreType.DMA((2,)), pltpu.SemaphoreType.DMA((2,))],
        compiler_params=_params(48, 1),
        name="mlp",
    )(xp2d, xs2d, g.reshape(1, D), fg.reshape(1, D), w1, w2)


POOL_HALO = 16


def _pool_thunks(x_ref, g_ref, w_ref, sc_ref, write_out, buf_ref, ext_ref, *, T, t):
    st = {}

    def head():
        ext_ref[0:POOL_HALO, :] = jnp.where(t == 0, 0.0, ext_ref[0:POOL_HALO, :])
        x = x_ref[...]
        h = _rms(x, g_ref[...])
        ext_ref[POOL_HALO:POOL_HALO + T, :] = h
        st["x"], st["h"] = x, h
        st["pos"] = t * T + lax.broadcasted_iota(jnp.int32, (T, 1), 0)
        st["y"] = [None] * len(POOL_WINDOWS)

    def group(gi, w):
        def run():
            c0, c1 = gi * POOL_GROUP, (gi + 1) * POOL_GROUP
            hg = st["h"][:, c0:c1]
            s = hg
            for k in range(1, w):
                s = s + ext_ref[POOL_HALO - k:POOL_HALO - k + T, c0:c1]
            cnt = jnp.minimum(st["pos"] + 1, w).astype(F32)
            d = s / cnt - hg
            st["y"][gi] = _dot(d.astype(BF16), w_ref[gi])
        return run

    def tail():
        y = jnp.concatenate(st["y"], axis=-1) * sc_ref[...]
        write_out(st["x"] + y)
        last = ext_ref[T:T + POOL_HALO, :]
        ext_ref[0:POOL_HALO, :] = last
        buf_ref[...] = last[POOL_HALO - POOL_BUF:, :]

    return [head] + [group(gi, w) for gi, w in enumerate(POOL_WINDOWS)] + [tail]


def _pool_mlp_body(x_ref, xs_ref, pg_ref, pw_ref, psc_ref, g_ref, fg_ref, w1_hbm, w2_hbm,
                   o_ref, os_ref, buf_ref, ext_ref, x1_scr, w1b, w2b, stg1, stg2, sem1, sem2,
                   *, T, tiles_per_seq, n, li):
    s = pl.program_id(0)
    t = lax.rem(s, tiles_per_seq)
    slot = lax.rem(s, 2)
    prime, fetch_w1, fetch_w2 = _weight_stream(w1_hbm, w2_hbm, li, w1b, w2b, stg1, stg2, sem1, sem2)

    def write_x1(v):
        x1_scr[slot] = v

    def mixer():
        return _pool_thunks(x_ref, pg_ref, pw_ref, psc_ref, write_x1, buf_ref, ext_ref, T=T, t=t)

    def mlp(*fetch):
        return _mlp_thunks(lambda: x1_scr[1 - slot], _store(o_ref), g_ref, fg_ref, w1b, w2b, False, *fetch)

    @pl.when(s == 0)
    def _():
        ext_ref[0:POOL_HALO, :] = jnp.zeros((POOL_HALO, D), F32)
        prime()
        _run(mixer())

    @pl.when(s == 1)
    def _():
        _interleave(mlp(fetch_w1, fetch_w2), mixer())

    @pl.when((s > 1) & (s < n))
    def _():
        _interleave(mlp(), mixer())

    @pl.when(s == n)
    def _():
        _run(mlp())
        _run(_mlp_thunks(lambda: xs_ref[...], _store(os_ref), g_ref, fg_ref, w1b, w2b, False))


def _pool_mlp(x, xs2d, pg, pwb, psc, g, w1, w2, fg, *, li, T=512):
    B, L, _ = x.shape
    ns = xs2d.shape[0]
    tps = L // T
    n = B * tps
    tile = lambda s: (jnp.minimum(s, n - 1) // tps, lax.rem(jnp.minimum(s, n - 1), tps), 0)
    prev = lambda s: (jnp.maximum(s - 1, 0) // tps, lax.rem(jnp.maximum(s - 1, 0), tps), 0)
    hbm = pl.BlockSpec(memory_space=pl.ANY)
    return pl.pallas_call(
        functools.partial(_pool_mlp_body, T=T, tiles_per_seq=tps, n=n, li=li),
        grid=(n + 1,),
        in_specs=[pl.BlockSpec((None, T, D), tile), _const((ns, D)), _const((1, D)),
                  _const((len(POOL_WINDOWS), POOL_GROUP, POOL_GROUP)), _const((1, D)),
                  _const((1, D)), _const((1, D)), hbm, hbm],
        out_specs=[pl.BlockSpec((None, T, D), prev), pl.BlockSpec((ns, D), lambda s: (0, 0)),
                   pl.BlockSpec((None, POOL_BUF, D), lambda s: (jnp.minimum(s, n - 1) // tps, 0, 0))],
        out_shape=[jax.ShapeDtypeStruct((B, L, D), F32), jax.ShapeDtypeStruct((ns, D), F32),
                   jax.ShapeDtypeStruct((B, POOL_BUF, D), F32)],
        scratch_shapes=[pltpu.VMEM((T + POOL_HALO, D), F32), pltpu.VMEM((2, T, D), F32),
                        pltpu.VMEM((D, D_FF), BF16), pltpu.VMEM((D_FF, D), BF16),
                        pltpu.VMEM((2, D, MLP_FC), F32), pltpu.VMEM((2, MLP_FC, D), F32),
                        pltpu.SemaphoreType.DMA((2,)), pltpu.SemaphoreType.DMA((2,))],
        compiler_params=_params(56, 1),
        name="pool_mlp",
    )(x, xs2d, pg.reshape(1, D), pwb, psc.reshape(1, D), g.reshape(1, D), fg.reshape(1, D), w1, w2)


def _pool_sample_body(x_ref, buf_ref, g_ref, w_ref, sc_ref, o_ref, nbuf_ref, *, pos0):
    x = x_ref[...]
    h = _rms(x, g_ref[...])
    ys = []
    for gi, w in enumerate(POOL_WINDOWS):
        c0, c1 = gi * POOL_GROUP, (gi + 1) * POOL_GROUP
        hg = h[:, c0:c1]
        s = hg
        for k in range(1, w):
            s = s + buf_ref[POOL_BUF - k, :, c0:c1]
        cnt = float(min(pos0 + 1, w))
        d = s / cnt - hg
        ys.append(_dot(d.astype(BF16), w_ref[gi]))
    y = jnp.concatenate(ys, axis=-1) * sc_ref[...]
    o_ref[...] = x + y
    nbuf_ref[0:POOL_BUF - 1] = buf_ref[1:POOL_BUF]
    nbuf_ref[POOL_BUF - 1] = h


def _pool_sample(x2d, buf, g, wb, sc, *, pos0, bt=32):
    n = x2d.shape[0]
    row = pl.BlockSpec((bt, D), lambda i: (i, 0))
    brow = pl.BlockSpec((POOL_BUF, bt, D), lambda i: (0, i, 0))
    return pl.pallas_call(
        functools.partial(_pool_sample_body, pos0=pos0),
        grid=(n // bt,),
        in_specs=[row, brow, _const((1, D)), _const((len(POOL_WINDOWS), POOL_GROUP, POOL_GROUP)), _const((1, D))],
        out_specs=[row, brow],
        out_shape=[jax.ShapeDtypeStruct((n, D), F32), jax.ShapeDtypeStruct((POOL_BUF, n, D), F32)],
        compiler_params=_params(32, 1),
        name="pool_sample",
    )(x2d, buf, g.reshape(1, D), wb, sc.reshape(1, D))


def _gmlp_front(x, g_ref, win_ref, bin_ref, lng_ref, lnb_ref):
    h = _rms(x, g_ref[...]).astype(BF16)
    z = _gelu(_dot(h, win_ref[...]) + bin_ref[...])
    u, v = z[:, :D], z[:, D:]
    mu = jnp.mean(v, axis=-1, keepdims=True)
    vc = v - mu
    var = jnp.mean(jnp.square(vc), axis=-1, keepdims=True)
    vn = vc * lax.rsqrt(var + EPS) * lng_ref[...] + lnb_ref[...]
    return u, vn


def _gmlp_prompt_body(x_ref, g_ref, win_ref, bin_ref, lng_ref, lnb_ref, ws_ref, bs_ref, wout_ref,
                      o_ref, *, T):
    ti = lax.broadcasted_iota(jnp.int32, (CHUNK, CHUNK), 0)
    si = lax.broadcasted_iota(jnp.int32, (CHUNK, CHUNK), 1)
    causal = ti >= si
    wsm = [jnp.where(causal, ws_ref[gi], 0.0).astype(BF16) for gi in range(GM_GROUPS)]
    x = x_ref[...]
    u, vn = _gmlp_front(x, g_ref, win_ref, bin_ref, lng_ref, lnb_ref)
    vnb = vn.astype(BF16)
    rows = []
    for n in range(T // CHUNK):
        r0, r1 = n * CHUNK, (n + 1) * CHUNK
        cols = [_dot(wsm[gi], vnb[r0:r1, gi * CHUNK:(gi + 1) * CHUNK]) for gi in range(GM_GROUPS)]
        rows.append(jnp.concatenate(cols, axis=-1) + bs_ref[...])
    mixed = jnp.concatenate(rows, axis=0)
    y = _dot((u * mixed).astype(BF16), wout_ref[...])
    o_ref[...] = x + y


def _gmlp_prompt(x, g, winb, b_in, ln_g, ln_b, w_s, bs_full, woutb, *, T=1024):
    B, L, _ = x.shape
    blk = pl.BlockSpec((None, T, D), lambda b, t: (b, t, 0))
    return pl.pallas_call(
        functools.partial(_gmlp_prompt_body, T=T),
        grid=(B, L // T),
        in_specs=[blk, _const((1, D)), _const((D, 2 * D)), _const((1, 2 * D)), _const((1, D)), _const((1, D)),
                  _const((GM_GROUPS, CHUNK, CHUNK)), _const((CHUNK, D)), _const((D, D))],
        out_specs=blk,
        out_shape=jax.ShapeDtypeStruct((B, L, D), F32),
        compiler_params=_params(48, 2),
        name="gmlp_prompt",
    )(x, g.reshape(1, D), winb, b_in.reshape(1, 2 * D), ln_g.reshape(1, D), ln_b.reshape(1, D),
      w_s, bs_full, woutb)


def _gmlp_sample_body(x_ref, g_ref, win_ref, bin_ref, lng_ref, lnb_ref, sw_ref, sb_ref, wout_ref,
                      o_ref, vn_ref):
    x = x_ref[...]
    u, vn = _gmlp_front(x, g_ref, win_ref, bin_ref, lng_ref, lnb_ref)
    mixed = vn * sw_ref[...] + sb_ref[...]
    y = _dot((u * mixed).astype(BF16), wout_ref[...])
    o_ref[...] = x + y
    vn_ref[...] = vn


def _gmlp_sample(x2d, g, winb, b_in, ln_g, ln_b, sw, sb, woutb):
    n = x2d.shape[0]
    full = _const((n, D))
    return pl.pallas_call(
        _gmlp_sample_body,
        grid=(1,),
        in_specs=[full, _const((1, D)), _const((D, 2 * D)), _const((1, 2 * D)), _const((1, D)), _const((1, D)),
                  _const((1, D)), _const((1, D)), _const((D, D))],
        out_specs=[pl.BlockSpec((n, D), lambda i: (0, 0)), pl.BlockSpec((n, D), lambda i: (0, 0))],
        out_shape=[jax.ShapeDtypeStruct((n, D), F32), jax.ShapeDtypeStruct((n, D), F32)],
        compiler_params=_params(32, 1),
        name="gmlp_sample",
    )(x2d, g.reshape(1, D), winb, b_in.reshape(1, 2 * D), ln_g.reshape(1, D), ln_b.reshape(1, D),
      sw, sb, woutb)


def _ret_constants(C):
    log_gamma = np.log1p(-np.exp2(-5.0 - np.arange(HEADS, dtype=np.float64)))
    idx = np.arange(C, dtype=np.float64)
    diff = idx[:, None] - idx[None, :]
    dmask = np.where(diff[None] >= 0, np.exp(log_gamma[:, None, None] * np.maximum(diff, 0.0)[None]), 0.0)
    q_dec = np.exp(log_gamma[:, None] * (idx + 1.0))
    k_dec = np.exp(log_gamma[:, None] * (C - 1.0 - idx))
    chunk_dec = np.exp(log_gamma * C)
    return dmask.astype(np.float32), q_dec.astype(np.float32), k_dec.astype(np.float32), chunk_dec


def _rope_tables(pos):
    half = DK // 2
    freqs = np.exp(-math.log(ROPE_BASE) * np.arange(half, dtype=np.float64) / half)
    ang = np.asarray(pos, dtype=np.float64)[:, None] * freqs[None]
    cos = np.concatenate([np.cos(ang), np.cos(ang)], axis=-1)
    sin = np.concatenate([-np.sin(ang), np.sin(ang)], axis=-1)
    return cos, sin


def _rope(t, cos, sin):
    return t * cos + pltpu.roll(t, DK // 2, axis=1) * sin


def _group_norm_gate(o, gate, gng, gnb):
    mu = jnp.mean(o, axis=-1, keepdims=True)
    oc = o - mu
    var = jnp.mean(jnp.square(oc), axis=-1, keepdims=True)
    on = oc * lax.rsqrt(var + GN_EPS) * gng + gnb
    return jax.nn.silu(gate) * on


def _ret_prompt_body(x_ref, g_ref, win_ref, cq_ref, sq_ref, ck_ref, sk_ref, dm_ref, qd_ref, kd_ref,
                     gng_ref, gnb_ref, wout_ref, o_ref, s_out_ref, gated_scr, s_scr,
                     *, T, chunk_dec):
    t = pl.program_id(1)

    @pl.when(t == 0)
    def _():
        s_scr[...] = jnp.zeros_like(s_scr)

    x = x_ref[...]
    h = _rms(x, g_ref[...]).astype(BF16)
    cq, sq, ck, sk = cq_ref[...], sq_ref[...], ck_ref[...], sk_ref[...]
    n_pairs = HEADS // 2

    def projection(pair, dst):
        def piece(name, lo, width):
            def run():
                dst[name] = _dot(h, win_ref[:, lo:lo + width])
            return run
        v0, g0 = 2 * QK + 2 * DV * pair, 2 * QK + VW + 2 * DV * pair
        return [piece("q", 2 * DK * pair, 2 * DK), piece("k", QK + 2 * DK * pair, 2 * DK),
                piece("v0", v0, DV), piece("v1", v0 + DV, DV),
                piece("g0", g0, DV), piece("g1", g0 + DV, DV)]

    def chunk_work(pair, src):
        state = {}

        def prep(sub):
            def run():
                q = _rope(src["q"][:, sub * DK:(sub + 1) * DK], cq, sq)
                k = _rope(src["k"][:, sub * DK:(sub + 1) * DK], ck, sk)
                state[sub] = dict(q=q, k=k, qb=q.astype(BF16), kb=k.astype(BF16), s=s_scr[2 * pair + sub])
            return run

        def chunk(sub, c):
            def run():
                hd = 2 * pair + sub
                st = state[sub]
                rows = slice(c * CHUNK, (c + 1) * CHUNK)
                vb = src["v%d" % sub][rows, :].astype(BF16)
                scores = lax.dot_general(st["qb"][rows], st["kb"][rows], (((1,), (1,)), ((), ())),
                                         preferred_element_type=F32) * dm_ref[hd]
                o = (_dot(scores.astype(BF16), vb)
                     + _dot((st["q"][rows] * qd_ref[hd]).astype(BF16), st["s"].astype(BF16)))
                kt = (st["k"][rows] * kd_ref[hd]).T.astype(BF16)
                st["s"] = st["s"] * chunk_dec[hd] + _dot(kt, vb)
                gated = _group_norm_gate(o, src["g%d" % sub][rows, :], gng_ref[:, hd * DV:(hd + 1) * DV],
                                         gnb_ref[:, hd * DV:(hd + 1) * DV])
                gated_scr[rows, hd * DV:(hd + 1) * DV] = gated.astype(BF16)
            return run

        def finish(sub):
            def run():
                s_scr[2 * pair + sub] = state[sub]["s"]
            return run

        return ([prep(0), prep(1)] + [chunk(sub, c) for c in range(T // CHUNK) for sub in range(2)]
                + [finish(0), finish(1)])

    acc = [x]

    def out_projection(pair):
        def run():
            cols = slice(pair * 2 * DV, (pair + 1) * 2 * DV)
            acc[0] = acc[0] + _dot(gated_scr[:, cols], wout_ref[cols, :])
        return run

    cur = {}
    for run in projection(0, cur):
        run()
    for pair in range(n_pairs):
        nxt = {}
        if pair + 1 < n_pairs:
            matmul_queue = projection(pair + 1, nxt)
        else:
            matmul_queue = [out_projection(p) for p in range(n_pairs - 1)]
        _interleave(matmul_queue, chunk_work(pair, cur))
        cur = nxt
    out_projection(n_pairs - 1)()
    o_ref[...] = acc[0]
    s_out_ref[...] = s_scr[...]


def _ret_prompt(x, g, winb, gn_g, gn_b, woutb, *, T=512):
    B, L, _ = x.shape
    dmask, q_dec, k_dec, chunk_dec = _ret_constants(CHUNK)
    cos, sin = _rope_tables(np.arange(L))
    scale = DK ** -0.5
    cq, sq = jnp.asarray(cos, F32), jnp.asarray(sin, F32)
    ck, sk = jnp.asarray(cos * scale, F32), jnp.asarray(sin * scale, F32)
    qd = jnp.asarray(np.broadcast_to(q_dec[:, :, None], (HEADS, CHUNK, DK)))
    kd = jnp.asarray(np.broadcast_to(k_dec[:, :, None], (HEADS, CHUNK, DK)))
    blk = pl.BlockSpec((None, T, D), lambda b, t: (b, t, 0))
    tab = pl.BlockSpec((T, DK), lambda b, t: (t, 0))
    hcc = _const((HEADS, CHUNK, CHUNK))
    return pl.pallas_call(
        functools.partial(_ret_prompt_body, T=T, chunk_dec=tuple(float(c) for c in chunk_dec)),
        grid=(B, L // T),
        in_specs=[blk, _const((1, D)), _const((D, 2 * QK + 2 * VW)), tab, tab, tab, tab,
                  hcc, hcc, hcc, _const((1, VW)), _const((1, VW)), _const((VW, D))],
        out_specs=[blk, pl.BlockSpec((None, HEADS, DK, DV), lambda b, t: (b, 0, 0, 0))],
        out_shape=[jax.ShapeDtypeStruct((B, L, D), F32), jax.ShapeDtypeStruct((B, HEADS, DK, DV), F32)],
        scratch_shapes=[pltpu.VMEM((T, VW), BF16), pltpu.VMEM((HEADS, DK, DV), F32)],
        compiler_params=_params(56, 2),
        name="ret_prompt",
    )(x, g.reshape(1, D), winb, cq, sq, ck, sk, jnp.asarray(dmask), qd, kd,
      gn_g.reshape(1, VW), gn_b.reshape(1, VW), woutb)


def _ret_proj_sample_body(x_ref, g_ref, win_ref, cq_ref, sq_ref, ck_ref, sk_ref,
                          qt_ref, kt_ref, v_ref, gate_ref, ov_ref):
    h = _rms(x_ref[...], g_ref[...]).astype(BF16)
    p = _dot(h, win_ref[...])
    for hd in range(HEADS):
        q = _rope(p[:, hd * DK:(hd + 1) * DK], cq_ref[...], sq_ref[...])
        k = _rope(p[:, QK + hd * DK:QK + (hd + 1) * DK], ck_ref[...], sk_ref[...])
        v = p[:, 2 * QK + hd * DV:2 * QK + (hd + 1) * DV]
        qt_ref[hd] = q.T
        kt_ref[hd] = k.T
        ov_ref[:, hd * DV:(hd + 1) * DV] = jnp.sum(q * k, axis=-1, keepdims=True) * v
    v_ref[...] = p[:, 2 * QK:2 * QK + VW]
    gate_ref[...] = p[:, 2 * QK + VW:]


def _ret_proj_sample(x2d, g, winb, *, pos0):
    n = x2d.shape[0]
    cos, sin = _rope_tables(np.array([pos0]))
    scale = DK ** -0.5
    tabs = [jnp.asarray(a, F32) for a in (cos, sin, cos * scale, sin * scale)]
    row = pl.BlockSpec((n, VW), lambda i: (0, 0))
    tr = pl.BlockSpec((HEADS, DK, n), lambda i: (0, 0, 0))
    return pl.pallas_call(
        _ret_proj_sample_body,
        grid=(1,),
        in_specs=[_const((n, D)), _const((1, D)), _const((D, 2 * QK + 2 * VW))] + [_const((1, DK))] * 4,
        out_specs=[tr, tr, row, row, row],
        out_shape=[jax.ShapeDtypeStruct((HEADS, DK, n), F32)] * 2 + [jax.ShapeDtypeStruct((n, VW), F32)] * 3,
        compiler_params=_params(40, 1),
        name="ret_proj_sample",
    )(x2d, g.reshape(1, D), winb, *tabs)


def _mlp_ret_body(xp_ref, xs_ref, g_ref, fg_ref, w1_hbm, w2_hbm,
                  qt_ref, kt_ref, v_ref, ov_ref, gate_ref, gng_ref, gnb_ref, rwout_ref, s_hbm,
                  op_ref, os_ref, snew_hbm,
                  w1b, w2b, stg1, stg2, sem1, sem2, s_in, s_out, sem_in, sem_out, o_scr,
                  *, li, n_p, ns, gamma):
    i = pl.program_id(0)
    spb = ns // n_p
    prime, fetch_w1, fetch_w2 = _weight_stream(w1_hbm, w2_hbm, li, w1b, w2b, stg1, stg2, sem1, sem2)

    def load(c, slot):
        return pltpu.make_async_copy(s_hbm.at[c], s_in.at[slot], sem_in.at[slot])

    def store(c, slot):
        return pltpu.make_async_copy(s_out.at[slot], snew_hbm.at[c], sem_out.at[slot])

    def state_update(j):
        def run():
            c = i * spb + j
            slot = j % 2
            load(c, slot).wait()
            store(c, slot).wait()
            lane = lax.broadcasted_iota(jnp.int32, (DK, ns), 1) == c
            for hd in range(HEADS):
                cols = slice(hd * DV, (hd + 1) * DV)
                s_old = s_in[slot, hd]
                qcol = jnp.sum(jnp.where(lane, qt_ref[hd], 0.0), axis=1, keepdims=True)
                kcol = jnp.sum(jnp.where(lane, kt_ref[hd], 0.0), axis=1, keepdims=True)
                v = v_ref[pl.ds(c, 1), cols]
                o_scr[pl.ds(c, 1), cols] = jnp.sum(s_old * (qcol * gamma[hd]), axis=0, keepdims=True)
                s_out[slot, hd] = s_old * gamma[hd] + kcol * v
            store(c, slot).start()
            load(jnp.minimum(c + 2, ns - 1), slot).start()
        return run

    def states():
        return [state_update(j) for j in range(spb)]

    @pl.when(i == 0)
    def _():
        prime()
        s_out[...] = jnp.zeros_like(s_out)
        for slot in range(2):
            load(slot, slot).start()
            store(slot, slot).start()
        _interleave(_mlp_thunks(lambda: xp_ref[...], _store(op_ref), g_ref, fg_ref, w1b, w2b, False,
                                fetch_w1, fetch_w2), states())

    @pl.when((i > 0) & (i < n_p))
    def _():
        _interleave(_mlp_thunks(lambda: xp_ref[...], _store(op_ref), g_ref, fg_ref, w1b, w2b, False), states())

    @pl.when(i == n_p)
    def _():
        for slot in range(2):
            load(ns - 1, slot).wait()
            store(ns - 1, slot).wait()
        parts = []
        for hd in range(HEADS):
            cols = slice(hd * DV, (hd + 1) * DV)
            o = ov_ref[:, cols] + o_scr[:, cols]
            parts.append(_group_norm_gate(o, gate_ref[:, cols], gng_ref[:, cols], gnb_ref[:, cols]))
        gated = jnp.concatenate(parts, axis=-1).astype(BF16)
        xs1 = xs_ref[...] + _dot(gated, rwout_ref[...])
        _run(_mlp_thunks(lambda: xs1, _store(os_ref), g_ref, fg_ref, w1b, w2b, False))


def _mlp_ret(xp2d, xs2d, g, w1, w2, fg, qt, kt, v, ov, gate, gn_g, gn_b, rwoutb, s, *, li, tm=512):
    n, ns = xp2d.shape[0], xs2d.shape[0]
    n_p = n // tm
    _, _, _, chunk_dec = _ret_constants(1)
    row = pl.BlockSpec((tm, D), lambda i: (jnp.minimum(i, n_p - 1), 0))
    srow = pl.BlockSpec((ns, D), lambda i: (0, 0))
    hbm = pl.BlockSpec(memory_space=pl.ANY)
    wide = _const((ns, VW))
    return pl.pallas_call(
        functools.partial(_mlp_ret_body, li=li, n_p=n_p, ns=ns, gamma=tuple(float(c) for c in chunk_dec)),
        grid=(n_p + 1,),
        in_specs=[row, _const((ns, D)), _const((1, D)), _const((1, D)), hbm, hbm,
                  _const((HEADS, DK, ns)), _const((HEADS, DK, ns)), wide, wide, wide,
                  _const((1, VW)), _const((1, VW)), _const((VW, D)), hbm],
        out_specs=[row, srow, hbm],
        out_shape=[jax.ShapeDtypeStruct((n, D), F32), jax.ShapeDtypeStruct((ns, D), F32),
                   jax.ShapeDtypeStruct(s.shape, F32)],
        scratch_shapes=[pltpu.VMEM((D, D_FF), BF16), pltpu.VMEM((D_FF, D), BF16),
                        pltpu.VMEM((2, D, MLP_FC), F32), pltpu.VMEM((2, MLP_FC, D), F32),
                        pltpu.SemaphoreType.DMA((2,)), pltpu.SemaphoreType.DMA((2,)),
                        pltpu.VMEM((2, HEADS, DK, DV), F32), pltpu.VMEM((2, HEADS, DK, DV), F32),
                        pltpu.SemaphoreType.DMA((2,)), pltpu.SemaphoreType.DMA((2,)),
                        pltpu.VMEM((ns, VW), F32)],
        compiler_params=_params(58, 1),
        name="mlp_ret",
    )(xp2d, xs2d, g.reshape(1, D), fg.reshape(1, D), w1, w2, qt, kt, v, ov, gate,
      gn_g.reshape(1, VW), gn_b.reshape(1, VW), rwoutb, s)


def _lru_gates(xc, wax_ref, ba, bx, lam, heads):
    xcb = xc.astype(BF16)
    rs, is_ = [], []
    for n, hd in enumerate(heads):
        ri = _dot(xcb[:, n * DK:(n + 1) * DK], wax_ref[hd])
        rs.append(ri[:, :DK])
        is_.append(ri[:, DK:])
    r = jax.nn.sigmoid(jnp.concatenate(rs, axis=-1) + ba)
    i = jax.nn.sigmoid(jnp.concatenate(is_, axis=-1) + bx)
    log_a = r * (-LRU_C * _softplus(-lam))
    a = jnp.exp(log_a)
    mult = jnp.sqrt(-jnp.tanh(log_a) * (a * a + 1.0))
    return a, mult, i


LRU_GROUP_HEADS = 4


def _lru_prompt_body(x_ref, g_ref, win_ref, cw_ref, cb_ref, wax_ref, ba_ref, bx_ref, lam_ref, wout_ref,
                     o_ref, conv_ref, hlast_ref, xb_scr, hs_scr, carry_scr, h_scr, *, NB, T):
    t = pl.program_id(0)
    N = NB * T
    PS = T + 1
    halo = (CONV_W - 1) * NB
    gw = LRU_GROUP_HEADS * DK
    gblk = gw // LANES

    @pl.when(t == 0)
    def _():
        carry_scr[...] = jnp.zeros_like(carry_scr)
        h_scr[...] = jnp.zeros_like(h_scr)

    x = x_ref[...].reshape(N, D)
    h = _rms(x, g_ref[...]).astype(BF16)
    row = lax.broadcasted_iota(jnp.int32, (N, 1), 0)
    pos = t * T + lax.shift_right_logical(row, NB.bit_length() - 1)
    out = x
    def project(grp):
        c0, c1 = grp * gw, (grp + 1) * gw
        return _dot(h, win_ref[:, c0:c1]), _dot(h, win_ref[:, D + c0:D + c1])

    ngrp = D // gw
    nxt = project(0)
    for grp in range(ngrp):
        c0, c1 = grp * gw, (grp + 1) * gw
        blks = range(grp * gblk, (grp + 1) * gblk)
        zg, xb = nxt
        if grp + 1 < ngrp:
            nxt = project(grp + 1)
        gate = _gelu(zg)
        for n, j in enumerate(blks):
            for b in range(NB):
                xb_scr[j, b * PS:b * PS + T, :] = xb[b * T:(b + 1) * T, n * LANES:(n + 1) * LANES]
        ext = jnp.concatenate(
            [carry_scr[:, c0:c1]]
            + [jnp.concatenate([xb_scr[j, pl.ds(s, NB, stride=PS), :] for j in blks], axis=-1)
               for s in range(T)], axis=0)
        carry_scr[:, c0:c1] = ext[N:N + halo, :]
        acc = ext[0:N, :] * cw_ref[0:1, c0:c1]
        for j in range(1, CONV_W):
            acc = acc + ext[j * NB:j * NB + N, :] * cw_ref[j:j + 1, c0:c1]
        xc = cb_ref[:, c0:c1] + acc
        heads = range(grp * LRU_GROUP_HEADS, (grp + 1) * LRU_GROUP_HEADS)
        a, mult, i = _lru_gates(xc, wax_ref, ba_ref[:, c0:c1], bx_ref[:, c0:c1], lam_ref[:, c0:c1], heads)
        bvec = jnp.where(pos == 0, 1.0, mult) * (i * xc)
        hcur = h_scr[:, c0:c1]
        for s in range(T):
            hcur = a[s * NB:(s + 1) * NB, :] * hcur + bvec[s * NB:(s + 1) * NB, :]
            for n, j in enumerate(blks):
                hs_scr[j, pl.ds(s, NB, stride=PS), :] = hcur[:, n * LANES:(n + 1) * LANES]
        h_scr[:, c0:c1] = hcur
        hs = jnp.concatenate(
            [jnp.concatenate([hs_scr[j, b * PS:b * PS + T, :] for b in range(NB)], axis=0) for j in blks],
            axis=-1)
        out = out + _dot((hs * gate).astype(BF16), wout_ref[c0:c1, :])
    o_ref[...] = out.reshape(NB, T, D)
    hlast_ref[...] = h_scr[...]

    @pl.when(t == pl.num_programs(0) - 1)
    def _():
        for j in range(CONV_W - 1):
            conv_ref[:, j, :] = carry_scr[j * NB:(j + 1) * NB, :]


def _lru_prompt(x, g, winb, conv_w, conv_b, waxb, b_a, b_x, lam, woutb, *, T=128):
    B, L, _ = x.shape
    blk = pl.BlockSpec((B, T, D), lambda t: (0, t, 0))
    vec = _const((1, D))
    return pl.pallas_call(
        functools.partial(_lru_prompt_body, NB=B, T=T),
        grid=(L // T,),
        in_specs=[blk, vec, _const((D, 2 * D)), _const((CONV_W, D)), vec, _const((HEADS, DK, 2 * DK)),
                  vec, vec, vec, _const((D, D))],
        out_specs=[blk, pl.BlockSpec((B, CONV_W - 1, D), lambda t: (0, 0, 0)),
                   pl.BlockSpec((B, D), lambda t: (0, 0))],
        out_shape=[jax.ShapeDtypeStruct((B, L, D), F32), jax.ShapeDtypeStruct((B, CONV_W - 1, D), F32),
                   jax.ShapeDtypeStruct((B, D), F32)],
        scratch_shapes=[pltpu.VMEM((D // LANES, B * (T + 1), LANES), F32)] * 2
        + [pltpu.VMEM(((CONV_W - 1) * B, D), F32), pltpu.VMEM((B, D), F32)],
        compiler_params=_params(48, 1),
        name="lru_prompt",
    )(x, g.reshape(1, D), winb, conv_w, conv_b.reshape(1, D), waxb, b_a.reshape(1, D), b_x.reshape(1, D),
      lam.reshape(1, D), woutb)


def _lru_sample_body(x_ref, cbuf_ref, h0_ref, g_ref, win_ref, cw_ref, cb_ref, wax_ref, ba_ref, bx_ref,
                     lam_ref, wout_ref, o_ref, nconv_ref, hnew_ref, *, pos0):
    x = x_ref[...]
    h = _rms(x, g_ref[...]).astype(BF16)
    z = _dot(h, win_ref[...])
    gate = _gelu(z[:, :D])
    xb = z[:, D:]
    cw = cw_ref[...]
    acc = cbuf_ref[0] * cw[0:1, :]
    for j in range(1, CONV_W - 1):
        acc = acc + cbuf_ref[j] * cw[j:j + 1, :]
    acc = acc + xb * cw[CONV_W - 1:CONV_W, :]
    xc = cb_ref[...] + acc
    a, mult, i = _lru_gates(xc, wax_ref, ba_ref[...], bx_ref[...], lam_ref[...], range(HEADS))
    if pos0 == 0:
        mult = jnp.ones_like(mult)
    hnew = a * h0_ref[...] + mult * (i * xc)
    hnew_ref[...] = hnew
    nconv_ref[0:CONV_W - 2] = cbuf_ref[1:CONV_W - 1]
    nconv_ref[CONV_W - 2] = xb
    o_ref[...] = x + _dot((hnew * gate).astype(BF16), wout_ref[...])


def _lru_sample(x2d, cbuf, h0, g, winb, conv_w, conv_b, waxb, b_a, b_x, lam, woutb, *, pos0):
    n = x2d.shape[0]
    vec = _const((1, D))
    cshape = (CONV_W - 1, n, D)
    out = pl.BlockSpec((n, D), lambda i: (0, 0))
    return pl.pallas_call(
        functools.partial(_lru_sample_body, pos0=pos0),
        grid=(1,),
        in_specs=[_const((n, D)), _const(cshape), _const((n, D)), vec, _const((D, 2 * D)), _const((CONV_W, D)),
                  vec, _const((HEADS, DK, 2 * DK)), vec, vec, vec, _const((D, D))],
        out_specs=[out, pl.BlockSpec(cshape, lambda i: (0, 0, 0)), out],
        out_shape=[jax.ShapeDtypeStruct((n, D), F32), jax.ShapeDtypeStruct(cshape, F32),
                   jax.ShapeDtypeStruct((n, D), F32)],
        compiler_params=_params(32, 1),
        name="lru_sample",
    )(x2d, cbuf, h0, g.reshape(1, D), winb, conv_w, conv_b.reshape(1, D), waxb, b_a.reshape(1, D),
      b_x.reshape(1, D), lam.reshape(1, D), woutb)


def kernel(x_prompt, x_sample, state_pool, state_ret, state_conv, state_lru, pool_norm, pool_w, pool_scale, gm_norm, gm_w_in, gm_b_in, gm_ln_g, gm_ln_b, gm_w_s, gm_b_s, gm_w_out, ret_norm, ret_w_in, ret_gn_g, ret_gn_b, ret_w_out, lru_norm, lru_w_in, lru_conv_w, lru_conv_b, lru_w_a, lru_b_a, lru_w_x, lru_b_x, lru_lam, lru_w_out, mlp_norm, mlp_w1, mlp_w2, final_norm):
    B, L, _ = x_prompt.shape
    NS = x_sample.shape[0]
    bf = lambda w: w.astype(BF16)

    def mlp(xp, xs, li, final=False):
        yp, ys = _mlp(xp.reshape(B * L, D), xs, mlp_norm[li], mlp_w1, mlp_w2, final_norm, li=li, final=final)
        return yp.reshape(B, L, D), ys

    pool_wb = bf(pool_w[0])
    xs, pool_s = _pool_sample(x_sample.reshape(NS, D), jnp.swapaxes(state_pool[0], 0, 1),
                              pool_norm[0], pool_wb, pool_scale[0], pos0=PAST_LEN)
    xp, xs, pool_p = _pool_mlp(x_prompt, xs, pool_norm[0], pool_wb, pool_scale[0], mlp_norm[0],
                               mlp_w1, mlp_w2, final_norm, li=0)

    gm_winb, gm_woutb = bf(gm_w_in[0]), bf(gm_w_out[0])
    bs_full = jnp.repeat(gm_b_s[0].T, CHUNK, axis=1)
    xp = _gmlp_prompt(xp, gm_norm[0], gm_winb, gm_b_in[0], gm_ln_g[0], gm_ln_b[0], gm_w_s[0], bs_full, gm_woutb)
    sw = jnp.repeat(gm_w_s[0][:, 0, 0], CHUNK).reshape(1, D)
    sb = jnp.repeat(gm_b_s[0][:, 0], CHUNK).reshape(1, D)
    xs, v_s = _gmlp_sample(xs, gm_norm[0], gm_winb, gm_b_in[0], gm_ln_g[0], gm_ln_b[0], sw, sb, gm_woutb)
    xp, xs = mlp(xp, xs, 1)

    ret_winb, ret_woutb = bf(ret_w_in[0]), bf(ret_w_out[0])
    xp, ret_p = _ret_prompt(xp, ret_norm[0], ret_winb, ret_gn_g[0], ret_gn_b[0], ret_woutb)
    qt, kt, v, gate, ov = _ret_proj_sample(xs, ret_norm[0], ret_winb, pos0=PAST_LEN)
    xp, xs, ret_s = _mlp_ret(xp.reshape(B * L, D), xs, mlp_norm[2], mlp_w1, mlp_w2, final_norm,
                             qt, kt, v, ov, gate, ret_gn_g[0], ret_gn_b[0], ret_woutb, state_ret[0], li=2)
    xp = xp.reshape(B, L, D)

    lru_winb, lru_woutb = bf(lru_w_in[0]), bf(lru_w_out[0])
    waxb = bf(jnp.concatenate([lru_w_a[0], lru_w_x[0]], axis=-1))
    xp, conv_p, lru_p = _lru_prompt(xp, lru_norm[0], lru_winb, lru_conv_w[0], lru_conv_b[0], waxb,
                                    lru_b_a[0], lru_b_x[0], lru_lam[0], lru_woutb)
    xs, conv_s, lru_s = _lru_sample(xs, jnp.swapaxes(state_conv[0], 0, 1), state_lru[0],
                                    lru_norm[0], lru_winb, lru_conv_w[0], lru_conv_b[0], waxb,
                                    lru_b_a[0], lru_b_x[0], lru_lam[0], lru_woutb, pos0=PAST_LEN)
    yp, ys = mlp(xp, xs, 3, final=True)

    return (yp, ys.reshape(NS, 1, D),
            pool_p[None], jnp.swapaxes(pool_s, 0, 1)[None],
            v_s.reshape(1, NS, 1, D),
            ret_p[None], ret_s[None],
            conv_p[None], jnp.swapaxes(conv_s, 0, 1)[None],
            lru_p[None], lru_s[None])
```

```python
import functools
import math

import jax
import jax.numpy as jnp
import numpy as np
from jax import lax
from jax.experimental import pallas as pl
from jax.experimental.pallas import tpu as pltpu

F32 = jnp.float32
BF16 = jnp.bfloat16

D = 1024
EPS = 1e-6
GN_EPS = 1e-5
PAST_LEN = 16384
POOL_WINDOWS = (2, 4, 8, 16)
POOL_GROUP = D // len(POOL_WINDOWS)
POOL_BUF = max(POOL_WINDOWS) - 1
CHUNK = 128
GM_GROUPS = 8
HEADS = 8
DK = D // HEADS
DV = 2 * D // HEADS
QK = HEADS * DK
VW = HEADS * DV
ROPE_BASE = 10000.0
CONV_W = 4
LRU_C = 8.0
D_FF = 4 * D

MIB = 1024 * 1024
SUBLANES = 8
LANES = 128


def _params(vmem_mib, n_grid):
    return pltpu.CompilerParams(
        dimension_semantics=("arbitrary",) * n_grid,
        vmem_limit_bytes=vmem_mib * MIB,
    )


def _const(shape):
    zeros = (0,) * len(shape)
    return pl.BlockSpec(shape, lambda *_: zeros, pipeline_mode=pl.Buffered(1))


def _rms(x, g):
    ms = jnp.mean(x * x, axis=-1, keepdims=True)
    return x * lax.rsqrt(ms + EPS) * g


def _dot(a, b):
    return jnp.dot(a, b, preferred_element_type=F32)


def _interleave(*queues):
    pos = [0] * len(queues)
    while any(p < len(q) for p, q in zip(pos, queues)):
        _, i = min((pos[i] / len(q), i) for i, q in enumerate(queues) if pos[i] < len(q))
        queues[i][pos[i]]()
        pos[i] += 1


def _softplus(x):
    return jnp.maximum(x, 0.0) + jnp.log1p(jnp.exp(-jnp.abs(x)))


GELU_K = math.sqrt(2.0 / math.pi)


def _gelu(x):
    half = 0.5 * x
    return half + half * jnp.tanh(x * (GELU_K + (GELU_K * 0.044715) * (x * x)))


MLP_FC = 512
STAGE_BYTES = 2 * MIB


def _weight_scratch(rows, cols):
    ch = rows
    while ch * cols * 4 > STAGE_BYTES:
        ch //= 2
    assert rows % ch == 0
    return [pltpu.VMEM((rows, cols), BF16), pltpu.VMEM((2, ch, cols), F32), pltpu.SemaphoreType.DMA((2,))]


def _load_cast(src, dst, stg, sem):
    ch = stg.shape[1]
    n = src.shape[0] // ch

    def copy(c, slot):
        return pltpu.make_async_copy(src.at[pl.ds(c * ch, ch), :], stg.at[slot], sem.at[slot])

    copy(0, 0).start()
    for c in range(n):
        slot = c % 2
        if c + 1 < n:
            copy(c + 1, 1 - slot).start()
        copy(c, slot).wait()
        dst[c * ch:(c + 1) * ch, :] = stg[slot].astype(BF16)


def _mlp_thunks(read_x, write_out, g_ref, fg_ref, w1b, w2b, final, fetch_w1=None, fetch_w2=None):
    st = {}

    def head():
        x = read_x()
        st["h"] = _rms(x, g_ref[...]).astype(BF16)
        st["acc"] = x

    def up(c):
        def run():
            if fetch_w1 is not None:
                fetch_w1(c)
            a = _dot(st["h"], w1b[:, c * MLP_FC:(c + 1) * MLP_FC])
            st["a"] = jnp.square(jnp.maximum(a, 0.0)).astype(BF16)
        return run

    def down(c):
        def run():
            if fetch_w2 is not None:
                fetch_w2(c)
            st["acc"] = st["acc"] + _dot(st["a"], w2b[c * MLP_FC:(c + 1) * MLP_FC, :])
        return run

    def tail():
        acc = st["acc"]
        write_out(_rms(acc, fg_ref[...]) if final else acc)

    return [head] + [f(c) for c in range(D_FF // MLP_FC) for f in (up, down)] + [tail]


def _run(thunks):
    for thunk in thunks:
        thunk()


def _store(ref):
    def write(v):
        ref[...] = v
    return write


def _weight_stream(w1_hbm, w2_hbm, li, w1b, w2b, stg1, stg2, sem1, sem2):
    nch = D_FF // MLP_FC

    def w1_copy(c):
        return pltpu.make_async_copy(w1_hbm.at[li, :, pl.ds(c * MLP_FC, MLP_FC)], stg1.at[c % 2], sem1.at[c % 2])

    def w2_copy(c):
        return pltpu.make_async_copy(w2_hbm.at[li, pl.ds(c * MLP_FC, MLP_FC), :], stg2.at[c % 2], sem2.at[c % 2])

    def prime():
        for c in range(min(2, nch)):
            w1_copy(c).start()
            w2_copy(c).start()

    def fetch_w1(c):
        w1_copy(c).wait()
        w1b[:, c * MLP_FC:(c + 1) * MLP_FC] = stg1[c % 2].astype(BF16)
        if c + 2 < nch:
            w1_copy(c + 2).start()

    def fetch_w2(c):
        w2_copy(c).wait()
        w2b[c * MLP_FC:(c + 1) * MLP_FC, :] = stg2[c % 2].astype(BF16)
        if c + 2 < nch:
            w2_copy(c + 2).start()

    return prime, fetch_w1, fetch_w2


def _mlp_body(xp_ref, xs_ref, g_ref, fg_ref, w1_hbm, w2_hbm, op_ref, os_ref,
              w1b, w2b, stg1, stg2, sem1, sem2, *, li, final, n_p):
    i = pl.program_id(0)
    prime, fetch_w1, fetch_w2 = _weight_stream(w1_hbm, w2_hbm, li, w1b, w2b, stg1, stg2, sem1, sem2)

    @pl.when(i == 0)
    def _():
        prime()
        _run(_mlp_thunks(lambda: xp_ref[...], _store(op_ref), g_ref, fg_ref, w1b, w2b, final,
                         fetch_w1, fetch_w2))

    @pl.when((i > 0) & (i < n_p))
    def _():
        _run(_mlp_thunks(lambda: xp_ref[...], _store(op_ref), g_ref, fg_ref, w1b, w2b, final))

    @pl.when(i == n_p)
    def _():
        _run(_mlp_thunks(lambda: xs_ref[...], _store(os_ref), g_ref, fg_ref, w1b, w2b, final))


def _mlp(xp2d, xs2d, g, w1, w2, fg, *, li, final, tm=512):
    n, ns = xp2d.shape[0], xs2d.shape[0]
    n_p = n // tm
    row = pl.BlockSpec((tm, D), lambda i: (jnp.minimum(i, n_p - 1), 0))
    srow = pl.BlockSpec((ns, D), lambda i: (0, 0))
    hbm = pl.BlockSpec(memory_space=pl.ANY)
    return pl.pallas_call(
        functools.partial(_mlp_body, li=li, final=final, n_p=n_p),
        grid=(n_p + 1,),
        in_specs=[row, _const((ns, D)), _const((1, D)), _const((1, D)), hbm, hbm],
        out_specs=[row, srow],
        out_shape=[jax.ShapeDtypeStruct((n, D), F32), jax.ShapeDtypeStruct((ns, D), F32)],
        scratch_shapes=[pltpu.VMEM((D, D_FF), BF16), pltpu.VMEM((D_FF, D), BF16),
                        pltpu.VMEM((2, D, MLP_FC), F32), pltpu.VMEM((2, MLP_FC, D), F32),
                        pltpu.SemaphoreType.DMA((2,)), pltpu.SemaphoreType.DMA((2,))],
        compiler_params=_params(48, 1),
        name="mlp",
    )(xp2d, xs2d, g.reshape(1, D), fg.reshape(1, D), w1, w2)


POOL_HALO = 16


def _pool_thunks(x_ref, g_ref, w_ref, sc_ref, write_out, buf_ref, ext_ref, *, T, t):
    st = {}

    def head():
        ext_ref[0:POOL_HALO, :] = jnp.where(t == 0, 0.0, ext_ref[0:POOL_HALO, :])
        x = x_ref[...]
        h = _rms(x, g_ref[...])
        ext_ref[POOL_HALO:POOL_HALO + T, :] = h
        st["x"], st["h"] = x, h
        st["pos"] = t * T + lax.broadcasted_iota(jnp.int32, (T, 1), 0)
        st["y"] = [None] * len(POOL_WINDOWS)

    def group(gi, w):
        def run():
            c0, c1 = gi * POOL_GROUP, (gi + 1) * POOL_GROUP
            hg = st["h"][:, c0:c1]
            s = hg
            for k in range(1, w):
                s = s + ext_ref[POOL_HALO - k:POOL_HALO - k + T, c0:c1]
            cnt = jnp.minimum(st["pos"] + 1, w).astype(F32)
            d = s / cnt - hg
            st["y"][gi] = _dot(d.astype(BF16), w_ref[gi])
        return run

    def tail():
        y = jnp.concatenate(st["y"], axis=-1) * sc_ref[...]
        write_out(st["x"] + y)
        last = ext_ref[T:T + POOL_HALO, :]
        ext_ref[0:POOL_HALO, :] = last
        buf_ref[...] = last[POOL_HALO - POOL_BUF:, :]

    return [head] + [group(gi, w) for gi, w in enumerate(POOL_WINDOWS)] + [tail]


def _pool_mlp_body(x_ref, xs_ref, pg_ref, pw_ref, psc_ref, g_ref, fg_ref, w1_hbm, w2_hbm,
                   o_ref, os_ref, buf_ref, ext_ref, x1_scr, w1b, w2b, stg1, stg2, sem1, sem2,
                   *, T, tiles_per_seq, n, li):
    s = pl.program_id(0)
    t = lax.rem(s, tiles_per_seq)
    slot = lax.rem(s, 2)
    prime, fetch_w1, fetch_w2 = _weight_stream(w1_hbm, w2_hbm, li, w1b, w2b, stg1, stg2, sem1, sem2)

    def write_x1(v):
        x1_scr[slot] = v

    def mixer():
        return _pool_thunks(x_ref, pg_ref, pw_ref, psc_ref, write_x1, buf_ref, ext_ref, T=T, t=t)

    def mlp(*fetch):
        return _mlp_thunks(lambda: x1_scr[1 - slot], _store(o_ref), g_ref, fg_ref, w1b, w2b, False, *fetch)

    @pl.when(s == 0)
    def _():
        ext_ref[0:POOL_HALO, :] = jnp.zeros((POOL_HALO, D), F32)
        prime()
        _run(mixer())

    @pl.when(s == 1)
    def _():
        _interleave(mlp(fetch_w1, fetch_w2), mixer())

    @pl.when((s > 1) & (s < n))
    def _():
        _interleave(mlp(), mixer())

    @pl.when(s == n)
    def _():
        _run(mlp())
        _run(_mlp_thunks(lambda: xs_ref[...], _store(os_ref), g_ref, fg_ref, w1b, w2b, False))


def _pool_mlp(x, xs2d, pg, pwb, psc, g, w1, w2, fg, *, li, T=512):
    B, L, _ = x.shape
    ns = xs2d.shape[0]
    tps = L // T
    n = B * tps
    tile = lambda s: (jnp.minimum(s, n - 1) // tps, lax.rem(jnp.minimum(s, n - 1), tps), 0)
    prev = lambda s: (jnp.maximum(s - 1, 0) // tps, lax.rem(jnp.maximum(s - 1, 0), tps), 0)
    hbm = pl.BlockSpec(memory_space=pl.ANY)
    return pl.pallas_call(
        functools.partial(_pool_mlp_body, T=T, tiles_per_seq=tps, n=n, li=li),
        grid=(n + 1,),
        in_specs=[pl.BlockSpec((None, T, D), tile), _const((ns, D)), _const((1, D)),
                  _const((len(POOL_WINDOWS), POOL_GROUP, POOL_GROUP)), _const((1, D)),
                  _const((1, D)), _const((1, D)), hbm, hbm],
        out_specs=[pl.BlockSpec((None, T, D), prev), pl.BlockSpec((ns, D), lambda s: (0, 0)),
                   pl.BlockSpec((None, POOL_BUF, D), lambda s: (jnp.minimum(s, n - 1) // tps, 0, 0))],
        out_shape=[jax.ShapeDtypeStruct((B, L, D), F32), jax.ShapeDtypeStruct((ns, D), F32),
                   jax.ShapeDtypeStruct((B, POOL_BUF, D), F32)],
        scratch_shapes=[pltpu.VMEM((T + POOL_HALO, D), F32), pltpu.VMEM((2, T, D), F32),
                        pltpu.VMEM((D, D_FF), BF16), pltpu.VMEM((D_FF, D), BF16),
                        pltpu.VMEM((2, D, MLP_FC), F32), pltpu.VMEM((2, MLP_FC, D), F32),
                        pltpu.SemaphoreType.DMA((2,)), pltpu.SemaphoreType.DMA((2,))],
        compiler_params=_params(56, 1),
        name="pool_mlp",
    )(x, xs2d, pg.reshape(1, D), pwb, psc.reshape(1, D), g.reshape(1, D), fg.reshape(1, D), w1, w2)


def _pool_sample_body(x_ref, buf_ref, g_ref, w_ref, sc_ref, o_ref, nbuf_ref, *, pos0):
    x = x_ref[...]
    h = _rms(x, g_ref[...])
    ys = []
    for gi, w in enumerate(POOL_WINDOWS):
        c0, c1 = gi * POOL_GROUP, (gi + 1) * POOL_GROUP
        hg = h[:, c0:c1]
        s = hg
        for k in range(1, w):
            s = s + buf_ref[POOL_BUF - k, :, c0:c1]
        cnt = float(min(pos0 + 1, w))
        d = s / cnt - hg
        ys.append(_dot(d.astype(BF16), w_ref[gi]))
    y = jnp.concatenate(ys, axis=-1) * sc_ref[...]
    o_ref[...] = x + y
    nbuf_ref[0:POOL_BUF - 1] = buf_ref[1:POOL_BUF]
    nbuf_ref[POOL_BUF - 1] = h


def _pool_sample(x2d, buf, g, wb, sc, *, pos0, bt=32):
    n = x2d.shape[0]
    row = pl.BlockSpec((bt, D), lambda i: (i, 0))
    brow = pl.BlockSpec((POOL_BUF, bt, D), lambda i: (0, i, 0))
    return pl.pallas_call(
        functools.partial(_pool_sample_body, pos0=pos0),
        grid=(n // bt,),
        in_specs=[row, brow, _const((1, D)), _const((len(POOL_WINDOWS), POOL_GROUP, POOL_GROUP)), _const((1, D))],
        out_specs=[row, brow],
        out_shape=[jax.ShapeDtypeStruct((n, D), F32), jax.ShapeDtypeStruct((POOL_BUF, n, D), F32)],
        compiler_params=_params(32, 1),
        name="pool_sample",
    )(x2d, buf, g.reshape(1, D), wb, sc.reshape(1, D))


def _gmlp_front(x, g_ref, win_ref, bin_ref, lng_ref, lnb_ref):
    h = _rms(x, g_ref[...]).astype(BF16)
    z = _gelu(_dot(h, win_ref[...].astype(BF16)) + bin_ref[...])
    u, v = z[:, :D], z[:, D:]
    mu = jnp.mean(v, axis=-1, keepdims=True)
    vc = v - mu
    var = jnp.mean(jnp.square(vc), axis=-1, keepdims=True)
    vn = vc * lax.rsqrt(var + EPS) * lng_ref[...] + lnb_ref[...]
    return u, vn


def _gmlp_prompt_body(x_ref, g_ref, win_hbm, bin_ref, lng_ref, lnb_ref, ws_ref, bs_ref, wout_hbm,
                      o_ref, win_ref, stg_in, sem_in, wout_ref, stg_out, sem_out, *, T):
    @pl.when((pl.program_id(0) == 0) & (pl.program_id(1) == 0))
    def _():
        _load_cast(win_hbm, win_ref, stg_in, sem_in)
        _load_cast(wout_hbm, wout_ref, stg_out, sem_out)

    ti = lax.broadcasted_iota(jnp.int32, (CHUNK, CHUNK), 0)
    si = lax.broadcasted_iota(jnp.int32, (CHUNK, CHUNK), 1)
    causal = ti >= si
    wsm = [jnp.where(causal, ws_ref[gi], 0.0).astype(BF16) for gi in range(GM_GROUPS)]
    x = x_ref[...]
    u, vn = _gmlp_front(x, g_ref, win_ref, bin_ref, lng_ref, lnb_ref)
    vnb = vn.astype(BF16)
    rows = []
    for n in range(T // CHUNK):
        r0, r1 = n * CHUNK, (n + 1) * CHUNK
        cols = [_dot(wsm[gi], vnb[r0:r1, gi * CHUNK:(gi + 1) * CHUNK]) for gi in range(GM_GROUPS)]
        rows.append(jnp.concatenate(cols, axis=-1) + bs_ref[...])
    mixed = jnp.concatenate(rows, axis=0)
    y = _dot((u * mixed).astype(BF16), wout_ref[...])
    o_ref[...] = x + y


def _gmlp_prompt(x, g, w_in, b_in, ln_g, ln_b, w_s, bs_full, w_out, *, T=1024):
    B, L, _ = x.shape
    blk = pl.BlockSpec((None, T, D), lambda b, t: (b, t, 0))
    hbm = pl.BlockSpec(memory_space=pl.ANY)
    return pl.pallas_call(
        functools.partial(_gmlp_prompt_body, T=T),
        grid=(B, L // T),
        in_specs=[blk, _const((1, D)), hbm, _const((1, 2 * D)), _const((1, D)), _const((1, D)),
                  _const((GM_GROUPS, CHUNK, CHUNK)), _const((CHUNK, D)), hbm],
        out_specs=blk,
        out_shape=jax.ShapeDtypeStruct((B, L, D), F32),
        scratch_shapes=_weight_scratch(D, 2 * D) + _weight_scratch(D, D),
        compiler_params=_params(52, 2),
        name="gmlp_prompt",
    )(x, g.reshape(1, D), w_in, b_in.reshape(1, 2 * D), ln_g.reshape(1, D), ln_b.reshape(1, D),
      w_s, bs_full, w_out)


def _gmlp_sample_body(x_ref, g_ref, win_ref, bin_ref, lng_ref, lnb_ref, sw_ref, sb_ref, wout_ref,
                      o_ref, vn_ref):
    x = x_ref[...]
    u, vn = _gmlp_front(x, g_ref, win_ref, bin_ref, lng_ref, lnb_ref)
    mixed = vn * sw_ref[...] + sb_ref[...]
    y = _dot((u * mixed).astype(BF16), wout_ref[...].astype(BF16))
    o_ref[...] = x + y
    vn_ref[...] = vn


def _gmlp_sample(x2d, g, winb, b_in, ln_g, ln_b, sw, sb, woutb):
    n = x2d.shape[0]
    full = _const((n, D))
    return pl.pallas_call(
        _gmlp_sample_body,
        grid=(1,),
        in_specs=[full, _const((1, D)), _const((D, 2 * D)), _const((1, 2 * D)), _const((1, D)), _const((1, D)),
                  _const((1, D)), _const((1, D)), _const((D, D))],
        out_specs=[pl.BlockSpec((n, D), lambda i: (0, 0)), pl.BlockSpec((n, D), lambda i: (0, 0))],
        out_shape=[jax.ShapeDtypeStruct((n, D), F32), jax.ShapeDtypeStruct((n, D), F32)],
        compiler_params=_params(32, 1),
        name="gmlp_sample",
    )(x2d, g.reshape(1, D), winb, b_in.reshape(1, 2 * D), ln_g.reshape(1, D), ln_b.reshape(1, D),
      sw, sb, woutb)


def _ret_constants(C):
    log_gamma = np.log1p(-np.exp2(-5.0 - np.arange(HEADS, dtype=np.float64)))
    idx = np.arange(C, dtype=np.float64)
    diff = idx[:, None] - idx[None, :]
    dmask = np.where(diff[None] >= 0, np.exp(log_gamma[:, None, None] * np.maximum(diff, 0.0)[None]), 0.0)
    q_dec = np.exp(log_gamma[:, None] * (idx + 1.0))
    k_dec = np.exp(log_gamma[:, None] * (C - 1.0 - idx))
    chunk_dec = np.exp(log_gamma * C)
    return dmask.astype(np.float32), q_dec.astype(np.float32), k_dec.astype(np.float32), chunk_dec


def _rope_tables(pos):
    half = DK // 2
    freqs = np.exp(-math.log(ROPE_BASE) * np.arange(half, dtype=np.float64) / half)
    ang = np.asarray(pos, dtype=np.float64)[:, None] * freqs[None]
    cos = np.concatenate([np.cos(ang), np.cos(ang)], axis=-1)
    sin = np.concatenate([-np.sin(ang), np.sin(ang)], axis=-1)
    return cos, sin


def _rope(t, cos, sin):
    return t * cos + pltpu.roll(t, DK // 2, axis=1) * sin


def _group_norm_gate(o, gate, gng, gnb):
    mu = jnp.mean(o, axis=-1, keepdims=True)
    oc = o - mu
    var = jnp.mean(jnp.square(oc), axis=-1, keepdims=True)
    on = oc * lax.rsqrt(var + GN_EPS) * gng + gnb
    return jax.nn.silu(gate) * on


RET_GROUP_HEADS = 2


def _ret_prompt_body(x_ref, g_ref, win_hbm, cq_ref, sq_ref, ck_ref, sk_ref, dm_ref, qd_ref, kd_ref,
                     gng_ref, gnb_ref, wout_ref, o_ref, s_out_ref, gated_scr, s_scr, win_ref, stg_in, sem_in,
                     *, T, chunk_dec):
    t = pl.program_id(1)

    @pl.when((pl.program_id(0) == 0) & (t == 0))
    def _():
        _load_cast(win_hbm, win_ref, stg_in, sem_in)

    @pl.when(t == 0)
    def _():
        s_scr[...] = jnp.zeros_like(s_scr)

    x = x_ref[...]
    h = _rms(x, g_ref[...]).astype(BF16)
    cq, sq, ck, sk = cq_ref[...], sq_ref[...], ck_ref[...], sk_ref[...]
    gh = RET_GROUP_HEADS
    n_pairs = HEADS // gh

    def projection(pair, dst):
        def piece(name, lo, width):
            def run():
                dst[name] = _dot(h, win_ref[:, lo:lo + width])
            return run
        v0, g0 = 2 * QK + gh * DV * pair, 2 * QK + VW + gh * DV * pair
        return ([piece("q", gh * DK * pair, gh * DK), piece("k", QK + gh * DK * pair, gh * DK)]
                + [piece("v%d" % sub, v0 + sub * DV, DV) for sub in range(gh)]
                + [piece("g%d" % sub, g0 + sub * DV, DV) for sub in range(gh)])

    def chunk_work(pair, src):
        state = {}

        def prep(sub):
            def run():
                q = _rope(src["q"][:, sub * DK:(sub + 1) * DK], cq, sq)
                k = _rope(src["k"][:, sub * DK:(sub + 1) * DK], ck, sk)
                state[sub] = dict(q=q, k=k, qb=q.astype(BF16), kb=k.astype(BF16), s=s_scr[gh * pair + sub])
            return run

        def chunk(sub, c):
            def run():
                hd = gh * pair + sub
                st = state[sub]
                rows = slice(c * CHUNK, (c + 1) * CHUNK)
                vb = src["v%d" % sub][rows, :].astype(BF16)
                scores = lax.dot_general(st["qb"][rows], st["kb"][rows], (((1,), (1,)), ((), ())),
                                         preferred_element_type=F32) * dm_ref[hd]
                o = (_dot(scores.astype(BF16), vb)
                     + _dot((st["q"][rows] * qd_ref[hd]).astype(BF16), st["s"].astype(BF16)))
                kt = (st["k"][rows] * kd_ref[hd]).T.astype(BF16)
                st["s"] = st["s"] * chunk_dec[hd] + _dot(kt, vb)
                gated = _group_norm_gate(o, src["g%d" % sub][rows, :], gng_ref[:, hd * DV:(hd + 1) * DV],
                                         gnb_ref[:, hd * DV:(hd + 1) * DV])
                gated_scr[rows, hd * DV:(hd + 1) * DV] = gated.astype(BF16)
            return run

        def finish(sub):
            def run():
                s_scr[gh * pair + sub] = state[sub]["s"]
            return run

        return ([prep(sub) for sub in range(gh)]
                + [chunk(sub, c) for c in range(T // CHUNK) for sub in range(gh)]
                + [finish(sub) for sub in range(gh)])

    acc = [x]

    def out_projection(pair):
        def run():
            cols = slice(pair * gh * DV, (pair + 1) * gh * DV)
            acc[0] = acc[0] + _dot(gated_scr[:, cols], wout_ref[cols, :])
        return run

    cur = {}
    for run in projection(0, cur):
        run()
    for pair in range(n_pairs):
        nxt = {}
        if pair + 1 < n_pairs:
            matmul_queue = projection(pair + 1, nxt)
        else:
            matmul_queue = [out_projection(p) for p in range(n_pairs - 1)]
        _interleave(matmul_queue, chunk_work(pair, cur))
        cur = nxt
    out_projection(n_pairs - 1)()
    o_ref[...] = acc[0]
    s_out_ref[...] = s_scr[...]


def _ret_prompt(x, g, w_in, gn_g, gn_b, woutb, *, T=512):
    B, L, _ = x.shape
    dmask, q_dec, k_dec, chunk_dec = _ret_constants(CHUNK)
    cos, sin = _rope_tables(np.arange(L))
    scale = DK ** -0.5
    cq, sq = jnp.asarray(cos, F32), jnp.asarray(sin, F32)
    ck, sk = jnp.asarray(cos * scale, F32), jnp.asarray(sin * scale, F32)
    qd = jnp.asarray(np.broadcast_to(q_dec[:, :, None], (HEADS, CHUNK, DK)))
    kd = jnp.asarray(np.broadcast_to(k_dec[:, :, None], (HEADS, CHUNK, DK)))
    blk = pl.BlockSpec((None, T, D), lambda b, t: (b, t, 0))
    tab = pl.BlockSpec((T, DK), lambda b, t: (t, 0))
    hcc = _const((HEADS, CHUNK, CHUNK))
    return pl.pallas_call(
        functools.partial(_ret_prompt_body, T=T, chunk_dec=tuple(float(c) for c in chunk_dec)),
        grid=(B, L // T),
        in_specs=[blk, _const((1, D)), pl.BlockSpec(memory_space=pl.ANY), tab, tab, tab, tab,
                  hcc, hcc, hcc, _const((1, VW)), _const((1, VW)), _const((VW, D))],
        out_specs=[blk, pl.BlockSpec((None, HEADS, DK, DV), lambda b, t: (b, 0, 0, 0))],
        out_shape=[jax.ShapeDtypeStruct((B, L, D), F32), jax.ShapeDtypeStruct((B, HEADS, DK, DV), F32)],
        scratch_shapes=[pltpu.VMEM((T, VW), BF16), pltpu.VMEM((HEADS, DK, DV), F32)]
        + _weight_scratch(D, 2 * QK + 2 * VW),
        compiler_params=_params(56, 2),
        name="ret_prompt",
    )(x, g.reshape(1, D), w_in, cq, sq, ck, sk, jnp.asarray(dmask), qd, kd,
      gn_g.reshape(1, VW), gn_b.reshape(1, VW), woutb)


def _ret_proj_sample_body(x_ref, g_ref, win_hbm, cq_ref, sq_ref, ck_ref, sk_ref,
                          qt_ref, kt_ref, v_ref, gate_ref, ov_ref, win_ref, stg_in, sem_in):
    _load_cast(win_hbm, win_ref, stg_in, sem_in)
    h = _rms(x_ref[...], g_ref[...]).astype(BF16)
    p = _dot(h, win_ref[...])
    for hd in range(HEADS):
        q = _rope(p[:, hd * DK:(hd + 1) * DK], cq_ref[...], sq_ref[...])
        k = _rope(p[:, QK + hd * DK:QK + (hd + 1) * DK], ck_ref[...], sk_ref[...])
        v = p[:, 2 * QK + hd * DV:2 * QK + (hd + 1) * DV]
        qt_ref[hd] = q.T
        kt_ref[hd] = k.T
        ov_ref[:, hd * DV:(hd + 1) * DV] = jnp.sum(q * k, axis=-1, keepdims=True) * v
    v_ref[...] = p[:, 2 * QK:2 * QK + VW]
    gate_ref[...] = p[:, 2 * QK + VW:]


def _ret_proj_sample(x2d, g, w_in, *, pos0):
    n = x2d.shape[0]
    cos, sin = _rope_tables(np.array([pos0]))
    scale = DK ** -0.5
    tabs = [jnp.asarray(a, F32) for a in (cos, sin, cos * scale, sin * scale)]
    row = pl.BlockSpec((n, VW), lambda i: (0, 0))
    tr = pl.BlockSpec((HEADS, DK, n), lambda i: (0, 0, 0))
    return pl.pallas_call(
        _ret_proj_sample_body,
        grid=(1,),
        in_specs=[_const((n, D)), _const((1, D)), pl.BlockSpec(memory_space=pl.ANY)] + [_const((1, DK))] * 4,
        out_specs=[tr, tr, row, row, row],
        out_shape=[jax.ShapeDtypeStruct((HEADS, DK, n), F32)] * 2 + [jax.ShapeDtypeStruct((n, VW), F32)] * 3,
        scratch_shapes=_weight_scratch(D, 2 * QK + 2 * VW),
        compiler_params=_params(40, 1),
        name="ret_proj_sample",
    )(x2d, g.reshape(1, D), w_in, *tabs)


def _mlp_ret_body(xp_ref, xs_ref, g_ref, fg_ref, w1_hbm, w2_hbm,
                  qt_ref, kt_ref, v_ref, ov_ref, gate_ref, gng_ref, gnb_ref, rwout_ref, s_hbm,
                  op_ref, os_ref, snew_hbm,
                  w1b, w2b, stg1, stg2, sem1, sem2, s_in, s_out, sem_in, sem_out, o_scr,
                  *, li, n_p, ns, gamma):
    i = pl.program_id(0)
    spb = ns // n_p
    prime, fetch_w1, fetch_w2 = _weight_stream(w1_hbm, w2_hbm, li, w1b, w2b, stg1, stg2, sem1, sem2)

    def load(c, slot):
        return pltpu.make_async_copy(s_hbm.at[c], s_in.at[slot], sem_in.at[slot])

    def store(c, slot):
        return pltpu.make_async_copy(s_out.at[slot], snew_hbm.at[c], sem_out.at[slot])

    def state_update(j):
        def run():
            c = i * spb + j
            slot = j % 2
            load(c, slot).wait()
            store(c, slot).wait()
            lane = lax.broadcasted_iota(jnp.int32, (DK, ns), 1) == c
            for hd in range(HEADS):
                cols = slice(hd * DV, (hd + 1) * DV)
                s_old = s_in[slot, hd]
                qcol = jnp.sum(jnp.where(lane, qt_ref[hd], 0.0), axis=1, keepdims=True)
                kcol = jnp.sum(jnp.where(lane, kt_ref[hd], 0.0), axis=1, keepdims=True)
                v = v_ref[pl.ds(c, 1), cols]
                o_scr[pl.ds(c, 1), cols] = jnp.sum(s_old * (qcol * gamma[hd]), axis=0, keepdims=True)
                s_out[slot, hd] = s_old * gamma[hd] + kcol * v
            store(c, slot).start()
            load(jnp.minimum(c + 2, ns - 1), slot).start()
        return run

    def states():
        return [state_update(j) for j in range(spb)]

    @pl.when(i == 0)
    def _():
        prime()
        s_out[...] = jnp.zeros_like(s_out)
        for slot in range(2):
            load(slot, slot).start()
            store(slot, slot).start()
        _interleave(_mlp_thunks(lambda: xp_ref[...], _store(op_ref), g_ref, fg_ref, w1b, w2b, False,
                                fetch_w1, fetch_w2), states())

    @pl.when((i > 0) & (i < n_p))
    def _():
        _interleave(_mlp_thunks(lambda: xp_ref[...], _store(op_ref), g_ref, fg_ref, w1b, w2b, False), states())

    @pl.when(i == n_p)
    def _():
        for slot in range(2):
            load(ns - 1, slot).wait()
            store(ns - 1, slot).wait()
        parts = []
        for hd in range(HEADS):
            cols = slice(hd * DV, (hd + 1) * DV)
            o = ov_ref[:, cols] + o_scr[:, cols]
            parts.append(_group_norm_gate(o, gate_ref[:, cols], gng_ref[:, cols], gnb_ref[:, cols]))
        gated = jnp.concatenate(parts, axis=-1).astype(BF16)
        xs1 = xs_ref[...] + _dot(gated, rwout_ref[...])
        _run(_mlp_thunks(lambda: xs1, _store(os_ref), g_ref, fg_ref, w1b, w2b, False))


def _mlp_ret(xp2d, xs2d, g, w1, w2, fg, qt, kt, v, ov, gate, gn_g, gn_b, rwoutb, s, *, li, tm=512):
    n, ns = xp2d.shape[0], xs2d.shape[0]
    n_p = n // tm
    _, _, _, chunk_dec = _ret_constants(1)
    row = pl.BlockSpec((tm, D), lambda i: (jnp.minimum(i, n_p - 1), 0))
    srow = pl.BlockSpec((ns, D), lambda i: (0, 0))
    hbm = pl.BlockSpec(memory_space=pl.ANY)
    wide = _const((ns, VW))
    return pl.pallas_call(
        functools.partial(_mlp_ret_body, li=li, n_p=n_p, ns=ns, gamma=tuple(float(c) for c in chunk_dec)),
        grid=(n_p + 1,),
        in_specs=[row, _const((ns, D)), _const((1, D)), _const((1, D)), hbm, hbm,
                  _const((HEADS, DK, ns)), _const((HEADS, DK, ns)), wide, wide, wide,
                  _const((1, VW)), _const((1, VW)), _const((VW, D)), hbm],
        out_specs=[row, srow, hbm],
        out_shape=[jax.ShapeDtypeStruct((n, D), F32), jax.ShapeDtypeStruct((ns, D), F32),
                   jax.ShapeDtypeStruct(s.shape, F32)],
        scratch_shapes=[pltpu.VMEM((D, D_FF), BF16), pltpu.VMEM((D_FF, D), BF16),
                        pltpu.VMEM((2, D, MLP_FC), F32), pltpu.VMEM((2, MLP_FC, D), F32),
                        pltpu.SemaphoreType.DMA((2,)), pltpu.SemaphoreType.DMA((2,)),
                        pltpu.VMEM((2, HEADS, DK, DV), F32), pltpu.VMEM((2, HEADS, DK, DV), F32),
                        pltpu.SemaphoreType.DMA((2,)), pltpu.SemaphoreType.DMA((2,)),
                        pltpu.VMEM((ns, VW), F32)],
        compiler_params=_params(58, 1),
        name="mlp_ret",
    )(xp2d, xs2d, g.reshape(1, D), fg.reshape(1, D), w1, w2, qt, kt, v, ov, gate,
      gn_g.reshape(1, VW), gn_b.reshape(1, VW), rwoutb, s)


def _lru_gates(xc, wax_ref, ba, bx, lam, heads):
    xcb = xc.astype(BF16)
    rs, is_ = [], []
    for n, hd in enumerate(heads):
        ri = _dot(xcb[:, n * DK:(n + 1) * DK], wax_ref[hd])
        rs.append(ri[:, :DK])
        is_.append(ri[:, DK:])
    r = jax.nn.sigmoid(jnp.concatenate(rs, axis=-1) + ba)
    i = jax.nn.sigmoid(jnp.concatenate(is_, axis=-1) + bx)
    log_a = r * (-LRU_C * _softplus(-lam))
    a = jnp.exp(log_a)
    mult = jnp.sqrt(-jnp.tanh(log_a) * (a * a + 1.0))
    return a, mult, i


LRU_GROUP_HEADS = 4


def _lru_prompt_body(x_ref, g_ref, win_hbm, cw_ref, cb_ref, wax_ref, ba_ref, bx_ref, lam_ref, wout_hbm,
                     o_ref, conv_ref, hlast_ref, xb_scr, hs_scr, carry_scr, h_scr,
                     win_ref, stg_in, sem_in, wout_ref, stg_out, sem_out, *, NB, T):
    t = pl.program_id(0)
    N = NB * T
    PS = T + 1
    halo = (CONV_W - 1) * NB
    gw = LRU_GROUP_HEADS * DK
    gblk = gw // LANES

    @pl.when(t == 0)
    def _():
        carry_scr[...] = jnp.zeros_like(carry_scr)
        h_scr[...] = jnp.zeros_like(h_scr)
        _load_cast(win_hbm, win_ref, stg_in, sem_in)
        _load_cast(wout_hbm, wout_ref, stg_out, sem_out)

    x = x_ref[...].reshape(N, D)
    h = _rms(x, g_ref[...]).astype(BF16)
    row = lax.broadcasted_iota(jnp.int32, (N, 1), 0)
    pos = t * T + lax.shift_right_logical(row, NB.bit_length() - 1)
    out = x
    def project(grp):
        c0, c1 = grp * gw, (grp + 1) * gw
        return _dot(h, win_ref[:, c0:c1]), _dot(h, win_ref[:, D + c0:D + c1])

    ngrp = D // gw
    nxt = project(0)
    for grp in range(ngrp):
        c0, c1 = grp * gw, (grp + 1) * gw
        blks = range(grp * gblk, (grp + 1) * gblk)
        zg, xb = nxt
        if grp + 1 < ngrp:
            nxt = project(grp + 1)
        gate = _gelu(zg)
        for n, j in enumerate(blks):
            for b in range(NB):
                xb_scr[j, b * PS:b * PS + T, :] = xb[b * T:(b + 1) * T, n * LANES:(n + 1) * LANES]
        ext = jnp.concatenate(
            [carry_scr[:, c0:c1]]
            + [jnp.concatenate([xb_scr[j, pl.ds(s, NB, stride=PS), :] for j in blks], axis=-1)
               for s in range(T)], axis=0)
        carry_scr[:, c0:c1] = ext[N:N + halo, :]
        acc = ext[0:N, :] * cw_ref[0:1, c0:c1]
        for j in range(1, CONV_W):
            acc = acc + ext[j * NB:j * NB + N, :] * cw_ref[j:j + 1, c0:c1]
        xc = cb_ref[:, c0:c1] + acc
        heads = range(grp * LRU_GROUP_HEADS, (grp + 1) * LRU_GROUP_HEADS)
        a, mult, i = _lru_gates(xc, wax_ref, ba_ref[:, c0:c1], bx_ref[:, c0:c1], lam_ref[:, c0:c1], heads)
        bvec = jnp.where(pos == 0, 1.0, mult) * (i * xc)
        hcur = h_scr[:, c0:c1]
        for s in range(T):
            hcur = a[s * NB:(s + 1) * NB, :] * hcur + bvec[s * NB:(s + 1) * NB, :]
            for n, j in enumerate(blks):
                hs_scr[j, pl.ds(s, NB, stride=PS), :] = hcur[:, n * LANES:(n + 1) * LANES]
        h_scr[:, c0:c1] = hcur
        hs = jnp.concatenate(
            [jnp.concatenate([hs_scr[j, b * PS:b * PS + T, :] for b in range(NB)], axis=0) for j in blks],
            axis=-1)
        out = out + _dot((hs * gate).astype(BF16), wout_ref[c0:c1, :])
    o_ref[...] = out.reshape(NB, T, D)
    hlast_ref[...] = h_scr[...]

    @pl.when(t == pl.num_programs(0) - 1)
    def _():
        for j in range(CONV_W - 1):
            conv_ref[:, j, :] = carry_scr[j * NB:(j + 1) * NB, :]


def _lru_prompt(x, g, w_in, conv_w, conv_b, waxb, b_a, b_x, lam, w_out, *, T=128):
    B, L, _ = x.shape
    blk = pl.BlockSpec((B, T, D), lambda t: (0, t, 0))
    vec = _const((1, D))
    hbm = pl.BlockSpec(memory_space=pl.ANY)
    return pl.pallas_call(
        functools.partial(_lru_prompt_body, NB=B, T=T),
        grid=(L // T,),
        in_specs=[blk, vec, hbm, _const((CONV_W, D)), vec, _const((HEADS, DK, 2 * DK)),
                  vec, vec, vec, hbm],
        out_specs=[blk, pl.BlockSpec((B, CONV_W - 1, D), lambda t: (0, 0, 0)),
                   pl.BlockSpec((B, D), lambda t: (0, 0))],
        out_shape=[jax.ShapeDtypeStruct((B, L, D), F32), jax.ShapeDtypeStruct((B, CONV_W - 1, D), F32),
                   jax.ShapeDtypeStruct((B, D), F32)],
        scratch_shapes=[pltpu.VMEM((D // LANES, B * (T + 1), LANES), F32)] * 2
        + [pltpu.VMEM(((CONV_W - 1) * B, D), F32), pltpu.VMEM((B, D), F32)]
        + _weight_scratch(D, 2 * D) + _weight_scratch(D, D),
        compiler_params=_params(52, 1),
        name="lru_prompt",
    )(x, g.reshape(1, D), w_in, conv_w, conv_b.reshape(1, D), waxb, b_a.reshape(1, D), b_x.reshape(1, D),
      lam.reshape(1, D), w_out)


def _lru_sample_body(x_ref, cbuf_ref, h0_ref, g_ref, win_ref, cw_ref, cb_ref, wax_ref, ba_ref, bx_ref,
                     lam_ref, wout_ref, o_ref, nconv_ref, hnew_ref, *, pos0):
    x = x_ref[...]
    h = _rms(x, g_ref[...]).astype(BF16)
    z = _dot(h, win_ref[...].astype(BF16))
    gate = _gelu(z[:, :D])
    xb = z[:, D:]
    cw = cw_ref[...]
    acc = cbuf_ref[0] * cw[0:1, :]
    for j in range(1, CONV_W - 1):
        acc = acc + cbuf_ref[j] * cw[j:j + 1, :]
    acc = acc + xb * cw[CONV_W - 1:CONV_W, :]
    xc = cb_ref[...] + acc
    a, mult, i = _lru_gates(xc, wax_ref, ba_ref[...], bx_ref[...], lam_ref[...], range(HEADS))
    if pos0 == 0:
        mult = jnp.ones_like(mult)
    hnew = a * h0_ref[...] + mult * (i * xc)
    hnew_ref[...] = hnew
    nconv_ref[0:CONV_W - 2] = cbuf_ref[1:CONV_W - 1]
    nconv_ref[CONV_W - 2] = xb
    o_ref[...] = x + _dot((hnew * gate).astype(BF16), wout_ref[...].astype(BF16))


def _lru_sample(x2d, cbuf, h0, g, winb, conv_w, conv_b, waxb, b_a, b_x, lam, woutb, *, pos0):
    n = x2d.shape[0]
    vec = _const((1, D))
    cshape = (CONV_W - 1, n, D)
    out = pl.BlockSpec((n, D), lambda i: (0, 0))
    return pl.pallas_call(
        functools.partial(_lru_sample_body, pos0=pos0),
        grid=(1,),
        in_specs=[_const((n, D)), _const(cshape), _const((n, D)), vec, _const((D, 2 * D)), _const((CONV_W, D)),
                  vec, _const((HEADS, DK, 2 * DK)), vec, vec, vec, _const((D, D))],
        out_specs=[out, pl.BlockSpec(cshape, lambda i: (0, 0, 0)), out],
        out_shape=[jax.ShapeDtypeStruct((n, D), F32), jax.ShapeDtypeStruct(cshape, F32),
                   jax.ShapeDtypeStruct((n, D), F32)],
        compiler_params=_params(32, 1),
        name="lru_sample",
    )(x2d, cbuf, h0, g.reshape(1, D), winb, conv_w, conv_b.reshape(1, D), waxb, b_a.reshape(1, D),
      b_x.reshape(1, D), lam.reshape(1, D), woutb)


def kernel(x_prompt, x_sample, state_pool, state_ret, state_conv, state_lru, pool_norm, pool_w, pool_scale, gm_norm, gm_w_in, gm_b_in, gm_ln_g, gm_ln_b, gm_w_s, gm_b_s, gm_w_out, ret_norm, ret_w_in, ret_gn_g, ret_gn_b, ret_w_out, lru_norm, lru_w_in, lru_conv_w, lru_conv_b, lru_w_a, lru_b_a, lru_w_x, lru_b_x, lru_lam, lru_w_out, mlp_norm, mlp_w1, mlp_w2, final_norm):
    B, L, _ = x_prompt.shape
    NS = x_sample.shape[0]
    bf = lambda w: w.astype(BF16)

    def mlp(xp, xs, li, final=False):
        yp, ys = _mlp(xp.reshape(B * L, D), xs, mlp_norm[li], mlp_w1, mlp_w2, final_norm, li=li, final=final)
        return yp.reshape(B, L, D), ys

    pool_wb = bf(pool_w[0])
    xs, pool_s = _pool_sample(x_sample.reshape(NS, D), jnp.swapaxes(state_pool[0], 0, 1),
                              pool_norm[0], pool_wb, pool_scale[0], pos0=PAST_LEN)
    xp, xs, pool_p = _pool_mlp(x_prompt, xs, pool_norm[0], pool_wb, pool_scale[0], mlp_norm[0],
                               mlp_w1, mlp_w2, final_norm, li=0)

    bs_full = jnp.repeat(gm_b_s[0].T, CHUNK, axis=1)
    xp = _gmlp_prompt(xp, gm_norm[0], gm_w_in[0], gm_b_in[0], gm_ln_g[0], gm_ln_b[0], gm_w_s[0], bs_full,
                      gm_w_out[0])
    sw = jnp.repeat(gm_w_s[0][:, 0, 0], CHUNK).reshape(1, D)
    sb = jnp.repeat(gm_b_s[0][:, 0], CHUNK).reshape(1, D)
    xs, v_s = _gmlp_sample(xs, gm_norm[0], gm_w_in[0], gm_b_in[0], gm_ln_g[0], gm_ln_b[0], sw, sb, gm_w_out[0])
    xp, xs = mlp(xp, xs, 1)

    ret_woutb = bf(ret_w_out[0])
    xp, ret_p = _ret_prompt(xp, ret_norm[0], ret_w_in[0], ret_gn_g[0], ret_gn_b[0], ret_woutb)
    qt, kt, v, gate, ov = _ret_proj_sample(xs, ret_norm[0], ret_w_in[0], pos0=PAST_LEN)
    xp, xs, ret_s = _mlp_ret(xp.reshape(B * L, D), xs, mlp_norm[2], mlp_w1, mlp_w2, final_norm,
                             qt, kt, v, ov, gate, ret_gn_g[0], ret_gn_b[0], ret_woutb, state_ret[0], li=2)
    xp = xp.reshape(B, L, D)

    waxb = bf(jnp.concatenate([lru_w_a[0], lru_w_x[0]], axis=-1))
    xp, conv_p, lru_p = _lru_prompt(xp, lru_norm[0], lru_w_in[0], lru_conv_w[0], lru_conv_b[0], waxb,
                                    lru_b_a[0], lru_b_x[0], lru_lam[0], lru_w_out[0])
    xs, conv_s, lru_s = _lru_sample(xs, jnp.swapaxes(state_conv[0], 0, 1), state_lru[0],
                                    lru_norm[0], lru_w_in[0], lru_conv_w[0], lru_conv_b[0], waxb,
                                    lru_b_a[0], lru_b_x[0], lru_lam[0], lru_w_out[0], pos0=PAST_LEN)
    yp, ys = mlp(xp, xs, 3, final=True)

    return (yp, ys.reshape(NS, 1, D),
            pool_p[None], jnp.swapaxes(pool_s, 0, 1)[None],
            v_s.reshape(1, NS, 1, D),
            ret_p[None], ret_s[None],
            conv_p[None], jnp.swapaxes(conv_s, 0, 1)[None],
            lru_p[None], lru_s[None])
```

```python
import functools
import math

import jax
import jax.numpy as jnp
import numpy as np
from jax import lax
from jax.experimental import pallas as pl
from jax.experimental.pallas import tpu as pltpu

F32 = jnp.float32
BF16 = jnp.bfloat16

D = 1024
EPS = 1e-6
GN_EPS = 1e-5
PAST_LEN = 16384
POOL_WINDOWS = (2, 4, 8, 16)
POOL_GROUP = D // len(POOL_WINDOWS)
POOL_BUF = max(POOL_WINDOWS) - 1
CHUNK = 128
GM_GROUPS = 8
HEADS = 8
DK = D // HEADS
DV = 2 * D // HEADS
QK = HEADS * DK
VW = HEADS * DV
ROPE_BASE = 10000.0
CONV_W = 4
LRU_C = 8.0
D_FF = 4 * D

MIB = 1024 * 1024
SUBLANES = 8
LANES = 128


def _params(vmem_mib, n_grid):
    return pltpu.CompilerParams(
        dimension_semantics=("arbitrary",) * n_grid,
        vmem_limit_bytes=vmem_mib * MIB,
    )


def _const(shape):
    zeros = (0,) * len(shape)
    return pl.BlockSpec(shape, lambda *_: zeros, pipeline_mode=pl.Buffered(1))


def _rms(x, g):
    ms = jnp.mean(x * x, axis=-1, keepdims=True)
    return x * lax.rsqrt(ms + EPS) * g


def _dot(a, b):
    return jnp.dot(a, b, preferred_element_type=F32)


def _interleave(*queues):
    pos = [0] * len(queues)
    while any(p < len(q) for p, q in zip(pos, queues)):
        _, i = min((pos[i] / len(q), i) for i, q in enumerate(queues) if pos[i] < len(q))
        queues[i][pos[i]]()
        pos[i] += 1


def _softplus(x):
    return jnp.maximum(x, 0.0) + jnp.log1p(jnp.exp(-jnp.abs(x)))


GELU_K = math.sqrt(2.0 / math.pi)


def _gelu(x):
    half = 0.5 * x
    return half + half * jnp.tanh(x * (GELU_K + (GELU_K * 0.044715) * (x * x)))


MLP_FC = 512
STAGE_BYTES = 2 * MIB


def _load_cast(src, dst, stg, sem):
    ch = stg.shape[1]
    n = src.shape[0] // ch

    def copy(c, slot):
        return pltpu.make_async_copy(src.at[pl.ds(c * ch, ch), :], stg.at[slot], sem.at[slot])

    copy(0, 0).start()
    for c in range(n):
        slot = c % 2
        if c + 1 < n:
            copy(c + 1, 1 - slot).start()
        copy(c, slot).wait()
        dst[c * ch:(c + 1) * ch, :] = stg[slot].astype(BF16)


def _mlp_thunks(read_x, write_out, g_ref, fg_ref, w1b, w2b, final, fetch_w1=None, fetch_w2=None):
    st = {}

    def head():
        x = read_x()
        st["h"] = _rms(x, g_ref[...]).astype(BF16)
        st["acc"] = x

    def up(c):
        def run():
            if fetch_w1 is not None:
                fetch_w1(c)
            a = _dot(st["h"], w1b[:, c * MLP_FC:(c + 1) * MLP_FC])
            st["a"] = jnp.square(jnp.maximum(a, 0.0)).astype(BF16)
        return run

    def down(c):
        def run():
            if fetch_w2 is not None:
                fetch_w2(c)
            st["acc"] = st["acc"] + _dot(st["a"], w2b[c * MLP_FC:(c + 1) * MLP_FC, :])
        return run

    def tail():
        acc = st["acc"]
        write_out(_rms(acc, fg_ref[...]) if final else acc)

    return [head] + [f(c) for c in range(D_FF // MLP_FC) for f in (up, down)] + [tail]


def _run(thunks):
    for thunk in thunks:
        thunk()


def _store(ref):
    def write(v):
        ref[...] = v
    return write


def _weight_stream(w1_hbm, w2_hbm, li, w1b, w2b, stg1, stg2, sem1, sem2):
    nch = D_FF // MLP_FC

    def w1_copy(c):
        return pltpu.make_async_copy(w1_hbm.at[li, :, pl.ds(c * MLP_FC, MLP_FC)], stg1.at[c % 2], sem1.at[c % 2])

    def w2_copy(c):
        return pltpu.make_async_copy(w2_hbm.at[li, pl.ds(c * MLP_FC, MLP_FC), :], stg2.at[c % 2], sem2.at[c % 2])

    def prime():
        for c in range(min(2, nch)):
            w1_copy(c).start()
            w2_copy(c).start()

    def fetch_w1(c):
        w1_copy(c).wait()
        w1b[:, c * MLP_FC:(c + 1) * MLP_FC] = stg1[c % 2].astype(BF16)
        if c + 2 < nch:
            w1_copy(c + 2).start()

    def fetch_w2(c):
        w2_copy(c).wait()
        w2b[c * MLP_FC:(c + 1) * MLP_FC, :] = stg2[c % 2].astype(BF16)
        if c + 2 < nch:
            w2_copy(c + 2).start()

    return prime, fetch_w1, fetch_w2


def _mlp_body(xp_ref, xs_ref, g_ref, fg_ref, w1_hbm, w2_hbm, op_ref, os_ref,
              w1b, w2b, stg1, stg2, sem1, sem2, *, li, final, n_p):
    i = pl.program_id(0)
    prime, fetch_w1, fetch_w2 = _weight_stream(w1_hbm, w2_hbm, li, w1b, w2b, stg1, stg2, sem1, sem2)

    @pl.when(i == 0)
    def _():
        prime()
        _run(_mlp_thunks(lambda: xp_ref[...], _store(op_ref), g_ref, fg_ref, w1b, w2b, final,
                         fetch_w1, fetch_w2))

    @pl.when((i > 0) & (i < n_p))
    def _():
        _run(_mlp_thunks(lambda: xp_ref[...], _store(op_ref), g_ref, fg_ref, w1b, w2b, final))

    @pl.when(i == n_p)
    def _():
        _run(_mlp_thunks(lambda: xs_ref[...], _store(os_ref), g_ref, fg_ref, w1b, w2b, final))


def _mlp(xp2d, xs2d, g, w1, w2, fg, *, li, final, tm=512):
    n, ns = xp2d.shape[0], xs2d.shape[0]
    n_p = n // tm
    row = pl.BlockSpec((tm, D), lambda i: (jnp.minimum(i, n_p - 1), 0))
    srow = pl.BlockSpec((ns, D), lambda i: (0, 0))
    hbm = pl.BlockSpec(memory_space=pl.ANY)
    return pl.pallas_call(
        functools.partial(_mlp_body, li=li, final=final, n_p=n_p),
        grid=(n_p + 1,),
        in_specs=[row, _const((ns, D)), _const((1, D)), _const((1, D)), hbm, hbm],
        out_specs=[row, srow],
        out_shape=[jax.ShapeDtypeStruct((n, D), F32), jax.ShapeDtypeStruct((ns, D), F32)],
        scratch_shapes=[pltpu.VMEM((D, D_FF), BF16), pltpu.VMEM((D_FF, D), BF16),
                        pltpu.VMEM((2, D, MLP_FC), F32), pltpu.VMEM((2, MLP_FC, D), F32),
                        pltpu.SemaphoreType.DMA((2,)), pltpu.SemaphoreType.DMA((2,))],
        compiler_params=_params(48, 1),
        name="mlp",
    )(xp2d, xs2d, g.reshape(1, D), fg.reshape(1, D), w1, w2)


POOL_HALO = 32


def _pool_thunks(x_ref, g_ref, w_ref, sc_ref, write_out, buf_ref, ext_ref, *, T, t):
    st = {}

    def head():
        x = x_ref[...]
        h = _rms(x, g_ref[...])
        halo = jnp.where(t == 0, 0.0, ext_ref[...])
        st["x"], st["h"] = x, h
        st["ext"] = jnp.concatenate([halo, h], axis=0)
        st["pos"] = t * T + lax.broadcasted_iota(jnp.int32, (T, 1), 0)
        st["y"] = [None] * len(POOL_WINDOWS)

    def group(gi, w):
        def run():
            c0, c1 = gi * POOL_GROUP, (gi + 1) * POOL_GROUP
            levels = w.bit_length() - 1
            a = st["ext"][POOL_HALO - SUBLANES * levels:, c0:c1]
            for k in range(levels):
                shift = 1 << k
                n = a.shape[0] - SUBLANES
                a = a[SUBLANES:, :] + a[SUBLANES - shift:SUBLANES - shift + n, :]
            hg = st["h"][:, c0:c1]
            cnt = jnp.minimum(st["pos"] + 1, w).astype(F32)
            d = a / cnt - hg
            st["y"][gi] = _dot(d.astype(BF16), w_ref[gi])
        return run

    def tail():
        y = jnp.concatenate(st["y"], axis=-1) * sc_ref[...]
        write_out(st["x"] + y)
        ext_ref[...] = st["h"][T - POOL_HALO:, :]
        buf_ref[...] = st["h"][T - POOL_BUF:, :]

    return [head] + [group(gi, w) for gi, w in enumerate(POOL_WINDOWS)] + [tail]


def _pool_mlp_body(x_ref, xs_ref, pg_ref, pw_ref, psc_ref, g_ref, fg_ref, w1_hbm, w2_hbm,
                   o_ref, os_ref, buf_ref, ext_ref, x1_scr, w1b, w2b, stg1, stg2, sem1, sem2,
                   *, T, tiles_per_seq, n, li):
    s = pl.program_id(0)
    t = lax.rem(s, tiles_per_seq)
    slot = lax.rem(s, 2)
    prime, fetch_w1, fetch_w2 = _weight_stream(w1_hbm, w2_hbm, li, w1b, w2b, stg1, stg2, sem1, sem2)

    def write_x1(v):
        x1_scr[slot] = v

    def mixer():
        return _pool_thunks(x_ref, pg_ref, pw_ref, psc_ref, write_x1, buf_ref, ext_ref, T=T, t=t)

    def mlp(*fetch):
        return _mlp_thunks(lambda: x1_scr[1 - slot], _store(o_ref), g_ref, fg_ref, w1b, w2b, False, *fetch)

    @pl.when(s == 0)
    def _():
        ext_ref[0:POOL_HALO, :] = jnp.zeros((POOL_HALO, D), F32)
        prime()
        _run(mixer())

    @pl.when(s == 1)
    def _():
        _interleave(mlp(fetch_w1, fetch_w2), mixer())

    @pl.when((s > 1) & (s < n))
    def _():
        _interleave(mlp(), mixer())

    @pl.when(s == n)
    def _():
        _run(mlp())
        _run(_mlp_thunks(lambda: xs_ref[...], _store(os_ref), g_ref, fg_ref, w1b, w2b, False))


def _pool_mlp(x, xs2d, pg, pwb, psc, g, w1, w2, fg, *, li, T=512):
    B, L, _ = x.shape
    ns = xs2d.shape[0]
    tps = L // T
    n = B * tps
    tile = lambda s: (jnp.minimum(s, n - 1) // tps, lax.rem(jnp.minimum(s, n - 1), tps), 0)
    prev = lambda s: (jnp.maximum(s - 1, 0) // tps, lax.rem(jnp.maximum(s - 1, 0), tps), 0)
    hbm = pl.BlockSpec(memory_space=pl.ANY)
    return pl.pallas_call(
        functools.partial(_pool_mlp_body, T=T, tiles_per_seq=tps, n=n, li=li),
        grid=(n + 1,),
        in_specs=[pl.BlockSpec((None, T, D), tile), _const((ns, D)), _const((1, D)),
                  _const((len(POOL_WINDOWS), POOL_GROUP, POOL_GROUP)), _const((1, D)),
                  _const((1, D)), _const((1, D)), hbm, hbm],
        out_specs=[pl.BlockSpec((None, T, D), prev), pl.BlockSpec((ns, D), lambda s: (0, 0)),
                   pl.BlockSpec((None, POOL_BUF, D), lambda s: (jnp.minimum(s, n - 1) // tps, 0, 0))],
        out_shape=[jax.ShapeDtypeStruct((B, L, D), F32), jax.ShapeDtypeStruct((ns, D), F32),
                   jax.ShapeDtypeStruct((B, POOL_BUF, D), F32)],
        scratch_shapes=[pltpu.VMEM((POOL_HALO, D), F32), pltpu.VMEM((2, T, D), F32),
                        pltpu.VMEM((D, D_FF), BF16), pltpu.VMEM((D_FF, D), BF16),
                        pltpu.VMEM((2, D, MLP_FC), F32), pltpu.VMEM((2, MLP_FC, D), F32),
                        pltpu.SemaphoreType.DMA((2,)), pltpu.SemaphoreType.DMA((2,))],
        compiler_params=_params(56, 1),
        name="pool_mlp",
    )(x, xs2d, pg.reshape(1, D), pwb, psc.reshape(1, D), g.reshape(1, D), fg.reshape(1, D), w1, w2)


def _pool_sample_body(x_ref, buf_ref, g_ref, w_ref, sc_ref, o_ref, nbuf_ref, *, pos0):
    x = x_ref[...]
    h = _rms(x, g_ref[...])
    ys = []
    for gi, w in enumerate(POOL_WINDOWS):
        c0, c1 = gi * POOL_GROUP, (gi + 1) * POOL_GROUP
        hg = h[:, c0:c1]
        s = hg
        for k in range(1, w):
            s = s + buf_ref[POOL_BUF - k, :, c0:c1]
        cnt = float(min(pos0 + 1, w))
        d = s / cnt - hg
        ys.append(_dot(d.astype(BF16), w_ref[gi]))
    y = jnp.concatenate(ys, axis=-1) * sc_ref[...]
    o_ref[...] = x + y
    nbuf_ref[0:POOL_BUF - 1] = buf_ref[1:POOL_BUF]
    nbuf_ref[POOL_BUF - 1] = h


def _pool_sample(x2d, buf, g, wb, sc, *, pos0, bt=32):
    n = x2d.shape[0]
    row = pl.BlockSpec((bt, D), lambda i: (i, 0))
    brow = pl.BlockSpec((POOL_BUF, bt, D), lambda i: (0, i, 0))
    return pl.pallas_call(
        functools.partial(_pool_sample_body, pos0=pos0),
        grid=(n // bt,),
        in_specs=[row, brow, _const((1, D)), _const((len(POOL_WINDOWS), POOL_GROUP, POOL_GROUP)), _const((1, D))],
        out_specs=[row, brow],
        out_shape=[jax.ShapeDtypeStruct((n, D), F32), jax.ShapeDtypeStruct((POOL_BUF, n, D), F32)],
        compiler_params=_params(32, 1),
        name="pool_sample",
    )(x2d, buf, g.reshape(1, D), wb, sc.reshape(1, D))


def _gmlp_front(x, g_ref, win_ref, bin_ref, lng_ref, lnb_ref):
    h = _rms(x, g_ref[...]).astype(BF16)
    z = _gelu(_dot(h, win_ref[...]) + bin_ref[...])
    u, v = z[:, :D], z[:, D:]
    mu = jnp.mean(v, axis=-1, keepdims=True)
    vc = v - mu
    var = jnp.mean(jnp.square(vc), axis=-1, keepdims=True)
    vn = vc * lax.rsqrt(var + EPS) * lng_ref[...] + lnb_ref[...]
    return u, vn


def _gmlp_prompt_body(x_ref, g_ref, win_ref, bin_ref, lng_ref, lnb_ref, ws_ref, bs_ref, wout_ref,
                      o_ref, *, T):
    ti = lax.broadcasted_iota(jnp.int32, (CHUNK, CHUNK), 0)
    si = lax.broadcasted_iota(jnp.int32, (CHUNK, CHUNK), 1)
    causal = ti >= si
    wsm = [jnp.where(causal, ws_ref[gi], 0.0).astype(BF16) for gi in range(GM_GROUPS)]
    x = x_ref[...]
    u, vn = _gmlp_front(x, g_ref, win_ref, bin_ref, lng_ref, lnb_ref)
    vnb = vn.astype(BF16)
    rows = []
    for n in range(T // CHUNK):
        r0, r1 = n * CHUNK, (n + 1) * CHUNK
        cols = [_dot(wsm[gi], vnb[r0:r1, gi * CHUNK:(gi + 1) * CHUNK]) for gi in range(GM_GROUPS)]
        rows.append(jnp.concatenate(cols, axis=-1) + bs_ref[...])
    mixed = jnp.concatenate(rows, axis=0)
    y = _dot((u * mixed).astype(BF16), wout_ref[...])
    o_ref[...] = x + y


def _gmlp_prompt(x, g, winb, b_in, ln_g, ln_b, w_s, bs_full, woutb, *, T=1024):
    B, L, _ = x.shape
    blk = pl.BlockSpec((None, T, D), lambda b, t: (b, t, 0))
    return pl.pallas_call(
        functools.partial(_gmlp_prompt_body, T=T),
        grid=(B, L // T),
        in_specs=[blk, _const((1, D)), _const((D, 2 * D)), _const((1, 2 * D)), _const((1, D)), _const((1, D)),
                  _const((GM_GROUPS, CHUNK, CHUNK)), _const((CHUNK, D)), _const((D, D))],
        out_specs=blk,
        out_shape=jax.ShapeDtypeStruct((B, L, D), F32),
        compiler_params=_params(48, 2),
        name="gmlp_prompt",
    )(x, g.reshape(1, D), winb, b_in.reshape(1, 2 * D), ln_g.reshape(1, D), ln_b.reshape(1, D),
      w_s, bs_full, woutb)


def _gmlp_sample_body(x_ref, g_ref, win_ref, bin_ref, lng_ref, lnb_ref, sw_ref, sb_ref, wout_ref,
                      o_ref, vn_ref):
    x = x_ref[...]
    u, vn = _gmlp_front(x, g_ref, win_ref, bin_ref, lng_ref, lnb_ref)
    mixed = vn * sw_ref[...] + sb_ref[...]
    y = _dot((u * mixed).astype(BF16), wout_ref[...])
    o_ref[...] = x + y
    vn_ref[...] = vn


def _gmlp_sample(x2d, g, winb, b_in, ln_g, ln_b, sw, sb, woutb):
    n = x2d.shape[0]
    full = _const((n, D))
    return pl.pallas_call(
        _gmlp_sample_body,
        grid=(1,),
        in_specs=[full, _const((1, D)), _const((D, 2 * D)), _const((1, 2 * D)), _const((1, D)), _const((1, D)),
                  _const((1, D)), _const((1, D)), _const((D, D))],
        out_specs=[pl.BlockSpec((n, D), lambda i: (0, 0)), pl.BlockSpec((n, D), lambda i: (0, 0))],
        out_shape=[jax.ShapeDtypeStruct((n, D), F32), jax.ShapeDtypeStruct((n, D), F32)],
        compiler_params=_params(32, 1),
        name="gmlp_sample",
    )(x2d, g.reshape(1, D), winb, b_in.reshape(1, 2 * D), ln_g.reshape(1, D), ln_b.reshape(1, D),
      sw, sb, woutb)


def _ret_constants(C):
    log_gamma = np.log1p(-np.exp2(-5.0 - np.arange(HEADS, dtype=np.float64)))
    idx = np.arange(C, dtype=np.float64)
    diff = idx[:, None] - idx[None, :]
    dmask = np.where(diff[None] >= 0, np.exp(log_gamma[:, None, None] * np.maximum(diff, 0.0)[None]), 0.0)
    q_dec = np.exp(log_gamma[:, None] * (idx + 1.0))
    k_dec = np.exp(log_gamma[:, None] * (C - 1.0 - idx))
    chunk_dec = np.exp(log_gamma * C)
    return dmask.astype(np.float32), q_dec.astype(np.float32), k_dec.astype(np.float32), chunk_dec


def _rope_tables(pos):
    half = DK // 2
    freqs = np.exp(-math.log(ROPE_BASE) * np.arange(half, dtype=np.float64) / half)
    ang = np.asarray(pos, dtype=np.float64)[:, None] * freqs[None]
    cos = np.concatenate([np.cos(ang), np.cos(ang)], axis=-1)
    sin = np.concatenate([-np.sin(ang), np.sin(ang)], axis=-1)
    return cos, sin


def _rope(t, cos, sin):
    return t * cos + pltpu.roll(t, DK // 2, axis=1) * sin


def _group_norm_gate(o, gate, gng, gnb):
    mu = jnp.mean(o, axis=-1, keepdims=True)
    oc = o - mu
    var = jnp.mean(jnp.square(oc), axis=-1, keepdims=True)
    on = oc * lax.rsqrt(var + GN_EPS) * gng + gnb
    return jax.nn.silu(gate) * on


RET_GROUP_HEADS = 2


def _ret_prompt_body(x_ref, g_ref, win_ref, cq_ref, sq_ref, ck_ref, sk_ref, dm_ref, qd_ref, kd_ref,
                     gng_ref, gnb_ref, wout_ref, o_ref, s_out_ref, gated_scr, s_scr,
                     *, T, chunk_dec):
    t = pl.program_id(1)

    @pl.when(t == 0)
    def _():
        s_scr[...] = jnp.zeros_like(s_scr)

    x = x_ref[...]
    h = _rms(x, g_ref[...]).astype(BF16)
    cq, sq, ck, sk = cq_ref[...], sq_ref[...], ck_ref[...], sk_ref[...]
    gh = RET_GROUP_HEADS
    n_pairs = HEADS // gh

    def projection(pair, dst):
        def piece(name, lo, width):
            def run():
                dst[name] = _dot(h, win_ref[:, lo:lo + width])
            return run
        v0, g0 = 2 * QK + gh * DV * pair, 2 * QK + VW + gh * DV * pair
        return ([piece("q", gh * DK * pair, gh * DK), piece("k", QK + gh * DK * pair, gh * DK)]
                + [piece("v%d" % sub, v0 + sub * DV, DV) for sub in range(gh)]
                + [piece("g%d" % sub, g0 + sub * DV, DV) for sub in range(gh)])

    def chunk_work(pair, src):
        state = {}

        def prep(sub):
            def run():
                q = _rope(src["q"][:, sub * DK:(sub + 1) * DK], cq, sq)
                k = _rope(src["k"][:, sub * DK:(sub + 1) * DK], ck, sk)
                state[sub] = dict(q=q, k=k, qb=q.astype(BF16), kb=k.astype(BF16), s=s_scr[gh * pair + sub])
            return run

        def chunk(sub, c):
            def run():
                hd = gh * pair + sub
                st = state[sub]
                rows = slice(c * CHUNK, (c + 1) * CHUNK)
                vb = src["v%d" % sub][rows, :].astype(BF16)
                scores = lax.dot_general(st["qb"][rows], st["kb"][rows], (((1,), (1,)), ((), ())),
                                         preferred_element_type=F32) * dm_ref[hd]
                o = (_dot(scores.astype(BF16), vb)
                     + _dot((st["q"][rows] * qd_ref[hd]).astype(BF16), st["s"].astype(BF16)))
                kt = (st["k"][rows] * kd_ref[hd]).T.astype(BF16)
                st["s"] = st["s"] * chunk_dec[hd] + _dot(kt, vb)
                gated = _group_norm_gate(o, src["g%d" % sub][rows, :], gng_ref[:, hd * DV:(hd + 1) * DV],
                                         gnb_ref[:, hd * DV:(hd + 1) * DV])
                gated_scr[rows, hd * DV:(hd + 1) * DV] = gated.astype(BF16)
            return run

        def finish(sub):
            def run():
                s_scr[gh * pair + sub] = state[sub]["s"]
            return run

        return ([prep(sub) for sub in range(gh)]
                + [chunk(sub, c) for c in range(T // CHUNK) for sub in range(gh)]
                + [finish(sub) for sub in range(gh)])

    acc = [x]

    def out_projection(pair):
        def run():
            cols = slice(pair * gh * DV, (pair + 1) * gh * DV)
            acc[0] = acc[0] + _dot(gated_scr[:, cols], wout_ref[cols, :])
        return run

    cur = {}
    for run in projection(0, cur):
        run()
    for pair in range(n_pairs):
        nxt = {}
        if pair + 1 < n_pairs:
            matmul_queue = projection(pair + 1, nxt)
        else:
            matmul_queue = [out_projection(p) for p in range(n_pairs - 1)]
        _interleave(matmul_queue, chunk_work(pair, cur))
        cur = nxt
    out_projection(n_pairs - 1)()
    o_ref[...] = acc[0]
    s_out_ref[...] = s_scr[...]


def _ret_prompt(x, g, winb, gn_g, gn_b, woutb, *, T=512):
    B, L, _ = x.shape
    dmask, q_dec, k_dec, chunk_dec = _ret_constants(CHUNK)
    cos, sin = _rope_tables(np.arange(L))
    scale = DK ** -0.5
    cq, sq = jnp.asarray(cos, F32), jnp.asarray(sin, F32)
    ck, sk = jnp.asarray(cos * scale, F32), jnp.asarray(sin * scale, F32)
    qd = jnp.asarray(np.broadcast_to(q_dec[:, :, None], (HEADS, CHUNK, DK)))
    kd = jnp.asarray(np.broadcast_to(k_dec[:, :, None], (HEADS, CHUNK, DK)))
    blk = pl.BlockSpec((None, T, D), lambda b, t: (b, t, 0))
    tab = pl.BlockSpec((T, DK), lambda b, t: (t, 0))
    hcc = _const((HEADS, CHUNK, CHUNK))
    return pl.pallas_call(
        functools.partial(_ret_prompt_body, T=T, chunk_dec=tuple(float(c) for c in chunk_dec)),
        grid=(B, L // T),
        in_specs=[blk, _const((1, D)), _const((D, 2 * QK + 2 * VW)), tab, tab, tab, tab,
                  hcc, hcc, hcc, _const((1, VW)), _const((1, VW)), _const((VW, D))],
        out_specs=[blk, pl.BlockSpec((None, HEADS, DK, DV), lambda b, t: (b, 0, 0, 0))],
        out_shape=[jax.ShapeDtypeStruct((B, L, D), F32), jax.ShapeDtypeStruct((B, HEADS, DK, DV), F32)],
        scratch_shapes=[pltpu.VMEM((T, VW), BF16), pltpu.VMEM((HEADS, DK, DV), F32)],
        compiler_params=_params(56, 2),
        name="ret_prompt",
    )(x, g.reshape(1, D), winb, cq, sq, ck, sk, jnp.asarray(dmask), qd, kd,
      gn_g.reshape(1, VW), gn_b.reshape(1, VW), woutb)


def _ret_proj_sample_body(x_ref, g_ref, win_ref, cq_ref, sq_ref, ck_ref, sk_ref,
                          qt_ref, kt_ref, v_ref, gate_ref, ov_ref):
    h = _rms(x_ref[...], g_ref[...]).astype(BF16)
    p = _dot(h, win_ref[...])
    for hd in range(HEADS):
        q = _rope(p[:, hd * DK:(hd + 1) * DK], cq_ref[...], sq_ref[...])
        k = _rope(p[:, QK + hd * DK:QK + (hd + 1) * DK], ck_ref[...], sk_ref[...])
        v = p[:, 2 * QK + hd * DV:2 * QK + (hd + 1) * DV]
        qt_ref[hd] = q.T
        kt_ref[hd] = k.T
        ov_ref[:, hd * DV:(hd + 1) * DV] = jnp.sum(q * k, axis=-1, keepdims=True) * v
    v_ref[...] = p[:, 2 * QK:2 * QK + VW]
    gate_ref[...] = p[:, 2 * QK + VW:]


def _ret_proj_sample(x2d, g, winb, *, pos0):
    n = x2d.shape[0]
    cos, sin = _rope_tables(np.array([pos0]))
    scale = DK ** -0.5
    tabs = [jnp.asarray(a, F32) for a in (cos, sin, cos * scale, sin * scale)]
    row = pl.BlockSpec((n, VW), lambda i: (0, 0))
    tr = pl.BlockSpec((HEADS, DK, n), lambda i: (0, 0, 0))
    return pl.pallas_call(
        _ret_proj_sample_body,
        grid=(1,),
        in_specs=[_const((n, D)), _const((1, D)), _const((D, 2 * QK + 2 * VW))] + [_const((1, DK))] * 4,
        out_specs=[tr, tr, row, row, row],
        out_shape=[jax.ShapeDtypeStruct((HEADS, DK, n), F32)] * 2 + [jax.ShapeDtypeStruct((n, VW), F32)] * 3,
        compiler_params=_params(40, 1),
        name="ret_proj_sample",
    )(x2d, g.reshape(1, D), winb, *tabs)


def _mlp_ret_body(xp_ref, xs_ref, g_ref, fg_ref, w1_hbm, w2_hbm,
                  qt_ref, kt_ref, v_ref, ov_ref, gate_ref, gng_ref, gnb_ref, rwout_ref, s_hbm,
                  op_ref, os_ref, snew_hbm,
                  w1b, w2b, stg1, stg2, sem1, sem2, s_in, s_out, sem_in, sem_out, o_scr,
                  *, li, n_p, ns, gamma):
    i = pl.program_id(0)
    spb = ns // n_p
    prime, fetch_w1, fetch_w2 = _weight_stream(w1_hbm, w2_hbm, li, w1b, w2b, stg1, stg2, sem1, sem2)

    def load(c, slot):
        return pltpu.make_async_copy(s_hbm.at[c], s_in.at[slot], sem_in.at[slot])

    def store(c, slot):
        return pltpu.make_async_copy(s_out.at[slot], snew_hbm.at[c], sem_out.at[slot])

    def state_update(j):
        def run():
            c = i * spb + j
            slot = j % 2
            load(c, slot).wait()
            store(c, slot).wait()
            lane = lax.broadcasted_iota(jnp.int32, (DK, ns), 1) == c
            for hd in range(HEADS):
                cols = slice(hd * DV, (hd + 1) * DV)
                s_old = s_in[slot, hd]
                qcol = jnp.sum(jnp.where(lane, qt_ref[hd], 0.0), axis=1, keepdims=True)
                kcol = jnp.sum(jnp.where(lane, kt_ref[hd], 0.0), axis=1, keepdims=True)
                v = v_ref[pl.ds(c, 1), cols]
                o_scr[pl.ds(c, 1), cols] = jnp.sum(s_old * (qcol * gamma[hd]), axis=0, keepdims=True)
                s_out[slot, hd] = s_old * gamma[hd] + kcol * v
            store(c, slot).start()
            load(jnp.minimum(c + 2, ns - 1), slot).start()
        return run

    def states():
        return [state_update(j) for j in range(spb)]

    @pl.when(i == 0)
    def _():
        prime()
        s_out[...] = jnp.zeros_like(s_out)
        for slot in range(2):
            load(slot, slot).start()
            store(slot, slot).start()
        _interleave(_mlp_thunks(lambda: xp_ref[...], _store(op_ref), g_ref, fg_ref, w1b, w2b, False,
                                fetch_w1, fetch_w2), states())

    @pl.when((i > 0) & (i < n_p))
    def _():
        _interleave(_mlp_thunks(lambda: xp_ref[...], _store(op_ref), g_ref, fg_ref, w1b, w2b, False), states())

    @pl.when(i == n_p)
    def _():
        for slot in range(2):
            load(ns - 1, slot).wait()
            store(ns - 1, slot).wait()
        parts = []
        for hd in range(HEADS):
            cols = slice(hd * DV, (hd + 1) * DV)
            o = ov_ref[:, cols] + o_scr[:, cols]
            parts.append(_group_norm_gate(o, gate_ref[:, cols], gng_ref[:, cols], gnb_ref[:, cols]))
        gated = jnp.concatenate(parts, axis=-1).astype(BF16)
        xs1 = xs_ref[...] + _dot(gated, rwout_ref[...])
        _run(_mlp_thunks(lambda: xs1, _store(os_ref), g_ref, fg_ref, w1b, w2b, False))


def _mlp_ret(xp2d, xs2d, g, w1, w2, fg, qt, kt, v, ov, gate, gn_g, gn_b, rwoutb, s, *, li, tm=512):
    n, ns = xp2d.shape[0], xs2d.shape[0]
    n_p = n // tm
    _, _, _, chunk_dec = _ret_constants(1)
    row = pl.BlockSpec((tm, D), lambda i: (jnp.minimum(i, n_p - 1), 0))
    srow = pl.BlockSpec((ns, D), lambda i: (0, 0))
    hbm = pl.BlockSpec(memory_space=pl.ANY)
    wide = _const((ns, VW))
    return pl.pallas_call(
        functools.partial(_mlp_ret_body, li=li, n_p=n_p, ns=ns, gamma=tuple(float(c) for c in chunk_dec)),
        grid=(n_p + 1,),
        in_specs=[row, _const((ns, D)), _const((1, D)), _const((1, D)), hbm, hbm,
                  _const((HEADS, DK, ns)), _const((HEADS, DK, ns)), wide, wide, wide,
                  _const((1, VW)), _const((1, VW)), _const((VW, D)), hbm],
        out_specs=[row, srow, hbm],
        out_shape=[jax.ShapeDtypeStruct((n, D), F32), jax.ShapeDtypeStruct((ns, D), F32),
                   jax.ShapeDtypeStruct(s.shape, F32)],
        scratch_shapes=[pltpu.VMEM((D, D_FF), BF16), pltpu.VMEM((D_FF, D), BF16),
                        pltpu.VMEM((2, D, MLP_FC), F32), pltpu.VMEM((2, MLP_FC, D), F32),
                        pltpu.SemaphoreType.DMA((2,)), pltpu.SemaphoreType.DMA((2,)),
                        pltpu.VMEM((2, HEADS, DK, DV), F32), pltpu.VMEM((2, HEADS, DK, DV), F32),
                        pltpu.SemaphoreType.DMA((2,)), pltpu.SemaphoreType.DMA((2,)),
                        pltpu.VMEM((ns, VW), F32)],
        compiler_params=_params(58, 1),
        name="mlp_ret",
    )(xp2d, xs2d, g.reshape(1, D), fg.reshape(1, D), w1, w2, qt, kt, v, ov, gate,
      gn_g.reshape(1, VW), gn_b.reshape(1, VW), rwoutb, s)


def _lru_gates(xc, wax_ref, ba, bx, lam, heads):
    xcb = xc.astype(BF16)
    rs, is_ = [], []
    for n, hd in enumerate(heads):
        ri = _dot(xcb[:, n * DK:(n + 1) * DK], wax_ref[hd])
        rs.append(ri[:, :DK])
        is_.append(ri[:, DK:])
    r = jax.nn.sigmoid(jnp.concatenate(rs, axis=-1) + ba)
    i = jax.nn.sigmoid(jnp.concatenate(is_, axis=-1) + bx)
    log_a = r * (-LRU_C * _softplus(-lam))
    a = jnp.exp(log_a)
    mult = jnp.sqrt(-jnp.tanh(log_a) * (a * a + 1.0))
    return a, mult, i


LRU_GROUP_HEADS = 4


def _lru_prompt_body(x_ref, g_ref, win_ref, cw_ref, cb_ref, wax_ref, ba_ref, bx_ref, lam_ref, wout_ref,
                     o_ref, conv_ref, hlast_ref, xb_scr, hs_scr, carry_scr, h_scr, *, NB, T):
    t = pl.program_id(0)
    N = NB * T
    PS = T + 1
    halo = (CONV_W - 1) * NB
    gw = LRU_GROUP_HEADS * DK
    gblk = gw // LANES

    @pl.when(t == 0)
    def _():
        carry_scr[...] = jnp.zeros_like(carry_scr)
        h_scr[...] = jnp.zeros_like(h_scr)

    x = x_ref[...].reshape(N, D)
    h = _rms(x, g_ref[...]).astype(BF16)
    row = lax.broadcasted_iota(jnp.int32, (N, 1), 0)
    pos = t * T + lax.shift_right_logical(row, NB.bit_length() - 1)
    out = x
    def project(grp):
        c0, c1 = grp * gw, (grp + 1) * gw
        return _dot(h, win_ref[:, c0:c1]), _dot(h, win_ref[:, D + c0:D + c1])

    ngrp = D // gw
    nxt = project(0)
    for grp in range(ngrp):
        c0, c1 = grp * gw, (grp + 1) * gw
        blks = range(grp * gblk, (grp + 1) * gblk)
        zg, xb = nxt
        if grp + 1 < ngrp:
            nxt = project(grp + 1)
        gate = _gelu(zg)
        for n, j in enumerate(blks):
            for b in range(NB):
                xb_scr[j, b * PS:b * PS + T, :] = xb[b * T:(b + 1) * T, n * LANES:(n + 1) * LANES]
        ext = jnp.concatenate(
            [carry_scr[:, c0:c1]]
            + [jnp.concatenate([xb_scr[j, pl.ds(s, NB, stride=PS), :] for j in blks], axis=-1)
               for s in range(T)], axis=0)
        carry_scr[:, c0:c1] = ext[N:N + halo, :]
        acc = ext[0:N, :] * cw_ref[0:1, c0:c1]
        for j in range(1, CONV_W):
            acc = acc + ext[j * NB:j * NB + N, :] * cw_ref[j:j + 1, c0:c1]
        xc = cb_ref[:, c0:c1] + acc
        heads = range(grp * LRU_GROUP_HEADS, (grp + 1) * LRU_GROUP_HEADS)
        a, mult, i = _lru_gates(xc, wax_ref, ba_ref[:, c0:c1], bx_ref[:, c0:c1], lam_ref[:, c0:c1], heads)
        bvec = jnp.where(pos == 0, 1.0, mult) * (i * xc)
        hcur = h_scr[:, c0:c1]
        for s in range(T):
            hcur = a[s * NB:(s + 1) * NB, :] * hcur + bvec[s * NB:(s + 1) * NB, :]
            for n, j in enumerate(blks):
                hs_scr[j, pl.ds(s, NB, stride=PS), :] = hcur[:, n * LANES:(n + 1) * LANES]
        h_scr[:, c0:c1] = hcur
        hs = jnp.concatenate(
            [jnp.concatenate([hs_scr[j, b * PS:b * PS + T, :] for b in range(NB)], axis=0) for j in blks],
            axis=-1)
        out = out + _dot((hs * gate).astype(BF16), wout_ref[c0:c1, :])
    o_ref[...] = out.reshape(NB, T, D)
    hlast_ref[...] = h_scr[...]

    @pl.when(t == pl.num_programs(0) - 1)
    def _():
        for j in range(CONV_W - 1):
            conv_ref[:, j, :] = carry_scr[j * NB:(j + 1) * NB, :]


def _lru_prompt(x, g, winb, conv_w, conv_b, waxb, b_a, b_x, lam, woutb, *, T=128):
    B, L, _ = x.shape
    blk = pl.BlockSpec((B, T, D), lambda t: (0, t, 0))
    vec = _const((1, D))
    return pl.pallas_call(
        functools.partial(_lru_prompt_body, NB=B, T=T),
        grid=(L // T,),
        in_specs=[blk, vec, _const((D, 2 * D)), _const((CONV_W, D)), vec, _const((HEADS, DK, 2 * DK)),
                  vec, vec, vec, _const((D, D))],
        out_specs=[blk, pl.BlockSpec((B, CONV_W - 1, D), lambda t: (0, 0, 0)),
                   pl.BlockSpec((B, D), lambda t: (0, 0))],
        out_shape=[jax.ShapeDtypeStruct((B, L, D), F32), jax.ShapeDtypeStruct((B, CONV_W - 1, D), F32),
                   jax.ShapeDtypeStruct((B, D), F32)],
        scratch_shapes=[pltpu.VMEM((D // LANES, B * (T + 1), LANES), F32)] * 2
        + [pltpu.VMEM(((CONV_W - 1) * B, D), F32), pltpu.VMEM((B, D), F32)],
        compiler_params=_params(48, 1),
        name="lru_prompt",
    )(x, g.reshape(1, D), winb, conv_w, conv_b.reshape(1, D), waxb, b_a.reshape(1, D), b_x.reshape(1, D),
      lam.reshape(1, D), woutb)


def _lru_sample_body(x_ref, cbuf_ref, h0_ref, g_ref, win_ref, cw_ref, cb_ref, wax_ref, ba_ref, bx_ref,
                     lam_ref, wout_ref, o_ref, nconv_ref, hnew_ref, *, pos0):
    x = x_ref[...]
    h = _rms(x, g_ref[...]).astype(BF16)
    z = _dot(h, win_ref[...])
    gate = _gelu(z[:, :D])
    xb = z[:, D:]
    cw = cw_ref[...]
    acc = cbuf_ref[0] * cw[0:1, :]
    for j in range(1, CONV_W - 1):
        acc = acc + cbuf_ref[j] * cw[j:j + 1, :]
    acc = acc + xb * cw[CONV_W - 1:CONV_W, :]
    xc = cb_ref[...] + acc
    a, mult, i = _lru_gates(xc, wax_ref, ba_ref[...], bx_ref[...], lam_ref[...], range(HEADS))
    if pos0 == 0:
        mult = jnp.ones_like(mult)
    hnew = a * h0_ref[...] + mult * (i * xc)
    hnew_ref[...] = hnew
    nconv_ref[0:CONV_W - 2] = cbuf_ref[1:CONV_W - 1]
    nconv_ref[CONV_W - 2] = xb
    o_ref[...] = x + _dot((hnew * gate).astype(BF16), wout_ref[...])


def _lru_sample(x2d, cbuf, h0, g, winb, conv_w, conv_b, waxb, b_a, b_x, lam, woutb, *, pos0):
    n = x2d.shape[0]
    vec = _const((1, D))
    cshape = (CONV_W - 1, n, D)
    out = pl.BlockSpec((n, D), lambda i: (0, 0))
    return pl.pallas_call(
        functools.partial(_lru_sample_body, pos0=pos0),
        grid=(1,),
        in_specs=[_const((n, D)), _const(cshape), _const((n, D)), vec, _const((D, 2 * D)), _const((CONV_W, D)),
                  vec, _const((HEADS, DK, 2 * DK)), vec, vec, vec, _const((D, D))],
        out_specs=[out, pl.BlockSpec(cshape, lambda i: (0, 0, 0)), out],
        out_shape=[jax.ShapeDtypeStruct((n, D), F32), jax.ShapeDtypeStruct(cshape, F32),
                   jax.ShapeDtypeStruct((n, D), F32)],
        compiler_params=_params(32, 1),
        name="lru_sample",
    )(x2d, cbuf, h0, g.reshape(1, D), winb, conv_w, conv_b.reshape(1, D), waxb, b_a.reshape(1, D),
      b_x.reshape(1, D), lam.reshape(1, D), woutb)


def kernel(x_prompt, x_sample, state_pool, state_ret, state_conv, state_lru, pool_norm, pool_w, pool_scale, gm_norm, gm_w_in, gm_b_in, gm_ln_g, gm_ln_b, gm_w_s, gm_b_s, gm_w_out, ret_norm, ret_w_in, ret_gn_g, ret_gn_b, ret_w_out, lru_norm, lru_w_in, lru_conv_w, lru_conv_b, lru_w_a, lru_b_a, lru_w_x, lru_b_x, lru_lam, lru_w_out, mlp_norm, mlp_w1, mlp_w2, final_norm):
    B, L, _ = x_prompt.shape
    NS = x_sample.shape[0]
    bf = lambda w: w.astype(BF16)

    def mlp(xp, xs, li, final=False):
        yp, ys = _mlp(xp.reshape(B * L, D), xs, mlp_norm[li], mlp_w1, mlp_w2, final_norm, li=li, final=final)
        return yp.reshape(B, L, D), ys

    pool_wb = bf(pool_w[0])
    xs, pool_s = _pool_sample(x_sample.reshape(NS, D), jnp.swapaxes(state_pool[0], 0, 1),
                              pool_norm[0], pool_wb, pool_scale[0], pos0=PAST_LEN)
    xp, xs, pool_p = _pool_mlp(x_prompt, xs, pool_norm[0], pool_wb, pool_scale[0], mlp_norm[0],
                               mlp_w1, mlp_w2, final_norm, li=0)

    gm_winb, gm_woutb = bf(gm_w_in[0]), bf(gm_w_out[0])
    bs_full = jnp.repeat(gm_b_s[0].T, CHUNK, axis=1)
    xp = _gmlp_prompt(xp, gm_norm[0], gm_winb, gm_b_in[0], gm_ln_g[0], gm_ln_b[0], gm_w_s[0], bs_full, gm_woutb)
    sw = jnp.repeat(gm_w_s[0][:, 0, 0], CHUNK).reshape(1, D)
    sb = jnp.repeat(gm_b_s[0][:, 0], CHUNK).reshape(1, D)
    xs, v_s = _gmlp_sample(xs, gm_norm[0], gm_winb, gm_b_in[0], gm_ln_g[0], gm_ln_b[0], sw, sb, gm_woutb)
    xp, xs = mlp(xp, xs, 1)

    ret_winb, ret_woutb = bf(ret_w_in[0]), bf(ret_w_out[0])
    xp, ret_p = _ret_prompt(xp, ret_norm[0], ret_winb, ret_gn_g[0], ret_gn_b[0], ret_woutb)
    qt, kt, v, gate, ov = _ret_proj_sample(xs, ret_norm[0], ret_winb, pos0=PAST_LEN)
    xp, xs, ret_s = _mlp_ret(xp.reshape(B * L, D), xs, mlp_norm[2], mlp_w1, mlp_w2, final_norm,
                             qt, kt, v, ov, gate, ret_gn_g[0], ret_gn_b[0], ret_woutb, state_ret[0], li=2)
    xp = xp.reshape(B, L, D)

    lru_winb, lru_woutb = bf(lru_w_in[0]), bf(lru_w_out[0])
    waxb = bf(jnp.concatenate([lru_w_a[0], lru_w_x[0]], axis=-1))
    xp, conv_p, lru_p = _lru_prompt(xp, lru_norm[0], lru_winb, lru_conv_w[0], lru_conv_b[0], waxb,
                                    lru_b_a[0], lru_b_x[0], lru_lam[0], lru_woutb)
    xs, conv_s, lru_s = _lru_sample(xs, jnp.swapaxes(state_conv[0], 0, 1), state_lru[0],
                                    lru_norm[0], lru_winb, lru_conv_w[0], lru_conv_b[0], waxb,
                                    lru_b_a[0], lru_b_x[0], lru_lam[0], lru_woutb, pos0=PAST_LEN)
    yp, ys = mlp(xp, xs, 3, final=True)

    return (yp, ys.reshape(NS, 1, D),
            pool_p[None], jnp.swapaxes(pool_s, 0, 1)[None],
            v_s.reshape(1, NS, 1, D),
            ret_p[None], ret_s[None],
            conv_p[None], jnp.swapaxes(conv_s, 0, 1)[None],
            lru_p[None], lru_s[None])
```

```python
import functools
import math

import jax
import jax.numpy as jnp
import numpy as np
from jax import lax
from jax.experimental import pallas as pl
from jax.experimental.pallas import tpu as pltpu

F32 = jnp.float32
BF16 = jnp.bfloat16

D = 1024
EPS = 1e-6
GN_EPS = 1e-5
PAST_LEN = 16384
POOL_WINDOWS = (2, 4, 8, 16)
POOL_GROUP = D // len(POOL_WINDOWS)
POOL_BUF = max(POOL_WINDOWS) - 1
CHUNK = 128
GM_GROUPS = 8
HEADS = 8
DK = D // HEADS
DV = 2 * D // HEADS
QK = HEADS * DK
VW = HEADS * DV
ROPE_BASE = 10000.0
CONV_W = 4
LRU_C = 8.0
D_FF = 4 * D

MIB = 1024 * 1024
SUBLANES = 8
LANES = 128


def _params(vmem_mib, n_grid):
    return pltpu.CompilerParams(
        dimension_semantics=("arbitrary",) * n_grid,
        vmem_limit_bytes=vmem_mib * MIB,
    )


def _const(shape):
    zeros = (0,) * len(shape)
    return pl.BlockSpec(shape, lambda *_: zeros, pipeline_mode=pl.Buffered(1))


def _rms(x, g):
    ms = jnp.mean(x * x, axis=-1, keepdims=True)
    return x * lax.rsqrt(ms + EPS) * g


def _dot(a, b):
    return jnp.dot(a, b, preferred_element_type=F32)


def _interleave(*queues):
    pos = [0] * len(queues)
    while any(p < len(q) for p, q in zip(pos, queues)):
        _, i = min((pos[i] / len(q), i) for i, q in enumerate(queues) if pos[i] < len(q))
        queues[i][pos[i]]()
        pos[i] += 1


def _softplus(x):
    return jnp.maximum(x, 0.0) + jnp.log1p(jnp.exp(-jnp.abs(x)))


GELU_K = math.sqrt(2.0 / math.pi)


def _gelu(x):
    half = 0.5 * x
    return half + half * jnp.tanh(x * (GELU_K + (GELU_K * 0.044715) * (x * x)))


MLP_FC = 512
W_FC = 256


def _mlp_thunks(read_x, write_out, g_ref, fg_ref, w1b, w2b, final, fetch_w1=None, fetch_w2=None):
    st = {}

    def head():
        x = read_x()
        st["h"] = _rms(x, g_ref[...]).astype(BF16)
        st["acc"] = x

    def up(c):
        def run():
            if fetch_w1 is not None:
                fetch_w1(c)
            a = _dot(st["h"], w1b[:, c * MLP_FC:(c + 1) * MLP_FC])
            st["a"] = jnp.square(jnp.maximum(a, 0.0)).astype(BF16)
        return run

    def down(c):
        def run():
            if fetch_w2 is not None:
                fetch_w2(c)
            st["acc"] = st["acc"] + _dot(st["a"], w2b[c * MLP_FC:(c + 1) * MLP_FC, :])
        return run

    def tail():
        acc = st["acc"]
        write_out(_rms(acc, fg_ref[...]) if final else acc)

    return [head] + [f(c) for c in range(D_FF // MLP_FC) for f in (up, down)] + [tail]


def _run(thunks):
    for thunk in thunks:
        thunk()


def _store(ref):
    def write(v):
        ref[...] = v
    return write


def _weight_stream(w1_hbm, w2_hbm, li, w1b, w2b, stg1, stg2, sem1, sem2):
    npc = D_FF // W_FC
    per = MLP_FC // W_FC

    def w1_copy(p):
        return pltpu.make_async_copy(w1_hbm.at[li, :, pl.ds(p * W_FC, W_FC)], stg1.at[p % 2], sem1.at[p % 2])

    def w2_copy(p):
        return pltpu.make_async_copy(w2_hbm.at[li, pl.ds(p * W_FC, W_FC), :], stg2.at[p % 2], sem2.at[p % 2])

    def prime():
        for p in range(min(2, npc)):
            w1_copy(p).start()
            w2_copy(p).start()

    def fetch_w1(c):
        for p in range(c * per, (c + 1) * per):
            w1_copy(p).wait()
            w1b[:, p * W_FC:(p + 1) * W_FC] = stg1[p % 2].astype(BF16)
            if p + 2 < npc:
                w1_copy(p + 2).start()

    def fetch_w2(c):
        for p in range(c * per, (c + 1) * per):
            w2_copy(p).wait()
            w2b[p * W_FC:(p + 1) * W_FC, :] = stg2[p % 2].astype(BF16)
            if p + 2 < npc:
                w2_copy(p + 2).start()

    return prime, fetch_w1, fetch_w2


def _mlp_body(xp_ref, xs_ref, g_ref, fg_ref, w1_hbm, w2_hbm, op_ref, os_ref,
              w1b, w2b, stg1, stg2, sem1, sem2, *, li, final, n_p):
    i = pl.program_id(0)
    prime, fetch_w1, fetch_w2 = _weight_stream(w1_hbm, w2_hbm, li, w1b, w2b, stg1, stg2, sem1, sem2)

    @pl.when(i == 0)
    def _():
        prime()
        _run(_mlp_thunks(lambda: xp_ref[...], _store(op_ref), g_ref, fg_ref, w1b, w2b, final,
                         fetch_w1, fetch_w2))

    @pl.when((i > 0) & (i < n_p))
    def _():
        _run(_mlp_thunks(lambda: xp_ref[...], _store(op_ref), g_ref, fg_ref, w1b, w2b, final))

    @pl.when(i == n_p)
    def _():
        _run(_mlp_thunks(lambda: xs_ref[...], _store(os_ref), g_ref, fg_ref, w1b, w2b, final))


def _mlp(xp2d, xs2d, g, w1, w2, fg, *, li, final, tm=512):
    n, ns = xp2d.shape[0], xs2d.shape[0]
    n_p = n // tm
    row = pl.BlockSpec((tm, D), lambda i: (jnp.minimum(i, n_p - 1), 0))
    srow = pl.BlockSpec((ns, D), lambda i: (0, 0))
    hbm = pl.BlockSpec(memory_space=pl.ANY)
    return pl.pallas_call(
        functools.partial(_mlp_body, li=li, final=final, n_p=n_p),
        grid=(n_p + 1,),
        in_specs=[row, _const((ns, D)), _const((1, D)), _const((1, D)), hbm, hbm],
        out_specs=[row, srow],
        out_shape=[jax.ShapeDtypeStruct((n, D), F32), jax.ShapeDtypeStruct((ns, D), F32)],
        scratch_shapes=[pltpu.VMEM((D, D_FF), BF16), pltpu.VMEM((D_FF, D), BF16),
                        pltpu.VMEM((2, D, W_FC), F32), pltpu.VMEM((2, W_FC, D), F32),
                        pltpu.SemaphoreType.DMA((2,)), pltpu.SemaphoreType.DMA((2,))],
        compiler_params=_params(48, 1),
        name="mlp",
    )(xp2d, xs2d, g.reshape(1, D), fg.reshape(1, D), w1, w2)


POOL_HALO = 32


def _pool_thunks(x_ref, g_ref, w_ref, sc_ref, write_out, buf_ref, ext_ref, *, T, t):
    st = {}

    def head():
        x = x_ref[...]
        h = _rms(x, g_ref[...])
        halo = jnp.where(t == 0, 0.0, ext_ref[...])
        st["x"], st["h"] = x, h
        st["ext"] = jnp.concatenate([halo, h], axis=0)
        st["pos"] = t * T + lax.broadcasted_iota(jnp.int32, (T, 1), 0)
        st["y"] = [None] * len(POOL_WINDOWS)

    def group(gi, w):
        def run():
            c0, c1 = gi * POOL_GROUP, (gi + 1) * POOL_GROUP
            levels = w.bit_length() - 1
            a = st["ext"][POOL_HALO - SUBLANES * levels:, c0:c1]
            for k in range(levels):
                shift = 1 << k
                n = a.shape[0] - SUBLANES
                a = a[SUBLANES:, :] + a[SUBLANES - shift:SUBLANES - shift + n, :]
            hg = st["h"][:, c0:c1]
            cnt = jnp.minimum(st["pos"] + 1, w).astype(F32)
            d = a / cnt - hg
            st["y"][gi] = _dot(d.astype(BF16), w_ref[gi])
        return run

    def tail():
        y = jnp.concatenate(st["y"], axis=-1) * sc_ref[...]
        write_out(st["x"] + y)
        ext_ref[...] = st["h"][T - POOL_HALO:, :]
        buf_ref[...] = st["h"][T - POOL_BUF:, :]

    return [head] + [group(gi, w) for gi, w in enumerate(POOL_WINDOWS)] + [tail]


def _pool_mlp_body(x_ref, xs_ref, pg_ref, pw_ref, psc_ref, g_ref, fg_ref, w1_hbm, w2_hbm,
                   o_ref, os_ref, buf_ref, ext_ref, x1_scr, w1b, w2b, stg1, stg2, sem1, sem2,
                   *, T, tiles_per_seq, n, li):
    s = pl.program_id(0)
    t = lax.rem(s, tiles_per_seq)
    slot = lax.rem(s, 2)
    prime, fetch_w1, fetch_w2 = _weight_stream(w1_hbm, w2_hbm, li, w1b, w2b, stg1, stg2, sem1, sem2)

    def write_x1(v):
        x1_scr[slot] = v

    def mixer():
        return _pool_thunks(x_ref, pg_ref, pw_ref, psc_ref, write_x1, buf_ref, ext_ref, T=T, t=t)

    def mlp(*fetch):
        return _mlp_thunks(lambda: x1_scr[1 - slot], _store(o_ref), g_ref, fg_ref, w1b, w2b, False, *fetch)

    @pl.when(s == 0)
    def _():
        ext_ref[0:POOL_HALO, :] = jnp.zeros((POOL_HALO, D), F32)
        prime()
        _run(mixer())

    @pl.when(s == 1)
    def _():
        _interleave(mlp(fetch_w1, fetch_w2), mixer())

    @pl.when((s > 1) & (s < n))
    def _():
        _interleave(mlp(), mixer())

    @pl.when(s == n)
    def _():
        _run(mlp())
        _run(_mlp_thunks(lambda: xs_ref[...], _store(os_ref), g_ref, fg_ref, w1b, w2b, False))


def _pool_mlp(x, xs2d, pg, pwb, psc, g, w1, w2, fg, *, li, T=512):
    B, L, _ = x.shape
    ns = xs2d.shape[0]
    tps = L // T
    n = B * tps
    tile = lambda s: (jnp.minimum(s, n - 1) // tps, lax.rem(jnp.minimum(s, n - 1), tps), 0)
    prev = lambda s: (jnp.maximum(s - 1, 0) // tps, lax.rem(jnp.maximum(s - 1, 0), tps), 0)
    hbm = pl.BlockSpec(memory_space=pl.ANY)
    return pl.pallas_call(
        functools.partial(_pool_mlp_body, T=T, tiles_per_seq=tps, n=n, li=li),
        grid=(n + 1,),
        in_specs=[pl.BlockSpec((None, T, D), tile), _const((ns, D)), _const((1, D)),
                  _const((len(POOL_WINDOWS), POOL_GROUP, POOL_GROUP)), _const((1, D)),
                  _const((1, D)), _const((1, D)), hbm, hbm],
        out_specs=[pl.BlockSpec((None, T, D), prev), pl.BlockSpec((ns, D), lambda s: (0, 0)),
                   pl.BlockSpec((None, POOL_BUF, D), lambda s: (jnp.minimum(s, n - 1) // tps, 0, 0))],
        out_shape=[jax.ShapeDtypeStruct((B, L, D), F32), jax.ShapeDtypeStruct((ns, D), F32),
                   jax.ShapeDtypeStruct((B, POOL_BUF, D), F32)],
        scratch_shapes=[pltpu.VMEM((POOL_HALO, D), F32), pltpu.VMEM((2, T, D), F32),
                        pltpu.VMEM((D, D_FF), BF16), pltpu.VMEM((D_FF, D), BF16),
                        pltpu.VMEM((2, D, W_FC), F32), pltpu.VMEM((2, W_FC, D), F32),
                        pltpu.SemaphoreType.DMA((2,)), pltpu.SemaphoreType.DMA((2,))],
        compiler_params=_params(56, 1),
        name="pool_mlp",
    )(x, xs2d, pg.reshape(1, D), pwb, psc.reshape(1, D), g.reshape(1, D), fg.reshape(1, D), w1, w2)


def _pool_sample_body(x_ref, buf_ref, g_ref, w_ref, sc_ref, o_ref, nbuf_ref, *, pos0):
    x = x_ref[...]
    h = _rms(x, g_ref[...])
    ys = []
    for gi, w in enumerate(POOL_WINDOWS):
        c0, c1 = gi * POOL_GROUP, (gi + 1) * POOL_GROUP
        hg = h[:, c0:c1]
        s = hg
        for k in range(1, w):
            s = s + buf_ref[POOL_BUF - k, :, c0:c1]
        cnt = float(min(pos0 + 1, w))
        d = s / cnt - hg
        ys.append(_dot(d.astype(BF16), w_ref[gi]))
    y = jnp.concatenate(ys, axis=-1) * sc_ref[...]
    o_ref[...] = x + y
    nbuf_ref[0:POOL_BUF - 1] = buf_ref[1:POOL_BUF]
    nbuf_ref[POOL_BUF - 1] = h


def _pool_sample(x2d, buf, g, wb, sc, *, pos0, bt=32):
    n = x2d.shape[0]
    row = pl.BlockSpec((bt, D), lambda i: (i, 0))
    brow = pl.BlockSpec((POOL_BUF, bt, D), lambda i: (0, i, 0))
    return pl.pallas_call(
        functools.partial(_pool_sample_body, pos0=pos0),
        grid=(n // bt,),
        in_specs=[row, brow, _const((1, D)), _const((len(POOL_WINDOWS), POOL_GROUP, POOL_GROUP)), _const((1, D))],
        out_specs=[row, brow],
        out_shape=[jax.ShapeDtypeStruct((n, D), F32), jax.ShapeDtypeStruct((POOL_BUF, n, D), F32)],
        compiler_params=_params(32, 1),
        name="pool_sample",
    )(x2d, buf, g.reshape(1, D), wb, sc.reshape(1, D))


def _gmlp_front(x, g_ref, win_ref, bin_ref, lng_ref, lnb_ref):
    h = _rms(x, g_ref[...]).astype(BF16)
    z = _gelu(_dot(h, win_ref[...]) + bin_ref[...])
    u, v = z[:, :D], z[:, D:]
    mu = jnp.mean(v, axis=-1, keepdims=True)
    vc = v - mu
    var = jnp.mean(jnp.square(vc), axis=-1, keepdims=True)
    vn = vc * lax.rsqrt(var + EPS) * lng_ref[...] + lnb_ref[...]
    return u, vn


def _gmlp_prompt_body(x_ref, g_ref, win_ref, bin_ref, lng_ref, lnb_ref, ws_ref, bs_ref, wout_ref,
                      o_ref, *, T):
    ti = lax.broadcasted_iota(jnp.int32, (CHUNK, CHUNK), 0)
    si = lax.broadcasted_iota(jnp.int32, (CHUNK, CHUNK), 1)
    causal = ti >= si
    wsm = [jnp.where(causal, ws_ref[gi], 0.0).astype(BF16) for gi in range(GM_GROUPS)]
    x = x_ref[...]
    u, vn = _gmlp_front(x, g_ref, win_ref, bin_ref, lng_ref, lnb_ref)
    vnb = vn.astype(BF16)
    rows = []
    for n in range(T // CHUNK):
        r0, r1 = n * CHUNK, (n + 1) * CHUNK
        cols = [_dot(wsm[gi], vnb[r0:r1, gi * CHUNK:(gi + 1) * CHUNK]) for gi in range(GM_GROUPS)]
        rows.append(jnp.concatenate(cols, axis=-1) + bs_ref[...])
    mixed = jnp.concatenate(rows, axis=0)
    y = _dot((u * mixed).astype(BF16), wout_ref[...])
    o_ref[...] = x + y


def _gmlp_prompt(x, g, winb, b_in, ln_g, ln_b, w_s, bs_full, woutb, *, T=1024):
    B, L, _ = x.shape
    blk = pl.BlockSpec((None, T, D), lambda b, t: (b, t, 0))
    return pl.pallas_call(
        functools.partial(_gmlp_prompt_body, T=T),
        grid=(B, L // T),
        in_specs=[blk, _const((1, D)), _const((D, 2 * D)), _const((1, 2 * D)), _const((1, D)), _const((1, D)),
                  _const((GM_GROUPS, CHUNK, CHUNK)), _const((CHUNK, D)), _const((D, D))],
        out_specs=blk,
        out_shape=jax.ShapeDtypeStruct((B, L, D), F32),
        compiler_params=_params(48, 2),
        name="gmlp_prompt",
    )(x, g.reshape(1, D), winb, b_in.reshape(1, 2 * D), ln_g.reshape(1, D), ln_b.reshape(1, D),
      w_s, bs_full, woutb)


GM_PIECE = 512


def _gmlp_thunks(x_ref, g_ref, win_ref, bin_ref, lng_ref, lnb_ref, ws_ref, bs_ref, wout_ref, write_out, *, T):
    st = {"z": [], "mixed": []}

    def head():
        x = x_ref[...]
        st["x"] = x
        st["h"] = _rms(x, g_ref[...]).astype(BF16)
        ti = lax.broadcasted_iota(jnp.int32, (CHUNK, CHUNK), 0)
        si = lax.broadcasted_iota(jnp.int32, (CHUNK, CHUNK), 1)
        st["wsm"] = [jnp.where(ti >= si, ws_ref[gi], 0.0).astype(BF16) for gi in range(GM_GROUPS)]

    def project(j):
        def run():
            cols = slice(j * GM_PIECE, (j + 1) * GM_PIECE)
            st["z"].append(_gelu(_dot(st["h"], win_ref[:, cols]) + bin_ref[:, cols]))
        return run

    def normalise():
        z = jnp.concatenate(st["z"], axis=-1)
        v = z[:, D:]
        mu = jnp.mean(v, axis=-1, keepdims=True)
        vc = v - mu
        var = jnp.mean(jnp.square(vc), axis=-1, keepdims=True)
        st["u"] = z[:, :D]
        st["vnb"] = (vc * lax.rsqrt(var + EPS) * lng_ref[...] + lnb_ref[...]).astype(BF16)

    def spatial(n):
        def run():
            rows = slice(n * CHUNK, (n + 1) * CHUNK)
            cols = [_dot(st["wsm"][gi], st["vnb"][rows, gi * CHUNK:(gi + 1) * CHUNK]) for gi in range(GM_GROUPS)]
            st["mixed"].append(jnp.concatenate(cols, axis=-1) + bs_ref[...])
        return run

    def tail():
        mixed = jnp.concatenate(st["mixed"], axis=0)
        write_out(st["x"] + _dot((st["u"] * mixed).astype(BF16), wout_ref[...]))

    return ([head] + [project(j) for j in range(2 * D // GM_PIECE)] + [normalise]
            + [spatial(n) for n in range(T // CHUNK)] + [tail])


def _gmlp_mlp_body(x_ref, xs_ref, mg_ref, win_ref, bin_ref, lng_ref, lnb_ref, ws_ref, bs_ref, wout_ref,
                   g_ref, fg_ref, w1_hbm, w2_hbm, o_ref, os_ref,
                   x1_scr, w1b, w2b, stg1, stg2, sem1, sem2, *, T, n, li):
    s = pl.program_id(0)
    slot = lax.rem(s, 2)
    prime, fetch_w1, fetch_w2 = _weight_stream(w1_hbm, w2_hbm, li, w1b, w2b, stg1, stg2, sem1, sem2)

    def write_x1(v):
        x1_scr[slot] = v

    def mixer():
        return _gmlp_thunks(x_ref, mg_ref, win_ref, bin_ref, lng_ref, lnb_ref, ws_ref, bs_ref, wout_ref,
                            write_x1, T=T)

    def mlp(*fetch):
        return _mlp_thunks(lambda: x1_scr[1 - slot], _store(o_ref), g_ref, fg_ref, w1b, w2b, False, *fetch)

    @pl.when(s == 0)
    def _():
        prime()
        _run(mixer())

    @pl.when(s == 1)
    def _():
        _interleave(mlp(fetch_w1, fetch_w2), mixer())

    @pl.when((s > 1) & (s < n))
    def _():
        _interleave(mlp(), mixer())

    @pl.when(s == n)
    def _():
        _run(mlp())
        _run(_mlp_thunks(lambda: xs_ref[...], _store(os_ref), g_ref, fg_ref, w1b, w2b, False))


def _gmlp_mlp(x, xs2d, mg, winb, b_in, ln_g, ln_b, w_s, bs_full, woutb, g, w1, w2, fg, *, li, T=512):
    B, L, _ = x.shape
    ns = xs2d.shape[0]
    tps = L // T
    n = B * tps
    tile = lambda s: (jnp.minimum(s, n - 1) // tps, lax.rem(jnp.minimum(s, n - 1), tps), 0)
    prev = lambda s: (jnp.maximum(s - 1, 0) // tps, lax.rem(jnp.maximum(s - 1, 0), tps), 0)
    hbm = pl.BlockSpec(memory_space=pl.ANY)
    vec = _const((1, D))
    return pl.pallas_call(
        functools.partial(_gmlp_mlp_body, T=T, n=n, li=li),
        grid=(n + 1,),
        in_specs=[pl.BlockSpec((None, T, D), tile), _const((ns, D)), vec, _const((D, 2 * D)), _const((1, 2 * D)),
                  vec, vec, _const((GM_GROUPS, CHUNK, CHUNK)), _const((CHUNK, D)), _const((D, D)),
                  vec, vec, hbm, hbm],
        out_specs=[pl.BlockSpec((None, T, D), prev), pl.BlockSpec((ns, D), lambda s: (0, 0))],
        out_shape=[jax.ShapeDtypeStruct((B, L, D), F32), jax.ShapeDtypeStruct((ns, D), F32)],
        scratch_shapes=[pltpu.VMEM((2, T, D), F32),
                        pltpu.VMEM((D, D_FF), BF16), pltpu.VMEM((D_FF, D), BF16),
                        pltpu.VMEM((2, D, W_FC), F32), pltpu.VMEM((2, W_FC, D), F32),
                        pltpu.SemaphoreType.DMA((2,)), pltpu.SemaphoreType.DMA((2,))],
        compiler_params=_params(58, 1),
        name="gmlp_mlp",
    )(x, xs2d, mg.reshape(1, D), winb, b_in.reshape(1, 2 * D), ln_g.reshape(1, D), ln_b.reshape(1, D),
      w_s, bs_full, woutb, g.reshape(1, D), fg.reshape(1, D), w1, w2)


def _gmlp_sample_body(x_ref, g_ref, win_ref, bin_ref, lng_ref, lnb_ref, sw_ref, sb_ref, wout_ref,
                      o_ref, vn_ref):
    x = x_ref[...]
    u, vn = _gmlp_front(x, g_ref, win_ref, bin_ref, lng_ref, lnb_ref)
    mixed = vn * sw_ref[...] + sb_ref[...]
    y = _dot((u * mixed).astype(BF16), wout_ref[...])
    o_ref[...] = x + y
    vn_ref[...] = vn


def _gmlp_sample(x2d, g, winb, b_in, ln_g, ln_b, sw, sb, woutb):
    n = x2d.shape[0]
    full = _const((n, D))
    return pl.pallas_call(
        _gmlp_sample_body,
        grid=(1,),
        in_specs=[full, _const((1, D)), _const((D, 2 * D)), _const((1, 2 * D)), _const((1, D)), _const((1, D)),
                  _const((1, D)), _const((1, D)), _const((D, D))],
        out_specs=[pl.BlockSpec((n, D), lambda i: (0, 0)), pl.BlockSpec((n, D), lambda i: (0, 0))],
        out_shape=[jax.ShapeDtypeStruct((n, D), F32), jax.ShapeDtypeStruct((n, D), F32)],
        compiler_params=_params(32, 1),
        name="gmlp_sample",
    )(x2d, g.reshape(1, D), winb, b_in.reshape(1, 2 * D), ln_g.reshape(1, D), ln_b.reshape(1, D),
      sw, sb, woutb)


def _ret_constants(C):
    log_gamma = np.log1p(-np.exp2(-5.0 - np.arange(HEADS, dtype=np.float64)))
    idx = np.arange(C, dtype=np.float64)
    diff = idx[:, None] - idx[None, :]
    dmask = np.where(diff[None] >= 0, np.exp(log_gamma[:, None, None] * np.maximum(diff, 0.0)[None]), 0.0)
    q_dec = np.exp(log_gamma[:, None] * (idx + 1.0))
    k_dec = np.exp(log_gamma[:, None] * (C - 1.0 - idx))
    chunk_dec = np.exp(log_gamma * C)
    return dmask.astype(np.float32), q_dec.astype(np.float32), k_dec.astype(np.float32), chunk_dec


def _rope_tables(pos):
    half = DK // 2
    freqs = np.exp(-math.log(ROPE_BASE) * np.arange(half, dtype=np.float64) / half)
    ang = np.asarray(pos, dtype=np.float64)[:, None] * freqs[None]
    cos = np.concatenate([np.cos(ang), np.cos(ang)], axis=-1)
    sin = np.concatenate([-np.sin(ang), np.sin(ang)], axis=-1)
    return cos, sin


def _rope(t, cos, sin):
    return t * cos + pltpu.roll(t, DK // 2, axis=1) * sin


def _group_norm_gate(o, gate, gng, gnb):
    mu = jnp.mean(o, axis=-1, keepdims=True)
    oc = o - mu
    var = jnp.mean(jnp.square(oc), axis=-1, keepdims=True)
    on = oc * lax.rsqrt(var + GN_EPS) * gng + gnb
    return jax.nn.silu(gate) * on


RET_GROUP_HEADS = 2


def _ret_prompt_body(x_ref, g_ref, win_ref, cq_ref, sq_ref, ck_ref, sk_ref, dm_ref, qd_ref, kd_ref,
                     gng_ref, gnb_ref, wout_ref, o_ref, s_out_ref, gated_scr, s_scr,
                     *, T, chunk_dec):
    t = pl.program_id(1)

    @pl.when(t == 0)
    def _():
        s_scr[...] = jnp.zeros_like(s_scr)

    x = x_ref[...]
    h = _rms(x, g_ref[...]).astype(BF16)
    cq, sq, ck, sk = cq_ref[...], sq_ref[...], ck_ref[...], sk_ref[...]
    gh = RET_GROUP_HEADS
    n_pairs = HEADS // gh

    def projection(pair, dst):
        def piece(name, lo, width):
            def run():
                dst[name] = _dot(h, win_ref[:, lo:lo + width])
            return run
        v0, g0 = 2 * QK + gh * DV * pair, 2 * QK + VW + gh * DV * pair
        return ([piece("q", gh * DK * pair, gh * DK), piece("k", QK + gh * DK * pair, gh * DK)]
                + [piece("v%d" % sub, v0 + sub * DV, DV) for sub in range(gh)]
                + [piece("g%d" % sub, g0 + sub * DV, DV) for sub in range(gh)])

    def chunk_work(pair, src):
        state = {}

        def prep(sub):
            def run():
                q = _rope(src["q"][:, sub * DK:(sub + 1) * DK], cq, sq)
                k = _rope(src["k"][:, sub * DK:(sub + 1) * DK], ck, sk)
                state[sub] = dict(q=q, k=k, qb=q.astype(BF16), kb=k.astype(BF16), s=s_scr[gh * pair + sub])
            return run

        def chunk(sub, c):
            def run():
                hd = gh * pair + sub
                st = state[sub]
                rows = slice(c * CHUNK, (c + 1) * CHUNK)
                vb = src["v%d" % sub][rows, :].astype(BF16)
                scores = lax.dot_general(st["qb"][rows], st["kb"][rows], (((1,), (1,)), ((), ())),
                                         preferred_element_type=F32) * dm_ref[hd]
                o = (_dot(scores.astype(BF16), vb)
                     + _dot((st["q"][rows] * qd_ref[hd]).astype(BF16), st["s"].astype(BF16)))
                kt = (st["k"][rows] * kd_ref[hd]).T.astype(BF16)
                st["s"] = st["s"] * chunk_dec[hd] + _dot(kt, vb)
                gated = _group_norm_gate(o, src["g%d" % sub][rows, :], gng_ref[:, hd * DV:(hd + 1) * DV],
                                         gnb_ref[:, hd * DV:(hd + 1) * DV])
                gated_scr[rows, hd * DV:(hd + 1) * DV] = gated.astype(BF16)
            return run

        def finish(sub):
            def run():
                s_scr[gh * pair + sub] = state[sub]["s"]
            return run

        return ([prep(sub) for sub in range(gh)]
                + [chunk(sub, c) for c in range(T // CHUNK) for sub in range(gh)]
                + [finish(sub) for sub in range(gh)])

    acc = [x]

    def out_projection(pair):
        def run():
            cols = slice(pair * gh * DV, (pair + 1) * gh * DV)
            acc[0] = acc[0] + _dot(gated_scr[:, cols], wout_ref[cols, :])
        return run

    cur = {}
    for run in projection(0, cur):
        run()
    for pair in range(n_pairs):
        nxt = {}
        if pair + 1 < n_pairs:
            matmul_queue = projection(pair + 1, nxt)
        else:
            matmul_queue = [out_projection(p) for p in range(n_pairs - 1)]
        _interleave(matmul_queue, chunk_work(pair, cur))
        cur = nxt
    out_projection(n_pairs - 1)()
    o_ref[...] = acc[0]
    s_out_ref[...] = s_scr[...]


def _ret_prompt(x, g, winb, gn_g, gn_b, woutb, *, T=512):
    B, L, _ = x.shape
    dmask, q_dec, k_dec, chunk_dec = _ret_constants(CHUNK)
    cos, sin = _rope_tables(np.arange(L))
    scale = DK ** -0.5
    cq, sq = jnp.asarray(cos, F32), jnp.asarray(sin, F32)
    ck, sk = jnp.asarray(cos * scale, F32), jnp.asarray(sin * scale, F32)
    qd = jnp.asarray(np.broadcast_to(q_dec[:, :, None], (HEADS, CHUNK, DK)))
    kd = jnp.asarray(np.broadcast_to(k_dec[:, :, None], (HEADS, CHUNK, DK)))
    blk = pl.BlockSpec((None, T, D), lambda b, t: (b, t, 0))
    tab = pl.BlockSpec((T, DK), lambda b, t: (t, 0))
    hcc = _const((HEADS, CHUNK, CHUNK))
    return pl.pallas_call(
        functools.partial(_ret_prompt_body, T=T, chunk_dec=tuple(float(c) for c in chunk_dec)),
        grid=(B, L // T),
        in_specs=[blk, _const((1, D)), _const((D, 2 * QK + 2 * VW)), tab, tab, tab, tab,
                  hcc, hcc, hcc, _const((1, VW)), _const((1, VW)), _const((VW, D))],
        out_specs=[blk, pl.BlockSpec((None, HEADS, DK, DV), lambda b, t: (b, 0, 0, 0))],
        out_shape=[jax.ShapeDtypeStruct((B, L, D), F32), jax.ShapeDtypeStruct((B, HEADS, DK, DV), F32)],
        scratch_shapes=[pltpu.VMEM((T, VW), BF16), pltpu.VMEM((HEADS, DK, DV), F32)],
        compiler_params=_params(56, 2),
        name="ret_prompt",
    )(x, g.reshape(1, D), winb, cq, sq, ck, sk, jnp.asarray(dmask), qd, kd,
      gn_g.reshape(1, VW), gn_b.reshape(1, VW), woutb)


def _ret_proj_sample_body(x_ref, g_ref, win_ref, cq_ref, sq_ref, ck_ref, sk_ref,
                          qt_ref, kt_ref, v_ref, gate_ref, ov_ref):
    h = _rms(x_ref[...], g_ref[...]).astype(BF16)
    p = _dot(h, win_ref[...])
    for hd in range(HEADS):
        q = _rope(p[:, hd * DK:(hd + 1) * DK], cq_ref[...], sq_ref[...])
        k = _rope(p[:, QK + hd * DK:QK + (hd + 1) * DK], ck_ref[...], sk_ref[...])
        v = p[:, 2 * QK + hd * DV:2 * QK + (hd + 1) * DV]
        qt_ref[hd] = q.T
        kt_ref[hd] = k.T
        ov_ref[:, hd * DV:(hd + 1) * DV] = jnp.sum(q * k, axis=-1, keepdims=True) * v
    v_ref[...] = p[:, 2 * QK:2 * QK + VW]
    gate_ref[...] = p[:, 2 * QK + VW:]


def _ret_proj_sample(x2d, g, winb, *, pos0):
    n = x2d.shape[0]
    cos, sin = _rope_tables(np.array([pos0]))
    scale = DK ** -0.5
    tabs = [jnp.asarray(a, F32) for a in (cos, sin, cos * scale, sin * scale)]
    row = pl.BlockSpec((n, VW), lambda i: (0, 0))
    tr = pl.BlockSpec((HEADS, DK, n), lambda i: (0, 0, 0))
    return pl.pallas_call(
        _ret_proj_sample_body,
        grid=(1,),
        in_specs=[_const((n, D)), _const((1, D)), _const((D, 2 * QK + 2 * VW))] + [_const((1, DK))] * 4,
        out_specs=[tr, tr, row, row, row],
        out_shape=[jax.ShapeDtypeStruct((HEADS, DK, n), F32)] * 2 + [jax.ShapeDtypeStruct((n, VW), F32)] * 3,
        compiler_params=_params(40, 1),
        name="ret_proj_sample",
    )(x2d, g.reshape(1, D), winb, *tabs)


def _mlp_ret_body(xp_ref, xs_ref, g_ref, fg_ref, w1_hbm, w2_hbm,
                  qt_ref, kt_ref, v_ref, ov_ref, gate_ref, gng_ref, gnb_ref, rwout_ref, s_hbm,
                  op_ref, os_ref, snew_hbm,
                  w1b, w2b, stg1, stg2, sem1, sem2, s_in, s_out, sem_in, sem_out, o_scr,
                  *, li, n_p, ns, gamma):
    i = pl.program_id(0)
    spb = ns // n_p
    prime, fetch_w1, fetch_w2 = _weight_stream(w1_hbm, w2_hbm, li, w1b, w2b, stg1, stg2, sem1, sem2)

    def load(c, slot):
        return pltpu.make_async_copy(s_hbm.at[c], s_in.at[slot], sem_in.at[slot])

    def store(c, slot):
        return pltpu.make_async_copy(s_out.at[slot], snew_hbm.at[c], sem_out.at[slot])

    def state_update(j):
        def run():
            c = i * spb + j
            slot = j % 2
            load(c, slot).wait()
            store(c, slot).wait()
            lane = lax.broadcasted_iota(jnp.int32, (DK, ns), 1) == c
            for hd in range(HEADS):
                cols = slice(hd * DV, (hd + 1) * DV)
                s_old = s_in[slot, hd]
                qcol = jnp.sum(jnp.where(lane, qt_ref[hd], 0.0), axis=1, keepdims=True)
                kcol = jnp.sum(jnp.where(lane, kt_ref[hd], 0.0), axis=1, keepdims=True)
                v = v_ref[pl.ds(c, 1), cols]
                o_scr[pl.ds(c, 1), cols] = jnp.sum(s_old * (qcol * gamma[hd]), axis=0, keepdims=True)
                s_out[slot, hd] = s_old * gamma[hd] + kcol * v
            store(c, slot).start()
            load(jnp.minimum(c + 2, ns - 1), slot).start()
        return run

    def states():
        return [state_update(j) for j in range(spb)]

    @pl.when(i == 0)
    def _():
        prime()
        s_out[...] = jnp.zeros_like(s_out)
        for slot in range(2):
            load(slot, slot).start()
            store(slot, slot).start()
        _interleave(_mlp_thunks(lambda: xp_ref[...], _store(op_ref), g_ref, fg_ref, w1b, w2b, False,
                                fetch_w1, fetch_w2), states())

    @pl.when((i > 0) & (i < n_p))
    def _():
        _interleave(_mlp_thunks(lambda: xp_ref[...], _store(op_ref), g_ref, fg_ref, w1b, w2b, False), states())

    @pl.when(i == n_p)
    def _():
        for slot in range(2):
            load(ns - 1, slot).wait()
            store(ns - 1, slot).wait()
        parts = []
        for hd in range(HEADS):
            cols = slice(hd * DV, (hd + 1) * DV)
            o = ov_ref[:, cols] + o_scr[:, cols]
            parts.append(_group_norm_gate(o, gate_ref[:, cols], gng_ref[:, cols], gnb_ref[:, cols]))
        gated = jnp.concatenate(parts, axis=-1).astype(BF16)
        xs1 = xs_ref[...] + _dot(gated, rwout_ref[...])
        _run(_mlp_thunks(lambda: xs1, _store(os_ref), g_ref, fg_ref, w1b, w2b, False))


def _mlp_ret(xp2d, xs2d, g, w1, w2, fg, qt, kt, v, ov, gate, gn_g, gn_b, rwoutb, s, *, li, tm=512):
    n, ns = xp2d.shape[0], xs2d.shape[0]
    n_p = n // tm
    _, _, _, chunk_dec = _ret_constants(1)
    row = pl.BlockSpec((tm, D), lambda i: (jnp.minimum(i, n_p - 1), 0))
    srow = pl.BlockSpec((ns, D), lambda i: (0, 0))
    hbm = pl.BlockSpec(memory_space=pl.ANY)
    wide = _const((ns, VW))
    return pl.pallas_call(
        functools.partial(_mlp_ret_body, li=li, n_p=n_p, ns=ns, gamma=tuple(float(c) for c in chunk_dec)),
        grid=(n_p + 1,),
        in_specs=[row, _const((ns, D)), _const((1, D)), _const((1, D)), hbm, hbm,
                  _const((HEADS, DK, ns)), _const((HEADS, DK, ns)), wide, wide, wide,
                  _const((1, VW)), _const((1, VW)), _const((VW, D)), hbm],
        out_specs=[row, srow, hbm],
        out_shape=[jax.ShapeDtypeStruct((n, D), F32), jax.ShapeDtypeStruct((ns, D), F32),
                   jax.ShapeDtypeStruct(s.shape, F32)],
        scratch_shapes=[pltpu.VMEM((D, D_FF), BF16), pltpu.VMEM((D_FF, D), BF16),
                        pltpu.VMEM((2, D, W_FC), F32), pltpu.VMEM((2, W_FC, D), F32),
                        pltpu.SemaphoreType.DMA((2,)), pltpu.SemaphoreType.DMA((2,)),
                        pltpu.VMEM((2, HEADS, DK, DV), F32), pltpu.VMEM((2, HEADS, DK, DV), F32),
                        pltpu.SemaphoreType.DMA((2,)), pltpu.SemaphoreType.DMA((2,)),
                        pltpu.VMEM((ns, VW), F32)],
        compiler_params=_params(58, 1),
        name="mlp_ret",
    )(xp2d, xs2d, g.reshape(1, D), fg.reshape(1, D), w1, w2, qt, kt, v, ov, gate,
      gn_g.reshape(1, VW), gn_b.reshape(1, VW), rwoutb, s)


def _lru_gates(xc, wax_ref, ba, bx, lam, heads):
    xcb = xc.astype(BF16)
    rs, is_ = [], []
    for n, hd in enumerate(heads):
        ri = _dot(xcb[:, n * DK:(n + 1) * DK], wax_ref[hd])
        rs.append(ri[:, :DK])
        is_.append(ri[:, DK:])
    r = jax.nn.sigmoid(jnp.concatenate(rs, axis=-1) + ba)
    i = jax.nn.sigmoid(jnp.concatenate(is_, axis=-1) + bx)
    log_a = r * (-LRU_C * _softplus(-lam))
    a = jnp.exp(log_a)
    mult = jnp.sqrt(-jnp.tanh(log_a) * (a * a + 1.0))
    return a, mult, i


LRU_GROUP_HEADS = 4


def _lru_prompt_body(x_ref, g_ref, win_ref, cw_ref, cb_ref, wax_ref, ba_ref, bx_ref, lam_ref, wout_ref,
                     o_ref, conv_ref, hlast_ref, xb_scr, hs_scr, carry_scr, h_scr, *, NB, T):
    t = pl.program_id(0)
    N = NB * T
    PS = T + 1
    halo = (CONV_W - 1) * NB
    gw = LRU_GROUP_HEADS * DK
    gblk = gw // LANES

    @pl.when(t == 0)
    def _():
        carry_scr[...] = jnp.zeros_like(carry_scr)
        h_scr[...] = jnp.zeros_like(h_scr)

    x = x_ref[...].reshape(N, D)
    h = _rms(x, g_ref[...]).astype(BF16)
    row = lax.broadcasted_iota(jnp.int32, (N, 1), 0)
    pos = t * T + lax.shift_right_logical(row, NB.bit_length() - 1)
    out = x
    def project(grp):
        c0, c1 = grp * gw, (grp + 1) * gw
        return _dot(h, win_ref[:, c0:c1]), _dot(h, win_ref[:, D + c0:D + c1])

    ngrp = D // gw
    nxt = project(0)
    for grp in range(ngrp):
        c0, c1 = grp * gw, (grp + 1) * gw
        blks = range(grp * gblk, (grp + 1) * gblk)
        zg, xb = nxt
        if grp + 1 < ngrp:
            nxt = project(grp + 1)
        gate = _gelu(zg)
        for n, j in enumerate(blks):
            for b in range(NB):
                xb_scr[j, b * PS:b * PS + T, :] = xb[b * T:(b + 1) * T, n * LANES:(n + 1) * LANES]
        ext = jnp.concatenate(
            [carry_scr[:, c0:c1]]
            + [jnp.concatenate([xb_scr[j, pl.ds(s, NB, stride=PS), :] for j in blks], axis=-1)
               for s in range(T)], axis=0)
        carry_scr[:, c0:c1] = ext[N:N + halo, :]
        acc = ext[0:N, :] * cw_ref[0:1, c0:c1]
        for j in range(1, CONV_W):
            acc = acc + ext[j * NB:j * NB + N, :] * cw_ref[j:j + 1, c0:c1]
        xc = cb_ref[:, c0:c1] + acc
        heads = range(grp * LRU_GROUP_HEADS, (grp + 1) * LRU_GROUP_HEADS)
        a, mult, i = _lru_gates(xc, wax_ref, ba_ref[:, c0:c1], bx_ref[:, c0:c1], lam_ref[:, c0:c1], heads)
        bvec = jnp.where(pos == 0, 1.0, mult) * (i * xc)
        hcur = h_scr[:, c0:c1]
        for s in range(T):
            hcur = a[s * NB:(s + 1) * NB, :] * hcur + bvec[s * NB:(s + 1) * NB, :]
            for n, j in enumerate(blks):
                hs_scr[j, pl.ds(s, NB, stride=PS), :] = hcur[:, n * LANES:(n + 1) * LANES]
        h_scr[:, c0:c1] = hcur
        hs = jnp.concatenate(
            [jnp.concatenate([hs_scr[j, b * PS:b * PS + T, :] for b in range(NB)], axis=0) for j in blks],
            axis=-1)
        out = out + _dot((hs * gate).astype(BF16), wout_ref[c0:c1, :])
    o_ref[...] = out.reshape(NB, T, D)
    hlast_ref[...] = h_scr[...]

    @pl.when(t == pl.num_programs(0) - 1)
    def _():
        for j in range(CONV_W - 1):
            conv_ref[:, j, :] = carry_scr[j * NB:(j + 1) * NB, :]


def _lru_prompt(x, g, winb, conv_w, conv_b, waxb, b_a, b_x, lam, woutb, *, T=128):
    B, L, _ = x.shape
    blk = pl.BlockSpec((B, T, D), lambda t: (0, t, 0))
    vec = _const((1, D))
    return pl.pallas_call(
        functools.partial(_lru_prompt_body, NB=B, T=T),
        grid=(L // T,),
        in_specs=[blk, vec, _const((D, 2 * D)), _const((CONV_W, D)), vec, _const((HEADS, DK, 2 * DK)),
                  vec, vec, vec, _const((D, D))],
        out_specs=[blk, pl.BlockSpec((B, CONV_W - 1, D), lambda t: (0, 0, 0)),
                   pl.BlockSpec((B, D), lambda t: (0, 0))],
        out_shape=[jax.ShapeDtypeStruct((B, L, D), F32), jax.ShapeDtypeStruct((B, CONV_W - 1, D), F32),
                   jax.ShapeDtypeStruct((B, D), F32)],
        scratch_shapes=[pltpu.VMEM((D // LANES, B * (T + 1), LANES), F32)] * 2
        + [pltpu.VMEM(((CONV_W - 1) * B, D), F32), pltpu.VMEM((B, D), F32)],
        compiler_params=_params(48, 1),
        name="lru_prompt",
    )(x, g.reshape(1, D), winb, conv_w, conv_b.reshape(1, D), waxb, b_a.reshape(1, D), b_x.reshape(1, D),
      lam.reshape(1, D), woutb)


def _lru_sample_body(x_ref, cbuf_ref, h0_ref, g_ref, win_ref, cw_ref, cb_ref, wax_ref, ba_ref, bx_ref,
                     lam_ref, wout_ref, o_ref, nconv_ref, hnew_ref, *, pos0):
    x = x_ref[...]
    h = _rms(x, g_ref[...]).astype(BF16)
    z = _dot(h, win_ref[...])
    gate = _gelu(z[:, :D])
    xb = z[:, D:]
    cw = cw_ref[...]
    acc = cbuf_ref[0] * cw[0:1, :]
    for j in range(1, CONV_W - 1):
        acc = acc + cbuf_ref[j] * cw[j:j + 1, :]
    acc = acc + xb * cw[CONV_W - 1:CONV_W, :]
    xc = cb_ref[...] + acc
    a, mult, i = _lru_gates(xc, wax_ref, ba_ref[...], bx_ref[...], lam_ref[...], range(HEADS))
    if pos0 == 0:
        mult = jnp.ones_like(mult)
    hnew = a * h0_ref[...] + mult * (i * xc)
    hnew_ref[...] = hnew
    nconv_ref[0:CONV_W - 2] = cbuf_ref[1:CONV_W - 1]
    nconv_ref[CONV_W - 2] = xb
    o_ref[...] = x + _dot((hnew * gate).astype(BF16), wout_ref[...])


def _lru_sample(x2d, cbuf, h0, g, winb, conv_w, conv_b, waxb, b_a, b_x, lam, woutb, *, pos0):
    n = x2d.shape[0]
    vec = _const((1, D))
    cshape = (CONV_W - 1, n, D)
    out = pl.BlockSpec((n, D), lambda i: (0, 0))
    return pl.pallas_call(
        functools.partial(_lru_sample_body, pos0=pos0),
        grid=(1,),
        in_specs=[_const((n, D)), _const(cshape), _const((n, D)), vec, _const((D, 2 * D)), _const((CONV_W, D)),
                  vec, _const((HEADS, DK, 2 * DK)), vec, vec, vec, _const((D, D))],
        out_specs=[out, pl.BlockSpec(cshape, lambda i: (0, 0, 0)), out],
        out_shape=[jax.ShapeDtypeStruct((n, D), F32), jax.ShapeDtypeStruct(cshape, F32),
                   jax.ShapeDtypeStruct((n, D), F32)],
        compiler_params=_params(32, 1),
        name="lru_sample",
    )(x2d, cbuf, h0, g.reshape(1, D), winb, conv_w, conv_b.reshape(1, D), waxb, b_a.reshape(1, D),
      b_x.reshape(1, D), lam.reshape(1, D), woutb)


def kernel(x_prompt, x_sample, state_pool, state_ret, state_conv, state_lru, pool_norm, pool_w, pool_scale, gm_norm, gm_w_in, gm_b_in, gm_ln_g, gm_ln_b, gm_w_s, gm_b_s, gm_w_out, ret_norm, ret_w_in, ret_gn_g, ret_gn_b, ret_w_out, lru_norm, lru_w_in, lru_conv_w, lru_conv_b, lru_w_a, lru_b_a, lru_w_x, lru_b_x, lru_lam, lru_w_out, mlp_norm, mlp_w1, mlp_w2, final_norm):
    B, L, _ = x_prompt.shape
    NS = x_sample.shape[0]
    bf = lambda w: w.astype(BF16)

    def mlp(xp, xs, li, final=False):
        yp, ys = _mlp(xp.reshape(B * L, D), xs, mlp_norm[li], mlp_w1, mlp_w2, final_norm, li=li, final=final)
        return yp.reshape(B, L, D), ys

    pool_wb = bf(pool_w[0])
    xs, pool_s = _pool_sample(x_sample.reshape(NS, D), jnp.swapaxes(state_pool[0], 0, 1),
                              pool_norm[0], pool_wb, pool_scale[0], pos0=PAST_LEN)
    xp, xs, pool_p = _pool_mlp(x_prompt, xs, pool_norm[0], pool_wb, pool_scale[0], mlp_norm[0],
                               mlp_w1, mlp_w2, final_norm, li=0)

    gm_winb, gm_woutb = bf(gm_w_in[0]), bf(gm_w_out[0])
    bs_full = jnp.repeat(gm_b_s[0].T, CHUNK, axis=1)
    sw = jnp.repeat(gm_w_s[0][:, 0, 0], CHUNK).reshape(1, D)
    sb = jnp.repeat(gm_b_s[0][:, 0], CHUNK).reshape(1, D)
    xs, v_s = _gmlp_sample(xs, gm_norm[0], gm_winb, gm_b_in[0], gm_ln_g[0], gm_ln_b[0], sw, sb, gm_woutb)
    xp, xs = _gmlp_mlp(xp, xs, gm_norm[0], gm_winb, gm_b_in[0], gm_ln_g[0], gm_ln_b[0], gm_w_s[0], bs_full,
                       gm_woutb, mlp_norm[1], mlp_w1, mlp_w2, final_norm, li=1)

    ret_winb, ret_woutb = bf(ret_w_in[0]), bf(ret_w_out[0])
    xp, ret_p = _ret_prompt(xp, ret_norm[0], ret_winb, ret_gn_g[0], ret_gn_b[0], ret_woutb)
    qt, kt, v, gate, ov = _ret_proj_sample(xs, ret_norm[0], ret_winb, pos0=PAST_LEN)
    xp, xs, ret_s = _mlp_ret(xp.reshape(B * L, D), xs, mlp_norm[2], mlp_w1, mlp_w2, final_norm,
                             qt, kt, v, ov, gate, ret_gn_g[0], ret_gn_b[0], ret_woutb, state_ret[0], li=2)
    xp = xp.reshape(B, L, D)

    lru_winb, lru_woutb = bf(lru_w_in[0]), bf(lru_w_out[0])
    waxb = bf(jnp.concatenate([lru_w_a[0], lru_w_x[0]], axis=-1))
    xp, conv_p, lru_p = _lru_prompt(xp, lru_norm[0], lru_winb, lru_conv_w[0], lru_conv_b[0], waxb,
                                    lru_b_a[0], lru_b_x[0], lru_lam[0], lru_woutb)
    xs, conv_s, lru_s = _lru_sample(xs, jnp.swapaxes(state_conv[0], 0, 1), state_lru[0],
                                    lru_norm[0], lru_winb, lru_conv_w[0], lru_conv_b[0], waxb,
                                    lru_b_a[0], lru_b_x[0], lru_lam[0], lru_woutb, pos0=PAST_LEN)
    yp, ys = mlp(xp, xs, 3, final=True)

    return (yp, ys.reshape(NS, 1, D),
            pool_p[None], jnp.swapaxes(pool_s, 0, 1)[None],
            v_s.reshape(1, NS, 1, D),
            ret_p[None], ret_s[None],
            conv_p[None], jnp.swapaxes(conv_s, 0, 1)[None],
            lru_p[None], lru_s[None])
```

```python
import functools
import math

import jax
import jax.numpy as jnp
import numpy as np
from jax import lax
from jax.experimental import pallas as pl
from jax.experimental.pallas import tpu as pltpu

F32 = jnp.float32
BF16 = jnp.bfloat16

D = 1024
EPS = 1e-6
GN_EPS = 1e-5
PAST_LEN = 16384
POOL_WINDOWS = (2, 4, 8, 16)
POOL_GROUP = D // len(POOL_WINDOWS)
POOL_BUF = max(POOL_WINDOWS) - 1
CHUNK = 128
GM_GROUPS = 8
HEADS = 8
DK = D // HEADS
DV = 2 * D // HEADS
QK = HEADS * DK
VW = HEADS * DV
ROPE_BASE = 10000.0
CONV_W = 4
LRU_C = 8.0
D_FF = 4 * D

MIB = 1024 * 1024
SUBLANES = 8
LANES = 128


def _params(vmem_mib, n_grid):
    return pltpu.CompilerParams(
        dimension_semantics=("arbitrary",) * n_grid,
        vmem_limit_bytes=vmem_mib * MIB,
    )


def _const(shape):
    zeros = (0,) * len(shape)
    return pl.BlockSpec(shape, lambda *_: zeros, pipeline_mode=pl.Buffered(1))


def _rms(x, g):
    ms = jnp.mean(x * x, axis=-1, keepdims=True)
    return x * lax.rsqrt(ms + EPS) * g


def _dot(a, b):
    return jnp.dot(a, b, preferred_element_type=F32)


def _interleave(*queues):
    pos = [0] * len(queues)
    while any(p < len(q) for p, q in zip(pos, queues)):
        _, i = min((pos[i] / len(q), i) for i, q in enumerate(queues) if pos[i] < len(q))
        queues[i][pos[i]]()
        pos[i] += 1


def _softplus(x):
    return jnp.maximum(x, 0.0) + jnp.log1p(jnp.exp(-jnp.abs(x)))


GELU_K = math.sqrt(2.0 / math.pi)


def _gelu(x):
    half = 0.5 * x
    return half + half * jnp.tanh(x * (GELU_K + (GELU_K * 0.044715) * (x * x)))


MLP_FC = 512


def _weight_staging(piece):
    return [pltpu.VMEM((2, D, piece), F32), pltpu.VMEM((2, piece, D), F32),
            pltpu.SemaphoreType.DMA((2,)), pltpu.SemaphoreType.DMA((2,))]


def _mlp_thunks(read_x, write_out, g_ref, fg_ref, w1b, w2b, final, fetch_w1=None, fetch_w2=None):
    st = {}

    def head():
        x = read_x()
        st["h"] = _rms(x, g_ref[...]).astype(BF16)
        st["acc"] = x

    def up(c):
        def run():
            if fetch_w1 is not None:
                fetch_w1(c)
            a = _dot(st["h"], w1b[:, c * MLP_FC:(c + 1) * MLP_FC])
            st["a"] = jnp.square(jnp.maximum(a, 0.0)).astype(BF16)
        return run

    def down(c):
        def run():
            if fetch_w2 is not None:
                fetch_w2(c)
            st["acc"] = st["acc"] + _dot(st["a"], w2b[c * MLP_FC:(c + 1) * MLP_FC, :])
        return run

    def tail():
        acc = st["acc"]
        write_out(_rms(acc, fg_ref[...]) if final else acc)

    return [head] + [f(c) for c in range(D_FF // MLP_FC) for f in (up, down)] + [tail]


def _run(thunks):
    for thunk in thunks:
        thunk()


def _store(ref):
    def write(v):
        ref[...] = v
    return write


def _weight_stream(w1_hbm, w2_hbm, li, w1b, w2b, stg1, stg2, sem1, sem2):
    W_FC = stg1.shape[2]
    assert stg2.shape[1] == W_FC and MLP_FC % W_FC == 0
    npc = D_FF // W_FC
    per = MLP_FC // W_FC

    def w1_copy(p):
        return pltpu.make_async_copy(w1_hbm.at[li, :, pl.ds(p * W_FC, W_FC)], stg1.at[p % 2], sem1.at[p % 2])

    def w2_copy(p):
        return pltpu.make_async_copy(w2_hbm.at[li, pl.ds(p * W_FC, W_FC), :], stg2.at[p % 2], sem2.at[p % 2])

    def prime():
        for p in range(min(2, npc)):
            w1_copy(p).start()
            w2_copy(p).start()

    def fetch_w1(c):
        for p in range(c * per, (c + 1) * per):
            w1_copy(p).wait()
            w1b[:, p * W_FC:(p + 1) * W_FC] = stg1[p % 2].astype(BF16)
            if p + 2 < npc:
                w1_copy(p + 2).start()

    def fetch_w2(c):
        for p in range(c * per, (c + 1) * per):
            w2_copy(p).wait()
            w2b[p * W_FC:(p + 1) * W_FC, :] = stg2[p % 2].astype(BF16)
            if p + 2 < npc:
                w2_copy(p + 2).start()

    return prime, fetch_w1, fetch_w2


def _mlp_body(xp_ref, xs_ref, g_ref, fg_ref, w1_hbm, w2_hbm, op_ref, os_ref,
              w1b, w2b, stg1, stg2, sem1, sem2, *, li, final, n_p):
    i = pl.program_id(0)
    prime, fetch_w1, fetch_w2 = _weight_stream(w1_hbm, w2_hbm, li, w1b, w2b, stg1, stg2, sem1, sem2)

    @pl.when(i == 0)
    def _():
        prime()
        _run(_mlp_thunks(lambda: xp_ref[...], _store(op_ref), g_ref, fg_ref, w1b, w2b, final,
                         fetch_w1, fetch_w2))

    @pl.when((i > 0) & (i < n_p))
    def _():
        _run(_mlp_thunks(lambda: xp_ref[...], _store(op_ref), g_ref, fg_ref, w1b, w2b, final))

    @pl.when(i == n_p)
    def _():
        _run(_mlp_thunks(lambda: xs_ref[...], _store(os_ref), g_ref, fg_ref, w1b, w2b, final))


def _mlp(xp2d, xs2d, g, w1, w2, fg, *, li, final, tm=512):
    n, ns = xp2d.shape[0], xs2d.shape[0]
    n_p = n // tm
    row = pl.BlockSpec((tm, D), lambda i: (jnp.minimum(i, n_p - 1), 0))
    srow = pl.BlockSpec((ns, D), lambda i: (0, 0))
    hbm = pl.BlockSpec(memory_space=pl.ANY)
    return pl.pallas_call(
        functools.partial(_mlp_body, li=li, final=final, n_p=n_p),
        grid=(n_p + 1,),
        in_specs=[row, _const((ns, D)), _const((1, D)), _const((1, D)), hbm, hbm],
        out_specs=[row, srow],
        out_shape=[jax.ShapeDtypeStruct((n, D), F32), jax.ShapeDtypeStruct((ns, D), F32)],
        scratch_shapes=[pltpu.VMEM((D, D_FF), BF16), pltpu.VMEM((D_FF, D), BF16)] + _weight_staging(MLP_FC),
        compiler_params=_params(48, 1),
        name="mlp",
    )(xp2d, xs2d, g.reshape(1, D), fg.reshape(1, D), w1, w2)


POOL_HALO = 32


def _pool_thunks(x_ref, g_ref, w_ref, sc_ref, write_out, buf_ref, ext_ref, *, T, t):
    st = {}

    def head():
        x = x_ref[...]
        h = _rms(x, g_ref[...])
        halo = jnp.where(t == 0, 0.0, ext_ref[...])
        st["x"], st["h"] = x, h
        st["ext"] = jnp.concatenate([halo, h], axis=0)
        st["pos"] = t * T + lax.broadcasted_iota(jnp.int32, (T, 1), 0)
        st["y"] = [None] * len(POOL_WINDOWS)

    def group(gi, w):
        def run():
            c0, c1 = gi * POOL_GROUP, (gi + 1) * POOL_GROUP
            levels = w.bit_length() - 1
            a = st["ext"][POOL_HALO - SUBLANES * levels:, c0:c1]
            for k in range(levels):
                shift = 1 << k
                n = a.shape[0] - SUBLANES
                a = a[SUBLANES:, :] + a[SUBLANES - shift:SUBLANES - shift + n, :]
            hg = st["h"][:, c0:c1]
            cnt = jnp.minimum(st["pos"] + 1, w).astype(F32)
            d = a / cnt - hg
            st["y"][gi] = _dot(d.astype(BF16), w_ref[gi])
        return run

    def tail():
        y = jnp.concatenate(st["y"], axis=-1) * sc_ref[...]
        write_out(st["x"] + y)
        ext_ref[...] = st["h"][T - POOL_HALO:, :]
        buf_ref[...] = st["h"][T - POOL_BUF:, :]

    return [head] + [group(gi, w) for gi, w in enumerate(POOL_WINDOWS)] + [tail]


def _pool_mlp_body(x_ref, xs_ref, pg_ref, pw_ref, psc_ref, g_ref, fg_ref, w1_hbm, w2_hbm,
                   o_ref, os_ref, buf_ref, ext_ref, x1_scr, w1b, w2b, stg1, stg2, sem1, sem2,
                   *, T, tiles_per_seq, n, li):
    s = pl.program_id(0)
    t = lax.rem(s, tiles_per_seq)
    slot = lax.rem(s, 2)
    prime, fetch_w1, fetch_w2 = _weight_stream(w1_hbm, w2_hbm, li, w1b, w2b, stg1, stg2, sem1, sem2)

    def write_x1(v):
        x1_scr[slot] = v

    def mixer():
        return _pool_thunks(x_ref, pg_ref, pw_ref, psc_ref, write_x1, buf_ref, ext_ref, T=T, t=t)

    def mlp(*fetch):
        return _mlp_thunks(lambda: x1_scr[1 - slot], _store(o_ref), g_ref, fg_ref, w1b, w2b, False, *fetch)

    @pl.when(s == 0)
    def _():
        ext_ref[0:POOL_HALO, :] = jnp.zeros((POOL_HALO, D), F32)
        prime()
        _run(mixer())

    @pl.when(s == 1)
    def _():
        _interleave(mlp(fetch_w1, fetch_w2), mixer())

    @pl.when((s > 1) & (s < n))
    def _():
        _interleave(mlp(), mixer())

    @pl.when(s == n)
    def _():
        _run(mlp())
        _run(_mlp_thunks(lambda: xs_ref[...], _store(os_ref), g_ref, fg_ref, w1b, w2b, False))


def _pool_mlp(x, xs2d, pg, pwb, psc, g, w1, w2, fg, *, li, T=512):
    B, L, _ = x.shape
    ns = xs2d.shape[0]
    tps = L // T
    n = B * tps
    tile = lambda s: (jnp.minimum(s, n - 1) // tps, lax.rem(jnp.minimum(s, n - 1), tps), 0)
    prev = lambda s: (jnp.maximum(s - 1, 0) // tps, lax.rem(jnp.maximum(s - 1, 0), tps), 0)
    hbm = pl.BlockSpec(memory_space=pl.ANY)
    return pl.pallas_call(
        functools.partial(_pool_mlp_body, T=T, tiles_per_seq=tps, n=n, li=li),
        grid=(n + 1,),
        in_specs=[pl.BlockSpec((None, T, D), tile), _const((ns, D)), _const((1, D)),
                  _const((len(POOL_WINDOWS), POOL_GROUP, POOL_GROUP)), _const((1, D)),
                  _const((1, D)), _const((1, D)), hbm, hbm],
        out_specs=[pl.BlockSpec((None, T, D), prev), pl.BlockSpec((ns, D), lambda s: (0, 0)),
                   pl.BlockSpec((None, POOL_BUF, D), lambda s: (jnp.minimum(s, n - 1) // tps, 0, 0))],
        out_shape=[jax.ShapeDtypeStruct((B, L, D), F32), jax.ShapeDtypeStruct((ns, D), F32),
                   jax.ShapeDtypeStruct((B, POOL_BUF, D), F32)],
        scratch_shapes=[pltpu.VMEM((POOL_HALO, D), F32), pltpu.VMEM((2, T, D), F32),
                        pltpu.VMEM((D, D_FF), BF16), pltpu.VMEM((D_FF, D), BF16)]
        + _weight_staging(MLP_FC // 2),
        compiler_params=_params(56, 1),
        name="pool_mlp",
    )(x, xs2d, pg.reshape(1, D), pwb, psc.reshape(1, D), g.reshape(1, D), fg.reshape(1, D), w1, w2)


def _pool_sample_body(x_ref, buf_ref, g_ref, w_ref, sc_ref, o_ref, nbuf_ref, *, pos0):
    x = x_ref[...]
    h = _rms(x, g_ref[...])
    ys = []
    for gi, w in enumerate(POOL_WINDOWS):
        c0, c1 = gi * POOL_GROUP, (gi + 1) * POOL_GROUP
        hg = h[:, c0:c1]
        s = hg
        for k in range(1, w):
            s = s + buf_ref[POOL_BUF - k, :, c0:c1]
        cnt = float(min(pos0 + 1, w))
        d = s / cnt - hg
        ys.append(_dot(d.astype(BF16), w_ref[gi]))
    y = jnp.concatenate(ys, axis=-1) * sc_ref[...]
    o_ref[...] = x + y
    nbuf_ref[0:POOL_BUF - 1] = buf_ref[1:POOL_BUF]
    nbuf_ref[POOL_BUF - 1] = h


def _pool_sample(x2d, buf, g, wb, sc, *, pos0, bt=32):
    n = x2d.shape[0]
    row = pl.BlockSpec((bt, D), lambda i: (i, 0))
    brow = pl.BlockSpec((POOL_BUF, bt, D), lambda i: (0, i, 0))
    return pl.pallas_call(
        functools.partial(_pool_sample_body, pos0=pos0),
        grid=(n // bt,),
        in_specs=[row, brow, _const((1, D)), _const((len(POOL_WINDOWS), POOL_GROUP, POOL_GROUP)), _const((1, D))],
        out_specs=[row, brow],
        out_shape=[jax.ShapeDtypeStruct((n, D), F32), jax.ShapeDtypeStruct((POOL_BUF, n, D), F32)],
        compiler_params=_params(32, 1),
        name="pool_sample",
    )(x2d, buf, g.reshape(1, D), wb, sc.reshape(1, D))


def _gmlp_front(x, g_ref, win_ref, bin_ref, lng_ref, lnb_ref):
    h = _rms(x, g_ref[...]).astype(BF16)
    z = _gelu(_dot(h, win_ref[...]) + bin_ref[...])
    u, v = z[:, :D], z[:, D:]
    mu = jnp.mean(v, axis=-1, keepdims=True)
    vc = v - mu
    var = jnp.mean(jnp.square(vc), axis=-1, keepdims=True)
    vn = vc * lax.rsqrt(var + EPS) * lng_ref[...] + lnb_ref[...]
    return u, vn


def _gmlp_prompt_body(x_ref, g_ref, win_ref, bin_ref, lng_ref, lnb_ref, ws_ref, bs_ref, wout_ref,
                      o_ref, *, T):
    ti = lax.broadcasted_iota(jnp.int32, (CHUNK, CHUNK), 0)
    si = lax.broadcasted_iota(jnp.int32, (CHUNK, CHUNK), 1)
    causal = ti >= si
    wsm = [jnp.where(causal, ws_ref[gi], 0.0).astype(BF16) for gi in range(GM_GROUPS)]
    x = x_ref[...]
    u, vn = _gmlp_front(x, g_ref, win_ref, bin_ref, lng_ref, lnb_ref)
    vnb = vn.astype(BF16)
    rows = []
    for n in range(T // CHUNK):
        r0, r1 = n * CHUNK, (n + 1) * CHUNK
        cols = [_dot(wsm[gi], vnb[r0:r1, gi * CHUNK:(gi + 1) * CHUNK]) for gi in range(GM_GROUPS)]
        rows.append(jnp.concatenate(cols, axis=-1) + bs_ref[...])
    mixed = jnp.concatenate(rows, axis=0)
    y = _dot((u * mixed).astype(BF16), wout_ref[...])
    o_ref[...] = x + y


def _gmlp_prompt(x, g, winb, b_in, ln_g, ln_b, w_s, bs_full, woutb, *, T=1024):
    B, L, _ = x.shape
    blk = pl.BlockSpec((None, T, D), lambda b, t: (b, t, 0))
    return pl.pallas_call(
        functools.partial(_gmlp_prompt_body, T=T),
        grid=(B, L // T),
        in_specs=[blk, _const((1, D)), _const((D, 2 * D)), _const((1, 2 * D)), _const((1, D)), _const((1, D)),
                  _const((GM_GROUPS, CHUNK, CHUNK)), _const((CHUNK, D)), _const((D, D))],
        out_specs=blk,
        out_shape=jax.ShapeDtypeStruct((B, L, D), F32),
        compiler_params=_params(48, 2),
        name="gmlp_prompt",
    )(x, g.reshape(1, D), winb, b_in.reshape(1, 2 * D), ln_g.reshape(1, D), ln_b.reshape(1, D),
      w_s, bs_full, woutb)


def _gmlp_sample_body(x_ref, g_ref, win_ref, bin_ref, lng_ref, lnb_ref, sw_ref, sb_ref, wout_ref,
                      o_ref, vn_ref):
    x = x_ref[...]
    u, vn = _gmlp_front(x, g_ref, win_ref, bin_ref, lng_ref, lnb_ref)
    mixed = vn * sw_ref[...] + sb_ref[...]
    y = _dot((u * mixed).astype(BF16), wout_ref[...])
    o_ref[...] = x + y
    vn_ref[...] = vn


def _gmlp_sample(x2d, g, winb, b_in, ln_g, ln_b, sw, sb, woutb):
    n = x2d.shape[0]
    full = _const((n, D))
    return pl.pallas_call(
        _gmlp_sample_body,
        grid=(1,),
        in_specs=[full, _const((1, D)), _const((D, 2 * D)), _const((1, 2 * D)), _const((1, D)), _const((1, D)),
                  _const((1, D)), _const((1, D)), _const((D, D))],
        out_specs=[pl.BlockSpec((n, D), lambda i: (0, 0)), pl.BlockSpec((n, D), lambda i: (0, 0))],
        out_shape=[jax.ShapeDtypeStruct((n, D), F32), jax.ShapeDtypeStruct((n, D), F32)],
        compiler_params=_params(32, 1),
        name="gmlp_sample",
    )(x2d, g.reshape(1, D), winb, b_in.reshape(1, 2 * D), ln_g.reshape(1, D), ln_b.reshape(1, D),
      sw, sb, woutb)


def _ret_constants(C):
    log_gamma = np.log1p(-np.exp2(-5.0 - np.arange(HEADS, dtype=np.float64)))
    idx = np.arange(C, dtype=np.float64)
    diff = idx[:, None] - idx[None, :]
    dmask = np.where(diff[None] >= 0, np.exp(log_gamma[:, None, None] * np.maximum(diff, 0.0)[None]), 0.0)
    q_dec = np.exp(log_gamma[:, None] * (idx + 1.0))
    k_dec = np.exp(log_gamma[:, None] * (C - 1.0 - idx))
    chunk_dec = np.exp(log_gamma * C)
    return dmask.astype(np.float32), q_dec.astype(np.float32), k_dec.astype(np.float32), chunk_dec


def _rope_tables(pos):
    half = DK // 2
    freqs = np.exp(-math.log(ROPE_BASE) * np.arange(half, dtype=np.float64) / half)
    ang = np.asarray(pos, dtype=np.float64)[:, None] * freqs[None]
    cos = np.concatenate([np.cos(ang), np.cos(ang)], axis=-1)
    sin = np.concatenate([-np.sin(ang), np.sin(ang)], axis=-1)
    return cos, sin


def _rope(t, cos, sin):
    return t * cos + pltpu.roll(t, DK // 2, axis=1) * sin


def _group_norm_gate(o, gate, gng, gnb):
    mu = jnp.mean(o, axis=-1, keepdims=True)
    oc = o - mu
    var = jnp.mean(jnp.square(oc), axis=-1, keepdims=True)
    on = oc * lax.rsqrt(var + GN_EPS) * gng + gnb
    return jax.nn.silu(gate) * on


RET_GROUP_HEADS = 2


def _ret_prompt_body(x_ref, g_ref, win_ref, cq_ref, sq_ref, ck_ref, sk_ref, dm_ref, qd_ref, kd_ref,
                     gng_ref, gnb_ref, wout_ref, o_ref, s_out_ref, gated_scr, s_scr,
                     *, T, chunk_dec):
    t = pl.program_id(1)

    @pl.when(t == 0)
    def _():
        s_scr[...] = jnp.zeros_like(s_scr)

    x = x_ref[...]
    h = _rms(x, g_ref[...]).astype(BF16)
    cq, sq, ck, sk = cq_ref[...], sq_ref[...], ck_ref[...], sk_ref[...]
    gh = RET_GROUP_HEADS
    n_pairs = HEADS // gh

    def projection(pair, dst):
        def piece(name, lo, width):
            def run():
                dst[name] = _dot(h, win_ref[:, lo:lo + width])
            return run
        v0, g0 = 2 * QK + gh * DV * pair, 2 * QK + VW + gh * DV * pair
        return ([piece("q", gh * DK * pair, gh * DK), piece("k", QK + gh * DK * pair, gh * DK)]
                + [piece("v%d" % sub, v0 + sub * DV, DV) for sub in range(gh)]
                + [piece("g%d" % sub, g0 + sub * DV, DV) for sub in range(gh)])

    def chunk_work(pair, src):
        state = {}

        def prep(sub):
            def run():
                q = _rope(src["q"][:, sub * DK:(sub + 1) * DK], cq, sq)
                k = _rope(src["k"][:, sub * DK:(sub + 1) * DK], ck, sk)
                state[sub] = dict(q=q, k=k, qb=q.astype(BF16), kb=k.astype(BF16), s=s_scr[gh * pair + sub])
            return run

        def chunk(sub, c):
            def run():
                hd = gh * pair + sub
                st = state[sub]
                rows = slice(c * CHUNK, (c + 1) * CHUNK)
                vb = src["v%d" % sub][rows, :].astype(BF16)
                scores = lax.dot_general(st["qb"][rows], st["kb"][rows], (((1,), (1,)), ((), ())),
                                         preferred_element_type=F32) * dm_ref[hd]
                o = (_dot(scores.astype(BF16), vb)
                     + _dot((st["q"][rows] * qd_ref[hd]).astype(BF16), st["s"].astype(BF16)))
                kt = (st["k"][rows] * kd_ref[hd]).T.astype(BF16)
                st["s"] = st["s"] * chunk_dec[hd] + _dot(kt, vb)
                gated = _group_norm_gate(o, src["g%d" % sub][rows, :], gng_ref[:, hd * DV:(hd + 1) * DV],
                                         gnb_ref[:, hd * DV:(hd + 1) * DV])
                gated_scr[rows, hd * DV:(hd + 1) * DV] = gated.astype(BF16)
            return run

        def finish(sub):
            def run():
                s_scr[gh * pair + sub] = state[sub]["s"]
            return run

        return ([prep(sub) for sub in range(gh)]
                + [chunk(sub, c) for c in range(T // CHUNK) for sub in range(gh)]
                + [finish(sub) for sub in range(gh)])

    acc = [x]

    def out_projection(pair):
        def run():
            cols = slice(pair * gh * DV, (pair + 1) * gh * DV)
            acc[0] = acc[0] + _dot(gated_scr[:, cols], wout_ref[cols, :])
        return run

    cur = {}
    for run in projection(0, cur):
        run()
    for pair in range(n_pairs):
        nxt = {}
        if pair + 1 < n_pairs:
            matmul_queue = projection(pair + 1, nxt)
        else:
            matmul_queue = [out_projection(p) for p in range(n_pairs - 1)]
        _interleave(matmul_queue, chunk_work(pair, cur))
        cur = nxt
    out_projection(n_pairs - 1)()
    o_ref[...] = acc[0]
    s_out_ref[...] = s_scr[...]


def _ret_prompt(x, g, winb, gn_g, gn_b, woutb, *, T=512):
    B, L, _ = x.shape
    dmask, q_dec, k_dec, chunk_dec = _ret_constants(CHUNK)
    cos, sin = _rope_tables(np.arange(L))
    scale = DK ** -0.5
    cq, sq = jnp.asarray(cos, F32), jnp.asarray(sin, F32)
    ck, sk = jnp.asarray(cos * scale, F32), jnp.asarray(sin * scale, F32)
    qd = jnp.asarray(np.broadcast_to(q_dec[:, :, None], (HEADS, CHUNK, DK)))
    kd = jnp.asarray(np.broadcast_to(k_dec[:, :, None], (HEADS, CHUNK, DK)))
    blk = pl.BlockSpec((None, T, D), lambda b, t: (b, t, 0))
    tab = pl.BlockSpec((T, DK), lambda b, t: (t, 0))
    hcc = _const((HEADS, CHUNK, CHUNK))
    return pl.pallas_call(
        functools.partial(_ret_prompt_body, T=T, chunk_dec=tuple(float(c) for c in chunk_dec)),
        grid=(B, L // T),
        in_specs=[blk, _const((1, D)), _const((D, 2 * QK + 2 * VW)), tab, tab, tab, tab,
                  hcc, hcc, hcc, _const((1, VW)), _const((1, VW)), _const((VW, D))],
        out_specs=[blk, pl.BlockSpec((None, HEADS, DK, DV), lambda b, t: (b, 0, 0, 0))],
        out_shape=[jax.ShapeDtypeStruct((B, L, D), F32), jax.ShapeDtypeStruct((B, HEADS, DK, DV), F32)],
        scratch_shapes=[pltpu.VMEM((T, VW), BF16), pltpu.VMEM((HEADS, DK, DV), F32)],
        compiler_params=_params(56, 2),
        name="ret_prompt",
    )(x, g.reshape(1, D), winb, cq, sq, ck, sk, jnp.asarray(dmask), qd, kd,
      gn_g.reshape(1, VW), gn_b.reshape(1, VW), woutb)


def _ret_proj_sample_body(x_ref, g_ref, win_ref, cq_ref, sq_ref, ck_ref, sk_ref,
                          qt_ref, kt_ref, v_ref, gate_ref, ov_ref):
    h = _rms(x_ref[...], g_ref[...]).astype(BF16)
    p = _dot(h, win_ref[...])
    for hd in range(HEADS):
        q = _rope(p[:, hd * DK:(hd + 1) * DK], cq_ref[...], sq_ref[...])
        k = _rope(p[:, QK + hd * DK:QK + (hd + 1) * DK], ck_ref[...], sk_ref[...])
        v = p[:, 2 * QK + hd * DV:2 * QK + (hd + 1) * DV]
        qt_ref[hd] = q.T
        kt_ref[hd] = k.T
        ov_ref[:, hd * DV:(hd + 1) * DV] = jnp.sum(q * k, axis=-1, keepdims=True) * v
    v_ref[...] = p[:, 2 * QK:2 * QK + VW]
    gate_ref[...] = p[:, 2 * QK + VW:]


def _ret_proj_sample(x2d, g, winb, *, pos0):
    n = x2d.shape[0]
    cos, sin = _rope_tables(np.array([pos0]))
    scale = DK ** -0.5
    tabs = [jnp.asarray(a, F32) for a in (cos, sin, cos * scale, sin * scale)]
    row = pl.BlockSpec((n, VW), lambda i: (0, 0))
    tr = pl.BlockSpec((HEADS, DK, n), lambda i: (0, 0, 0))
    return pl.pallas_call(
        _ret_proj_sample_body,
        grid=(1,),
        in_specs=[_const((n, D)), _const((1, D)), _const((D, 2 * QK + 2 * VW))] + [_const((1, DK))] * 4,
        out_specs=[tr, tr, row, row, row],
        out_shape=[jax.ShapeDtypeStruct((HEADS, DK, n), F32)] * 2 + [jax.ShapeDtypeStruct((n, VW), F32)] * 3,
        compiler_params=_params(40, 1),
        name="ret_proj_sample",
    )(x2d, g.reshape(1, D), winb, *tabs)


def _mlp_ret_body(xp_ref, xs_ref, g_ref, fg_ref, w1_hbm, w2_hbm,
                  qt_ref, kt_ref, v_ref, ov_ref, gate_ref, gng_ref, gnb_ref, rwout_ref, s_hbm,
                  op_ref, os_ref, snew_hbm,
                  w1b, w2b, stg1, stg2, sem1, sem2, s_in, s_out, sem_in, sem_out, o_scr,
                  *, li, n_p, ns, gamma):
    i = pl.program_id(0)
    spb = ns // n_p
    prime, fetch_w1, fetch_w2 = _weight_stream(w1_hbm, w2_hbm, li, w1b, w2b, stg1, stg2, sem1, sem2)

    def load(c, slot):
        return pltpu.make_async_copy(s_hbm.at[c], s_in.at[slot], sem_in.at[slot])

    def store(c, slot):
        return pltpu.make_async_copy(s_out.at[slot], snew_hbm.at[c], sem_out.at[slot])

    def state_update(j):
        def run():
            c = i * spb + j
            slot = j % 2
            load(c, slot).wait()
            store(c, slot).wait()
            lane = lax.broadcasted_iota(jnp.int32, (DK, ns), 1) == c
            for hd in range(HEADS):
                cols = slice(hd * DV, (hd + 1) * DV)
                s_old = s_in[slot, hd]
                qcol = jnp.sum(jnp.where(lane, qt_ref[hd], 0.0), axis=1, keepdims=True)
                kcol = jnp.sum(jnp.where(lane, kt_ref[hd], 0.0), axis=1, keepdims=True)
                v = v_ref[pl.ds(c, 1), cols]
                o_scr[pl.ds(c, 1), cols] = jnp.sum(s_old * (qcol * gamma[hd]), axis=0, keepdims=True)
                s_out[slot, hd] = s_old * gamma[hd] + kcol * v
            store(c, slot).start()
            load(jnp.minimum(c + 2, ns - 1), slot).start()
        return run

    def states():
        return [state_update(j) for j in range(spb)]

    @pl.when(i == 0)
    def _():
        prime()
        s_out[...] = jnp.zeros_like(s_out)
        for slot in range(2):
            load(slot, slot).start()
            store(slot, slot).start()
        _interleave(_mlp_thunks(lambda: xp_ref[...], _store(op_ref), g_ref, fg_ref, w1b, w2b, False,
                                fetch_w1, fetch_w2), states())

    @pl.when((i > 0) & (i < n_p))
    def _():
        _interleave(_mlp_thunks(lambda: xp_ref[...], _store(op_ref), g_ref, fg_ref, w1b, w2b, False), states())

    @pl.when(i == n_p)
    def _():
        for slot in range(2):
            load(ns - 1, slot).wait()
            store(ns - 1, slot).wait()
        parts = []
        for hd in range(HEADS):
            cols = slice(hd * DV, (hd + 1) * DV)
            o = ov_ref[:, cols] + o_scr[:, cols]
            parts.append(_group_norm_gate(o, gate_ref[:, cols], gng_ref[:, cols], gnb_ref[:, cols]))
        gated = jnp.concatenate(parts, axis=-1).astype(BF16)
        xs1 = xs_ref[...] + _dot(gated, rwout_ref[...])
        _run(_mlp_thunks(lambda: xs1, _store(os_ref), g_ref, fg_ref, w1b, w2b, False))


def _mlp_ret(xp2d, xs2d, g, w1, w2, fg, qt, kt, v, ov, gate, gn_g, gn_b, rwoutb, s, *, li, tm=512):
    n, ns = xp2d.shape[0], xs2d.shape[0]
    n_p = n // tm
    _, _, _, chunk_dec = _ret_constants(1)
    row = pl.BlockSpec((tm, D), lambda i: (jnp.minimum(i, n_p - 1), 0))
    srow = pl.BlockSpec((ns, D), lambda i: (0, 0))
    hbm = pl.BlockSpec(memory_space=pl.ANY)
    wide = _const((ns, VW))
    return pl.pallas_call(
        functools.partial(_mlp_ret_body, li=li, n_p=n_p, ns=ns, gamma=tuple(float(c) for c in chunk_dec)),
        grid=(n_p + 1,),
        in_specs=[row, _const((ns, D)), _const((1, D)), _const((1, D)), hbm, hbm,
                  _const((HEADS, DK, ns)), _const((HEADS, DK, ns)), wide, wide, wide,
                  _const((1, VW)), _const((1, VW)), _const((VW, D)), hbm],
        out_specs=[row, srow, hbm],
        out_shape=[jax.ShapeDtypeStruct((n, D), F32), jax.ShapeDtypeStruct((ns, D), F32),
                   jax.ShapeDtypeStruct(s.shape, F32)],
        scratch_shapes=[pltpu.VMEM((D, D_FF), BF16), pltpu.VMEM((D_FF, D), BF16)] + _weight_staging(MLP_FC)
        + [pltpu.VMEM((2, HEADS, DK, DV), F32), pltpu.VMEM((2, HEADS, DK, DV), F32),
           pltpu.SemaphoreType.DMA((2,)), pltpu.SemaphoreType.DMA((2,)),
           pltpu.VMEM((ns, VW), F32)],
        compiler_params=_params(58, 1),
        name="mlp_ret",
    )(xp2d, xs2d, g.reshape(1, D), fg.reshape(1, D), w1, w2, qt, kt, v, ov, gate,
      gn_g.reshape(1, VW), gn_b.reshape(1, VW), rwoutb, s)


def _lru_gates(xc, wax_ref, ba, bx, lam, heads):
    xcb = xc.astype(BF16)
    rs, is_ = [], []
    for n, hd in enumerate(heads):
        ri = _dot(xcb[:, n * DK:(n + 1) * DK], wax_ref[hd])
        rs.append(ri[:, :DK])
        is_.append(ri[:, DK:])
    r = jax.nn.sigmoid(jnp.concatenate(rs, axis=-1) + ba)
    i = jax.nn.sigmoid(jnp.concatenate(is_, axis=-1) + bx)
    log_a = r * (-LRU_C * _softplus(-lam))
    a = jnp.exp(log_a)
    mult = jnp.sqrt(-jnp.tanh(log_a) * (a * a + 1.0))
    return a, mult, i


LRU_GROUP_HEADS = 4


def _lru_prompt_body(x_ref, g_ref, win_ref, cw_ref, cb_ref, wax_ref, ba_ref, bx_ref, lam_ref, wout_ref,
                     o_ref, conv_ref, hlast_ref, xb_scr, hs_scr, carry_scr, h_scr, *, NB, T):
    t = pl.program_id(0)
    N = NB * T
    PS = T + 1
    halo = (CONV_W - 1) * NB
    gw = LRU_GROUP_HEADS * DK
    gblk = gw // LANES

    @pl.when(t == 0)
    def _():
        carry_scr[...] = jnp.zeros_like(carry_scr)
        h_scr[...] = jnp.zeros_like(h_scr)

    x = x_ref[...].reshape(N, D)
    h = _rms(x, g_ref[...]).astype(BF16)
    row = lax.broadcasted_iota(jnp.int32, (N, 1), 0)
    pos = t * T + lax.shift_right_logical(row, NB.bit_length() - 1)
    out = x
    def project(grp):
        c0, c1 = grp * gw, (grp + 1) * gw
        return _dot(h, win_ref[:, c0:c1]), _dot(h, win_ref[:, D + c0:D + c1])

    ngrp = D // gw
    nxt = project(0)
    for grp in range(ngrp):
        c0, c1 = grp * gw, (grp + 1) * gw
        blks = range(grp * gblk, (grp + 1) * gblk)
        zg, xb = nxt
        if grp + 1 < ngrp:
            nxt = project(grp + 1)
        gate = _gelu(zg)
        for n, j in enumerate(blks):
            for b in range(NB):
                xb_scr[j, b * PS:b * PS + T, :] = xb[b * T:(b + 1) * T, n * LANES:(n + 1) * LANES]
        ext = jnp.concatenate(
            [carry_scr[:, c0:c1]]
            + [jnp.concatenate([xb_scr[j, pl.ds(s, NB, stride=PS), :] for j in blks], axis=-1)
               for s in range(T)], axis=0)
        carry_scr[:, c0:c1] = ext[N:N + halo, :]
        acc = ext[0:N, :] * cw_ref[0:1, c0:c1]
        for j in range(1, CONV_W):
            acc = acc + ext[j * NB:j * NB + N, :] * cw_ref[j:j + 1, c0:c1]
        xc = cb_ref[:, c0:c1] + acc
        heads = range(grp * LRU_GROUP_HEADS, (grp + 1) * LRU_GROUP_HEADS)
        a, mult, i = _lru_gates(xc, wax_ref, ba_ref[:, c0:c1], bx_ref[:, c0:c1], lam_ref[:, c0:c1], heads)
        bvec = jnp.where(pos == 0, 1.0, mult) * (i * xc)
        hcur = h_scr[:, c0:c1]
        for s in range(T):
            hcur = a[s * NB:(s + 1) * NB, :] * hcur + bvec[s * NB:(s + 1) * NB, :]
            for n, j in enumerate(blks):
                hs_scr[j, pl.ds(s, NB, stride=PS), :] = hcur[:, n * LANES:(n + 1) * LANES]
        h_scr[:, c0:c1] = hcur
        hs = jnp.concatenate(
            [jnp.concatenate([hs_scr[j, b * PS:b * PS + T, :] for b in range(NB)], axis=0) for j in blks],
            axis=-1)
        out = out + _dot((hs * gate).astype(BF16), wout_ref[c0:c1, :])
    o_ref[...] = out.reshape(NB, T, D)
    hlast_ref[...] = h_scr[...]

    @pl.when(t == pl.num_programs(0) - 1)
    def _():
        for j in range(CONV_W - 1):
            conv_ref[:, j, :] = carry_scr[j * NB:(j + 1) * NB, :]


def _lru_prompt(x, g, winb, conv_w, conv_b, waxb, b_a, b_x, lam, woutb, *, T=128):
    B, L, _ = x.shape
    blk = pl.BlockSpec((B, T, D), lambda t: (0, t, 0))
    vec = _const((1, D))
    return pl.pallas_call(
        functools.partial(_lru_prompt_body, NB=B, T=T),
        grid=(L // T,),
        in_specs=[blk, vec, _const((D, 2 * D)), _const((CONV_W, D)), vec, _const((HEADS, DK, 2 * DK)),
                  vec, vec, vec, _const((D, D))],
        out_specs=[blk, pl.BlockSpec((B, CONV_W - 1, D), lambda t: (0, 0, 0)),
                   pl.BlockSpec((B, D), lambda t: (0, 0))],
        out_shape=[jax.ShapeDtypeStruct((B, L, D), F32), jax.ShapeDtypeStruct((B, CONV_W - 1, D), F32),
                   jax.ShapeDtypeStruct((B, D), F32)],
        scratch_shapes=[pltpu.VMEM((D // LANES, B * (T + 1), LANES), F32)] * 2
        + [pltpu.VMEM(((CONV_W - 1) * B, D), F32), pltpu.VMEM((B, D), F32)],
        compiler_params=_params(48, 1),
        name="lru_prompt",
    )(x, g.reshape(1, D), winb, conv_w, conv_b.reshape(1, D), waxb, b_a.reshape(1, D), b_x.reshape(1, D),
      lam.reshape(1, D), woutb)


def _lru_sample_body(x_ref, cbuf_ref, h0_ref, g_ref, win_ref, cw_ref, cb_ref, wax_ref, ba_ref, bx_ref,
                     lam_ref, wout_ref, o_ref, nconv_ref, hnew_ref, *, pos0):
    x = x_ref[...]
    h = _rms(x, g_ref[...]).astype(BF16)
    z = _dot(h, win_ref[...])
    gate = _gelu(z[:, :D])
    xb = z[:, D:]
    cw = cw_ref[...]
    acc = cbuf_ref[0] * cw[0:1, :]
    for j in range(1, CONV_W - 1):
        acc = acc + cbuf_ref[j] * cw[j:j + 1, :]
    acc = acc + xb * cw[CONV_W - 1:CONV_W, :]
    xc = cb_ref[...] + acc
    a, mult, i = _lru_gates(xc, wax_ref, ba_ref[...], bx_ref[...], lam_ref[...], range(HEADS))
    if pos0 == 0:
        mult = jnp.ones_like(mult)
    hnew = a * h0_ref[...] + mult * (i * xc)
    hnew_ref[...] = hnew
    nconv_ref[0:CONV_W - 2] = cbuf_ref[1:CONV_W - 1]
    nconv_ref[CONV_W - 2] = xb
    o_ref[...] = x + _dot((hnew * gate).astype(BF16), wout_ref[...])


def _lru_sample(x2d, cbuf, h0, g, winb, conv_w, conv_b, waxb, b_a, b_x, lam, woutb, *, pos0):
    n = x2d.shape[0]
    vec = _const((1, D))
    cshape = (CONV_W - 1, n, D)
    out = pl.BlockSpec((n, D), lambda i: (0, 0))
    return pl.pallas_call(
        functools.partial(_lru_sample_body, pos0=pos0),
        grid=(1,),
        in_specs=[_const((n, D)), _const(cshape), _const((n, D)), vec, _const((D, 2 * D)), _const((CONV_W, D)),
                  vec, _const((HEADS, DK, 2 * DK)), vec, vec, vec, _const((D, D))],
        out_specs=[out, pl.BlockSpec(cshape, lambda i: (0, 0, 0)), out],
        out_shape=[jax.ShapeDtypeStruct((n, D), F32), jax.ShapeDtypeStruct(cshape, F32),
                   jax.ShapeDtypeStruct((n, D), F32)],
        compiler_params=_params(32, 1),
        name="lru_sample",
    )(x2d, cbuf, h0, g.reshape(1, D), winb, conv_w, conv_b.reshape(1, D), waxb, b_a.reshape(1, D),
      b_x.reshape(1, D), lam.reshape(1, D), woutb)


def kernel(x_prompt, x_sample, state_pool, state_ret, state_conv, state_lru, pool_norm, pool_w, pool_scale, gm_norm, gm_w_in, gm_b_in, gm_ln_g, gm_ln_b, gm_w_s, gm_b_s, gm_w_out, ret_norm, ret_w_in, ret_gn_g, ret_gn_b, ret_w_out, lru_norm, lru_w_in, lru_conv_w, lru_conv_b, lru_w_a, lru_b_a, lru_w_x, lru_b_x, lru_lam, lru_w_out, mlp_norm, mlp_w1, mlp_w2, final_norm):
    B, L, _ = x_prompt.shape
    NS = x_sample.shape[0]
    bf = lambda w: w.astype(BF16)

    def mlp(xp, xs, li, final=False):
        yp, ys = _mlp(xp.reshape(B * L, D), xs, mlp_norm[li], mlp_w1, mlp_w2, final_norm, li=li, final=final)
        return yp.reshape(B, L, D), ys

    pool_wb = bf(pool_w[0])
    xs, pool_s = _pool_sample(x_sample.reshape(NS, D), jnp.swapaxes(state_pool[0], 0, 1),
                              pool_norm[0], pool_wb, pool_scale[0], pos0=PAST_LEN)
    xp, xs, pool_p = _pool_mlp(x_prompt, xs, pool_norm[0], pool_wb, pool_scale[0], mlp_norm[0],
                               mlp_w1, mlp_w2, final_norm, li=0)

    gm_winb, gm_woutb = bf(gm_w_in[0]), bf(gm_w_out[0])
    bs_full = jnp.repeat(gm_b_s[0].T, CHUNK, axis=1)
    xp = _gmlp_prompt(xp, gm_norm[0], gm_winb, gm_b_in[0], gm_ln_g[0], gm_ln_b[0], gm_w_s[0], bs_full, gm_woutb)
    sw = jnp.repeat(gm_w_s[0][:, 0, 0], CHUNK).reshape(1, D)
    sb = jnp.repeat(gm_b_s[0][:, 0], CHUNK).reshape(1, D)
    xs, v_s = _gmlp_sample(xs, gm_norm[0], gm_winb, gm_b_in[0], gm_ln_g[0], gm_ln_b[0], sw, sb, gm_woutb)
    xp, xs = mlp(xp, xs, 1)

    ret_winb, ret_woutb = bf(ret_w_in[0]), bf(ret_w_out[0])
    xp, ret_p = _ret_prompt(xp, ret_norm[0], ret_winb, ret_gn_g[0], ret_gn_b[0], ret_woutb)
    qt, kt, v, gate, ov = _ret_proj_sample(xs, ret_norm[0], ret_winb, pos0=PAST_LEN)
    xp, xs, ret_s = _mlp_ret(xp.reshape(B * L, D), xs, mlp_norm[2], mlp_w1, mlp_w2, final_norm,
                             qt, kt, v, ov, gate, ret_gn_g[0], ret_gn_b[0], ret_woutb, state_ret[0], li=2)
    xp = xp.reshape(B, L, D)

    lru_winb, lru_woutb = bf(lru_w_in[0]), bf(lru_w_out[0])
    waxb = bf(jnp.concatenate([lru_w_a[0], lru_w_x[0]], axis=-1))
    xp, conv_p, lru_p = _lru_prompt(xp, lru_norm[0], lru_winb, lru_conv_w[0], lru_conv_b[0], waxb,
                                    lru_b_a[0], lru_b_x[0], lru_lam[0], lru_woutb)
    xs, conv_s, lru_s = _lru_sample(xs, jnp.swapaxes(state_conv[0], 0, 1), state_lru[0],
                                    lru_norm[0], lru_winb, lru_conv_w[0], lru_conv_b[0], waxb,
                                    lru_b_a[0], lru_b_x[0], lru_lam[0], lru_woutb, pos0=PAST_LEN)
    yp, ys = mlp(xp, xs, 3, final=True)

    return (yp, ys.reshape(NS, 1, D),
            pool_p[None], jnp.swapaxes(pool_s, 0, 1)[None],
            v_s.reshape(1, NS, 1, D),
            ret_p[None], ret_s[None],
            conv_p[None], jnp.swapaxes(conv_s, 0, 1)[None],
            lru_p[None], lru_s[None])
```

```python
import functools
import math

import jax
import jax.numpy as jnp
import numpy as np
from jax import lax
from jax.experimental import pallas as pl
from jax.experimental.pallas import tpu as pltpu

F32 = jnp.float32
BF16 = jnp.bfloat16

D = 1024
EPS = 1e-6
GN_EPS = 1e-5
PAST_LEN = 16384
POOL_WINDOWS = (2, 4, 8, 16)
POOL_GROUP = D // len(POOL_WINDOWS)
POOL_BUF = max(POOL_WINDOWS) - 1
CHUNK = 128
GM_GROUPS = 8
HEADS = 8
DK = D // HEADS
DV = 2 * D // HEADS
QK = HEADS * DK
VW = HEADS * DV
ROPE_BASE = 10000.0
CONV_W = 4
LRU_C = 8.0
D_FF = 4 * D

MIB = 1024 * 1024
SUBLANES = 8
LANES = 128


def _params(vmem_mib, n_grid):
    return pltpu.CompilerParams(
        dimension_semantics=("arbitrary",) * n_grid,
        vmem_limit_bytes=vmem_mib * MIB,
    )


def _const(shape):
    zeros = (0,) * len(shape)
    return pl.BlockSpec(shape, lambda *_: zeros, pipeline_mode=pl.Buffered(1))


def _rms(x, g):
    ms = jnp.mean(x * x, axis=-1, keepdims=True)
    return x * lax.rsqrt(ms + EPS) * g


def _dot(a, b):
    return jnp.dot(a, b, preferred_element_type=F32)


def _interleave(*queues):
    pos = [0] * len(queues)
    while any(p < len(q) for p, q in zip(pos, queues)):
        _, i = min((pos[i] / len(q), i) for i, q in enumerate(queues) if pos[i] < len(q))
        queues[i][pos[i]]()
        pos[i] += 1


def _softplus(x):
    return jnp.maximum(x, 0.0) + jnp.log1p(jnp.exp(-jnp.abs(x)))


GELU_K = math.sqrt(2.0 / math.pi)


def _gelu(x):
    half = 0.5 * x
    return half + half * jnp.tanh(x * (GELU_K + (GELU_K * 0.044715) * (x * x)))


MLP_FC = 512


def _weight_staging(piece):
    return [pltpu.VMEM((2, D, piece), F32), pltpu.VMEM((2, piece, D), F32),
            pltpu.SemaphoreType.DMA((2,)), pltpu.SemaphoreType.DMA((2,))]


def _mlp_thunks(read_x, write_out, g_ref, fg_ref, w1b, w2b, final, fetch_w1=None, fetch_w2=None):
    st = {}

    def head():
        x = read_x()
        st["h"] = _rms(x, g_ref[...]).astype(BF16)
        st["acc"] = x

    def up(c):
        def run():
            if fetch_w1 is not None:
                fetch_w1(c)
            a = _dot(st["h"], w1b[:, c * MLP_FC:(c + 1) * MLP_FC])
            st["a"] = jnp.square(jnp.maximum(a, 0.0)).astype(BF16)
        return run

    def down(c):
        def run():
            if fetch_w2 is not None:
                fetch_w2(c)
            st["acc"] = st["acc"] + _dot(st["a"], w2b[c * MLP_FC:(c + 1) * MLP_FC, :])
        return run

    def tail():
        acc = st["acc"]
        write_out(_rms(acc, fg_ref[...]) if final else acc)

    return [head] + [f(c) for c in range(D_FF // MLP_FC) for f in (up, down)] + [tail]


def _run(thunks):
    for thunk in thunks:
        thunk()


def _store(ref):
    def write(v):
        ref[...] = v
    return write


def _weight_stream(w1_hbm, w2_hbm, li, w1b, w2b, stg1, stg2, sem1, sem2):
    W_FC = stg1.shape[2]
    assert stg2.shape[1] == W_FC and MLP_FC % W_FC == 0
    npc = D_FF // W_FC
    per = MLP_FC // W_FC

    def w1_copy(p):
        return pltpu.make_async_copy(w1_hbm.at[li, :, pl.ds(p * W_FC, W_FC)], stg1.at[p % 2], sem1.at[p % 2])

    def w2_copy(p):
        return pltpu.make_async_copy(w2_hbm.at[li, pl.ds(p * W_FC, W_FC), :], stg2.at[p % 2], sem2.at[p % 2])

    def prime():
        for p in range(min(2, npc)):
            w1_copy(p).start()
            w2_copy(p).start()

    def fetch_w1(c):
        for p in range(c * per, (c + 1) * per):
            w1_copy(p).wait()
            w1b[:, p * W_FC:(p + 1) * W_FC] = stg1[p % 2].astype(BF16)
            if p + 2 < npc:
                w1_copy(p + 2).start()

    def fetch_w2(c):
        for p in range(c * per, (c + 1) * per):
            w2_copy(p).wait()
            w2b[p * W_FC:(p + 1) * W_FC, :] = stg2[p % 2].astype(BF16)
            if p + 2 < npc:
                w2_copy(p + 2).start()

    return prime, fetch_w1, fetch_w2


def _mlp_body(xp_ref, xs_ref, g_ref, fg_ref, w1_hbm, w2_hbm, op_ref, os_ref,
              w1b, w2b, stg1, stg2, sem1, sem2, *, li, final, n_p):
    i = pl.program_id(0)
    prime, fetch_w1, fetch_w2 = _weight_stream(w1_hbm, w2_hbm, li, w1b, w2b, stg1, stg2, sem1, sem2)

    @pl.when(i == 0)
    def _():
        prime()
        _run(_mlp_thunks(lambda: xp_ref[...], _store(op_ref), g_ref, fg_ref, w1b, w2b, final,
                         fetch_w1, fetch_w2))

    @pl.when((i > 0) & (i < n_p))
    def _():
        _run(_mlp_thunks(lambda: xp_ref[...], _store(op_ref), g_ref, fg_ref, w1b, w2b, final))

    @pl.when(i == n_p)
    def _():
        _run(_mlp_thunks(lambda: xs_ref[...], _store(os_ref), g_ref, fg_ref, w1b, w2b, final))


def _mlp(xp2d, xs2d, g, w1, w2, fg, *, li, final, tm=512):
    n, ns = xp2d.shape[0], xs2d.shape[0]
    n_p = n // tm
    row = pl.BlockSpec((tm, D), lambda i: (jnp.minimum(i, n_p - 1), 0))
    srow = pl.BlockSpec((ns, D), lambda i: (0, 0))
    hbm = pl.BlockSpec(memory_space=pl.ANY)
    return pl.pallas_call(
        functools.partial(_mlp_body, li=li, final=final, n_p=n_p),
        grid=(n_p + 1,),
        in_specs=[row, _const((ns, D)), _const((1, D)), _const((1, D)), hbm, hbm],
        out_specs=[row, srow],
        out_shape=[jax.ShapeDtypeStruct((n, D), F32), jax.ShapeDtypeStruct((ns, D), F32)],
        scratch_shapes=[pltpu.VMEM((D, D_FF), BF16), pltpu.VMEM((D_FF, D), BF16)] + _weight_staging(MLP_FC),
        compiler_params=_params(48, 1),
        name="mlp",
    )(xp2d, xs2d, g.reshape(1, D), fg.reshape(1, D), w1, w2)


POOL_HALO = 32


def _pool_thunks(x_ref, g_ref, w_ref, sc_ref, write_out, buf_ref, ext_ref, *, T, t):
    st = {}

    def head():
        x = x_ref[...]
        h = _rms(x, g_ref[...])
        halo = jnp.where(t == 0, 0.0, ext_ref[...])
        st["x"], st["h"] = x, h
        st["ext"] = jnp.concatenate([halo, h], axis=0)
        st["pos"] = t * T + lax.broadcasted_iota(jnp.int32, (T, 1), 0)
        st["y"] = [None] * len(POOL_WINDOWS)

    def group(gi, w):
        def run():
            c0, c1 = gi * POOL_GROUP, (gi + 1) * POOL_GROUP
            levels = w.bit_length() - 1
            a = st["ext"][POOL_HALO - SUBLANES * levels:, c0:c1]
            for k in range(levels):
                shift = 1 << k
                n = a.shape[0] - SUBLANES
                a = a[SUBLANES:, :] + a[SUBLANES - shift:SUBLANES - shift + n, :]
            hg = st["h"][:, c0:c1]
            cnt = jnp.minimum(st["pos"] + 1, w).astype(F32)
            d = a / cnt - hg
            st["y"][gi] = _dot(d.astype(BF16), w_ref[gi])
        return run

    def tail():
        y = jnp.concatenate(st["y"], axis=-1) * sc_ref[...]
        write_out(st["x"] + y)
        ext_ref[...] = st["h"][T - POOL_HALO:, :]
        buf_ref[...] = st["h"][T - POOL_BUF:, :]

    return [head] + [group(gi, w) for gi, w in enumerate(POOL_WINDOWS)] + [tail]


def _pool_mlp_body(x_ref, xs_ref, pg_ref, pw_ref, psc_ref, g_ref, fg_ref, w1_hbm, w2_hbm,
                   o_ref, os_ref, buf_ref, ext_ref, x1_scr, w1b, w2b, stg1, stg2, sem1, sem2,
                   *, T, tiles_per_seq, n, li):
    s = pl.program_id(0)
    t = lax.rem(s, tiles_per_seq)
    slot = lax.rem(s, 2)
    prime, fetch_w1, fetch_w2 = _weight_stream(w1_hbm, w2_hbm, li, w1b, w2b, stg1, stg2, sem1, sem2)

    def write_x1(v):
        x1_scr[slot] = v

    def mixer():
        return _pool_thunks(x_ref, pg_ref, pw_ref, psc_ref, write_x1, buf_ref, ext_ref, T=T, t=t)

    def mlp(*fetch):
        return _mlp_thunks(lambda: x1_scr[1 - slot], _store(o_ref), g_ref, fg_ref, w1b, w2b, False, *fetch)

    @pl.when(s == 0)
    def _():
        ext_ref[0:POOL_HALO, :] = jnp.zeros((POOL_HALO, D), F32)
        prime()
        _run(mixer())

    @pl.when(s == 1)
    def _():
        _interleave(mlp(fetch_w1, fetch_w2), mixer())

    @pl.when((s > 1) & (s < n))
    def _():
        _interleave(mlp(), mixer())

    @pl.when(s == n)
    def _():
        _run(mlp())
        _run(_mlp_thunks(lambda: xs_ref[...], _store(os_ref), g_ref, fg_ref, w1b, w2b, False))


def _pool_mlp(x, xs2d, pg, pwb, psc, g, w1, w2, fg, *, li, T=512):
    B, L, _ = x.shape
    ns = xs2d.shape[0]
    tps = L // T
    n = B * tps
    tile = lambda s: (jnp.minimum(s, n - 1) // tps, lax.rem(jnp.minimum(s, n - 1), tps), 0)
    prev = lambda s: (jnp.maximum(s - 1, 0) // tps, lax.rem(jnp.maximum(s - 1, 0), tps), 0)
    hbm = pl.BlockSpec(memory_space=pl.ANY)
    return pl.pallas_call(
        functools.partial(_pool_mlp_body, T=T, tiles_per_seq=tps, n=n, li=li),
        grid=(n + 1,),
        in_specs=[pl.BlockSpec((None, T, D), tile), _const((ns, D)), _const((1, D)),
                  _const((len(POOL_WINDOWS), POOL_GROUP, POOL_GROUP)), _const((1, D)),
                  _const((1, D)), _const((1, D)), hbm, hbm],
        out_specs=[pl.BlockSpec((None, T, D), prev), pl.BlockSpec((ns, D), lambda s: (0, 0)),
                   pl.BlockSpec((None, POOL_BUF, D), lambda s: (jnp.minimum(s, n - 1) // tps, 0, 0))],
        out_shape=[jax.ShapeDtypeStruct((B, L, D), F32), jax.ShapeDtypeStruct((ns, D), F32),
                   jax.ShapeDtypeStruct((B, POOL_BUF, D), F32)],
        scratch_shapes=[pltpu.VMEM((POOL_HALO, D), F32), pltpu.VMEM((2, T, D), F32),
                        pltpu.VMEM((D, D_FF), BF16), pltpu.VMEM((D_FF, D), BF16)]
        + _weight_staging(MLP_FC // 2),
        compiler_params=_params(56, 1),
        name="pool_mlp",
    )(x, xs2d, pg.reshape(1, D), pwb, psc.reshape(1, D), g.reshape(1, D), fg.reshape(1, D), w1, w2)


def _pool_sample_body(x_ref, buf_ref, g_ref, w_ref, sc_ref, o_ref, nbuf_ref, *, pos0):
    x = x_ref[...]
    h = _rms(x, g_ref[...])
    ys = []
    for gi, w in enumerate(POOL_WINDOWS):
        c0, c1 = gi * POOL_GROUP, (gi + 1) * POOL_GROUP
        hg = h[:, c0:c1]
        s = hg
        for k in range(1, w):
            s = s + buf_ref[POOL_BUF - k, :, c0:c1]
        cnt = float(min(pos0 + 1, w))
        d = s / cnt - hg
        ys.append(_dot(d.astype(BF16), w_ref[gi]))
    y = jnp.concatenate(ys, axis=-1) * sc_ref[...]
    o_ref[...] = x + y
    nbuf_ref[0:POOL_BUF - 1] = buf_ref[1:POOL_BUF]
    nbuf_ref[POOL_BUF - 1] = h


def _pool_sample(x2d, buf, g, wb, sc, *, pos0, bt=32):
    n = x2d.shape[0]
    row = pl.BlockSpec((bt, D), lambda i: (i, 0))
    brow = pl.BlockSpec((POOL_BUF, bt, D), lambda i: (0, i, 0))
    return pl.pallas_call(
        functools.partial(_pool_sample_body, pos0=pos0),
        grid=(n // bt,),
        in_specs=[row, brow, _const((1, D)), _const((len(POOL_WINDOWS), POOL_GROUP, POOL_GROUP)), _const((1, D))],
        out_specs=[row, brow],
        out_shape=[jax.ShapeDtypeStruct((n, D), F32), jax.ShapeDtypeStruct((POOL_BUF, n, D), F32)],
        compiler_params=_params(32, 1),
        name="pool_sample",
    )(x2d, buf, g.reshape(1, D), wb, sc.reshape(1, D))


def _gmlp_front(x, g_ref, win_ref, bin_ref, lng_ref, lnb_ref):
    h = _rms(x, g_ref[...]).astype(BF16)
    z = _gelu(_dot(h, win_ref[...]) + bin_ref[...])
    u, v = z[:, :D], z[:, D:]
    mu = jnp.mean(v, axis=-1, keepdims=True)
    vc = v - mu
    var = jnp.mean(jnp.square(vc), axis=-1, keepdims=True)
    vn = vc * lax.rsqrt(var + EPS) * lng_ref[...] + lnb_ref[...]
    return u, vn


def _gmlp_prompt_body(x_ref, g_ref, win_ref, bin_ref, lng_ref, lnb_ref, ws_ref, bs_ref, wout_ref,
                      o_ref, *, T):
    ti = lax.broadcasted_iota(jnp.int32, (CHUNK, CHUNK), 0)
    si = lax.broadcasted_iota(jnp.int32, (CHUNK, CHUNK), 1)
    causal = ti >= si
    wsm = [jnp.where(causal, ws_ref[gi], 0.0).astype(BF16) for gi in range(GM_GROUPS)]
    x = x_ref[...]
    u, vn = _gmlp_front(x, g_ref, win_ref, bin_ref, lng_ref, lnb_ref)
    vnb = vn.astype(BF16)
    rows = []
    for n in range(T // CHUNK):
        r0, r1 = n * CHUNK, (n + 1) * CHUNK
        cols = [_dot(wsm[gi], vnb[r0:r1, gi * CHUNK:(gi + 1) * CHUNK]) for gi in range(GM_GROUPS)]
        rows.append(jnp.concatenate(cols, axis=-1) + bs_ref[...])
    mixed = jnp.concatenate(rows, axis=0)
    y = _dot((u * mixed).astype(BF16), wout_ref[...])
    o_ref[...] = x + y


def _gmlp_prompt(x, g, winb, b_in, ln_g, ln_b, w_s, bs_full, woutb, *, T=1024):
    B, L, _ = x.shape
    blk = pl.BlockSpec((None, T, D), lambda b, t: (b, t, 0))
    return pl.pallas_call(
        functools.partial(_gmlp_prompt_body, T=T),
        grid=(B, L // T),
        in_specs=[blk, _const((1, D)), _const((D, 2 * D)), _const((1, 2 * D)), _const((1, D)), _const((1, D)),
                  _const((GM_GROUPS, CHUNK, CHUNK)), _const((CHUNK, D)), _const((D, D))],
        out_specs=blk,
        out_shape=jax.ShapeDtypeStruct((B, L, D), F32),
        compiler_params=_params(48, 2),
        name="gmlp_prompt",
    )(x, g.reshape(1, D), winb, b_in.reshape(1, 2 * D), ln_g.reshape(1, D), ln_b.reshape(1, D),
      w_s, bs_full, woutb)


def _gmlp_sample_body(x_ref, g_ref, win_ref, bin_ref, lng_ref, lnb_ref, sw_ref, sb_ref, wout_ref,
                      o_ref, vn_ref):
    x = x_ref[...]
    u, vn = _gmlp_front(x, g_ref, win_ref, bin_ref, lng_ref, lnb_ref)
    mixed = vn * sw_ref[...] + sb_ref[...]
    y = _dot((u * mixed).astype(BF16), wout_ref[...])
    o_ref[...] = x + y
    vn_ref[...] = vn


def _gmlp_sample(x2d, g, winb, b_in, ln_g, ln_b, sw, sb, woutb):
    n = x2d.shape[0]
    full = _const((n, D))
    return pl.pallas_call(
        _gmlp_sample_body,
        grid=(1,),
        in_specs=[full, _const((1, D)), _const((D, 2 * D)), _const((1, 2 * D)), _const((1, D)), _const((1, D)),
                  _const((1, D)), _const((1, D)), _const((D, D))],
        out_specs=[pl.BlockSpec((n, D), lambda i: (0, 0)), pl.BlockSpec((n, D), lambda i: (0, 0))],
        out_shape=[jax.ShapeDtypeStruct((n, D), F32), jax.ShapeDtypeStruct((n, D), F32)],
        compiler_params=_params(32, 1),
        name="gmlp_sample",
    )(x2d, g.reshape(1, D), winb, b_in.reshape(1, 2 * D), ln_g.reshape(1, D), ln_b.reshape(1, D),
      sw, sb, woutb)


def _ret_constants(C):
    log_gamma = np.log1p(-np.exp2(-5.0 - np.arange(HEADS, dtype=np.float64)))
    idx = np.arange(C, dtype=np.float64)
    diff = idx[:, None] - idx[None, :]
    dmask = np.where(diff[None] >= 0, np.exp(log_gamma[:, None, None] * np.maximum(diff, 0.0)[None]), 0.0)
    q_dec = np.exp(log_gamma[:, None] * (idx + 1.0))
    k_dec = np.exp(log_gamma[:, None] * (C - 1.0 - idx))
    chunk_dec = np.exp(log_gamma * C)
    return dmask.astype(np.float32), q_dec.astype(np.float32), k_dec.astype(np.float32), chunk_dec


def _rope_tables(pos):
    half = DK // 2
    freqs = np.exp(-math.log(ROPE_BASE) * np.arange(half, dtype=np.float64) / half)
    ang = np.asarray(pos, dtype=np.float64)[:, None] * freqs[None]
    cos = np.concatenate([np.cos(ang), np.cos(ang)], axis=-1)
    sin = np.concatenate([-np.sin(ang), np.sin(ang)], axis=-1)
    return cos, sin


def _rope(t, cos, sin):
    return t * cos + pltpu.roll(t, DK // 2, axis=1) * sin


def _group_norm_gate(o, gate, gng, gnb):
    mu = jnp.mean(o, axis=-1, keepdims=True)
    oc = o - mu
    var = jnp.mean(jnp.square(oc), axis=-1, keepdims=True)
    on = oc * lax.rsqrt(var + GN_EPS) * gng + gnb
    return jax.nn.silu(gate) * on


RET_GROUP_HEADS = 2


def _ret_prompt_body(x_ref, g_ref, win_ref, cq_ref, sq_ref, ck_ref, sk_ref, dm_ref, qd_ref, kd_ref,
                     gng_ref, gnb_ref, wout_ref, o_ref, s_out_ref, gated_scr, s_scr,
                     *, T, chunk_dec):
    t = pl.program_id(1)

    @pl.when(t == 0)
    def _():
        s_scr[...] = jnp.zeros_like(s_scr)

    x = x_ref[...]
    h = _rms(x, g_ref[...]).astype(BF16)
    cq, sq, ck, sk = cq_ref[...], sq_ref[...], ck_ref[...], sk_ref[...]
    gh = RET_GROUP_HEADS
    n_pairs = HEADS // gh

    def projection(pair, dst):
        def piece(name, lo, width):
            def run():
                dst[name] = _dot(h, win_ref[:, lo:lo + width])
            return run
        v0, g0 = 2 * QK + gh * DV * pair, 2 * QK + VW + gh * DV * pair
        return ([piece("q", gh * DK * pair, gh * DK), piece("k", QK + gh * DK * pair, gh * DK)]
                + [piece("v%d" % sub, v0 + sub * DV, DV) for sub in range(gh)]
                + [piece("g%d" % sub, g0 + sub * DV, DV) for sub in range(gh)])

    def chunk_work(pair, src):
        state = {}

        def prep(sub):
            def run():
                q = _rope(src["q"][:, sub * DK:(sub + 1) * DK], cq, sq)
                k = _rope(src["k"][:, sub * DK:(sub + 1) * DK], ck, sk)
                state[sub] = dict(q=q, k=k, qb=q.astype(BF16), kb=k.astype(BF16), s=s_scr[gh * pair + sub])
            return run

        def chunk(sub, c):
            def run():
                hd = gh * pair + sub
                st = state[sub]
                rows = slice(c * CHUNK, (c + 1) * CHUNK)
                vb = src["v%d" % sub][rows, :].astype(BF16)
                scores = lax.dot_general(st["qb"][rows], st["kb"][rows], (((1,), (1,)), ((), ())),
                                         preferred_element_type=F32) * dm_ref[hd]
                o = (_dot(scores.astype(BF16), vb)
                     + _dot((st["q"][rows] * qd_ref[hd]).astype(BF16), st["s"].astype(BF16)))
                kt = (st["k"][rows] * kd_ref[hd]).T.astype(BF16)
                st["s"] = st["s"] * chunk_dec[hd] + _dot(kt, vb)
                gated = _group_norm_gate(o, src["g%d" % sub][rows, :], gng_ref[:, hd * DV:(hd + 1) * DV],
                                         gnb_ref[:, hd * DV:(hd + 1) * DV])
                gated_scr[rows, hd * DV:(hd + 1) * DV] = gated.astype(BF16)
            return run

        def finish(sub):
            def run():
                s_scr[gh * pair + sub] = state[sub]["s"]
            return run

        return ([prep(sub) for sub in range(gh)]
                + [chunk(sub, c) for c in range(T // CHUNK) for sub in range(gh)]
                + [finish(sub) for sub in range(gh)])

    acc = [x]

    def out_projection(pair):
        def run():
            cols = slice(pair * gh * DV, (pair + 1) * gh * DV)
            acc[0] = acc[0] + _dot(gated_scr[:, cols], wout_ref[cols, :])
        return run

    cur = {}
    for run in projection(0, cur):
        run()
    for pair in range(n_pairs):
        nxt = {}
        if pair + 1 < n_pairs:
            matmul_queue = projection(pair + 1, nxt)
        else:
            matmul_queue = [out_projection(p) for p in range(n_pairs - 1)]
        _interleave(chunk_work(pair, cur), matmul_queue)
        cur = nxt
    out_projection(n_pairs - 1)()
    o_ref[...] = acc[0]
    s_out_ref[...] = s_scr[...]


def _ret_prompt(x, g, winb, gn_g, gn_b, woutb, *, T=512):
    B, L, _ = x.shape
    dmask, q_dec, k_dec, chunk_dec = _ret_constants(CHUNK)
    cos, sin = _rope_tables(np.arange(L))
    scale = DK ** -0.5
    cq, sq = jnp.asarray(cos, F32), jnp.asarray(sin, F32)
    ck, sk = jnp.asarray(cos * scale, F32), jnp.asarray(sin * scale, F32)
    qd = jnp.asarray(np.broadcast_to(q_dec[:, :, None], (HEADS, CHUNK, DK)))
    kd = jnp.asarray(np.broadcast_to(k_dec[:, :, None], (HEADS, CHUNK, DK)))
    blk = pl.BlockSpec((None, T, D), lambda b, t: (b, t, 0))
    tab = pl.BlockSpec((T, DK), lambda b, t: (t, 0))
    hcc = _const((HEADS, CHUNK, CHUNK))
    return pl.pallas_call(
        functools.partial(_ret_prompt_body, T=T, chunk_dec=tuple(float(c) for c in chunk_dec)),
        grid=(B, L // T),
        in_specs=[blk, _const((1, D)), _const((D, 2 * QK + 2 * VW)), tab, tab, tab, tab,
                  hcc, hcc, hcc, _const((1, VW)), _const((1, VW)), _const((VW, D))],
        out_specs=[blk, pl.BlockSpec((None, HEADS, DK, DV), lambda b, t: (b, 0, 0, 0))],
        out_shape=[jax.ShapeDtypeStruct((B, L, D), F32), jax.ShapeDtypeStruct((B, HEADS, DK, DV), F32)],
        scratch_shapes=[pltpu.VMEM((T, VW), BF16), pltpu.VMEM((HEADS, DK, DV), F32)],
        compiler_params=_params(56, 2),
        name="ret_prompt",
    )(x, g.reshape(1, D), winb, cq, sq, ck, sk, jnp.asarray(dmask), qd, kd,
      gn_g.reshape(1, VW), gn_b.reshape(1, VW), woutb)


def _ret_proj_sample_body(x_ref, g_ref, win_ref, cq_ref, sq_ref, ck_ref, sk_ref,
                          qt_ref, kt_ref, v_ref, gate_ref, ov_ref):
    h = _rms(x_ref[...], g_ref[...]).astype(BF16)
    p = _dot(h, win_ref[...])
    for hd in range(HEADS):
        q = _rope(p[:, hd * DK:(hd + 1) * DK], cq_ref[...], sq_ref[...])
        k = _rope(p[:, QK + hd * DK:QK + (hd + 1) * DK], ck_ref[...], sk_ref[...])
        v = p[:, 2 * QK + hd * DV:2 * QK + (hd + 1) * DV]
        qt_ref[hd] = q.T
        kt_ref[hd] = k.T
        ov_ref[:, hd * DV:(hd + 1) * DV] = jnp.sum(q * k, axis=-1, keepdims=True) * v
    v_ref[...] = p[:, 2 * QK:2 * QK + VW]
    gate_ref[...] = p[:, 2 * QK + VW:]


def _ret_proj_sample(x2d, g, winb, *, pos0):
    n = x2d.shape[0]
    cos, sin = _rope_tables(np.array([pos0]))
    scale = DK ** -0.5
    tabs = [jnp.asarray(a, F32) for a in (cos, sin, cos * scale, sin * scale)]
    row = pl.BlockSpec((n, VW), lambda i: (0, 0))
    tr = pl.BlockSpec((HEADS, DK, n), lambda i: (0, 0, 0))
    return pl.pallas_call(
        _ret_proj_sample_body,
        grid=(1,),
        in_specs=[_const((n, D)), _const((1, D)), _const((D, 2 * QK + 2 * VW))] + [_const((1, DK))] * 4,
        out_specs=[tr, tr, row, row, row],
        out_shape=[jax.ShapeDtypeStruct((HEADS, DK, n), F32)] * 2 + [jax.ShapeDtypeStruct((n, VW), F32)] * 3,
        compiler_params=_params(40, 1),
        name="ret_proj_sample",
    )(x2d, g.reshape(1, D), winb, *tabs)


def _mlp_ret_body(xp_ref, xs_ref, g_ref, fg_ref, w1_hbm, w2_hbm,
                  qt_ref, kt_ref, v_ref, ov_ref, gate_ref, gng_ref, gnb_ref, rwout_ref, s_hbm,
                  op_ref, os_ref, snew_hbm,
                  w1b, w2b, stg1, stg2, sem1, sem2, s_in, s_out, sem_in, sem_out, o_scr,
                  *, li, n_p, ns, gamma):
    i = pl.program_id(0)
    spb = ns // n_p
    prime, fetch_w1, fetch_w2 = _weight_stream(w1_hbm, w2_hbm, li, w1b, w2b, stg1, stg2, sem1, sem2)

    def load(c, slot):
        return pltpu.make_async_copy(s_hbm.at[c], s_in.at[slot], sem_in.at[slot])

    def store(c, slot):
        return pltpu.make_async_copy(s_out.at[slot], snew_hbm.at[c], sem_out.at[slot])

    def state_update(j):
        def run():
            c = i * spb + j
            slot = j % 2
            load(c, slot).wait()
            store(c, slot).wait()
            lane = lax.broadcasted_iota(jnp.int32, (DK, ns), 1) == c
            for hd in range(HEADS):
                cols = slice(hd * DV, (hd + 1) * DV)
                s_old = s_in[slot, hd]
                qcol = jnp.sum(jnp.where(lane, qt_ref[hd], 0.0), axis=1, keepdims=True)
                kcol = jnp.sum(jnp.where(lane, kt_ref[hd], 0.0), axis=1, keepdims=True)
                v = v_ref[pl.ds(c, 1), cols]
                o_scr[pl.ds(c, 1), cols] = jnp.sum(s_old * (qcol * gamma[hd]), axis=0, keepdims=True)
                s_out[slot, hd] = s_old * gamma[hd] + kcol * v
            store(c, slot).start()
            load(jnp.minimum(c + 2, ns - 1), slot).start()
        return run

    def states():
        return [state_update(j) for j in range(spb)]

    @pl.when(i == 0)
    def _():
        prime()
        s_out[...] = jnp.zeros_like(s_out)
        for slot in range(2):
            load(slot, slot).start()
            store(slot, slot).start()
        _interleave(_mlp_thunks(lambda: xp_ref[...], _store(op_ref), g_ref, fg_ref, w1b, w2b, False,
                                fetch_w1, fetch_w2), states())

    @pl.when((i > 0) & (i < n_p))
    def _():
        _interleave(states(), _mlp_thunks(lambda: xp_ref[...], _store(op_ref), g_ref, fg_ref, w1b, w2b, False))

    @pl.when(i == n_p)
    def _():
        for slot in range(2):
            load(ns - 1, slot).wait()
            store(ns - 1, slot).wait()
        parts = []
        for hd in range(HEADS):
            cols = slice(hd * DV, (hd + 1) * DV)
            o = ov_ref[:, cols] + o_scr[:, cols]
            parts.append(_group_norm_gate(o, gate_ref[:, cols], gng_ref[:, cols], gnb_ref[:, cols]))
        gated = jnp.concatenate(parts, axis=-1).astype(BF16)
        xs1 = xs_ref[...] + _dot(gated, rwout_ref[...])
        _run(_mlp_thunks(lambda: xs1, _store(os_ref), g_ref, fg_ref, w1b, w2b, False))


def _mlp_ret(xp2d, xs2d, g, w1, w2, fg, qt, kt, v, ov, gate, gn_g, gn_b, rwoutb, s, *, li, tm=512):
    n, ns = xp2d.shape[0], xs2d.shape[0]
    n_p = n // tm
    _, _, _, chunk_dec = _ret_constants(1)
    row = pl.BlockSpec((tm, D), lambda i: (jnp.minimum(i, n_p - 1), 0))
    srow = pl.BlockSpec((ns, D), lambda i: (0, 0))
    hbm = pl.BlockSpec(memory_space=pl.ANY)
    wide = _const((ns, VW))
    return pl.pallas_call(
        functools.partial(_mlp_ret_body, li=li, n_p=n_p, ns=ns, gamma=tuple(float(c) for c in chunk_dec)),
        grid=(n_p + 1,),
        in_specs=[row, _const((ns, D)), _const((1, D)), _const((1, D)), hbm, hbm,
                  _const((HEADS, DK, ns)), _const((HEADS, DK, ns)), wide, wide, wide,
                  _const((1, VW)), _const((1, VW)), _const((VW, D)), hbm],
        out_specs=[row, srow, hbm],
        out_shape=[jax.ShapeDtypeStruct((n, D), F32), jax.ShapeDtypeStruct((ns, D), F32),
                   jax.ShapeDtypeStruct(s.shape, F32)],
        scratch_shapes=[pltpu.VMEM((D, D_FF), BF16), pltpu.VMEM((D_FF, D), BF16)] + _weight_staging(MLP_FC)
        + [pltpu.VMEM((2, HEADS, DK, DV), F32), pltpu.VMEM((2, HEADS, DK, DV), F32),
           pltpu.SemaphoreType.DMA((2,)), pltpu.SemaphoreType.DMA((2,)),
           pltpu.VMEM((ns, VW), F32)],
        compiler_params=_params(58, 1),
        name="mlp_ret",
    )(xp2d, xs2d, g.reshape(1, D), fg.reshape(1, D), w1, w2, qt, kt, v, ov, gate,
      gn_g.reshape(1, VW), gn_b.reshape(1, VW), rwoutb, s)


def _lru_gates(xc, wax_ref, ba, bx, lam, heads):
    xcb = xc.astype(BF16)
    rs, is_ = [], []
    for n, hd in enumerate(heads):
        ri = _dot(xcb[:, n * DK:(n + 1) * DK], wax_ref[hd])
        rs.append(ri[:, :DK])
        is_.append(ri[:, DK:])
    r = jax.nn.sigmoid(jnp.concatenate(rs, axis=-1) + ba)
    i = jax.nn.sigmoid(jnp.concatenate(is_, axis=-1) + bx)
    log_a = r * (-LRU_C * _softplus(-lam))
    a = jnp.exp(log_a)
    mult = jnp.sqrt(-jnp.tanh(log_a) * (a * a + 1.0))
    return a, mult, i


LRU_GROUP_HEADS = 4


def _lru_prompt_body(x_ref, g_ref, win_ref, cw_ref, cb_ref, wax_ref, ba_ref, bx_ref, lam_ref, wout_ref,
                     o_ref, conv_ref, hlast_ref, xb_scr, hs_scr, carry_scr, h_scr, *, NB, T):
    t = pl.program_id(0)
    N = NB * T
    PS = T + 1
    halo = (CONV_W - 1) * NB
    gw = LRU_GROUP_HEADS * DK
    gblk = gw // LANES

    @pl.when(t == 0)
    def _():
        carry_scr[...] = jnp.zeros_like(carry_scr)
        h_scr[...] = jnp.zeros_like(h_scr)

    x = x_ref[...].reshape(N, D)
    h = _rms(x, g_ref[...]).astype(BF16)
    row = lax.broadcasted_iota(jnp.int32, (N, 1), 0)
    pos = t * T + lax.shift_right_logical(row, NB.bit_length() - 1)
    out = x
    def project(grp):
        c0, c1 = grp * gw, (grp + 1) * gw
        return _dot(h, win_ref[:, c0:c1]), _dot(h, win_ref[:, D + c0:D + c1])

    ngrp = D // gw
    nxt = project(0)
    for grp in range(ngrp):
        c0, c1 = grp * gw, (grp + 1) * gw
        blks = range(grp * gblk, (grp + 1) * gblk)
        zg, xb = nxt
        if grp + 1 < ngrp:
            nxt = project(grp + 1)
        gate = _gelu(zg)
        for n, j in enumerate(blks):
            for b in range(NB):
                xb_scr[j, b * PS:b * PS + T, :] = xb[b * T:(b + 1) * T, n * LANES:(n + 1) * LANES]
        ext = jnp.concatenate(
            [carry_scr[:, c0:c1]]
            + [jnp.concatenate([xb_scr[j, pl.ds(s, NB, stride=PS), :] for j in blks], axis=-1)
               for s in range(T)], axis=0)
        carry_scr[:, c0:c1] = ext[N:N + halo, :]
        acc = ext[0:N, :] * cw_ref[0:1, c0:c1]
        for j in range(1, CONV_W):
            acc = acc + ext[j * NB:j * NB + N, :] * cw_ref[j:j + 1, c0:c1]
        xc = cb_ref[:, c0:c1] + acc
        heads = range(grp * LRU_GROUP_HEADS, (grp + 1) * LRU_GROUP_HEADS)
        a, mult, i = _lru_gates(xc, wax_ref, ba_ref[:, c0:c1], bx_ref[:, c0:c1], lam_ref[:, c0:c1], heads)
        bvec = jnp.where(pos == 0, 1.0, mult) * (i * xc)
        hcur = h_scr[:, c0:c1]
        for s in range(T):
            hcur = a[s * NB:(s + 1) * NB, :] * hcur + bvec[s * NB:(s + 1) * NB, :]
            for n, j in enumerate(blks):
                hs_scr[j, pl.ds(s, NB, stride=PS), :] = hcur[:, n * LANES:(n + 1) * LANES]
        h_scr[:, c0:c1] = hcur
        hs = jnp.concatenate(
            [jnp.concatenate([hs_scr[j, b * PS:b * PS + T, :] for b in range(NB)], axis=0) for j in blks],
            axis=-1)
        out = out + _dot((hs * gate).astype(BF16), wout_ref[c0:c1, :])
    o_ref[...] = out.reshape(NB, T, D)
    hlast_ref[...] = h_scr[...]

    @pl.when(t == pl.num_programs(0) - 1)
    def _():
        for j in range(CONV_W - 1):
            conv_ref[:, j, :] = carry_scr[j * NB:(j + 1) * NB, :]


def _lru_prompt(x, g, winb, conv_w, conv_b, waxb, b_a, b_x, lam, woutb, *, T=128):
    B, L, _ = x.shape
    blk = pl.BlockSpec((B, T, D), lambda t: (0, t, 0))
    vec = _const((1, D))
    return pl.pallas_call(
        functools.partial(_lru_prompt_body, NB=B, T=T),
        grid=(L // T,),
        in_specs=[blk, vec, _const((D, 2 * D)), _const((CONV_W, D)), vec, _const((HEADS, DK, 2 * DK)),
                  vec, vec, vec, _const((D, D))],
        out_specs=[blk, pl.BlockSpec((B, CONV_W - 1, D), lambda t: (0, 0, 0)),
                   pl.BlockSpec((B, D), lambda t: (0, 0))],
        out_shape=[jax.ShapeDtypeStruct((B, L, D), F32), jax.ShapeDtypeStruct((B, CONV_W - 1, D), F32),
                   jax.ShapeDtypeStruct((B, D), F32)],
        scratch_shapes=[pltpu.VMEM((D // LANES, B * (T + 1), LANES), F32)] * 2
        + [pltpu.VMEM(((CONV_W - 1) * B, D), F32), pltpu.VMEM((B, D), F32)],
        compiler_params=_params(48, 1),
        name="lru_prompt",
    )(x, g.reshape(1, D), winb, conv_w, conv_b.reshape(1, D), waxb, b_a.reshape(1, D), b_x.reshape(1, D),
      lam.reshape(1, D), woutb)


def _lru_sample_body(x_ref, cbuf_ref, h0_ref, g_ref, win_ref, cw_ref, cb_ref, wax_ref, ba_ref, bx_ref,
                     lam_ref, wout_ref, o_ref, nconv_ref, hnew_ref, *, pos0):
    x = x_ref[...]
    h = _rms(x, g_ref[...]).astype(BF16)
    z = _dot(h, win_ref[...])
    gate = _gelu(z[:, :D])
    xb = z[:, D:]
    cw = cw_ref[...]
    acc = cbuf_ref[0] * cw[0:1, :]
    for j in range(1, CONV_W - 1):
        acc = acc + cbuf_ref[j] * cw[j:j + 1, :]
    acc = acc + xb * cw[CONV_W - 1:CONV_W, :]
    xc = cb_ref[...] + acc
    a, mult, i = _lru_gates(xc, wax_ref, ba_ref[...], bx_ref[...], lam_ref[...], range(HEADS))
    if pos0 == 0:
        mult = jnp.ones_like(mult)
    hnew = a * h0_ref[...] + mult * (i * xc)
    hnew_ref[...] = hnew
    nconv_ref[0:CONV_W - 2] = cbuf_ref[1:CONV_W - 1]
    nconv_ref[CONV_W - 2] = xb
    o_ref[...] = x + _dot((hnew * gate).astype(BF16), wout_ref[...])


def _lru_sample(x2d, cbuf, h0, g, winb, conv_w, conv_b, waxb, b_a, b_x, lam, woutb, *, pos0):
    n = x2d.shape[0]
    vec = _const((1, D))
    cshape = (CONV_W - 1, n, D)
    out = pl.BlockSpec((n, D), lambda i: (0, 0))
    return pl.pallas_call(
        functools.partial(_lru_sample_body, pos0=pos0),
        grid=(1,),
        in_specs=[_const((n, D)), _const(cshape), _const((n, D)), vec, _const((D, 2 * D)), _const((CONV_W, D)),
                  vec, _const((HEADS, DK, 2 * DK)), vec, vec, vec, _const((D, D))],
        out_specs=[out, pl.BlockSpec(cshape, lambda i: (0, 0, 0)), out],
        out_shape=[jax.ShapeDtypeStruct((n, D), F32), jax.ShapeDtypeStruct(cshape, F32),
                   jax.ShapeDtypeStruct((n, D), F32)],
        compiler_params=_params(32, 1),
        name="lru_sample",
    )(x2d, cbuf, h0, g.reshape(1, D), winb, conv_w, conv_b.reshape(1, D), waxb, b_a.reshape(1, D),
      b_x.reshape(1, D), lam.reshape(1, D), woutb)


def kernel(x_prompt, x_sample, state_pool, state_ret, state_conv, state_lru, pool_norm, pool_w, pool_scale, gm_norm, gm_w_in, gm_b_in, gm_ln_g, gm_ln_b, gm_w_s, gm_b_s, gm_w_out, ret_norm, ret_w_in, ret_gn_g, ret_gn_b, ret_w_out, lru_norm, lru_w_in, lru_conv_w, lru_conv_b, lru_w_a, lru_b_a, lru_w_x, lru_b_x, lru_lam, lru_w_out, mlp_norm, mlp_w1, mlp_w2, final_norm):
    B, L, _ = x_prompt.shape
    NS = x_sample.shape[0]
    bf = lambda w: w.astype(BF16)

    def mlp(xp, xs, li, final=False):
        yp, ys = _mlp(xp.reshape(B * L, D), xs, mlp_norm[li], mlp_w1, mlp_w2, final_norm, li=li, final=final)
        return yp.reshape(B, L, D), ys

    pool_wb = bf(pool_w[0])
    xs, pool_s = _pool_sample(x_sample.reshape(NS, D), jnp.swapaxes(state_pool[0], 0, 1),
                              pool_norm[0], pool_wb, pool_scale[0], pos0=PAST_LEN)
    xp, xs, pool_p = _pool_mlp(x_prompt, xs, pool_norm[0], pool_wb, pool_scale[0], mlp_norm[0],
                               mlp_w1, mlp_w2, final_norm, li=0)

    gm_winb, gm_woutb = bf(gm_w_in[0]), bf(gm_w_out[0])
    bs_full = jnp.repeat(gm_b_s[0].T, CHUNK, axis=1)
    xp = _gmlp_prompt(xp, gm_norm[0], gm_winb, gm_b_in[0], gm_ln_g[0], gm_ln_b[0], gm_w_s[0], bs_full, gm_woutb)
    sw = jnp.repeat(gm_w_s[0][:, 0, 0], CHUNK).reshape(1, D)
    sb = jnp.repeat(gm_b_s[0][:, 0], CHUNK).reshape(1, D)
    xs, v_s = _gmlp_sample(xs, gm_norm[0], gm_winb, gm_b_in[0], gm_ln_g[0], gm_ln_b[0], sw, sb, gm_woutb)
    xp, xs = mlp(xp, xs, 1)

    ret_winb, ret_woutb = bf(ret_w_in[0]), bf(ret_w_out[0])
    xp, ret_p = _ret_prompt(xp, ret_norm[0], ret_winb, ret_gn_g[0], ret_gn_b[0], ret_woutb)
    qt, kt, v, gate, ov = _ret_proj_sample(xs, ret_norm[0], ret_winb, pos0=PAST_LEN)
    xp, xs, ret_s = _mlp_ret(xp.reshape(B * L, D), xs, mlp_norm[2], mlp_w1, mlp_w2, final_norm,
                             qt, kt, v, ov, gate, ret_gn_g[0], ret_gn_b[0], ret_woutb, state_ret[0], li=2)
    xp = xp.reshape(B, L, D)

    lru_winb, lru_woutb = bf(lru_w_in[0]), bf(lru_w_out[0])
    waxb = bf(jnp.concatenate([lru_w_a[0], lru_w_x[0]], axis=-1))
    xp, conv_p, lru_p = _lru_prompt(xp, lru_norm[0], lru_winb, lru_conv_w[0], lru_conv_b[0], waxb,
                                    lru_b_a[0], lru_b_x[0], lru_lam[0], lru_woutb)
    xs, conv_s, lru_s = _lru_sample(xs, jnp.swapaxes(state_conv[0], 0, 1), state_lru[0],
                                    lru_norm[0], lru_winb, lru_conv_w[0], lru_conv_b[0], waxb,
                                    lru_b_a[0], lru_b_x[0], lru_lam[0], lru_woutb, pos0=PAST_LEN)
    yp, ys = mlp(xp, xs, 3, final=True)

    return (yp, ys.reshape(NS, 1, D),
            pool_p[None], jnp.swapaxes(pool_s, 0, 1)[None],
            v_s.reshape(1, NS, 1, D),
            ret_p[None], ret_s[None],
            conv_p[None], jnp.swapaxes(conv_s, 0, 1)[None],
            lru_p[None], lru_s[None])
```

```python
import functools
import math

import jax
import jax.numpy as jnp
import numpy as np
from jax import lax
from jax.experimental import pallas as pl
from jax.experimental.pallas import tpu as pltpu

F32 = jnp.float32
BF16 = jnp.bfloat16

D = 1024
EPS = 1e-6
GN_EPS = 1e-5
PAST_LEN = 16384
POOL_WINDOWS = (2, 4, 8, 16)
POOL_GROUP = D // len(POOL_WINDOWS)
POOL_BUF = max(POOL_WINDOWS) - 1
CHUNK = 128
GM_GROUPS = 8
HEADS = 8
DK = D // HEADS
DV = 2 * D // HEADS
QK = HEADS * DK
VW = HEADS * DV
ROPE_BASE = 10000.0
CONV_W = 4
LRU_C = 8.0
D_FF = 4 * D

MIB = 1024 * 1024
SUBLANES = 8
LANES = 128


def _params(vmem_mib, n_grid):
    return pltpu.CompilerParams(
        dimension_semantics=("arbitrary",) * n_grid,
        vmem_limit_bytes=vmem_mib * MIB,
    )


def _const(shape):
    zeros = (0,) * len(shape)
    return pl.BlockSpec(shape, lambda *_: zeros, pipeline_mode=pl.Buffered(1))


def _rms(x, g):
    ms = jnp.mean(x * x, axis=-1, keepdims=True)
    return x * lax.rsqrt(ms + EPS) * g


def _dot(a, b):
    return jnp.dot(a, b, preferred_element_type=F32)


def _interleave(*queues):
    pos = [0] * len(queues)
    while any(p < len(q) for p, q in zip(pos, queues)):
        _, i = min((pos[i] / len(q), i) for i, q in enumerate(queues) if pos[i] < len(q))
        queues[i][pos[i]]()
        pos[i] += 1


def _softplus(x):
    return jnp.maximum(x, 0.0) + jnp.log1p(jnp.exp(-jnp.abs(x)))


GELU_K = math.sqrt(2.0 / math.pi)


def _gelu(x):
    half = 0.5 * x
    return half + half * jnp.tanh(x * (GELU_K + (GELU_K * 0.044715) * (x * x)))


MLP_FC = 512


def _weight_staging(piece):
    return [pltpu.VMEM((2, D, piece), F32), pltpu.VMEM((2, piece, D), F32),
            pltpu.SemaphoreType.DMA((2,)), pltpu.SemaphoreType.DMA((2,))]


def _mlp_thunks(read_x, write_out, g_ref, fg_ref, w1b, w2b, final, fetch_w1=None, fetch_w2=None):
    st = {}

    def head():
        x = read_x()
        st["h"] = _rms(x, g_ref[...]).astype(BF16)
        st["acc"] = x

    def up(c):
        def run():
            if fetch_w1 is not None:
                fetch_w1(c)
            a = _dot(st["h"], w1b[:, c * MLP_FC:(c + 1) * MLP_FC])
            st["a"] = jnp.square(jnp.maximum(a, 0.0)).astype(BF16)
        return run

    def down(c):
        def run():
            if fetch_w2 is not None:
                fetch_w2(c)
            st["acc"] = st["acc"] + _dot(st["a"], w2b[c * MLP_FC:(c + 1) * MLP_FC, :])
        return run

    def tail():
        acc = st["acc"]
        write_out(_rms(acc, fg_ref[...]) if final else acc)

    return [head] + [f(c) for c in range(D_FF // MLP_FC) for f in (up, down)] + [tail]


def _run(thunks):
    for thunk in thunks:
        thunk()


def _store(ref):
    def write(v):
        ref[...] = v
    return write


def _weight_stream(w1_hbm, w2_hbm, li, w1b, w2b, stg1, stg2, sem1, sem2):
    W_FC = stg1.shape[2]
    assert stg2.shape[1] == W_FC and MLP_FC % W_FC == 0
    npc = D_FF // W_FC
    per = MLP_FC // W_FC

    def w1_copy(p):
        return pltpu.make_async_copy(w1_hbm.at[li, :, pl.ds(p * W_FC, W_FC)], stg1.at[p % 2], sem1.at[p % 2])

    def w2_copy(p):
        return pltpu.make_async_copy(w2_hbm.at[li, pl.ds(p * W_FC, W_FC), :], stg2.at[p % 2], sem2.at[p % 2])

    def prime():
        for p in range(min(2, npc)):
            w1_copy(p).start()
            w2_copy(p).start()

    def fetch_w1(c):
        for p in range(c * per, (c + 1) * per):
            w1_copy(p).wait()
            w1b[:, p * W_FC:(p + 1) * W_FC] = stg1[p % 2].astype(BF16)
            if p + 2 < npc:
                w1_copy(p + 2).start()

    def fetch_w2(c):
        for p in range(c * per, (c + 1) * per):
            w2_copy(p).wait()
            w2b[p * W_FC:(p + 1) * W_FC, :] = stg2[p % 2].astype(BF16)
            if p + 2 < npc:
                w2_copy(p + 2).start()

    return prime, fetch_w1, fetch_w2


def _mlp_body(xp_ref, xs_ref, g_ref, fg_ref, w1_hbm, w2_hbm, op_ref, os_ref,
              w1b, w2b, stg1, stg2, sem1, sem2, *, li, final, n_p):
    i = pl.program_id(0)
    prime, fetch_w1, fetch_w2 = _weight_stream(w1_hbm, w2_hbm, li, w1b, w2b, stg1, stg2, sem1, sem2)

    @pl.when(i == 0)
    def _():
        prime()
        _run(_mlp_thunks(lambda: xp_ref[...], _store(op_ref), g_ref, fg_ref, w1b, w2b, final,
                         fetch_w1, fetch_w2))

    @pl.when((i > 0) & (i < n_p))
    def _():
        _run(_mlp_thunks(lambda: xp_ref[...], _store(op_ref), g_ref, fg_ref, w1b, w2b, final))

    @pl.when(i == n_p)
    def _():
        _run(_mlp_thunks(lambda: xs_ref[...], _store(os_ref), g_ref, fg_ref, w1b, w2b, final))


def _mlp(xp2d, xs2d, g, w1, w2, fg, *, li, final, tm=512):
    n, ns = xp2d.shape[0], xs2d.shape[0]
    n_p = n // tm
    row = pl.BlockSpec((tm, D), lambda i: (jnp.minimum(i, n_p - 1), 0))
    srow = pl.BlockSpec((ns, D), lambda i: (0, 0))
    hbm = pl.BlockSpec(memory_space=pl.ANY)
    return pl.pallas_call(
        functools.partial(_mlp_body, li=li, final=final, n_p=n_p),
        grid=(n_p + 1,),
        in_specs=[row, _const((ns, D)), _const((1, D)), _const((1, D)), hbm, hbm],
        out_specs=[row, srow],
        out_shape=[jax.ShapeDtypeStruct((n, D), F32), jax.ShapeDtypeStruct((ns, D), F32)],
        scratch_shapes=[pltpu.VMEM((D, D_FF), BF16), pltpu.VMEM((D_FF, D), BF16)] + _weight_staging(MLP_FC),
        compiler_params=_params(48, 1),
        name="mlp",
    )(xp2d, xs2d, g.reshape(1, D), fg.reshape(1, D), w1, w2)


POOL_HALO = 32


def _pool_thunks(x_ref, g_ref, w_ref, sc_ref, write_out, buf_ref, ext_ref, *, T, t):
    st = {}

    def head():
        x = x_ref[...]
        h = _rms(x, g_ref[...])
        halo = jnp.where(t == 0, 0.0, ext_ref[...])
        st["x"], st["h"] = x, h
        st["ext"] = jnp.concatenate([halo, h], axis=0)
        st["pos"] = t * T + lax.broadcasted_iota(jnp.int32, (T, 1), 0)
        st["y"] = [None] * len(POOL_WINDOWS)

    def group(gi, w):
        def run():
            c0, c1 = gi * POOL_GROUP, (gi + 1) * POOL_GROUP
            levels = w.bit_length() - 1
            a = st["ext"][POOL_HALO - SUBLANES * levels:, c0:c1]
            for k in range(levels):
                shift = 1 << k
                n = a.shape[0] - SUBLANES
                a = a[SUBLANES:, :] + a[SUBLANES - shift:SUBLANES - shift + n, :]
            hg = st["h"][:, c0:c1]
            cnt = jnp.minimum(st["pos"] + 1, w).astype(F32)
            d = a / cnt - hg
            st["y"][gi] = _dot(d.astype(BF16), w_ref[gi])
        return run

    def tail():
        y = jnp.concatenate(st["y"], axis=-1) * sc_ref[...]
        write_out(st["x"] + y)
        ext_ref[...] = st["h"][T - POOL_HALO:, :]
        buf_ref[...] = st["h"][T - POOL_BUF:, :]

    return [head] + [group(gi, w) for gi, w in enumerate(POOL_WINDOWS)] + [tail]


def _pool_mlp_body(x_ref, xs_ref, pg_ref, pw_ref, psc_ref, g_ref, fg_ref, w1_hbm, w2_hbm,
                   o_ref, os_ref, buf_ref, ext_ref, x1_scr, w1b, w2b, stg1, stg2, sem1, sem2,
                   *, T, tiles_per_seq, n, li):
    s = pl.program_id(0)
    t = lax.rem(s, tiles_per_seq)
    slot = lax.rem(s, 2)
    prime, fetch_w1, fetch_w2 = _weight_stream(w1_hbm, w2_hbm, li, w1b, w2b, stg1, stg2, sem1, sem2)

    def write_x1(v):
        x1_scr[slot] = v

    def mixer():
        return _pool_thunks(x_ref, pg_ref, pw_ref, psc_ref, write_x1, buf_ref, ext_ref, T=T, t=t)

    def mlp(*fetch):
        return _mlp_thunks(lambda: x1_scr[1 - slot], _store(o_ref), g_ref, fg_ref, w1b, w2b, False, *fetch)

    @pl.when(s == 0)
    def _():
        ext_ref[0:POOL_HALO, :] = jnp.zeros((POOL_HALO, D), F32)
        prime()
        _run(mixer())

    @pl.when(s == 1)
    def _():
        _interleave(mlp(fetch_w1, fetch_w2), mixer())

    @pl.when((s > 1) & (s < n))
    def _():
        _interleave(mixer(), mlp())

    @pl.when(s == n)
    def _():
        _run(mlp())
        _run(_mlp_thunks(lambda: xs_ref[...], _store(os_ref), g_ref, fg_ref, w1b, w2b, False))


def _pool_mlp(x, xs2d, pg, pwb, psc, g, w1, w2, fg, *, li, T=512):
    B, L, _ = x.shape
    ns = xs2d.shape[0]
    tps = L // T
    n = B * tps
    tile = lambda s: (jnp.minimum(s, n - 1) // tps, lax.rem(jnp.minimum(s, n - 1), tps), 0)
    prev = lambda s: (jnp.maximum(s - 1, 0) // tps, lax.rem(jnp.maximum(s - 1, 0), tps), 0)
    hbm = pl.BlockSpec(memory_space=pl.ANY)
    return pl.pallas_call(
        functools.partial(_pool_mlp_body, T=T, tiles_per_seq=tps, n=n, li=li),
        grid=(n + 1,),
        in_specs=[pl.BlockSpec((None, T, D), tile), _const((ns, D)), _const((1, D)),
                  _const((len(POOL_WINDOWS), POOL_GROUP, POOL_GROUP)), _const((1, D)),
                  _const((1, D)), _const((1, D)), hbm, hbm],
        out_specs=[pl.BlockSpec((None, T, D), prev), pl.BlockSpec((ns, D), lambda s: (0, 0)),
                   pl.BlockSpec((None, POOL_BUF, D), lambda s: (jnp.minimum(s, n - 1) // tps, 0, 0))],
        out_shape=[jax.ShapeDtypeStruct((B, L, D), F32), jax.ShapeDtypeStruct((ns, D), F32),
                   jax.ShapeDtypeStruct((B, POOL_BUF, D), F32)],
        scratch_shapes=[pltpu.VMEM((POOL_HALO, D), F32), pltpu.VMEM((2, T, D), F32),
                        pltpu.VMEM((D, D_FF), BF16), pltpu.VMEM((D_FF, D), BF16)]
        + _weight_staging(MLP_FC // 2),
        compiler_params=_params(56, 1),
        name="pool_mlp",
    )(x, xs2d, pg.reshape(1, D), pwb, psc.reshape(1, D), g.reshape(1, D), fg.reshape(1, D), w1, w2)


def _pool_sample_body(x_ref, buf_ref, g_ref, w_ref, sc_ref, o_ref, nbuf_ref, *, pos0):
    x = x_ref[...]
    h = _rms(x, g_ref[...])
    ys = []
    for gi, w in enumerate(POOL_WINDOWS):
        c0, c1 = gi * POOL_GROUP, (gi + 1) * POOL_GROUP
        hg = h[:, c0:c1]
        s = hg
        for k in range(1, w):
            s = s + buf_ref[POOL_BUF - k, :, c0:c1]
        cnt = float(min(pos0 + 1, w))
        d = s / cnt - hg
        ys.append(_dot(d.astype(BF16), w_ref[gi]))
    y = jnp.concatenate(ys, axis=-1) * sc_ref[...]
    o_ref[...] = x + y
    nbuf_ref[0:POOL_BUF - 1] = buf_ref[1:POOL_BUF]
    nbuf_ref[POOL_BUF - 1] = h


def _pool_sample(x2d, buf, g, wb, sc, *, pos0, bt=32):
    n = x2d.shape[0]
    row = pl.BlockSpec((bt, D), lambda i: (i, 0))
    brow = pl.BlockSpec((POOL_BUF, bt, D), lambda i: (0, i, 0))
    return pl.pallas_call(
        functools.partial(_pool_sample_body, pos0=pos0),
        grid=(n // bt,),
        in_specs=[row, brow, _const((1, D)), _const((len(POOL_WINDOWS), POOL_GROUP, POOL_GROUP)), _const((1, D))],
        out_specs=[row, brow],
        out_shape=[jax.ShapeDtypeStruct((n, D), F32), jax.ShapeDtypeStruct((POOL_BUF, n, D), F32)],
        compiler_params=_params(32, 1),
        name="pool_sample",
    )(x2d, buf, g.reshape(1, D), wb, sc.reshape(1, D))


def _gmlp_front(x, g_ref, win_ref, bin_ref, lng_ref, lnb_ref):
    h = _rms(x, g_ref[...]).astype(BF16)
    z = _gelu(_dot(h, win_ref[...]) + bin_ref[...])
    u, v = z[:, :D], z[:, D:]
    mu = jnp.mean(v, axis=-1, keepdims=True)
    vc = v - mu
    var = jnp.mean(jnp.square(vc), axis=-1, keepdims=True)
    vn = vc * lax.rsqrt(var + EPS) * lng_ref[...] + lnb_ref[...]
    return u, vn


def _gmlp_prompt_body(x_ref, g_ref, win_ref, bin_ref, lng_ref, lnb_ref, ws_ref, bs_ref, wout_ref,
                      o_ref, *, T):
    ti = lax.broadcasted_iota(jnp.int32, (CHUNK, CHUNK), 0)
    si = lax.broadcasted_iota(jnp.int32, (CHUNK, CHUNK), 1)
    causal = ti >= si
    wsm = [jnp.where(causal, ws_ref[gi], 0.0).astype(BF16) for gi in range(GM_GROUPS)]
    x = x_ref[...]
    u, vn = _gmlp_front(x, g_ref, win_ref, bin_ref, lng_ref, lnb_ref)
    vnb = vn.astype(BF16)
    rows = []
    for n in range(T // CHUNK):
        r0, r1 = n * CHUNK, (n + 1) * CHUNK
        cols = [_dot(wsm[gi], vnb[r0:r1, gi * CHUNK:(gi + 1) * CHUNK]) for gi in range(GM_GROUPS)]
        rows.append(jnp.concatenate(cols, axis=-1) + bs_ref[...])
    mixed = jnp.concatenate(rows, axis=0)
    y = _dot((u * mixed).astype(BF16), wout_ref[...])
    o_ref[...] = x + y


def _gmlp_prompt(x, g, winb, b_in, ln_g, ln_b, w_s, bs_full, woutb, *, T=1024):
    B, L, _ = x.shape
    blk = pl.BlockSpec((None, T, D), lambda b, t: (b, t, 0))
    return pl.pallas_call(
        functools.partial(_gmlp_prompt_body, T=T),
        grid=(B, L // T),
        in_specs=[blk, _const((1, D)), _const((D, 2 * D)), _const((1, 2 * D)), _const((1, D)), _const((1, D)),
                  _const((GM_GROUPS, CHUNK, CHUNK)), _const((CHUNK, D)), _const((D, D))],
        out_specs=blk,
        out_shape=jax.ShapeDtypeStruct((B, L, D), F32),
        compiler_params=_params(48, 2),
        name="gmlp_prompt",
    )(x, g.reshape(1, D), winb, b_in.reshape(1, 2 * D), ln_g.reshape(1, D), ln_b.reshape(1, D),
      w_s, bs_full, woutb)


def _gmlp_sample_body(x_ref, g_ref, win_ref, bin_ref, lng_ref, lnb_ref, sw_ref, sb_ref, wout_ref,
                      o_ref, vn_ref):
    x = x_ref[...]
    u, vn = _gmlp_front(x, g_ref, win_ref, bin_ref, lng_ref, lnb_ref)
    mixed = vn * sw_ref[...] + sb_ref[...]
    y = _dot((u * mixed).astype(BF16), wout_ref[...])
    o_ref[...] = x + y
    vn_ref[...] = vn


def _gmlp_sample(x2d, g, winb, b_in, ln_g, ln_b, sw, sb, woutb):
    n = x2d.shape[0]
    full = _const((n, D))
    return pl.pallas_call(
        _gmlp_sample_body,
        grid=(1,),
        in_specs=[full, _const((1, D)), _const((D, 2 * D)), _const((1, 2 * D)), _const((1, D)), _const((1, D)),
                  _const((1, D)), _const((1, D)), _const((D, D))],
        out_specs=[pl.BlockSpec((n, D), lambda i: (0, 0)), pl.BlockSpec((n, D), lambda i: (0, 0))],
        out_shape=[jax.ShapeDtypeStruct((n, D), F32), jax.ShapeDtypeStruct((n, D), F32)],
        compiler_params=_params(32, 1),
        name="gmlp_sample",
    )(x2d, g.reshape(1, D), winb, b_in.reshape(1, 2 * D), ln_g.reshape(1, D), ln_b.reshape(1, D),
      sw, sb, woutb)


def _ret_constants(C):
    log_gamma = np.log1p(-np.exp2(-5.0 - np.arange(HEADS, dtype=np.float64)))
    idx = np.arange(C, dtype=np.float64)
    diff = idx[:, None] - idx[None, :]
    dmask = np.where(diff[None] >= 0, np.exp(log_gamma[:, None, None] * np.maximum(diff, 0.0)[None]), 0.0)
    q_dec = np.exp(log_gamma[:, None] * (idx + 1.0))
    k_dec = np.exp(log_gamma[:, None] * (C - 1.0 - idx))
    chunk_dec = np.exp(log_gamma * C)
    return dmask.astype(np.float32), q_dec.astype(np.float32), k_dec.astype(np.float32), chunk_dec


def _rope_tables(pos):
    half = DK // 2
    freqs = np.exp(-math.log(ROPE_BASE) * np.arange(half, dtype=np.float64) / half)
    ang = np.asarray(pos, dtype=np.float64)[:, None] * freqs[None]
    cos = np.concatenate([np.cos(ang), np.cos(ang)], axis=-1)
    sin = np.concatenate([-np.sin(ang), np.sin(ang)], axis=-1)
    return cos, sin


def _rope(t, cos, sin):
    return t * cos + pltpu.roll(t, DK // 2, axis=1) * sin


def _group_norm_gate(o, gate, gng, gnb):
    mu = jnp.mean(o, axis=-1, keepdims=True)
    oc = o - mu
    var = jnp.mean(jnp.square(oc), axis=-1, keepdims=True)
    on = oc * lax.rsqrt(var + GN_EPS) * gng + gnb
    return jax.nn.silu(gate) * on


RET_GROUP_HEADS = 2


def _ret_prompt_body(x_ref, g_ref, win_ref, cq_ref, sq_ref, ck_ref, sk_ref, dm_ref, qd_ref, kd_ref,
                     gng_ref, gnb_ref, wout_ref, o_ref, s_out_ref, gated_scr, s_scr,
                     *, T, chunk_dec):
    t = pl.program_id(1)

    @pl.when(t == 0)
    def _():
        s_scr[...] = jnp.zeros_like(s_scr)

    x = x_ref[...]
    h = _rms(x, g_ref[...]).astype(BF16)
    cq, sq, ck, sk = cq_ref[...], sq_ref[...], ck_ref[...], sk_ref[...]
    gh = RET_GROUP_HEADS
    n_pairs = HEADS // gh

    def projection(pair, dst):
        def piece(name, lo, width):
            def run():
                dst[name] = _dot(h, win_ref[:, lo:lo + width])
            return run
        v0, g0 = 2 * QK + gh * DV * pair, 2 * QK + VW + gh * DV * pair
        return ([piece("q", gh * DK * pair, gh * DK), piece("k", QK + gh * DK * pair, gh * DK)]
                + [piece("v%d" % sub, v0 + sub * DV, DV) for sub in range(gh)]
                + [piece("g%d" % sub, g0 + sub * DV, DV) for sub in range(gh)])

    def chunk_work(pair, src):
        state = {}

        def prep(sub):
            def run():
                q = _rope(src["q"][:, sub * DK:(sub + 1) * DK], cq, sq)
                k = _rope(src["k"][:, sub * DK:(sub + 1) * DK], ck, sk)
                state[sub] = dict(q=q, k=k, qb=q.astype(BF16), kb=k.astype(BF16), s=s_scr[gh * pair + sub])
            return run

        def chunk(sub, c):
            def run():
                hd = gh * pair + sub
                st = state[sub]
                rows = slice(c * CHUNK, (c + 1) * CHUNK)
                vb = src["v%d" % sub][rows, :].astype(BF16)
                scores = lax.dot_general(st["qb"][rows], st["kb"][rows], (((1,), (1,)), ((), ())),
                                         preferred_element_type=F32) * dm_ref[hd]
                o = (_dot(scores.astype(BF16), vb)
                     + _dot((st["q"][rows] * qd_ref[hd]).astype(BF16), st["s"].astype(BF16)))
                kt = (st["k"][rows] * kd_ref[hd]).T.astype(BF16)
                st["s"] = st["s"] * chunk_dec[hd] + _dot(kt, vb)
                gated = _group_norm_gate(o, src["g%d" % sub][rows, :], gng_ref[:, hd * DV:(hd + 1) * DV],
                                         gnb_ref[:, hd * DV:(hd + 1) * DV])
                gated_scr[rows, hd * DV:(hd + 1) * DV] = gated.astype(BF16)
            return run

        def finish(sub):
            def run():
                s_scr[gh * pair + sub] = state[sub]["s"]
            return run

        return ([prep(sub) for sub in range(gh)]
                + [chunk(sub, c) for c in range(T // CHUNK) for sub in range(gh)]
                + [finish(sub) for sub in range(gh)])

    acc = [x]

    def out_projection(pair):
        def run():
            cols = slice(pair * gh * DV, (pair + 1) * gh * DV)
            acc[0] = acc[0] + _dot(gated_scr[:, cols], wout_ref[cols, :])
        return run

    cur = {}
    for run in projection(0, cur):
        run()
    for pair in range(n_pairs):
        nxt = {}
        if pair + 1 < n_pairs:
            matmul_queue = projection(pair + 1, nxt)
        else:
            matmul_queue = [out_projection(p) for p in range(n_pairs - 1)]
        _interleave(chunk_work(pair, cur), matmul_queue)
        cur = nxt
    out_projection(n_pairs - 1)()
    o_ref[...] = acc[0]
    s_out_ref[...] = s_scr[...]


def _ret_prompt(x, g, winb, gn_g, gn_b, woutb, *, T=512):
    B, L, _ = x.shape
    dmask, q_dec, k_dec, chunk_dec = _ret_constants(CHUNK)
    cos, sin = _rope_tables(np.arange(L))
    scale = DK ** -0.5
    cq, sq = jnp.asarray(cos, F32), jnp.asarray(sin, F32)
    ck, sk = jnp.asarray(cos * scale, F32), jnp.asarray(sin * scale, F32)
    qd = jnp.asarray(np.broadcast_to(q_dec[:, :, None], (HEADS, CHUNK, DK)))
    kd = jnp.asarray(np.broadcast_to(k_dec[:, :, None], (HEADS, CHUNK, DK)))
    blk = pl.BlockSpec((None, T, D), lambda b, t: (b, t, 0))
    tab = pl.BlockSpec((T, DK), lambda b, t: (t, 0))
    hcc = _const((HEADS, CHUNK, CHUNK))
    return pl.pallas_call(
        functools.partial(_ret_prompt_body, T=T, chunk_dec=tuple(float(c) for c in chunk_dec)),
        grid=(B, L // T),
        in_specs=[blk, _const((1, D)), _const((D, 2 * QK + 2 * VW)), tab, tab, tab, tab,
                  hcc, hcc, hcc, _const((1, VW)), _const((1, VW)), _const((VW, D))],
        out_specs=[blk, pl.BlockSpec((None, HEADS, DK, DV), lambda b, t: (b, 0, 0, 0))],
        out_shape=[jax.ShapeDtypeStruct((B, L, D), F32), jax.ShapeDtypeStruct((B, HEADS, DK, DV), F32)],
        scratch_shapes=[pltpu.VMEM((T, VW), BF16), pltpu.VMEM((HEADS, DK, DV), F32)],
        compiler_params=_params(56, 2),
        name="ret_prompt",
    )(x, g.reshape(1, D), winb, cq, sq, ck, sk, jnp.asarray(dmask), qd, kd,
      gn_g.reshape(1, VW), gn_b.reshape(1, VW), woutb)


def _ret_proj_sample_body(x_ref, g_ref, win_ref, cq_ref, sq_ref, ck_ref, sk_ref,
                          qt_ref, kt_ref, v_ref, gate_ref, ov_ref):
    h = _rms(x_ref[...], g_ref[...]).astype(BF16)
    p = _dot(h, win_ref[...])
    for hd in range(HEADS):
        q = _rope(p[:, hd * DK:(hd + 1) * DK], cq_ref[...], sq_ref[...])
        k = _rope(p[:, QK + hd * DK:QK + (hd + 1) * DK], ck_ref[...], sk_ref[...])
        v = p[:, 2 * QK + hd * DV:2 * QK + (hd + 1) * DV]
        qt_ref[hd] = q.T
        kt_ref[hd] = k.T
        ov_ref[:, hd * DV:(hd + 1) * DV] = jnp.sum(q * k, axis=-1, keepdims=True) * v
    v_ref[...] = p[:, 2 * QK:2 * QK + VW]
    gate_ref[...] = p[:, 2 * QK + VW:]


def _ret_proj_sample(x2d, g, winb, *, pos0):
    n = x2d.shape[0]
    cos, sin = _rope_tables(np.array([pos0]))
    scale = DK ** -0.5
    tabs = [jnp.asarray(a, F32) for a in (cos, sin, cos * scale, sin * scale)]
    row = pl.BlockSpec((n, VW), lambda i: (0, 0))
    tr = pl.BlockSpec((HEADS, DK, n), lambda i: (0, 0, 0))
    return pl.pallas_call(
        _ret_proj_sample_body,
        grid=(1,),
        in_specs=[_const((n, D)), _const((1, D)), _const((D, 2 * QK + 2 * VW))] + [_const((1, DK))] * 4,
        out_specs=[tr, tr, row, row, row],
        out_shape=[jax.ShapeDtypeStruct((HEADS, DK, n), F32)] * 2 + [jax.ShapeDtypeStruct((n, VW), F32)] * 3,
        compiler_params=_params(40, 1),
        name="ret_proj_sample",
    )(x2d, g.reshape(1, D), winb, *tabs)


def _mlp_ret_body(xp_ref, xs_ref, g_ref, fg_ref, w1_hbm, w2_hbm,
                  qt_ref, kt_ref, v_ref, ov_ref, gate_ref, gng_ref, gnb_ref, rwout_ref, s_hbm,
                  op_ref, os_ref, snew_hbm,
                  w1b, w2b, stg1, stg2, sem1, sem2, s_in, s_out, sem_in, sem_out, o_scr,
                  *, li, n_p, ns, gamma):
    i = pl.program_id(0)
    spb = ns // n_p
    prime, fetch_w1, fetch_w2 = _weight_stream(w1_hbm, w2_hbm, li, w1b, w2b, stg1, stg2, sem1, sem2)

    def load(c, slot):
        return pltpu.make_async_copy(s_hbm.at[c], s_in.at[slot], sem_in.at[slot])

    def store(c, slot):
        return pltpu.make_async_copy(s_out.at[slot], snew_hbm.at[c], sem_out.at[slot])

    def state_update(j):
        def run():
            c = i * spb + j
            slot = j % 2
            load(c, slot).wait()
            store(c, slot).wait()
            lane = lax.broadcasted_iota(jnp.int32, (DK, ns), 1) == c
            for hd in range(HEADS):
                cols = slice(hd * DV, (hd + 1) * DV)
                s_old = s_in[slot, hd]
                qcol = jnp.sum(jnp.where(lane, qt_ref[hd], 0.0), axis=1, keepdims=True)
                kcol = jnp.sum(jnp.where(lane, kt_ref[hd], 0.0), axis=1, keepdims=True)
                v = v_ref[pl.ds(c, 1), cols]
                o_scr[pl.ds(c, 1), cols] = jnp.sum(s_old * (qcol * gamma[hd]), axis=0, keepdims=True)
                s_out[slot, hd] = s_old * gamma[hd] + kcol * v
            store(c, slot).start()
            load(jnp.minimum(c + 2, ns - 1), slot).start()
        return run

    def states():
        return [state_update(j) for j in range(spb)]

    @pl.when(i == 0)
    def _():
        prime()
        s_out[...] = jnp.zeros_like(s_out)
        for slot in range(2):
            load(slot, slot).start()
            store(slot, slot).start()
        _interleave(_mlp_thunks(lambda: xp_ref[...], _store(op_ref), g_ref, fg_ref, w1b, w2b, False,
                                fetch_w1, fetch_w2), states())

    @pl.when((i > 0) & (i < n_p))
    def _():
        _interleave(states(), _mlp_thunks(lambda: xp_ref[...], _store(op_ref), g_ref, fg_ref, w1b, w2b, False))

    @pl.when(i == n_p)
    def _():
        for slot in range(2):
            load(ns - 1, slot).wait()
            store(ns - 1, slot).wait()
        parts = []
        for hd in range(HEADS):
            cols = slice(hd * DV, (hd + 1) * DV)
            o = ov_ref[:, cols] + o_scr[:, cols]
            parts.append(_group_norm_gate(o, gate_ref[:, cols], gng_ref[:, cols], gnb_ref[:, cols]))
        gated = jnp.concatenate(parts, axis=-1).astype(BF16)
        xs1 = xs_ref[...] + _dot(gated, rwout_ref[...])
        _run(_mlp_thunks(lambda: xs1, _store(os_ref), g_ref, fg_ref, w1b, w2b, False))


def _mlp_ret(xp2d, xs2d, g, w1, w2, fg, qt, kt, v, ov, gate, gn_g, gn_b, rwoutb, s, *, li, tm=512):
    n, ns = xp2d.shape[0], xs2d.shape[0]
    n_p = n // tm
    _, _, _, chunk_dec = _ret_constants(1)
    row = pl.BlockSpec((tm, D), lambda i: (jnp.minimum(i, n_p - 1), 0))
    srow = pl.BlockSpec((ns, D), lambda i: (0, 0))
    hbm = pl.BlockSpec(memory_space=pl.ANY)
    wide = _const((ns, VW))
    return pl.pallas_call(
        functools.partial(_mlp_ret_body, li=li, n_p=n_p, ns=ns, gamma=tuple(float(c) for c in chunk_dec)),
        grid=(n_p + 1,),
        in_specs=[row, _const((ns, D)), _const((1, D)), _const((1, D)), hbm, hbm,
                  _const((HEADS, DK, ns)), _const((HEADS, DK, ns)), wide, wide, wide,
                  _const((1, VW)), _const((1, VW)), _const((VW, D)), hbm],
        out_specs=[row, srow, hbm],
        out_shape=[jax.ShapeDtypeStruct((n, D), F32), jax.ShapeDtypeStruct((ns, D), F32),
                   jax.ShapeDtypeStruct(s.shape, F32)],
        scratch_shapes=[pltpu.VMEM((D, D_FF), BF16), pltpu.VMEM((D_FF, D), BF16)] + _weight_staging(MLP_FC)
        + [pltpu.VMEM((2, HEADS, DK, DV), F32), pltpu.VMEM((2, HEADS, DK, DV), F32),
           pltpu.SemaphoreType.DMA((2,)), pltpu.SemaphoreType.DMA((2,)),
           pltpu.VMEM((ns, VW), F32)],
        compiler_params=_params(58, 1),
        name="mlp_ret",
    )(xp2d, xs2d, g.reshape(1, D), fg.reshape(1, D), w1, w2, qt, kt, v, ov, gate,
      gn_g.reshape(1, VW), gn_b.reshape(1, VW), rwoutb, s)


def _lru_gates(xc, wax_ref, ba, bx, lam, heads):
    xcb = xc.astype(BF16)
    rs, is_ = [], []
    for n, hd in enumerate(heads):
        ri = _dot(xcb[:, n * DK:(n + 1) * DK], wax_ref[hd])
        rs.append(ri[:, :DK])
        is_.append(ri[:, DK:])
    r = jax.nn.sigmoid(jnp.concatenate(rs, axis=-1) + ba)
    i = jax.nn.sigmoid(jnp.concatenate(is_, axis=-1) + bx)
    log_a = r * (-LRU_C * _softplus(-lam))
    a = jnp.exp(log_a)
    mult = jnp.sqrt(-jnp.tanh(log_a) * (a * a + 1.0))
    return a, mult, i


LRU_GROUP_HEADS = 4


def _lru_prompt_body(x_ref, g_ref, win_ref, cw_ref, cb_ref, wax_ref, ba_ref, bx_ref, lam_ref, wout_ref,
                     o_ref, conv_ref, hlast_ref, xb_scr, hs_scr, carry_scr, h_scr, *, NB, T):
    t = pl.program_id(0)
    N = NB * T
    PS = T + 1
    halo = (CONV_W - 1) * NB
    gw = LRU_GROUP_HEADS * DK
    gblk = gw // LANES

    @pl.when(t == 0)
    def _():
        carry_scr[...] = jnp.zeros_like(carry_scr)
        h_scr[...] = jnp.zeros_like(h_scr)

    x = x_ref[...].reshape(N, D)
    h = _rms(x, g_ref[...]).astype(BF16)
    row = lax.broadcasted_iota(jnp.int32, (N, 1), 0)
    pos = t * T + lax.shift_right_logical(row, NB.bit_length() - 1)
    out = x
    def project(grp):
        c0, c1 = grp * gw, (grp + 1) * gw
        return _dot(h, win_ref[:, c0:c1]), _dot(h, win_ref[:, D + c0:D + c1])

    ngrp = D // gw
    nxt = project(0)
    for grp in range(ngrp):
        c0, c1 = grp * gw, (grp + 1) * gw
        blks = range(grp * gblk, (grp + 1) * gblk)
        zg, xb = nxt
        if grp + 1 < ngrp:
            nxt = project(grp + 1)
        gate = _gelu(zg)
        for n, j in enumerate(blks):
            for b in range(NB):
                xb_scr[j, b * PS:b * PS + T, :] = xb[b * T:(b + 1) * T, n * LANES:(n + 1) * LANES]
        ext = jnp.concatenate(
            [carry_scr[:, c0:c1]]
            + [jnp.concatenate([xb_scr[j, pl.ds(s, NB, stride=PS), :] for j in blks], axis=-1)
               for s in range(T)], axis=0)
        carry_scr[:, c0:c1] = ext[N:N + halo, :]
        acc = ext[0:N, :] * cw_ref[0:1, c0:c1]
        for j in range(1, CONV_W):
            acc = acc + ext[j * NB:j * NB + N, :] * cw_ref[j:j + 1, c0:c1]
        xc = cb_ref[:, c0:c1] + acc
        heads = range(grp * LRU_GROUP_HEADS, (grp + 1) * LRU_GROUP_HEADS)
        a, mult, i = _lru_gates(xc, wax_ref, ba_ref[:, c0:c1], bx_ref[:, c0:c1], lam_ref[:, c0:c1], heads)
        bvec = jnp.where(pos == 0, 1.0, mult) * (i * xc)
        hcur = h_scr[:, c0:c1]
        for s in range(T):
            hcur = a[s * NB:(s + 1) * NB, :] * hcur + bvec[s * NB:(s + 1) * NB, :]
            for n, j in enumerate(blks):
                hs_scr[j, pl.ds(s, NB, stride=PS), :] = hcur[:, n * LANES:(n + 1) * LANES]
        h_scr[:, c0:c1] = hcur
        hs = jnp.concatenate(
            [jnp.concatenate([hs_scr[j, b * PS:b * PS + T, :] for b in range(NB)], axis=0) for j in blks],
            axis=-1)
        out = out + _dot((hs * gate).astype(BF16), wout_ref[c0:c1, :])
    o_ref[...] = out.reshape(NB, T, D)
    hlast_ref[...] = h_scr[...]

    @pl.when(t == pl.num_programs(0) - 1)
    def _():
        for j in range(CONV_W - 1):
            conv_ref[:, j, :] = carry_scr[j * NB:(j + 1) * NB, :]


def _lru_prompt(x, g, winb, conv_w, conv_b, waxb, b_a, b_x, lam, woutb, *, T=128):
    B, L, _ = x.shape
    blk = pl.BlockSpec((B, T, D), lambda t: (0, t, 0))
    vec = _const((1, D))
    return pl.pallas_call(
        functools.partial(_lru_prompt_body, NB=B, T=T),
        grid=(L // T,),
        in_specs=[blk, vec, _const((D, 2 * D)), _const((CONV_W, D)), vec, _const((HEADS, DK, 2 * DK)),
                  vec, vec, vec, _const((D, D))],
        out_specs=[blk, pl.BlockSpec((B, CONV_W - 1, D), lambda t: (0, 0, 0)),
                   pl.BlockSpec((B, D), lambda t: (0, 0))],
        out_shape=[jax.ShapeDtypeStruct((B, L, D), F32), jax.ShapeDtypeStruct((B, CONV_W - 1, D), F32),
                   jax.ShapeDtypeStruct((B, D), F32)],
        scratch_shapes=[pltpu.VMEM((D // LANES, B * (T + 1), LANES), F32)] * 2
        + [pltpu.VMEM(((CONV_W - 1) * B, D), F32), pltpu.VMEM((B, D), F32)],
        compiler_params=_params(48, 1),
        name="lru_prompt",
    )(x, g.reshape(1, D), winb, conv_w, conv_b.reshape(1, D), waxb, b_a.reshape(1, D), b_x.reshape(1, D),
      lam.reshape(1, D), woutb)


def _lru_sample_body(x_ref, cbuf_ref, h0_ref, g_ref, win_ref, cw_ref, cb_ref, wax_ref, ba_ref, bx_ref,
                     lam_ref, wout_ref, o_ref, nconv_ref, hnew_ref, *, pos0):
    x = x_ref[...]
    h = _rms(x, g_ref[...]).astype(BF16)
    z = _dot(h, win_ref[...])
    gate = _gelu(z[:, :D])
    xb = z[:, D:]
    cw = cw_ref[...]
    acc = cbuf_ref[0] * cw[0:1, :]
    for j in range(1, CONV_W - 1):
        acc = acc + cbuf_ref[j] * cw[j:j + 1, :]
    acc = acc + xb * cw[CONV_W - 1:CONV_W, :]
    xc = cb_ref[...] + acc
    a, mult, i = _lru_gates(xc, wax_ref, ba_ref[...], bx_ref[...], lam_ref[...], range(HEADS))
    if pos0 == 0:
        mult = jnp.ones_like(mult)
    hnew = a * h0_ref[...] + mult * (i * xc)
    hnew_ref[...] = hnew
    nconv_ref[0:CONV_W - 2] = cbuf_ref[1:CONV_W - 1]
    nconv_ref[CONV_W - 2] = xb
    o_ref[...] = x + _dot((hnew * gate).astype(BF16), wout_ref[...])


def _lru_sample(x2d, cbuf, h0, g, winb, conv_w, conv_b, waxb, b_a, b_x, lam, woutb, *, pos0):
    n = x2d.shape[0]
    vec = _const((1, D))
    cshape = (CONV_W - 1, n, D)
    out = pl.BlockSpec((n, D), lambda i: (0, 0))
    return pl.pallas_call(
        functools.partial(_lru_sample_body, pos0=pos0),
        grid=(1,),
        in_specs=[_const((n, D)), _const(cshape), _const((n, D)), vec, _const((D, 2 * D)), _const((CONV_W, D)),
                  vec, _const((HEADS, DK, 2 * DK)), vec, vec, vec, _const((D, D))],
        out_specs=[out, pl.BlockSpec(cshape, lambda i: (0, 0, 0)), out],
        out_shape=[jax.ShapeDtypeStruct((n, D), F32), jax.ShapeDtypeStruct(cshape, F32),
                   jax.ShapeDtypeStruct((n, D), F32)],
        compiler_params=_params(32, 1),
        name="lru_sample",
    )(x2d, cbuf, h0, g.reshape(1, D), winb, conv_w, conv_b.reshape(1, D), waxb, b_a.reshape(1, D),
      b_x.reshape(1, D), lam.reshape(1, D), woutb)


def kernel(x_prompt, x_sample, state_pool, state_ret, state_conv, state_lru, pool_norm, pool_w, pool_scale, gm_norm, gm_w_in, gm_b_in, gm_ln_g, gm_ln_b, gm_w_s, gm_b_s, gm_w_out, ret_norm, ret_w_in, ret_gn_g, ret_gn_b, ret_w_out, lru_norm, lru_w_in, lru_conv_w, lru_conv_b, lru_w_a, lru_b_a, lru_w_x, lru_b_x, lru_lam, lru_w_out, mlp_norm, mlp_w1, mlp_w2, final_norm):
    B, L, _ = x_prompt.shape
    NS = x_sample.shape[0]
    bf = lambda w: w.astype(BF16)

    def mlp(xp, xs, li, final=False):
        yp, ys = _mlp(xp.reshape(B * L, D), xs, mlp_norm[li], mlp_w1, mlp_w2, final_norm, li=li, final=final)
        return yp.reshape(B, L, D), ys

    pool_wb = bf(pool_w[0])
    xs, pool_s = _pool_sample(x_sample.reshape(NS, D), jnp.swapaxes(state_pool[0], 0, 1),
                              pool_norm[0], pool_wb, pool_scale[0], pos0=PAST_LEN)
    xp, xs, pool_p = _pool_mlp(x_prompt, xs, pool_norm[0], pool_wb, pool_scale[0], mlp_norm[0],
                               mlp_w1, mlp_w2, final_norm, li=0)

    gm_winb, gm_woutb = bf(gm_w_in[0]), bf(gm_w_out[0])
    bs_full = jnp.repeat(gm_b_s[0].T, CHUNK, axis=1)
    xp = _gmlp_prompt(xp, gm_norm[0], gm_winb, gm_b_in[0], gm_ln_g[0], gm_ln_b[0], gm_w_s[0], bs_full, gm_woutb)
    sw = jnp.repeat(gm_w_s[0][:, 0, 0], CHUNK).reshape(1, D)
    sb = jnp.repeat(gm_b_s[0][:, 0], CHUNK).reshape(1, D)
    xs, v_s = _gmlp_sample(xs, gm_norm[0], gm_winb, gm_b_in[0], gm_ln_g[0], gm_ln_b[0], sw, sb, gm_woutb)
    xp, xs = mlp(xp, xs, 1)

    ret_winb, ret_woutb = bf(ret_w_in[0]), bf(ret_w_out[0])
    xp, ret_p = _ret_prompt(xp, ret_norm[0], ret_winb, ret_gn_g[0], ret_gn_b[0], ret_woutb)
    qt, kt, v, gate, ov = _ret_proj_sample(xs, ret_norm[0], ret_winb, pos0=PAST_LEN)
    xp, xs, ret_s = _mlp_ret(xp.reshape(B * L, D), xs, mlp_norm[2], mlp_w1, mlp_w2, final_norm,
                             qt, kt, v, ov, gate, ret_gn_g[0], ret_gn_b[0], ret_woutb, state_ret[0], li=2)
    xp = xp.reshape(B, L, D)

    lru_winb, lru_woutb = bf(lru_w_in[0]), bf(lru_w_out[0])
    waxb = bf(jnp.concatenate([lru_w_a[0], lru_w_x[0]], axis=-1))
    xp, conv_p, lru_p = _lru_prompt(xp, lru_norm[0], lru_winb, lru_conv_w[0], lru_conv_b[0], waxb,
                                    lru_b_a[0], lru_b_x[0], lru_lam[0], lru_woutb)
    xs, conv_s, lru_s = _lru_sample(xs, jnp.swapaxes(state_conv[0], 0, 1), state_lru[0],
                                    lru_norm[0], lru_winb, lru_conv_w[0], lru_conv_b[0], waxb,
                                    lru_b_a[0], lru_b_x[0], lru_lam[0], lru_woutb, pos0=PAST_LEN)
    yp, ys = mlp(xp, xs, 3, final=True)

    return (yp, ys.reshape(NS, 1, D),
            pool_p[None], jnp.swapaxes(pool_s, 0, 1)[None],
            v_s.reshape(1, NS, 1, D),
            ret_p[None], ret_s[None],
            conv_p[None], jnp.swapaxes(conv_s, 0, 1)[None],
            lru_p[None], lru_s[None])
```
